```python
import jax, jax.numpy as jnp
from jax import lax
import numpy as np

D_MODEL = 1024
BATCH = 8
SEQ = 2048
DEPTH = 2
DEC_BATCH = 128
DEC_SEQ = 1
PAST_LEN = 8192
PAGE_SIZE = 128

N_A_LAYERS = DEPTH // 2
N_B_LAYERS = DEPTH - N_A_LAYERS
CHUNK = 128
A_WIDTH = 2 * D_MODEL
A_GROUPS = 8
A_GROUP_DIM = A_WIDTH // A_GROUPS
HEAD_DIM = 64
N_HEADS = D_MODEL // HEAD_DIM
N_KV_HEADS = 4
GQA_GROUP = N_HEADS // N_KV_HEADS
WINDOW = 128
Q_BLOCK = 128
ROT_DIM = HEAD_DIM // 4
ROPE_THETA = 500000.0
EPS = 1e-5
BUF_LEN = min(WINDOW, PAST_LEN)

kernel_name = "yoco_chunk_gmlp_swa_sink_step"


def rms_norm(x, g):
    xf = x.astype(jnp.float32)
    y = xf * lax.rsqrt(jnp.mean(xf * xf, axis=-1, keepdims=True) + EPS)
    return (y * g.astype(jnp.float32)).astype(x.dtype)


def rotary(x, start):
    L = x.shape[1]
    pos = (start + jnp.arange(L)).astype(jnp.float32)
    inv = ROPE_THETA ** (-jnp.arange(0, ROT_DIM, 2, dtype=jnp.float32) / ROT_DIM)
    ang = pos[:, None] * inv[None, :]
    cos = jnp.cos(ang)[None, :, None, :]
    sin = jnp.sin(ang)[None, :, None, :]
    xr = x[..., :ROT_DIM].astype(jnp.float32)
    x1, x2 = xr[..., :ROT_DIM // 2], xr[..., ROT_DIM // 2:]
    rot = jnp.concatenate([x1 * cos - x2 * sin, x2 * cos + x1 * sin], axis=-1).astype(x.dtype)
    return jnp.concatenate([rot, x[..., ROT_DIM:]], axis=-1)


def chunk_gmlp_mixer(h, norm_g, w_in, v_norm_g, w_s, b_s, w_out):
    B, L, _ = h.shape
    xn = rms_norm(h, norm_g)
    proj = jnp.einsum('bld,de->ble', xn, w_in)
    u, v, gate = jnp.split(proj, 3, axis=-1)
    v = rms_norm(v, v_norm_g)
    cl = CHUNK if L >= CHUNK else L
    n = -(-L // cl)
    pad = n * cl - L
    vp = jnp.pad(v, ((0, 0), (0, pad), (0, 0))).reshape(B, n, cl, A_GROUPS, A_GROUP_DIM)
    causal = jnp.tril(jnp.ones((cl, cl), dtype=bool))
    ws = jnp.where(causal[None], w_s[:, :cl, :cl], 0)
    z = jnp.einsum('gij,bnjgc->bnigc', ws, vp) + b_s[:, :cl].T[None, None, :, :, None]
    z = z.reshape(B, n * cl, A_WIDTH)[:, :L]
    y = u * z * jax.nn.silu(gate)
    return jnp.einsum('ble,ed->bld', y, w_out), v


def shared_kv(h, kv_norm, w_kv, start):
    B, L, _ = h.shape
    xn = rms_norm(h, kv_norm)
    kv = jnp.einsum('bld,de->ble', xn, w_kv)
    k, v = jnp.split(kv, 2, axis=-1)
    k = rotary(k.reshape(B, L, N_KV_HEADS, HEAD_DIM), start)
    return k, v.reshape(B, L, N_KV_HEADS, HEAD_DIM)


def sliding_window_attention(q, k_ext, v_ext, sinks, start):
    B, L = q.shape[:2]
    qb = Q_BLOCK if L >= Q_BLOCK else L
    n = -(-L // qb)
    pad = n * qb - L
    q = jnp.pad(q, ((0, 0), (0, pad), (0, 0), (0, 0))).reshape(B, n, qb, N_KV_HEADS, GQA_GROUP, HEAD_DIM)
    k_ext = jnp.pad(k_ext, ((0, 0), (0, pad), (0, 0), (0, 0)))
    v_ext = jnp.pad(v_ext, ((0, 0), (0, pad), (0, 0), (0, 0)))
    span = WINDOW + qb
    idx = (jnp.arange(n) * qb)[:, None] + jnp.arange(span)[None, :]
    kb = k_ext[:, idx]
    vb = v_ext[:, idx]
    s = jnp.einsum('bnqkgd,bnskd->bnkgqs', q, kb, preferred_element_type=jnp.float32) * (HEAD_DIM ** -0.5)
    i = jnp.arange(qb)[:, None]
    j = jnp.arange(span)[None, :]
    band = (j >= i) & (j <= WINDOW + i)
    key_pos = start - WINDOW + idx
    valid = band[None] & (key_pos >= 0)[:, None, :]
    s = jnp.where(valid[None, :, None, None], s, -jnp.inf)
    sink = jnp.broadcast_to(sinks.astype(jnp.float32).reshape(1, 1, N_KV_HEADS, GQA_GROUP, 1, 1), s.shape[:-1] + (1,))
    p = jax.nn.softmax(jnp.concatenate([s, sink], axis=-1), axis=-1)[..., :-1]
    o = jnp.einsum('bnkgqs,bnskd->bnqkgd', p.astype(v_ext.dtype), vb)
    return o.reshape(B, n * qb, N_HEADS * HEAD_DIM)[:, :L]


def swa_mixer(h, k_ext, v_ext, start, norm_g, w_in, sinks, w_out):
    B, L, _ = h.shape
    xn = rms_norm(h, norm_g)
    proj = jnp.einsum('bld,de->ble', xn, w_in)
    q, gate = jnp.split(proj, 2, axis=-1)
    q = rotary(q.reshape(B, L, N_HEADS, HEAD_DIM), start)
    o = sliding_window_attention(q, k_ext, v_ext, sinks, start)
    return jnp.einsum('ble,ed->bld', o * jax.nn.silu(gate), w_out)


def trunk(x, start, k_past, v_past, norm_a, w_in_a, v_norm_a, w_s_a, b_s_a, w_out_a,
          kv_norm, w_kv, norm_b, w_in_b, sinks_b, w_out_b, final_norm):
    B, L, _ = x.shape
    front = WINDOW - k_past.shape[1]
    h = x
    a_rows = []
    k_ext = None
    v_ext = None
    for layer in range(DEPTH):
        if layer < N_A_LAYERS:
            out, v_rows = chunk_gmlp_mixer(h, norm_a[layer], w_in_a[layer], v_norm_a[layer],
                                           w_s_a[layer], b_s_a[layer], w_out_a[layer])
            h = h + out
            a_rows.append(v_rows)
        else:
            if layer == N_A_LAYERS:
                k_new, v_new = shared_kv(h, kv_norm, w_kv, start)
                k_ext = jnp.concatenate([jnp.pad(k_past, ((0, 0), (front, 0), (0, 0), (0, 0))).astype(k_new.dtype), k_new], axis=1)
                v_ext = jnp.concatenate([jnp.pad(v_past, ((0, 0), (front, 0), (0, 0), (0, 0))).astype(v_new.dtype), v_new], axis=1)
            lb = layer - N_A_LAYERS
            h = h + swa_mixer(h, k_ext, v_ext, start, norm_b[lb], w_in_b[lb], sinks_b[lb], w_out_b[lb])
    y = rms_norm(h, final_norm)
    keep = min(WINDOW, start + L)
    return y, k_ext[:, -keep:], v_ext[:, -keep:], jnp.stack(a_rows)


def setup_inputs(seed: int = 0) -> dict:
    key = jax.random.key(seed)
    ks = jax.random.split(key, 20)
    f32 = jnp.float32
    nrm = lambda k, shape, scale: jax.random.normal(k, shape, f32) * scale
    return {
        'x_prompt': nrm(ks[0], (BATCH, SEQ, D_MODEL), 1.0),
        'x_sample': nrm(ks[1], (DEC_BATCH, DEC_SEQ, D_MODEL), 1.0),
        'cache_k': nrm(ks[2], (DEC_BATCH, BUF_LEN, N_KV_HEADS, HEAD_DIM), 1.0),
        'cache_v': nrm(ks[3], (DEC_BATCH, BUF_LEN, N_KV_HEADS, HEAD_DIM), 1.0),
        'norm_a': 1.0 + nrm(ks[4], (N_A_LAYERS, D_MODEL), 0.1),
        'w_in_a': nrm(ks[5], (N_A_LAYERS, D_MODEL, 3 * A_WIDTH), D_MODEL ** -0.5),
        'v_norm_a': 1.0 + nrm(ks[6], (N_A_LAYERS, A_WIDTH), 0.1),
        'w_s_a': nrm(ks[7], (N_A_LAYERS, A_GROUPS, CHUNK, CHUNK), CHUNK ** -0.5),
        'b_s_a': 1.0 + nrm(ks[8], (N_A_LAYERS, A_GROUPS, CHUNK), 0.1),
        'w_out_a': nrm(ks[9], (N_A_LAYERS, A_WIDTH, D_MODEL), A_WIDTH ** -0.5),
        'kv_norm': 1.0 + nrm(ks[10], (D_MODEL,), 0.1),
        'w_kv': nrm(ks[11], (D_MODEL, 2 * N_KV_HEADS * HEAD_DIM), D_MODEL ** -0.5),
        'norm_b': 1.0 + nrm(ks[12], (N_B_LAYERS, D_MODEL), 0.1),
        'w_in_b': nrm(ks[13], (N_B_LAYERS, D_MODEL, 2 * N_HEADS * HEAD_DIM), D_MODEL ** -0.5),
        'sinks_b': nrm(ks[14], (N_B_LAYERS, N_HEADS), 1.0),
        'w_out_b': nrm(ks[15], (N_B_LAYERS, N_HEADS * HEAD_DIM, D_MODEL), (N_HEADS * HEAD_DIM) ** -0.5),
        'final_norm': 1.0 + nrm(ks[16], (D_MODEL,), 0.1),
    }


def reference(x_prompt, x_sample, cache_k, cache_v, norm_a, w_in_a, v_norm_a, w_s_a, b_s_a, w_out_a,
              kv_norm, w_kv, norm_b, w_in_b, sinks_b, w_out_b, final_norm):
    weights = (norm_a, w_in_a, v_norm_a, w_s_a, b_s_a, w_out_a, kv_norm, w_kv, norm_b, w_in_b, sinks_b, w_out_b, final_norm)
    empty = jnp.zeros((x_prompt.shape[0], 0, N_KV_HEADS, HEAD_DIM), x_prompt.dtype)
    y_prompt, new_k_prompt, new_v_prompt, _ = trunk(x_prompt, 0, empty, empty, *weights)
    y_sample, new_k_sample, new_v_sample, new_av_sample = trunk(x_sample, PAST_LEN, cache_k, cache_v, *weights)
    return (y_prompt, y_sample, new_k_prompt, new_v_prompt, new_k_sample, new_v_sample, new_av_sample)
```

```python
import functools

import jax
import jax.numpy as jnp
from jax import lax
from jax.experimental import pallas as pl
from jax.experimental.pallas import tpu as pltpu

D_MODEL = 1024
PAST_LEN = 8192
CHUNK = 128
A_WIDTH = 2 * D_MODEL
A_GROUPS = 8
A_GROUP_DIM = A_WIDTH // A_GROUPS
HEAD_DIM = 64
N_HEADS = D_MODEL // HEAD_DIM
N_KV_HEADS = 4
GQA_GROUP = N_HEADS // N_KV_HEADS
KV_DIM = N_KV_HEADS * HEAD_DIM
WINDOW = 128
Q_BLOCK = 128
ROT_DIM = HEAD_DIM // 4
ROPE_THETA = 500000.0
EPS = 1e-5

LANES = 128
VMEM_LIMIT_BYTES = 56 * 1024 * 1024

A_TILE = 256
B_TILE = 256
SAMPLE_B_TILE = 16

F32 = jnp.float32
BF16 = jnp.bfloat16


def _rms(x, g):
    return x * lax.rsqrt(jnp.mean(x * x, axis=-1, keepdims=True) + EPS) * g


def _silu(x):
    return x * jax.nn.sigmoid(x)


def _dot(a, b):
    return jnp.dot(a, b, preferred_element_type=F32)


def _dot_nt(a, b):
    return lax.dot_general(a, b, (((1,), (1,)), ((), ())), preferred_element_type=F32)


def _rotate(x, cos, sin_lo, sin_hi):
    return (x * cos + pltpu.roll(x, LANES - ROT_DIM // 2, 1) * sin_lo
            + pltpu.roll(x, ROT_DIM // 2, 1) * sin_hi)


def _rotary_tables(positions):
    inv = ROPE_THETA ** (-jnp.arange(0, ROT_DIM, 2, dtype=F32) / ROT_DIM)
    ang = positions[:, None] * inv[None, :]
    cos8, sin8 = jnp.cos(ang), jnp.sin(ang)
    lane = jnp.arange(LANES) % HEAD_DIM
    freq = lane % (ROT_DIM // 2)
    first = lane < ROT_DIM // 2
    second = (lane >= ROT_DIM // 2) & (lane < ROT_DIM)
    cos = jnp.where((first | second)[None, :], cos8[:, freq], 1.0)
    sin_lo = jnp.where(first[None, :], -sin8[:, freq], 0.0)
    sin_hi = jnp.where(second[None, :], sin8[:, freq], 0.0)
    return cos, sin_lo, sin_hi


def _layer_a_kernel(x_ref, ng_ref, win_ref, vg_ref, ws_ref, bs_ref, wout_ref, *out_refs,
                    tile, chunked):
    if chunked:
        h_ref, y_scr = out_refs
    else:
        h_ref, av_ref, y_scr = out_refs
    x = x_ref[...]
    xn = _rms(x, ng_ref[...]).astype(BF16)
    u = _dot(xn, win_ref[:, 0:A_WIDTH])
    v = _dot(xn, win_ref[:, A_WIDTH:2 * A_WIDTH])
    gate = _dot(xn, win_ref[:, 2 * A_WIDTH:3 * A_WIDTH])
    v = _rms(v, vg_ref[...])
    if chunked:
        vb = v.astype(BF16)
        row = lax.broadcasted_iota(jnp.int32, (CHUNK, CHUNK), 0)
        col = lax.broadcasted_iota(jnp.int32, (CHUNK, CHUNK), 1)
        ws = [jnp.where(row >= col, ws_ref[g], 0.0).astype(BF16) for g in range(A_GROUPS)]
        for c in range(tile // CHUNK):
            rows = slice(c * CHUNK, (c + 1) * CHUNK)
            z = jnp.concatenate(
                [_dot(ws[g], vb[rows, g * A_GROUP_DIM:(g + 1) * A_GROUP_DIM])
                 for g in range(A_GROUPS)], axis=1) + bs_ref[...]
            y_scr[rows, :] = (u[rows, :] * z * _silu(gate[rows, :])).astype(BF16)
    else:
        av_ref[...] = v
        z = v * ws_ref[...] + bs_ref[...]
        y_scr[...] = (u * z * _silu(gate)).astype(BF16)
    h_ref[...] = x + _dot(y_scr[...], wout_ref[...])


def _const_spec(shape):
    return pl.BlockSpec(shape, lambda *_: (0,) * len(shape), pipeline_mode=pl.Buffered(1))


def _layer_a(x, norm_g, w_in, v_norm_g, ws, bs, w_out, *, tile, chunked):
    n_tok = x.shape[0]
    tok_spec = pl.BlockSpec((tile, D_MODEL), lambda i: (i, 0))
    out_shape = [jax.ShapeDtypeStruct((n_tok, D_MODEL), F32)]
    out_specs = [tok_spec]
    if not chunked:
        out_shape.append(jax.ShapeDtypeStruct((n_tok, A_WIDTH), F32))
        out_specs.append(pl.BlockSpec((tile, A_WIDTH), lambda i: (i, 0)))
    return pl.pallas_call(
        functools.partial(_layer_a_kernel, tile=tile, chunked=chunked),
        grid=(n_tok // tile,),
        in_specs=[tok_spec, _const_spec(norm_g.shape), _const_spec(w_in.shape),
                  _const_spec(v_norm_g.shape), _const_spec(ws.shape), _const_spec(bs.shape),
                  _const_spec(w_out.shape)],
        out_specs=out_specs,
        out_shape=out_shape,
        scratch_shapes=[pltpu.VMEM((tile, A_WIDTH), BF16)],
        compiler_params=pltpu.CompilerParams(
            dimension_semantics=("arbitrary",), vmem_limit_bytes=VMEM_LIMIT_BYTES),
        name="layer_a_prompt" if chunked else "layer_a_sample",
    )(x, norm_g, w_in, v_norm_g, ws, bs, w_out)


def _layer_b_prompt_kernel(sinks_ref, h_ref, kvg_ref, nbg_ref, fg_ref, wkv_ref, winb_ref,
                           woutb_ref, cos_ref, slo_ref, shi_ref,
                           y_ref, kout_ref, vout_ref,
                           kext_scr, vext_scr, q_scr, og_scr, *, tile):
    t = pl.program_id(1)

    @pl.when(t == 0)
    def _():
        kext_scr[0:WINDOW, :] = jnp.zeros((WINDOW, KV_DIM), BF16)
        vext_scr[0:WINDOW, :] = jnp.zeros((WINDOW, KV_DIM), BF16)

    h = h_ref[...]
    hn = h * lax.rsqrt(jnp.mean(h * h, axis=-1, keepdims=True) + EPS)
    kv = _dot((hn * kvg_ref[...]).astype(BF16), wkv_ref[...])
    qg = _dot((hn * nbg_ref[...]).astype(BF16), winb_ref[...])
    cos, slo, shi = cos_ref[...], slo_ref[...], shi_ref[...]
    k = jnp.concatenate(
        [_rotate(kv[:, c * LANES:(c + 1) * LANES], cos, slo, shi) for c in range(KV_DIM // LANES)],
        axis=1)
    v = kv[:, KV_DIM:]
    kout_ref[0] = k[tile - WINDOW:, :]
    vout_ref[0] = v[tile - WINDOW:, :]
    kext_scr[WINDOW:, :] = k.astype(BF16)
    vext_scr[WINDOW:, :] = v.astype(BF16)
    for c in range(D_MODEL // LANES):
        cols = slice(c * LANES, (c + 1) * LANES)
        q_scr[:, cols] = (_rotate(qg[:, cols], cos, slo, shi) * HEAD_DIM ** -0.5).astype(BF16)

    n_stack = GQA_GROUP * Q_BLOCK
    i = lax.broadcasted_iota(jnp.int32, (n_stack, WINDOW + Q_BLOCK), 0) % Q_BLOCK
    j = lax.broadcasted_iota(jnp.int32, (n_stack, WINDOW + Q_BLOCK), 1)
    band = (j >= i) & (j <= WINDOW + i)
    stack_head = lax.broadcasted_iota(jnp.int32, (n_stack, 1), 0) // Q_BLOCK
    for qb in range(tile // Q_BLOCK):
        prev_ok = jnp.logical_or(t > 0, qb > 0)
        valid = band & ((j >= WINDOW) | prev_ok)
        qrows = slice(qb * Q_BLOCK, (qb + 1) * Q_BLOCK)
        krows = slice(qb * Q_BLOCK, qb * Q_BLOCK + WINDOW + Q_BLOCK)
        for kh in range(N_KV_HEADS):
            kcols = slice(kh * HEAD_DIM, (kh + 1) * HEAD_DIM)
            heads = [kh * GQA_GROUP + r for r in range(GQA_GROUP)]
            qs = jnp.concatenate(
                [q_scr[qrows, hd * HEAD_DIM:(hd + 1) * HEAD_DIM] for hd in heads], axis=0)
            s = jnp.where(valid, _dot_nt(qs, kext_scr[krows, kcols]), -jnp.inf)
            sink = jnp.zeros((n_stack, 1), F32)
            for r, hd in enumerate(heads):
                sink = jnp.where(stack_head == r, sinks_ref[hd], sink)
            m =jnp.maximum(jnp.max(s, axis=1, keepdims=True), sink)
            p = jnp.exp(s - m)
            denom = jnp.sum(p, axis=1, keepdims=True) + jnp.exp(sink - m)
            o = _dot(p.astype(BF16), vext_scr[krows, kcols]) / denom
            for r, hd in enumerate(heads):
                og_scr[qrows, hd * HEAD_DIM:(hd + 1) * HEAD_DIM] = o[r * Q_BLOCK:(r + 1) * Q_BLOCK, :]

    kext_scr[0:WINDOW, :] = kext_scr[tile:tile + WINDOW, :]
    vext_scr[0:WINDOW, :] = vext_scr[tile:tile + WINDOW, :]

    og = (og_scr[...] * _silu(qg[:, D_MODEL:])).astype(BF16)
    h2 = h + _dot(og, woutb_ref[...])
    y_ref[...] = _rms(h2, fg_ref[...])


def _layer_b_prompt(h, sinks, kv_norm, norm_b, final_norm, w_kv, w_in_b, w_out_b, *, batch, seq):
    tile = B_TILE
    n_t = seq // tile
    cos, slo, shi = _rotary_tables(jnp.arange(seq, dtype=F32))
    tok_spec = pl.BlockSpec((tile, D_MODEL), lambda b, t, *_: (b * n_t + t, 0))
    rot_spec = pl.BlockSpec((tile, LANES), lambda b, t, *_: (t, 0))
    last_spec = pl.BlockSpec((1, WINDOW, KV_DIM), lambda b, t, *_: (b, 0, 0))

    def const(shape):
        return pl.BlockSpec(shape, lambda *_: (0,) * len(shape), pipeline_mode=pl.Buffered(1))

    return pl.pallas_call(
        functools.partial(_layer_b_prompt_kernel, tile=tile),
        grid_spec=pltpu.PrefetchScalarGridSpec(
            num_scalar_prefetch=1,
            grid=(batch, n_t),
            in_specs=[tok_spec, const(kv_norm.shape), const(norm_b.shape), const(final_norm.shape),
                      const(w_kv.shape), const(w_in_b.shape), const(w_out_b.shape),
                      rot_spec, rot_spec, rot_spec],
            out_specs=[tok_spec, last_spec, last_spec],
            scratch_shapes=[pltpu.VMEM((WINDOW + tile, KV_DIM), BF16),
                            pltpu.VMEM((WINDOW + tile, KV_DIM), BF16),
                            pltpu.VMEM((tile, D_MODEL), BF16),
                            pltpu.VMEM((tile, D_MODEL), F32)]),
        out_shape=[jax.ShapeDtypeStruct((batch * seq, D_MODEL), F32),
                   jax.ShapeDtypeStruct((batch, WINDOW, KV_DIM), F32),
                   jax.ShapeDtypeStruct((batch, WINDOW, KV_DIM), F32)],
        compiler_params=pltpu.CompilerParams(
            dimension_semantics=("arbitrary", "arbitrary"), vmem_limit_bytes=VMEM_LIMIT_BYTES),
        name="layer_b_prompt",
    )(sinks, h, kv_norm, norm_b, final_norm, w_kv, w_in_b, w_out_b, cos, slo, shi)


def _layer_b_sample_kernel(sinks_ref, h_ref, kvg_ref, nbg_ref, fg_ref, wkv_ref, winb_ref,
                           woutb_ref, cos_ref, slo_ref, shi_ref, ck_ref, cv_ref,
                           y_ref, kout_ref, vout_ref,
                           q_scr, gate_scr, knew_scr, vnew_scr, o_scr, *, n_seq, b_tile):
    step = pl.program_id(0)

    @pl.when(step == 0)
    def _():
        h = h_ref[...]
        hn = h * lax.rsqrt(jnp.mean(h * h, axis=-1, keepdims=True) + EPS)
        kv = _dot((hn * kvg_ref[...]).astype(BF16), wkv_ref[...])
        qg = _dot((hn * nbg_ref[...]).astype(BF16), winb_ref[...])
        cos, slo, shi = cos_ref[...], slo_ref[...], shi_ref[...]
        for c in range(KV_DIM // LANES):
            cols = slice(c * LANES, (c + 1) * LANES)
            knew_scr[:, cols] = _rotate(kv[:, cols], cos, slo, shi)
        vnew_scr[...] = kv[:, KV_DIM:]
        for c in range(D_MODEL // LANES):
            cols = slice(c * LANES, (c + 1) * LANES)
            q_scr[:, cols] = _rotate(qg[:, cols], cos, slo, shi) * HEAD_DIM ** -0.5
        gate_scr[...] = qg[:, D_MODEL:]

    n_rows = GQA_GROUP * N_KV_HEADS
    row_kh = lax.broadcasted_iota(jnp.int32, (n_rows, KV_DIM), 0) % N_KV_HEADS
    lane_kh = lax.broadcasted_iota(jnp.int32, (n_rows, KV_DIM), 1) // HEAD_DIM
    own = row_kh == lane_kh
    row_id = lax.broadcasted_iota(jnp.int32, (n_rows, 1), 0)
    sink = jnp.zeros((n_rows, 1), F32)
    for r in range(GQA_GROUP):
        for kh in range(N_KV_HEADS):
            sink = jnp.where(row_id == r * N_KV_HEADS + kh, sinks_ref[kh * GQA_GROUP + r], sink)

    def one_sequence(b, carry):
        g = step * b_tile + b
        qrow = q_scr[pl.ds(g, 1), :]
        qr = [jnp.concatenate(
            [qrow[:, (kh * GQA_GROUP + r) * HEAD_DIM:(kh * GQA_GROUP + r + 1) * HEAD_DIM]
             for kh in range(N_KV_HEADS)], axis=1) for r in range(GQA_GROUP)]
        qexp = jnp.concatenate(
            [jnp.broadcast_to(qr[r], (N_KV_HEADS, KV_DIM)) for r in range(GQA_GROUP)], axis=0)
        qexp = jnp.where(own, qexp, 0.0)
        kc = ck_ref[b]
        vc = cv_ref[b]
        knew = knew_scr[pl.ds(g, 1), :]
        vnew = vnew_scr[pl.ds(g, 1), :]
        s_old = _dot_nt(qexp.astype(BF16), kc.astype(BF16))
        s_new = jnp.sum(qexp.astype(BF16).astype(F32) * knew.astype(BF16).astype(F32),
                        axis=1, keepdims=True)
        m = jnp.maximum(jnp.maximum(jnp.max(s_old, axis=1, keepdims=True), s_new), sink)
        p_old = jnp.exp(s_old - m)
        p_new = jnp.exp(s_new - m)
        denom = jnp.sum(p_old, axis=1, keepdims=True) + p_new + jnp.exp(sink - m)
        o = (_dot(p_old.astype(BF16), vc.astype(BF16))
             + p_new.astype(BF16).astype(F32) * vnew.astype(BF16).astype(F32)) / denom
        o = jnp.where(own, o, 0.0)
        orow = [jnp.sum(o[r * N_KV_HEADS:(r + 1) * N_KV_HEADS, :], axis=0, keepdims=True)
                for r in range(GQA_GROUP)]
        o_scr[pl.ds(g, 1), :] = jnp.concatenate(
            [orow[r][:, kh * HEAD_DIM:(kh + 1) * HEAD_DIM]
             for kh in range(N_KV_HEADS) for r in range(GQA_GROUP)], axis=1)
        kout_ref[b, 0:WINDOW - 1, :] = kc[1:WINDOW, :]
        kout_ref[b, WINDOW - 1:WINDOW, :] = knew
        vout_ref[b, 0:WINDOW - 1, :] = vc[1:WINDOW, :]
        vout_ref[b, WINDOW - 1:WINDOW, :] = vnew
        return carry

    lax.fori_loop(0, b_tile, one_sequence, 0)

    @pl.when(step == pl.num_programs(0) - 1)
    def _():
        og = (o_scr[...] * _silu(gate_scr[...])).astype(BF16)
        h2 = h_ref[...] + _dot(og, woutb_ref[...])
        y_ref[...] = _rms(h2, fg_ref[...])


def _layer_b_sample(h, sinks, kv_norm, norm_b, final_norm, w_kv, w_in_b, w_out_b, cache_k, cache_v):
    n_seq = h.shape[0]
    b_tile = SAMPLE_B_TILE
    cos, slo, shi = _rotary_tables(jnp.full((1,), PAST_LEN, F32))

    def const(shape):
        return pl.BlockSpec(shape, lambda *_: (0,) * len(shape))

    cache_spec = pl.BlockSpec((b_tile, WINDOW, KV_DIM), lambda i, *_: (i, 0, 0))
    return pl.pallas_call(
        functools.partial(_layer_b_sample_kernel, n_seq=n_seq, b_tile=b_tile),
        grid_spec=pltpu.PrefetchScalarGridSpec(
            num_scalar_prefetch=1,
            grid=(n_seq // b_tile,),
            in_specs=[const(h.shape), const(kv_norm.shape), const(norm_b.shape),
                      const(final_norm.shape), const(w_kv.shape), const(w_in_b.shape),
                      const(w_out_b.shape), const(cos.shape), const(slo.shape), const(shi.shape),
                      cache_spec, cache_spec],
            out_specs=[const((n_seq, D_MODEL)), cache_spec, cache_spec],
            scratch_shapes=[pltpu.VMEM((n_seq, D_MODEL), F32),
                            pltpu.VMEM((n_seq, D_MODEL), F32),
                            pltpu.VMEM((n_seq, KV_DIM), F32),
                            pltpu.VMEM((n_seq, KV_DIM), F32),
                            pltpu.VMEM((n_seq, D_MODEL), F32)]),
        out_shape=[jax.ShapeDtypeStruct((n_seq, D_MODEL), F32),
                   jax.ShapeDtypeStruct(cache_k.shape, F32),
                   jax.ShapeDtypeStruct(cache_v.shape, F32)],
        compiler_params=pltpu.CompilerParams(
            dimension_semantics=("arbitrary",), vmem_limit_bytes=VMEM_LIMIT_BYTES),
        name="layer_b_sample",
    )(sinks, h, kv_norm, norm_b, final_norm, w_kv, w_in_b, w_out_b, cos, slo, shi, cache_k, cache_v)


def kernel(x_prompt, x_sample, cache_k, cache_v, norm_a, w_in_a, v_norm_a, w_s_a, b_s_a, w_out_a,
           kv_norm, w_kv, norm_b, w_in_b, sinks_b, w_out_b, final_norm):
    batch, seq, _ = x_prompt.shape
    n_seq, dec_seq, _ = x_sample.shape
    assert dec_seq == 1 and seq % CHUNK == 0 and cache_k.shape[1] == WINDOW
    assert norm_a.shape[0] == 1 and norm_b.shape[0] == 1

    row = lambda g: g.reshape(1, -1)
    w_in_a16 = w_in_a[0].astype(BF16)
    w_out_a16 = w_out_a[0].astype(BF16)
    w_kv16 = w_kv.astype(BF16)
    w_in_b16 = w_in_b[0].astype(BF16)
    w_out_b16 = w_out_b[0].astype(BF16)
    bs_chunk = jnp.repeat(b_s_a[0].T, A_GROUP_DIM, axis=1)
    ws_one = jnp.repeat(w_s_a[0, :, 0, 0], A_GROUP_DIM).reshape(1, A_WIDTH)
    bs_one = jnp.repeat(b_s_a[0, :, 0], A_GROUP_DIM).reshape(1, A_WIDTH)

    h_p = _layer_a(x_prompt.reshape(batch * seq, D_MODEL), row(norm_a[0]), w_in_a16,
                   row(v_norm_a[0]), w_s_a[0], bs_chunk, w_out_a16, tile=A_TILE, chunked=True)[0]
    h_s, av_s = _layer_a(x_sample.reshape(n_seq, D_MODEL), row(norm_a[0]), w_in_a16,
                         row(v_norm_a[0]), ws_one, bs_one, w_out_a16, tile=n_seq, chunked=False)

    y_p, k_p, v_p = _layer_b_prompt(h_p, sinks_b[0], row(kv_norm), row(norm_b[0]), row(final_norm),
                                    w_kv16, w_in_b16, w_out_b16, batch=batch, seq=seq)
    y_s, k_s, v_s = _layer_b_sample(h_s, sinks_b[0], row(kv_norm), row(norm_b[0]), row(final_norm),
                                    w_kv16, w_in_b16, w_out_b16,
                                    cache_k.reshape(n_seq, WINDOW, KV_DIM),
                                    cache_v.reshape(n_seq, WINDOW, KV_DIM))

    return (y_p.reshape(batch, seq, D_MODEL),
            y_s.reshape(n_seq, 1, D_MODEL),
            k_p.reshape(batch, WINDOW, N_KV_HEADS, HEAD_DIM),
            v_p.reshape(batch, WINDOW, N_KV_HEADS, HEAD_DIM),
            k_s.reshape(n_seq, WINDOW, N_KV_HEADS, HEAD_DIM),
            v_s.reshape(n_seq, WINDOW, N_KV_HEADS, HEAD_DIM),
            av_s.reshape(1, n_seq, 1, A_WIDTH))
```

```python
import functools

import jax
import jax.numpy as jnp
from jax import lax
from jax.experimental import pallas as pl
from jax.experimental.pallas import tpu as pltpu

D_MODEL = 1024
PAST_LEN = 8192
CHUNK = 128
A_WIDTH = 2 * D_MODEL
A_GROUPS = 8
A_GROUP_DIM = A_WIDTH // A_GROUPS
HEAD_DIM = 64
N_HEADS = D_MODEL // HEAD_DIM
N_KV_HEADS = 4
GQA_GROUP = N_HEADS // N_KV_HEADS
KV_DIM = N_KV_HEADS * HEAD_DIM
WINDOW = 128
Q_BLOCK = 128
ROT_DIM = HEAD_DIM // 4
ROPE_THETA = 500000.0
EPS = 1e-5

LANES = 128
VMEM_LIMIT_BYTES = 56 * 1024 * 1024

A_TILE = 256
B_TILE = 512
SAMPLE_B_TILE = 16

F32 = jnp.float32
BF16 = jnp.bfloat16


def _rms(x, g):
    return x * lax.rsqrt(jnp.mean(x * x, axis=-1, keepdims=True) + EPS) * g


def _silu(x):
    return x * jax.nn.sigmoid(x)


def _dot(a, b):
    return jnp.dot(a, b, preferred_element_type=F32)


def _dot_nt(a, b):
    return lax.dot_general(a, b, (((1,), (1,)), ((), ())), preferred_element_type=F32)


def _zero_of(x):
    bits = pltpu.bitcast(x, jnp.uint32)
    return ((bits >> 16) >> 16).astype(F32)


def _rotate(x, cos, sin_lo, sin_hi):
    return (x * cos + pltpu.roll(x, LANES - ROT_DIM // 2, 1) * sin_lo
            + pltpu.roll(x, ROT_DIM // 2, 1) * sin_hi)


def _rotary_tables(positions):
    inv = ROPE_THETA ** (-jnp.arange(0, ROT_DIM, 2, dtype=F32) / ROT_DIM)
    ang = positions[:, None] * inv[None, :]
    cos8, sin8 = jnp.cos(ang), jnp.sin(ang)
    lane = jnp.arange(LANES) % HEAD_DIM
    freq = lane % (ROT_DIM // 2)
    first = lane < ROT_DIM // 2
    second = (lane >= ROT_DIM // 2) & (lane < ROT_DIM)
    cos = jnp.where((first | second)[None, :], cos8[:, freq], 1.0)
    sin_lo = jnp.where(first[None, :], -sin8[:, freq], 0.0)
    sin_hi = jnp.where(second[None, :], sin8[:, freq], 0.0)
    return cos, sin_lo, sin_hi


def _layer_a_kernel(x_ref, ng_ref, win_ref, vg_ref, ws_ref, bs_ref, wout_ref, *out_refs,
                    tile, chunked):
    if chunked:
        h_ref, y_scr = out_refs
    else:
        h_ref, av_ref, y_scr = out_refs
    x = x_ref[...]
    xn = _rms(x, ng_ref[...]).astype(BF16)
    u = _dot(xn, win_ref[:, 0:A_WIDTH])
    v = _dot(xn, win_ref[:, A_WIDTH:2 * A_WIDTH])
    gate = _dot(xn, win_ref[:, 2 * A_WIDTH:3 * A_WIDTH])
    v = _rms(v, vg_ref[...])
    if chunked:
        vb = v.astype(BF16)
        row = lax.broadcasted_iota(jnp.int32, (CHUNK, CHUNK), 0)
        col = lax.broadcasted_iota(jnp.int32, (CHUNK, CHUNK), 1)
        ws = [jnp.where(row >= col, ws_ref[g], 0.0).astype(BF16) for g in range(A_GROUPS)]
        for c in range(tile // CHUNK):
            rows = slice(c * CHUNK, (c + 1) * CHUNK)
            z = jnp.concatenate(
                [_dot(ws[g], vb[rows, g * A_GROUP_DIM:(g + 1) * A_GROUP_DIM])
                 for g in range(A_GROUPS)], axis=1) + bs_ref[...]
            y_scr[rows, :] = (u[rows, :] * z * _silu(gate[rows, :])).astype(BF16)
    else:
        av_ref[...] = v
        z = v * ws_ref[...] + bs_ref[...]
        y_scr[...] = (u * z * _silu(gate)).astype(BF16)
    h_ref[...] = x + _dot(y_scr[...], wout_ref[...])


def _const_spec(shape):
    return pl.BlockSpec(shape, lambda *_: (0,) * len(shape), pipeline_mode=pl.Buffered(1))


def _layer_a(x, norm_g, w_in, v_norm_g, ws, bs, w_out, *, tile, chunked):
    n_tok = x.shape[0]
    tok_spec = pl.BlockSpec((tile, D_MODEL), lambda i: (i, 0))
    out_shape = [jax.ShapeDtypeStruct((n_tok, D_MODEL), F32)]
    out_specs = [tok_spec]
    if not chunked:
        out_shape.append(jax.ShapeDtypeStruct((n_tok, A_WIDTH), F32))
        out_specs.append(pl.BlockSpec((tile, A_WIDTH), lambda i: (i, 0)))
    return pl.pallas_call(
        functools.partial(_layer_a_kernel, tile=tile, chunked=chunked),
        grid=(n_tok // tile,),
        in_specs=[tok_spec, _const_spec(norm_g.shape), _const_spec(w_in.shape),
                  _const_spec(v_norm_g.shape), _const_spec(ws.shape), _const_spec(bs.shape),
                  _const_spec(w_out.shape)],
        out_specs=out_specs,
        out_shape=out_shape,
        scratch_shapes=[pltpu.VMEM((tile, A_WIDTH), BF16)],
        compiler_params=pltpu.CompilerParams(
            dimension_semantics=("arbitrary",), vmem_limit_bytes=VMEM_LIMIT_BYTES),
        name="layer_a_prompt" if chunked else "layer_a_sample",
    )(x, norm_g, w_in, v_norm_g, ws, bs, w_out)


def _layer_b_prompt_kernel(sinks_ref, h_ref, kvg_ref, nbg_ref, fg_ref, wkvt_ref, wqt_ref,
                           wgt_ref, woutt_ref, cost_ref, sint_ref,
                           y_ref, kout_ref, vout_ref,
                           kext_scr, vtext_scr, qt_scr, ogt_scr, bias_scr, *, tile, n_t):
    t = pl.program_id(1)
    n_keys = WINDOW + Q_BLOCK

    @pl.when((pl.program_id(0) == 0) & (t == 0))
    def _():
        j = lax.broadcasted_iota(jnp.int32, (n_keys, Q_BLOCK), 0)
        i = lax.broadcasted_iota(jnp.int32, (n_keys, Q_BLOCK), 1)
        band = (j >= i) & (j <= WINDOW + i)
        bias_scr[0] = jnp.where(band & (j >= WINDOW), 0.0, -jnp.inf)
        bias_scr[1] = jnp.where(band, 0.0, -jnp.inf)

    @pl.when(t == 0)
    def _():
        kext_scr[0:WINDOW, :] = jnp.zeros((WINDOW, KV_DIM), BF16)
        vtext_scr[:, 0:WINDOW] = jnp.zeros((KV_DIM, WINDOW), BF16)

    h = h_ref[...]
    hn = h * lax.rsqrt(jnp.mean(h * h, axis=-1, keepdims=True) + EPS)
    xkv = (hn * kvg_ref[...]).astype(BF16)
    xb = (hn * nbg_ref[...]).astype(BF16)

    cost, sint = cost_ref[...], sint_ref[...]
    half = ROT_DIM // 2

    def rotate_head(rows):
        lo, hi = rows[0:half, :], rows[half:ROT_DIM, :]
        return jnp.concatenate(
            [lo * cost - hi * sint, hi * cost + lo * sint, rows[ROT_DIM:, :]], axis=0)

    kvt = _dot_nt(wkvt_ref[...], xkv)
    kt = jnp.concatenate(
        [rotate_head(kvt[kh * HEAD_DIM:(kh + 1) * HEAD_DIM, :]) for kh in range(N_KV_HEADS)],
        axis=0)
    vt = kvt[KV_DIM:, :]
    kext_scr[WINDOW:, :] = kt.T.astype(BF16)
    vtext_scr[:, WINDOW:] = vt.astype(BF16)

    @pl.when(t == n_t - 1)
    def _():
        kout_ref[0] = kt[:, tile - WINDOW:]
        vout_ref[0] = vt[:, tile - WINDOW:]

    qt = _dot_nt(wqt_ref[...], xb)
    for hd in range(N_HEADS):
        rot = rotate_head(qt[hd * HEAD_DIM:(hd + 1) * HEAD_DIM, :])
        qt_scr[hd * HEAD_DIM:(hd + 1) * HEAD_DIM, :] = (rot * HEAD_DIM ** -0.5).astype(BF16)

    lane_head = lax.broadcasted_iota(jnp.int32, (1, GQA_GROUP * Q_BLOCK), 1) // Q_BLOCK
    zeros_half = jnp.zeros((HEAD_DIM, GQA_GROUP * Q_BLOCK), BF16)

    def scores(qb, kh):
        qcols = slice(qb * Q_BLOCK, (qb + 1) * Q_BLOCK)
        keys = slice(qb * Q_BLOCK, qb * Q_BLOCK + n_keys)
        q4 = jnp.concatenate(
            [qt_scr[(kh * GQA_GROUP + r) * HEAD_DIM:(kh * GQA_GROUP + r + 1) * HEAD_DIM, qcols]
             for r in range(GQA_GROUP)], axis=1)
        q4 = jnp.concatenate([q4, zeros_half] if kh % 2 == 0 else [zeros_half, q4], axis=0)
        kblk = kext_scr[keys, (kh // 2) * LANES:(kh // 2 + 1) * LANES]
        return _dot(kblk, q4)

    def finish(qb, kh, s, s_ahead):
        qcols = slice(qb * Q_BLOCK, (qb + 1) * Q_BLOCK)
        keys = slice(qb * Q_BLOCK, qb * Q_BLOCK + n_keys)
        bias = bias_scr[jnp.where(jnp.logical_or(t > 0, qb > 0), 1, 0)]
        s = s + jnp.concatenate([bias] * GQA_GROUP, axis=1)
        sink = jnp.zeros((1, GQA_GROUP * Q_BLOCK), F32)
        for r in range(GQA_GROUP):
            sink = jnp.where(lane_head == r, sinks_ref[kh * GQA_GROUP + r], sink)
        m = jnp.maximum(jnp.max(s, axis=0, keepdims=True), sink)
        p = jnp.exp(s - m)
        denom = jnp.sum(p, axis=0, keepdims=True) + jnp.exp(sink - m)
        if s_ahead is not None:
            p = jnp.concatenate(
                [p[:n_keys - 8, :], p[n_keys - 8:, :] + _zero_of(s_ahead[0:8, :])], axis=0)
        ot = _dot(vtext_scr[kh * HEAD_DIM:(kh + 1) * HEAD_DIM, keys], p.astype(BF16))
        ot = ot * (1.0 / denom)
        for r in range(GQA_GROUP):
            hd = kh * GQA_GROUP + r
            ogt_scr[hd * HEAD_DIM:(hd + 1) * HEAD_DIM, qcols] = ot[:, r * Q_BLOCK:(r + 1) * Q_BLOCK]

    blocks = [(qb, kh) for qb in range(tile // Q_BLOCK) for kh in range(N_KV_HEADS)]
    s_next = scores(*blocks[0])
    for n, blk in enumerate(blocks):
        s_cur, s_next = s_next, (scores(*blocks[n + 1]) if n + 1 < len(blocks) else None)
        finish(*blk, s_cur, s_next)

    kext_scr[0:WINDOW, :] = kext_scr[tile:tile + WINDOW, :]
    vtext_scr[:, 0:WINDOW] = vtext_scr[:, tile:tile + WINDOW]

    gt = _dot_nt(wgt_ref[...], xb)
    ogt = (ogt_scr[...] * _silu(gt)).astype(BF16)
    out_t = _dot(woutt_ref[...], ogt)
    h2 = h + out_t.T
    y_ref[...] = _rms(h2, fg_ref[...])


def _layer_b_prompt(h, sinks, kv_norm, norm_b, final_norm, w_kv_t, w_q_t, w_g_t, w_out_t,
                    *, batch, seq):
    tile = B_TILE
    n_t = seq // tile
    pos = jnp.arange(seq, dtype=F32)
    inv = ROPE_THETA ** (-jnp.arange(0, ROT_DIM, 2, dtype=F32) / ROT_DIM)
    ang_t = inv[:, None] * pos[None, :]
    cos_t, sin_t = jnp.cos(ang_t), jnp.sin(ang_t)
    tok_spec = pl.BlockSpec((tile, D_MODEL), lambda b, t, *_: (b * n_t + t, 0))
    rot_t_spec = pl.BlockSpec((ROT_DIM // 2, tile), lambda b, t, *_: (0, t))
    last_spec = pl.BlockSpec((1, KV_DIM, WINDOW), lambda b, t, *_: (b, 0, 0))

    def const(shape):
        return pl.BlockSpec(shape, lambda *_: (0,) * len(shape), pipeline_mode=pl.Buffered(1))

    return pl.pallas_call(
        functools.partial(_layer_b_prompt_kernel, tile=tile, n_t=n_t),
        grid_spec=pltpu.PrefetchScalarGridSpec(
            num_scalar_prefetch=1,
            grid=(batch, n_t),
            in_specs=[tok_spec, const(kv_norm.shape), const(norm_b.shape), const(final_norm.shape),
                      const(w_kv_t.shape), const(w_q_t.shape), const(w_g_t.shape),
                      const(w_out_t.shape), rot_t_spec, rot_t_spec],
            out_specs=[tok_spec, last_spec, last_spec],
            scratch_shapes=[pltpu.VMEM((WINDOW + tile, KV_DIM), BF16),
                            pltpu.VMEM((KV_DIM, WINDOW + tile), BF16),
                            pltpu.VMEM((D_MODEL, tile), BF16),
                            pltpu.VMEM((D_MODEL, tile), F32),
                            pltpu.VMEM((2, WINDOW + Q_BLOCK, Q_BLOCK), F32)]),
        out_shape=[jax.ShapeDtypeStruct((batch * seq, D_MODEL), F32),
                   jax.ShapeDtypeStruct((batch, KV_DIM, WINDOW), F32),
                   jax.ShapeDtypeStruct((batch, KV_DIM, WINDOW), F32)],
        compiler_params=pltpu.CompilerParams(
            dimension_semantics=("arbitrary", "arbitrary"), vmem_limit_bytes=VMEM_LIMIT_BYTES),
        name="layer_b_prompt",
    )(sinks, h, kv_norm, norm_b, final_norm, w_kv_t, w_q_t, w_g_t, w_out_t, cos_t, sin_t)


def _layer_b_sample_kernel(sinks_ref, h_ref, kvg_ref, nbg_ref, fg_ref, wkv_ref, winb_ref,
                           woutb_ref, cos_ref, slo_ref, shi_ref, ck_ref, cv_ref,
                           y_ref, kout_ref, vout_ref,
                           q_scr, gate_scr, knew_scr, vnew_scr, o_scr, *, n_seq, b_tile):
    step = pl.program_id(0)

    @pl.when(step == 0)
    def _():
        h = h_ref[...]
        hn = h * lax.rsqrt(jnp.mean(h * h, axis=-1, keepdims=True) + EPS)
        kv = _dot((hn * kvg_ref[...]).astype(BF16), wkv_ref[...])
        qg = _dot((hn * nbg_ref[...]).astype(BF16), winb_ref[...])
        cos, slo, shi = cos_ref[...], slo_ref[...], shi_ref[...]
        for c in range(KV_DIM // LANES):
            cols = slice(c * LANES, (c + 1) * LANES)
            knew_scr[:, cols] = _rotate(kv[:, cols], cos, slo, shi)
        vnew_scr[...] = kv[:, KV_DIM:]
        for c in range(D_MODEL // LANES):
            cols = slice(c * LANES, (c + 1) * LANES)
            q_scr[:, cols] = _rotate(qg[:, cols], cos, slo, shi) * HEAD_DIM ** -0.5
        gate_scr[...] = qg[:, D_MODEL:]

    n_rows = GQA_GROUP * N_KV_HEADS
    row_kh = lax.broadcasted_iota(jnp.int32, (n_rows, KV_DIM), 0) % N_KV_HEADS
    lane_kh = lax.broadcasted_iota(jnp.int32, (n_rows, KV_DIM), 1) // HEAD_DIM
    own = row_kh == lane_kh
    row_id = lax.broadcasted_iota(jnp.int32, (n_rows, 1), 0)
    sink = jnp.zeros((n_rows, 1), F32)
    for r in range(GQA_GROUP):
        for kh in range(N_KV_HEADS):
            sink = jnp.where(row_id == r * N_KV_HEADS + kh, sinks_ref[kh * GQA_GROUP + r], sink)

    def one_sequence(b, carry):
        g = step * b_tile + b
        qrow = q_scr[pl.ds(g, 1), :]
        qr = [jnp.concatenate(
            [qrow[:, (kh * GQA_GROUP + r) * HEAD_DIM:(kh * GQA_GROUP + r + 1) * HEAD_DIM]
             for kh in range(N_KV_HEADS)], axis=1) for r in range(GQA_GROUP)]
        qexp = jnp.concatenate(
            [jnp.broadcast_to(qr[r], (N_KV_HEADS, KV_DIM)) for r in range(GQA_GROUP)], axis=0)
        qexp = jnp.where(own, qexp, 0.0)
        kc = ck_ref[b]
        vc = cv_ref[b]
        knew = knew_scr[pl.ds(g, 1), :]
        vnew = vnew_scr[pl.ds(g, 1), :]
        s_old = _dot_nt(qexp.astype(BF16), kc.astype(BF16))
        s_new = jnp.sum(qexp.astype(BF16).astype(F32) * knew.astype(BF16).astype(F32),
                        axis=1, keepdims=True)
        m = jnp.maximum(jnp.maximum(jnp.max(s_old, axis=1, keepdims=True), s_new), sink)
        p_old = jnp.exp(s_old - m)
        p_new = jnp.exp(s_new - m)
        denom = jnp.sum(p_old, axis=1, keepdims=True) + p_new + jnp.exp(sink - m)
        o = (_dot(p_old.astype(BF16), vc.astype(BF16))
             + p_new.astype(BF16).astype(F32) * vnew.astype(BF16).astype(F32)) / denom
        o = jnp.where(own, o, 0.0)
        orow = [jnp.sum(o[r * N_KV_HEADS:(r + 1) * N_KV_HEADS, :], axis=0, keepdims=True)
                for r in range(GQA_GROUP)]
        o_scr[pl.ds(g, 1), :] = jnp.concatenate(
            [orow[r][:, kh * HEAD_DIM:(kh + 1) * HEAD_DIM]
             for kh in range(N_KV_HEADS) for r in range(GQA_GROUP)], axis=1)
        kout_ref[b, 0:WINDOW - 1, :] = kc[1:WINDOW, :]
        kout_ref[b, WINDOW - 1:WINDOW, :] = knew
        vout_ref[b, 0:WINDOW - 1, :] = vc[1:WINDOW, :]
        vout_ref[b, WINDOW - 1:WINDOW, :] = vnew
        return carry

    lax.fori_loop(0, b_tile, one_sequence, 0)

    @pl.when(step == pl.num_programs(0) - 1)
    def _():
        og = (o_scr[...] * _silu(gate_scr[...])).astype(BF16)
        h2 = h_ref[...] + _dot(og, woutb_ref[...])
        y_ref[...] = _rms(h2, fg_ref[...])


def _layer_b_sample(h, sinks, kv_norm, norm_b, final_norm, w_kv, w_in_b, w_out_b, cache_k, cache_v):
    n_seq = h.shape[0]
    b_tile = SAMPLE_B_TILE
    cos, slo, shi = _rotary_tables(jnp.full((1,), PAST_LEN, F32))

    def const(shape):
        return pl.BlockSpec(shape, lambda *_: (0,) * len(shape))

    cache_spec = pl.BlockSpec((b_tile, WINDOW, KV_DIM), lambda i, *_: (i, 0, 0))
    return pl.pallas_call(
        functools.partial(_layer_b_sample_kernel, n_seq=n_seq, b_tile=b_tile),
        grid_spec=pltpu.PrefetchScalarGridSpec(
            num_scalar_prefetch=1,
            grid=(n_seq // b_tile,),
            in_specs=[const(h.shape), const(kv_norm.shape), const(norm_b.shape),
                      const(final_norm.shape), const(w_kv.shape), const(w_in_b.shape),
                      const(w_out_b.shape), const(cos.shape), const(slo.shape), const(shi.shape),
                      cache_spec, cache_spec],
            out_specs=[const((n_seq, D_MODEL)), cache_spec, cache_spec],
            scratch_shapes=[pltpu.VMEM((n_seq, D_MODEL), F32),
                            pltpu.VMEM((n_seq, D_MODEL), F32),
                            pltpu.VMEM((n_seq, KV_DIM), F32),
                            pltpu.VMEM((n_seq, KV_DIM), F32),
                            pltpu.VMEM((n_seq, D_MODEL), F32)]),
        out_shape=[jax.ShapeDtypeStruct((n_seq, D_MODEL), F32),
                   jax.ShapeDtypeStruct(cache_k.shape, F32),
                   jax.ShapeDtypeStruct(cache_v.shape, F32)],
        compiler_params=pltpu.CompilerParams(
            dimension_semantics=("arbitrary",), vmem_limit_bytes=VMEM_LIMIT_BYTES),
        name="layer_b_sample",
    )(sinks, h, kv_norm, norm_b, final_norm, w_kv, w_in_b, w_out_b, cos, slo, shi, cache_k, cache_v)


def kernel(x_prompt, x_sample, cache_k, cache_v, norm_a, w_in_a, v_norm_a, w_s_a, b_s_a, w_out_a,
           kv_norm, w_kv, norm_b, w_in_b, sinks_b, w_out_b, final_norm):
    batch, seq, _ = x_prompt.shape
    n_seq, dec_seq, _ = x_sample.shape
    assert dec_seq == 1 and seq % CHUNK == 0 and cache_k.shape[1] == WINDOW
    assert norm_a.shape[0] == 1 and norm_b.shape[0] == 1

    row = lambda g: g.reshape(1, -1)
    w_in_a16 = w_in_a[0].astype(BF16)
    w_out_a16 = w_out_a[0].astype(BF16)
    w_kv16 = w_kv.astype(BF16)
    w_in_b16 = w_in_b[0].astype(BF16)
    w_out_b16 = w_out_b[0].astype(BF16)
    bs_chunk = jnp.repeat(b_s_a[0].T, A_GROUP_DIM, axis=1)
    ws_one = jnp.repeat(w_s_a[0, :, 0, 0], A_GROUP_DIM).reshape(1, A_WIDTH)
    bs_one = jnp.repeat(b_s_a[0, :, 0], A_GROUP_DIM).reshape(1, A_WIDTH)

    h_p = _layer_a(x_prompt.reshape(batch * seq, D_MODEL), row(norm_a[0]), w_in_a16,
                   row(v_norm_a[0]), w_s_a[0], bs_chunk, w_out_a16, tile=A_TILE, chunked=True)[0]
    h_s, av_s = _layer_a(x_sample.reshape(n_seq, D_MODEL), row(norm_a[0]), w_in_a16,
                         row(v_norm_a[0]), ws_one, bs_one, w_out_a16, tile=n_seq, chunked=False)

    y_p, kt_p, vt_p = _layer_b_prompt(
        h_p, sinks_b[0], row(kv_norm), row(norm_b[0]), row(final_norm),
        w_kv16.T, w_in_b16[:, :D_MODEL].T, w_in_b16[:, D_MODEL:].T, w_out_b16.T,
        batch=batch, seq=seq)
    k_p = kt_p.reshape(batch, N_KV_HEADS, HEAD_DIM, WINDOW).transpose(0, 3, 1, 2)
    v_p = vt_p.reshape(batch, N_KV_HEADS, HEAD_DIM, WINDOW).transpose(0, 3, 1, 2)
    y_s, k_s, v_s = _layer_b_sample(h_s, sinks_b[0], row(kv_norm), row(norm_b[0]), row(final_norm),
                                    w_kv16, w_in_b16, w_out_b16,
                                    cache_k.reshape(n_seq, WINDOW, KV_DIM),
                                    cache_v.reshape(n_seq, WINDOW, KV_DIM))

    return (y_p.reshape(batch, seq, D_MODEL),
            y_s.reshape(n_seq, 1, D_MODEL),
            k_p,
            v_p,
            k_s.reshape(n_seq, WINDOW, N_KV_HEADS, HEAD_DIM),
            v_s.reshape(n_seq, WINDOW, N_KV_HEADS, HEAD_DIM),
            av_s.reshape(1, n_seq, 1, A_WIDTH))
```

```python
import functools

import jax
import jax.numpy as jnp
from jax import lax
from jax.experimental import pallas as pl
from jax.experimental.pallas import tpu as pltpu

D_MODEL = 1024
PAST_LEN = 8192
CHUNK = 128
A_WIDTH = 2 * D_MODEL
A_GROUPS = 8
A_GROUP_DIM = A_WIDTH // A_GROUPS
HEAD_DIM = 64
N_HEADS = D_MODEL // HEAD_DIM
N_KV_HEADS = 4
GQA_GROUP = N_HEADS // N_KV_HEADS
KV_DIM = N_KV_HEADS * HEAD_DIM
WINDOW = 128
Q_BLOCK = 128
ROT_DIM = HEAD_DIM // 4
ROPE_THETA = 500000.0
EPS = 1e-5

LANES = 128
VMEM_LIMIT_BYTES = 56 * 1024 * 1024

A_TILE = 256
B_TILE = 512
SAMPLE_B_TILE = 16
SAMPLE_GROUP = 4

F32 = jnp.float32
BF16 = jnp.bfloat16


def _rms(x, g):
    return x * lax.rsqrt(jnp.mean(x * x, axis=-1, keepdims=True) + EPS) * g


def _silu(x):
    return x * jax.nn.sigmoid(x)


def _dot(a, b):
    return jnp.dot(a, b, preferred_element_type=F32)


def _dot_nt(a, b):
    return lax.dot_general(a, b, (((1,), (1,)), ((), ())), preferred_element_type=F32)


def _zero_of(x):
    bits = pltpu.bitcast(x, jnp.uint32)
    return ((bits >> 16) >> 16).astype(F32)


def _rotate(x, cos, sin_lo, sin_hi):
    return (x * cos + pltpu.roll(x, LANES - ROT_DIM // 2, 1) * sin_lo
            + pltpu.roll(x, ROT_DIM // 2, 1) * sin_hi)


def _rotary_tables(positions):
    inv = ROPE_THETA ** (-jnp.arange(0, ROT_DIM, 2, dtype=F32) / ROT_DIM)
    ang = positions[:, None] * inv[None, :]
    cos8, sin8 = jnp.cos(ang), jnp.sin(ang)
    lane = jnp.arange(LANES) % HEAD_DIM
    freq = lane % (ROT_DIM // 2)
    first = lane < ROT_DIM // 2
    second = (lane >= ROT_DIM // 2) & (lane < ROT_DIM)
    cos = jnp.where((first | second)[None, :], cos8[:, freq], 1.0)
    sin_lo = jnp.where(first[None, :], -sin8[:, freq], 0.0)
    sin_hi = jnp.where(second[None, :], sin8[:, freq], 0.0)
    return cos, sin_lo, sin_hi


def _layer_a_kernel(x_ref, ng_ref, win_ref, vg_ref, ws_ref, bs_ref, wout_ref, *out_refs,
                    tile, chunked):
    if chunked:
        h_ref, y_scr = out_refs
    else:
        h_ref, av_ref, y_scr = out_refs
    x = x_ref[...]
    xn = _rms(x, ng_ref[...]).astype(BF16)
    u = _dot(xn, win_ref[:, 0:A_WIDTH])
    v = _dot(xn, win_ref[:, A_WIDTH:2 * A_WIDTH])
    gate = _dot(xn, win_ref[:, 2 * A_WIDTH:3 * A_WIDTH])
    v = _rms(v, vg_ref[...])
    if chunked:
        vb = v.astype(BF16)
        row = lax.broadcasted_iota(jnp.int32, (CHUNK, CHUNK), 0)
        col = lax.broadcasted_iota(jnp.int32, (CHUNK, CHUNK), 1)
        ws = [jnp.where(row >= col, ws_ref[g], 0.0).astype(BF16) for g in range(A_GROUPS)]
        for c in range(tile // CHUNK):
            rows = slice(c * CHUNK, (c + 1) * CHUNK)
            z = jnp.concatenate(
                [_dot(ws[g], vb[rows, g * A_GROUP_DIM:(g + 1) * A_GROUP_DIM])
                 for g in range(A_GROUPS)], axis=1) + bs_ref[...]
            y_scr[rows, :] = (u[rows, :] * z * _silu(gate[rows, :])).astype(BF16)
    else:
        av_ref[...] = v
        z = v * ws_ref[...] + bs_ref[...]
        y_scr[...] = (u * z * _silu(gate)).astype(BF16)
    h_ref[...] = x + _dot(y_scr[...], wout_ref[...])


def _const_spec(shape):
    return pl.BlockSpec(shape, lambda *_: (0,) * len(shape), pipeline_mode=pl.Buffered(1))


def _layer_a(x, norm_g, w_in, v_norm_g, ws, bs, w_out, *, tile, chunked):
    n_tok = x.shape[0]
    tok_spec = pl.BlockSpec((tile, D_MODEL), lambda i: (i, 0))
    out_shape = [jax.ShapeDtypeStruct((n_tok, D_MODEL), F32)]
    out_specs = [tok_spec]
    if not chunked:
        out_shape.append(jax.ShapeDtypeStruct((n_tok, A_WIDTH), F32))
        out_specs.append(pl.BlockSpec((tile, A_WIDTH), lambda i: (i, 0)))
    return pl.pallas_call(
        functools.partial(_layer_a_kernel, tile=tile, chunked=chunked),
        grid=(n_tok // tile,),
        in_specs=[tok_spec, _const_spec(norm_g.shape), _const_spec(w_in.shape),
                  _const_spec(v_norm_g.shape), _const_spec(ws.shape), _const_spec(bs.shape),
                  _const_spec(w_out.shape)],
        out_specs=out_specs,
        out_shape=out_shape,
        scratch_shapes=[pltpu.VMEM((tile, A_WIDTH), BF16)],
        compiler_params=pltpu.CompilerParams(
            dimension_semantics=("arbitrary",), vmem_limit_bytes=VMEM_LIMIT_BYTES),
        name="layer_a_prompt" if chunked else "layer_a_sample",
    )(x, norm_g, w_in, v_norm_g, ws, bs, w_out)


def _layer_b_prompt_kernel(sinks_ref, h_ref, kvg_ref, nbg_ref, fg_ref, wkvt_ref, wqt_ref,
                           wgt_ref, woutt_ref, cost_ref, sint_ref,
                           y_ref, kout_ref, vout_ref,
                           kext_scr, vtext_scr, qt_scr, ogt_scr, bias_scr, *, tile, n_t):
    t = pl.program_id(1)
    n_keys = WINDOW + Q_BLOCK

    @pl.when((pl.program_id(0) == 0) & (t == 0))
    def _():
        j = lax.broadcasted_iota(jnp.int32, (n_keys, Q_BLOCK), 0)
        i = lax.broadcasted_iota(jnp.int32, (n_keys, Q_BLOCK), 1)
        band = (j >= i) & (j <= WINDOW + i)
        bias_scr[0] = jnp.where(band & (j >= WINDOW), 0.0, -jnp.inf)
        bias_scr[1] = jnp.where(band, 0.0, -jnp.inf)

    @pl.when(t == 0)
    def _():
        kext_scr[0:WINDOW, :] = jnp.zeros((WINDOW, KV_DIM), BF16)
        vtext_scr[:, 0:WINDOW] = jnp.zeros((KV_DIM, WINDOW), BF16)

    h = h_ref[...]
    hn = h * lax.rsqrt(jnp.mean(h * h, axis=-1, keepdims=True) + EPS)
    xkv = (hn * kvg_ref[...]).astype(BF16)
    xb = (hn * nbg_ref[...]).astype(BF16)

    cost, sint = cost_ref[...], sint_ref[...]
    half = ROT_DIM // 2

    def rotate_head(rows):
        lo, hi = rows[0:half, :], rows[half:ROT_DIM, :]
        return jnp.concatenate(
            [lo * cost - hi * sint, hi * cost + lo * sint, rows[ROT_DIM:, :]], axis=0)

    kvt = _dot_nt(wkvt_ref[...], xkv)
    kt = jnp.concatenate(
        [rotate_head(kvt[kh * HEAD_DIM:(kh + 1) * HEAD_DIM, :]) for kh in range(N_KV_HEADS)],
        axis=0)
    vt = kvt[KV_DIM:, :]
    kext_scr[WINDOW:, :] = kt.T.astype(BF16)
    vtext_scr[:, WINDOW:] = vt.astype(BF16)

    @pl.when(t == n_t - 1)
    def _():
        kout_ref[0] = kt[:, tile - WINDOW:]
        vout_ref[0] = vt[:, tile - WINDOW:]

    qt = _dot_nt(wqt_ref[...], xb)
    for hd in range(N_HEADS):
        rot = rotate_head(qt[hd * HEAD_DIM:(hd + 1) * HEAD_DIM, :])
        qt_scr[hd * HEAD_DIM:(hd + 1) * HEAD_DIM, :] = (rot * HEAD_DIM ** -0.5).astype(BF16)

    lane_head = lax.broadcasted_iota(jnp.int32, (1, GQA_GROUP * Q_BLOCK), 1) // Q_BLOCK
    zeros_half = jnp.zeros((HEAD_DIM, GQA_GROUP * Q_BLOCK), BF16)

    def scores(qb, kh):
        qcols = slice(qb * Q_BLOCK, (qb + 1) * Q_BLOCK)
        keys = slice(qb * Q_BLOCK, qb * Q_BLOCK + n_keys)
        q4 = jnp.concatenate(
            [qt_scr[(kh * GQA_GROUP + r) * HEAD_DIM:(kh * GQA_GROUP + r + 1) * HEAD_DIM, qcols]
             for r in range(GQA_GROUP)], axis=1)
        q4 = jnp.concatenate([q4, zeros_half] if kh % 2 == 0 else [zeros_half, q4], axis=0)
        kblk = kext_scr[keys, (kh // 2) * LANES:(kh // 2 + 1) * LANES]
        return _dot(kblk, q4)

    def finish(qb, kh, s, s_ahead):
        qcols = slice(qb * Q_BLOCK, (qb + 1) * Q_BLOCK)
        keys = slice(qb * Q_BLOCK, qb * Q_BLOCK + n_keys)
        bias = bias_scr[jnp.where(jnp.logical_or(t > 0, qb > 0), 1, 0)]
        s = s + jnp.concatenate([bias] * GQA_GROUP, axis=1)
        sink = jnp.zeros((1, GQA_GROUP * Q_BLOCK), F32)
        for r in range(GQA_GROUP):
            sink = jnp.where(lane_head == r, sinks_ref[kh * GQA_GROUP + r], sink)
        m = jnp.maximum(jnp.max(s, axis=0, keepdims=True), sink)
        p = jnp.exp(s - m)
        denom = jnp.sum(p, axis=0, keepdims=True) + jnp.exp(sink - m)
        if s_ahead is not None:
            p = jnp.concatenate(
                [p[:n_keys - 8, :], p[n_keys - 8:, :] + _zero_of(s_ahead[0:8, :])], axis=0)
        ot = _dot(vtext_scr[kh * HEAD_DIM:(kh + 1) * HEAD_DIM, keys], p.astype(BF16))
        ot = ot * (1.0 / denom)
        for r in range(GQA_GROUP):
            hd = kh * GQA_GROUP + r
            ogt_scr[hd * HEAD_DIM:(hd + 1) * HEAD_DIM, qcols] = ot[:, r * Q_BLOCK:(r + 1) * Q_BLOCK]

    blocks = [(qb, kh) for qb in range(tile // Q_BLOCK) for kh in range(N_KV_HEADS)]
    s_next = scores(*blocks[0])
    for n, blk in enumerate(blocks):
        s_cur, s_next = s_next, (scores(*blocks[n + 1]) if n + 1 < len(blocks) else None)
        finish(*blk, s_cur, s_next)

    kext_scr[0:WINDOW, :] = kext_scr[tile:tile + WINDOW, :]
    vtext_scr[:, 0:WINDOW] = vtext_scr[:, tile:tile + WINDOW]

    gt = _dot_nt(wgt_ref[...], xb)
    ogt = (ogt_scr[...] * _silu(gt)).astype(BF16)
    out_t = _dot(woutt_ref[...], ogt)
    h2 = h + out_t.T
    y_ref[...] = _rms(h2, fg_ref[...])


def _layer_b_prompt(h, sinks, kv_norm, norm_b, final_norm, w_kv_t, w_q_t, w_g_t, w_out_t,
                    *, batch, seq):
    tile = B_TILE
    n_t = seq // tile
    pos = jnp.arange(seq, dtype=F32)
    inv = ROPE_THETA ** (-jnp.arange(0, ROT_DIM, 2, dtype=F32) / ROT_DIM)
    ang_t = inv[:, None] * pos[None, :]
    cos_t, sin_t = jnp.cos(ang_t), jnp.sin(ang_t)
    tok_spec = pl.BlockSpec((tile, D_MODEL), lambda b, t, *_: (b * n_t + t, 0))
    rot_t_spec = pl.BlockSpec((ROT_DIM // 2, tile), lambda b, t, *_: (0, t))
    last_spec = pl.BlockSpec((1, KV_DIM, WINDOW), lambda b, t, *_: (b, 0, 0))

    def const(shape):
        return pl.BlockSpec(shape, lambda *_: (0,) * len(shape), pipeline_mode=pl.Buffered(1))

    return pl.pallas_call(
        functools.partial(_layer_b_prompt_kernel, tile=tile, n_t=n_t),
        grid_spec=pltpu.PrefetchScalarGridSpec(
            num_scalar_prefetch=1,
            grid=(batch, n_t),
            in_specs=[tok_spec, const(kv_norm.shape), const(norm_b.shape), const(final_norm.shape),
                      const(w_kv_t.shape), const(w_q_t.shape), const(w_g_t.shape),
                      const(w_out_t.shape), rot_t_spec, rot_t_spec],
            out_specs=[tok_spec, last_spec, last_spec],
            scratch_shapes=[pltpu.VMEM((WINDOW + tile, KV_DIM), BF16),
                            pltpu.VMEM((KV_DIM, WINDOW + tile), BF16),
                            pltpu.VMEM((D_MODEL, tile), BF16),
                            pltpu.VMEM((D_MODEL, tile), F32),
                            pltpu.VMEM((2, WINDOW + Q_BLOCK, Q_BLOCK), F32)]),
        out_shape=[jax.ShapeDtypeStruct((batch * seq, D_MODEL), F32),
                   jax.ShapeDtypeStruct((batch, KV_DIM, WINDOW), F32),
                   jax.ShapeDtypeStruct((batch, KV_DIM, WINDOW), F32)],
        compiler_params=pltpu.CompilerParams(
            dimension_semantics=("arbitrary", "arbitrary"), vmem_limit_bytes=VMEM_LIMIT_BYTES),
        name="layer_b_prompt",
    )(sinks, h, kv_norm, norm_b, final_norm, w_kv_t, w_q_t, w_g_t, w_out_t, cos_t, sin_t)


def _layer_b_sample_kernel(sinks_ref, h_ref, kvg_ref, nbg_ref, fg_ref, wkv_ref, winb_ref,
                           woutb_ref, cos_ref, slo_ref, shi_ref, ck_ref, cv_ref,
                           y_ref, kout_ref, vout_ref,
                           q_scr, gate_scr, knew_scr, vnew_scr, knewt_scr, vnewt_scr, o_scr,
                           *, n_seq, b_tile):
    step = pl.program_id(0)

    @pl.when(step == 0)
    def _():
        h = h_ref[...]
        hn = h * lax.rsqrt(jnp.mean(h * h, axis=-1, keepdims=True) + EPS)
        kv = _dot((hn * kvg_ref[...]).astype(BF16), wkv_ref[...])
        qg = _dot((hn * nbg_ref[...]).astype(BF16), winb_ref[...])
        cos, slo, shi = cos_ref[...], slo_ref[...], shi_ref[...]
        for c in range(KV_DIM // LANES):
            cols = slice(c * LANES, (c + 1) * LANES)
            knew_scr[:, cols] = _rotate(kv[:, cols], cos, slo, shi)
        vnew_scr[...] = kv[:, KV_DIM:]
        knewt_scr[...] = knew_scr[...].T
        vnewt_scr[...] = kv[:, KV_DIM:].T
        for c in range(D_MODEL // LANES):
            cols = slice(c * LANES, (c + 1) * LANES)
            q_scr[:, cols] = _rotate(qg[:, cols], cos, slo, shi) * HEAD_DIM ** -0.5
        gate_scr[...] = qg[:, D_MODEL:]

    n_rows = GQA_GROUP * N_KV_HEADS
    row_kh = lax.broadcasted_iota(jnp.int32, (n_rows, KV_DIM), 0) % N_KV_HEADS
    lane_kh = lax.broadcasted_iota(jnp.int32, (n_rows, KV_DIM), 1) // HEAD_DIM
    own = row_kh == lane_kh
    row_id = lax.broadcasted_iota(jnp.int32, (n_rows, 1), 0)
    sink = jnp.zeros((n_rows, 1), F32)
    for r in range(GQA_GROUP):
        for kh in range(N_KV_HEADS):
            sink = jnp.where(row_id == r * N_KV_HEADS + kh, sinks_ref[kh * GQA_GROUP + r], sink)

    is_last = lax.broadcasted_iota(jnp.int32, (KV_DIM, WINDOW), 1) == WINDOW - 1

    def scores(b):
        g = step * b_tile + b
        qrow = q_scr[pl.ds(g, 1), :]
        qr = [jnp.concatenate(
            [qrow[:, (kh * GQA_GROUP + r) * HEAD_DIM:(kh * GQA_GROUP + r + 1) * HEAD_DIM]
             for kh in range(N_KV_HEADS)], axis=1) for r in range(GQA_GROUP)]
        qexp = jnp.concatenate(
            [jnp.broadcast_to(qr[r], (N_KV_HEADS, KV_DIM)) for r in range(GQA_GROUP)], axis=0)
        qexp = jnp.where(own, qexp, 0.0)
        kct = ck_ref[b]
        knew = knew_scr[pl.ds(g, 1), :]
        s_old = _dot(qexp.astype(BF16), kct.astype(BF16))
        s_new = jnp.sum(qexp.astype(BF16).astype(F32) * knew.astype(BF16).astype(F32),
                        axis=1, keepdims=True)
        return s_old, s_new

    def finish(b, s_old, s_new, s_ahead):
        g = step * b_tile + b
        kct = ck_ref[b]
        vct = cv_ref[b]
        vnew = vnew_scr[pl.ds(g, 1), :]
        m = jnp.maximum(jnp.maximum(jnp.max(s_old, axis=1, keepdims=True), s_new), sink)
        p_old = jnp.exp(s_old - m)
        p_new = jnp.exp(s_new - m)
        denom = jnp.sum(p_old, axis=1, keepdims=True) + p_new + jnp.exp(sink - m)
        if s_ahead is not None:
            p_old = p_old + _zero_of(s_ahead)
        o = (_dot_nt(p_old.astype(BF16), vct.astype(BF16))
             + p_new.astype(BF16).astype(F32) * vnew.astype(BF16).astype(F32)) / denom
        o = jnp.where(own, o, 0.0)
        orow = [jnp.sum(o[r * N_KV_HEADS:(r + 1) * N_KV_HEADS, :], axis=0, keepdims=True)
                for r in range(GQA_GROUP)]
        o_scr[pl.ds(g, 1), :] = jnp.concatenate(
            [orow[r][:, kh * HEAD_DIM:(kh + 1) * HEAD_DIM]
             for kh in range(N_KV_HEADS) for r in range(GQA_GROUP)], axis=1)
        blk = pl.ds(pl.multiple_of((g // LANES) * LANES, LANES), LANES)
        to_last = LANES - 1 - g % LANES
        kout_ref[b] = jnp.where(is_last, pltpu.roll(knewt_scr[:, blk], to_last, 1),
                                pltpu.roll(kct, WINDOW - 1, 1))
        vout_ref[b] = jnp.where(is_last, pltpu.roll(vnewt_scr[:, blk], to_last, 1),
                                pltpu.roll(vct, WINDOW - 1, 1))

    def one_sequence(b, carry):
        finish(b, *scores(b), None)
        return carry

    lax.fori_loop(0, b_tile, one_sequence, 0, unroll=SAMPLE_GROUP)

    @pl.when(step == pl.num_programs(0) - 1)
    def _():
        og = (o_scr[...] * _silu(gate_scr[...])).astype(BF16)
        h2 = h_ref[...] + _dot(og, woutb_ref[...])
        y_ref[...] = _rms(h2, fg_ref[...])


def _layer_b_sample(h, sinks, kv_norm, norm_b, final_norm, w_kv, w_in_b, w_out_b, cache_k, cache_v):
    n_seq = h.shape[0]
    b_tile = SAMPLE_B_TILE
    cos, slo, shi = _rotary_tables(jnp.full((1,), PAST_LEN, F32))

    def const(shape):
        return pl.BlockSpec(shape, lambda *_: (0,) * len(shape))

    assert n_seq % LANES == 0 and cache_k.shape == (n_seq, KV_DIM, WINDOW)
    cache_spec = pl.BlockSpec((b_tile, KV_DIM, WINDOW), lambda i, *_: (i, 0, 0))
    return pl.pallas_call(
        functools.partial(_layer_b_sample_kernel, n_seq=n_seq, b_tile=b_tile),
        grid_spec=pltpu.PrefetchScalarGridSpec(
            num_scalar_prefetch=1,
            grid=(n_seq // b_tile,),
            in_specs=[const(h.shape), const(kv_norm.shape), const(norm_b.shape),
                      const(final_norm.shape), const(w_kv.shape), const(w_in_b.shape),
                      const(w_out_b.shape), const(cos.shape), const(slo.shape), const(shi.shape),
                      cache_spec, cache_spec],
            out_specs=[const((n_seq, D_MODEL)), cache_spec, cache_spec],
            scratch_shapes=[pltpu.VMEM((n_seq, D_MODEL), F32),
                            pltpu.VMEM((n_seq, D_MODEL), F32),
                            pltpu.VMEM((n_seq, KV_DIM), F32),
                            pltpu.VMEM((n_seq, KV_DIM), F32),
                            pltpu.VMEM((KV_DIM, n_seq), F32),
                            pltpu.VMEM((KV_DIM, n_seq), F32),
                            pltpu.VMEM((n_seq, D_MODEL), F32)]),
        out_shape=[jax.ShapeDtypeStruct((n_seq, D_MODEL), F32),
                   jax.ShapeDtypeStruct(cache_k.shape, F32),
                   jax.ShapeDtypeStruct(cache_v.shape, F32)],
        compiler_params=pltpu.CompilerParams(
            dimension_semantics=("arbitrary",), vmem_limit_bytes=VMEM_LIMIT_BYTES),
        name="layer_b_sample",
    )(sinks, h, kv_norm, norm_b, final_norm, w_kv, w_in_b, w_out_b, cos, slo, shi, cache_k, cache_v)


def kernel(x_prompt, x_sample, cache_k, cache_v, norm_a, w_in_a, v_norm_a, w_s_a, b_s_a, w_out_a,
           kv_norm, w_kv, norm_b, w_in_b, sinks_b, w_out_b, final_norm):
    batch, seq, _ = x_prompt.shape
    n_seq, dec_seq, _ = x_sample.shape
    assert dec_seq == 1 and seq % CHUNK == 0 and cache_k.shape[1] == WINDOW
    assert norm_a.shape[0] == 1 and norm_b.shape[0] == 1

    row = lambda g: g.reshape(1, -1)
    w_in_a16 = w_in_a[0].astype(BF16)
    w_out_a16 = w_out_a[0].astype(BF16)
    w_kv16 = w_kv.astype(BF16)
    w_in_b16 = w_in_b[0].astype(BF16)
    w_out_b16 = w_out_b[0].astype(BF16)
    bs_chunk = jnp.repeat(b_s_a[0].T, A_GROUP_DIM, axis=1)
    ws_one = jnp.repeat(w_s_a[0, :, 0, 0], A_GROUP_DIM).reshape(1, A_WIDTH)
    bs_one = jnp.repeat(b_s_a[0, :, 0], A_GROUP_DIM).reshape(1, A_WIDTH)

    h_p = _layer_a(x_prompt.reshape(batch * seq, D_MODEL), row(norm_a[0]), w_in_a16,
                   row(v_norm_a[0]), w_s_a[0], bs_chunk, w_out_a16, tile=A_TILE, chunked=True)[0]
    h_s, av_s = _layer_a(x_sample.reshape(n_seq, D_MODEL), row(norm_a[0]), w_in_a16,
                         row(v_norm_a[0]), ws_one, bs_one, w_out_a16, tile=n_seq, chunked=False)

    y_p, kt_p, vt_p = _layer_b_prompt(
        h_p, sinks_b[0], row(kv_norm), row(norm_b[0]), row(final_norm),
        w_kv16.T, w_in_b16[:, :D_MODEL].T, w_in_b16[:, D_MODEL:].T, w_out_b16.T,
        batch=batch, seq=seq)
    def to_window(x_t):
        n = x_t.shape[0]
        return x_t.reshape(n, N_KV_HEADS, HEAD_DIM, WINDOW).transpose(0, 3, 1, 2)

    def from_window(x):
        return x.transpose(0, 2, 3, 1).reshape(x.shape[0], KV_DIM, WINDOW)

    y_s, kt_s, vt_s = _layer_b_sample(h_s, sinks_b[0], row(kv_norm), row(norm_b[0]),
                                      row(final_norm), w_kv16, w_in_b16, w_out_b16,
                                      from_window(cache_k), from_window(cache_v))

    return (y_p.reshape(batch, seq, D_MODEL),
            y_s.reshape(n_seq, 1, D_MODEL),
            to_window(kt_p),
            to_window(vt_p),
            to_window(kt_s),
            to_window(vt_s),
            av_s.reshape(1, n_seq, 1, A_WIDTH))
```

```python
import functools

import jax
import jax.numpy as jnp
from jax import lax
from jax.experimental import pallas as pl
from jax.experimental.pallas import tpu as pltpu

D_MODEL = 1024
PAST_LEN = 8192
CHUNK = 128
A_WIDTH = 2 * D_MODEL
A_GROUPS = 8
A_GROUP_DIM = A_WIDTH // A_GROUPS
HEAD_DIM = 64
N_HEADS = D_MODEL // HEAD_DIM
N_KV_HEADS = 4
GQA_GROUP = N_HEADS // N_KV_HEADS
KV_DIM = N_KV_HEADS * HEAD_DIM
WINDOW = 128
Q_BLOCK = 128
ROT_DIM = HEAD_DIM // 4
ROPE_THETA = 500000.0
EPS = 1e-5

LANES = 128
VMEM_LIMIT_BYTES = 56 * 1024 * 1024

A_TILE = 512
B_TILE = 512
SAMPLE_B_TILE = 16
SAMPLE_GROUP = 4

F32 = jnp.float32
BF16 = jnp.bfloat16


def _rms(x, g):
    return x * lax.rsqrt(jnp.mean(x * x, axis=-1, keepdims=True) + EPS) * g


def _silu(x):
    return x * jax.nn.sigmoid(x)


def _dot(a, b):
    return jnp.dot(a, b, preferred_element_type=F32)


def _dot_nt(a, b):
    return lax.dot_general(a, b, (((1,), (1,)), ((), ())), preferred_element_type=F32)


def _zero_of(x):
    bits = pltpu.bitcast(x, jnp.uint32)
    return ((bits >> 16) >> 16).astype(F32)


def _rotate(x, cos, sin_lo, sin_hi):
    return (x * cos + pltpu.roll(x, LANES - ROT_DIM // 2, 1) * sin_lo
            + pltpu.roll(x, ROT_DIM // 2, 1) * sin_hi)


def _rotary_tables(positions):
    inv = ROPE_THETA ** (-jnp.arange(0, ROT_DIM, 2, dtype=F32) / ROT_DIM)
    ang = positions[:, None] * inv[None, :]
    cos8, sin8 = jnp.cos(ang), jnp.sin(ang)
    lane = jnp.arange(LANES) % HEAD_DIM
    freq = lane % (ROT_DIM // 2)
    first = lane < ROT_DIM // 2
    second = (lane >= ROT_DIM // 2) & (lane < ROT_DIM)
    cos = jnp.where((first | second)[None, :], cos8[:, freq], 1.0)
    sin_lo = jnp.where(first[None, :], -sin8[:, freq], 0.0)
    sin_hi = jnp.where(second[None, :], sin8[:, freq], 0.0)
    return cos, sin_lo, sin_hi


def _layer_a_kernel(x_ref, ng_ref, win_ref, vg_ref, ws_ref, bs_ref, wout_ref, *out_refs,
                    tile, chunked):
    if chunked:
        h_ref, y_scr = out_refs
    else:
        h_ref, av_ref, y_scr = out_refs
    x = x_ref[...]
    xn = _rms(x, ng_ref[...]).astype(BF16)
    v = _rms(_dot(xn, win_ref[:, A_WIDTH:2 * A_WIDTH]), vg_ref[...])
    if chunked:
        vb = v.astype(BF16)
        row = lax.broadcasted_iota(jnp.int32, (CHUNK, CHUNK), 0)
        col = lax.broadcasted_iota(jnp.int32, (CHUNK, CHUNK), 1)
        tri = row >= col
    else:
        av_ref[...] = v
    width = 2 * A_GROUP_DIM
    for pair in range(A_GROUPS // 2):
        cols = slice(pair * width, (pair + 1) * width)
        u = _dot(xn, win_ref[:, cols])
        gate = _dot(xn, win_ref[:, 2 * A_WIDTH + pair * width:2 * A_WIDTH + (pair + 1) * width])
        if chunked:
            ws = [jnp.where(tri, ws_ref[g], 0.0).astype(BF16) for g in (2 * pair, 2 * pair + 1)]
            z = jnp.concatenate(
                [jnp.concatenate(
                    [_dot(ws[i], vb[c * CHUNK:(c + 1) * CHUNK,
                                    (2 * pair + i) * A_GROUP_DIM:(2 * pair + i + 1) * A_GROUP_DIM])
                     for i in range(2)], axis=1) + bs_ref[:, cols]
                 for c in range(tile // CHUNK)], axis=0)
        else:
            z = v[:, cols] * ws_ref[:, cols] + bs_ref[:, cols]
        y_scr[:, cols] = (u * z * _silu(gate)).astype(BF16)
    h_ref[...] = x + _dot(y_scr[...], wout_ref[...])


def _const_spec(shape):
    return pl.BlockSpec(shape, lambda *_: (0,) * len(shape), pipeline_mode=pl.Buffered(1))


def _layer_a(x, norm_g, w_in, v_norm_g, ws, bs, w_out, *, tile, chunked):
    n_tok = x.shape[0]
    tok_spec = pl.BlockSpec((tile, D_MODEL), lambda i: (i, 0))
    out_shape = [jax.ShapeDtypeStruct((n_tok, D_MODEL), F32)]
    out_specs = [tok_spec]
    if not chunked:
        out_shape.append(jax.ShapeDtypeStruct((n_tok, A_WIDTH), F32))
        out_specs.append(pl.BlockSpec((tile, A_WIDTH), lambda i: (i, 0)))
    return pl.pallas_call(
        functools.partial(_layer_a_kernel, tile=tile, chunked=chunked),
        grid=(n_tok // tile,),
        in_specs=[tok_spec, _const_spec(norm_g.shape), _const_spec(w_in.shape),
                  _const_spec(v_norm_g.shape), _const_spec(ws.shape), _const_spec(bs.shape),
                  _const_spec(w_out.shape)],
        out_specs=out_specs,
        out_shape=out_shape,
        scratch_shapes=[pltpu.VMEM((tile, A_WIDTH), BF16)],
        compiler_params=pltpu.CompilerParams(
            dimension_semantics=("arbitrary",), vmem_limit_bytes=VMEM_LIMIT_BYTES),
        name="layer_a_prompt" if chunked else "layer_a_sample",
    )(x, norm_g, w_in, v_norm_g, ws, bs, w_out)


def _layer_b_prompt_kernel(sinks_ref, h_ref, kvg_ref, nbg_ref, fg_ref, wkvt_ref, wqt_ref,
                           wgt_ref, woutt_ref, cost_ref, sint_ref,
                           y_ref, kout_ref, vout_ref,
                           kext_scr, vtext_scr, qt_scr, ogt_scr, bias_scr, *, tile, n_t):
    t = pl.program_id(1)
    n_keys = WINDOW + Q_BLOCK

    @pl.when((pl.program_id(0) == 0) & (t == 0))
    def _():
        j = lax.broadcasted_iota(jnp.int32, (n_keys, Q_BLOCK), 0)
        i = lax.broadcasted_iota(jnp.int32, (n_keys, Q_BLOCK), 1)
        band = (j >= i) & (j <= WINDOW + i)
        bias_scr[0] = jnp.where(band & (j >= WINDOW), 0.0, -jnp.inf)
        bias_scr[1] = jnp.where(band, 0.0, -jnp.inf)

    @pl.when(t == 0)
    def _():
        kext_scr[0:WINDOW, :] = jnp.zeros((WINDOW, KV_DIM), BF16)
        vtext_scr[:, 0:WINDOW] = jnp.zeros((KV_DIM, WINDOW), BF16)

    h = h_ref[...]
    hn = h * lax.rsqrt(jnp.mean(h * h, axis=-1, keepdims=True) + EPS)
    xkv = (hn * kvg_ref[...]).astype(BF16)
    xb = (hn * nbg_ref[...]).astype(BF16)

    cost, sint = cost_ref[...], sint_ref[...]
    half = ROT_DIM // 2

    def rotate_head(rows):
        lo, hi = rows[0:half, :], rows[half:ROT_DIM, :]
        return jnp.concatenate(
            [lo * cost - hi * sint, hi * cost + lo * sint, rows[ROT_DIM:, :]], axis=0)

    kvt = _dot_nt(wkvt_ref[...], xkv)
    kt = jnp.concatenate(
        [rotate_head(kvt[kh * HEAD_DIM:(kh + 1) * HEAD_DIM, :]) for kh in range(N_KV_HEADS)],
        axis=0)
    vt = kvt[KV_DIM:, :]
    kext_scr[WINDOW:, :] = kt.T.astype(BF16)
    vtext_scr[:, WINDOW:] = vt.astype(BF16)

    @pl.when(t == n_t - 1)
    def _():
        kout_ref[0] = kt[:, tile - WINDOW:]
        vout_ref[0] = vt[:, tile - WINDOW:]

    qt = _dot_nt(wqt_ref[...], xb)
    for hd in range(N_HEADS):
        rot = rotate_head(qt[hd * HEAD_DIM:(hd + 1) * HEAD_DIM, :])
        qt_scr[hd * HEAD_DIM:(hd + 1) * HEAD_DIM, :] = (rot * HEAD_DIM ** -0.5).astype(BF16)

    lane_head = lax.broadcasted_iota(jnp.int32, (1, GQA_GROUP * Q_BLOCK), 1) // Q_BLOCK
    zeros_half = jnp.zeros((HEAD_DIM, GQA_GROUP * Q_BLOCK), BF16)

    def scores(qb, kh):
        qcols = slice(qb * Q_BLOCK, (qb + 1) * Q_BLOCK)
        keys = slice(qb * Q_BLOCK, qb * Q_BLOCK + n_keys)
        q4 = jnp.concatenate(
            [qt_scr[(kh * GQA_GROUP + r) * HEAD_DIM:(kh * GQA_GROUP + r + 1) * HEAD_DIM, qcols]
             for r in range(GQA_GROUP)], axis=1)
        q4 = jnp.concatenate([q4, zeros_half] if kh % 2 == 0 else [zeros_half, q4], axis=0)
        kblk = kext_scr[keys, (kh // 2) * LANES:(kh // 2 + 1) * LANES]
        return _dot(kblk, q4)

    def finish(qb, kh, s, s_ahead):
        qcols = slice(qb * Q_BLOCK, (qb + 1) * Q_BLOCK)
        keys = slice(qb * Q_BLOCK, qb * Q_BLOCK + n_keys)
        bias = bias_scr[jnp.where(jnp.logical_or(t > 0, qb > 0), 1, 0)]
        s = s + jnp.concatenate([bias] * GQA_GROUP, axis=1)
        sink = jnp.zeros((1, GQA_GROUP * Q_BLOCK), F32)
        for r in range(GQA_GROUP):
            sink = jnp.where(lane_head == r, sinks_ref[kh * GQA_GROUP + r], sink)
        m = jnp.maximum(jnp.max(s, axis=0, keepdims=True), sink)
        p = jnp.exp(s - m)
        denom = jnp.sum(p, axis=0, keepdims=True) + jnp.exp(sink - m)
        if s_ahead is not None:
            p = jnp.concatenate(
                [p[:n_keys - 8, :], p[n_keys - 8:, :] + _zero_of(s_ahead[0:8, :])], axis=0)
        ot = _dot(vtext_scr[kh * HEAD_DIM:(kh + 1) * HEAD_DIM, keys], p.astype(BF16))
        ot = ot * (1.0 / denom)
        for r in range(GQA_GROUP):
            hd = kh * GQA_GROUP + r
            ogt_scr[hd * HEAD_DIM:(hd + 1) * HEAD_DIM, qcols] = ot[:, r * Q_BLOCK:(r + 1) * Q_BLOCK]

    blocks = [(qb, kh) for qb in range(tile // Q_BLOCK) for kh in range(N_KV_HEADS)]
    s_next = scores(*blocks[0])
    for n, blk in enumerate(blocks):
        s_cur, s_next = s_next, (scores(*blocks[n + 1]) if n + 1 < len(blocks) else None)
        finish(*blk, s_cur, s_next)

    kext_scr[0:WINDOW, :] = kext_scr[tile:tile + WINDOW, :]
    vtext_scr[:, 0:WINDOW] = vtext_scr[:, tile:tile + WINDOW]

    gt = _dot_nt(wgt_ref[...], xb)
    ogt = (ogt_scr[...] * _silu(gt)).astype(BF16)
    out_t = _dot(woutt_ref[...], ogt)
    h2 = h + out_t.T
    y_ref[...] = _rms(h2, fg_ref[...])


def _layer_b_prompt(h, sinks, kv_norm, norm_b, final_norm, w_kv_t, w_q_t, w_g_t, w_out_t,
                    *, batch, seq):
    tile = B_TILE
    n_t = seq // tile
    pos = jnp.arange(seq, dtype=F32)
    inv = ROPE_THETA ** (-jnp.arange(0, ROT_DIM, 2, dtype=F32) / ROT_DIM)
    ang_t = inv[:, None] * pos[None, :]
    cos_t, sin_t = jnp.cos(ang_t), jnp.sin(ang_t)
    tok_spec = pl.BlockSpec((tile, D_MODEL), lambda b, t, *_: (b * n_t + t, 0))
    rot_t_spec = pl.BlockSpec((ROT_DIM // 2, tile), lambda b, t, *_: (0, t))
    last_spec = pl.BlockSpec((1, KV_DIM, WINDOW), lambda b, t, *_: (b, 0, 0))

    def const(shape):
        return pl.BlockSpec(shape, lambda *_: (0,) * len(shape), pipeline_mode=pl.Buffered(1))

    return pl.pallas_call(
        functools.partial(_layer_b_prompt_kernel, tile=tile, n_t=n_t),
        grid_spec=pltpu.PrefetchScalarGridSpec(
            num_scalar_prefetch=1,
            grid=(batch, n_t),
            in_specs=[tok_spec, const(kv_norm.shape), const(norm_b.shape), const(final_norm.shape),
                      const(w_kv_t.shape), const(w_q_t.shape), const(w_g_t.shape),
                      const(w_out_t.shape), rot_t_spec, rot_t_spec],
            out_specs=[tok_spec, last_spec, last_spec],
            scratch_shapes=[pltpu.VMEM((WINDOW + tile, KV_DIM), BF16),
                            pltpu.VMEM((KV_DIM, WINDOW + tile), BF16),
                            pltpu.VMEM((D_MODEL, tile), BF16),
                            pltpu.VMEM((D_MODEL, tile), F32),
                            pltpu.VMEM((2, WINDOW + Q_BLOCK, Q_BLOCK), F32)]),
        out_shape=[jax.ShapeDtypeStruct((batch * seq, D_MODEL), F32),
                   jax.ShapeDtypeStruct((batch, KV_DIM, WINDOW), F32),
                   jax.ShapeDtypeStruct((batch, KV_DIM, WINDOW), F32)],
        compiler_params=pltpu.CompilerParams(
            dimension_semantics=("arbitrary", "arbitrary"), vmem_limit_bytes=VMEM_LIMIT_BYTES),
        name="layer_b_prompt",
    )(sinks, h, kv_norm, norm_b, final_norm, w_kv_t, w_q_t, w_g_t, w_out_t, cos_t, sin_t)


def _layer_b_sample_kernel(sinks_ref, h_ref, kvg_ref, nbg_ref, fg_ref, wkv_ref, winb_ref,
                           woutb_ref, cos_ref, slo_ref, shi_ref, ck_ref, cv_ref,
                           y_ref, kout_ref, vout_ref,
                           q_scr, gate_scr, knew_scr, vnew_scr, knewt_scr, vnewt_scr, o_scr,
                           *, n_seq, b_tile):
    step = pl.program_id(0)

    @pl.when(step == 0)
    def _():
        h = h_ref[...]
        hn = h * lax.rsqrt(jnp.mean(h * h, axis=-1, keepdims=True) + EPS)
        kv = _dot((hn * kvg_ref[...]).astype(BF16), wkv_ref[...])
        qg = _dot((hn * nbg_ref[...]).astype(BF16), winb_ref[...])
        cos, slo, shi = cos_ref[...], slo_ref[...], shi_ref[...]
        for c in range(KV_DIM // LANES):
            cols = slice(c * LANES, (c + 1) * LANES)
            knew_scr[:, cols] = _rotate(kv[:, cols], cos, slo, shi)
        vnew_scr[...] = kv[:, KV_DIM:]
        knewt_scr[...] = knew_scr[...].T
        vnewt_scr[...] = kv[:, KV_DIM:].T
        for c in range(D_MODEL // LANES):
            cols = slice(c * LANES, (c + 1) * LANES)
            q_scr[:, cols] = _rotate(qg[:, cols], cos, slo, shi) * HEAD_DIM ** -0.5
        gate_scr[...] = qg[:, D_MODEL:]

    n_rows = GQA_GROUP * N_KV_HEADS
    row_kh = lax.broadcasted_iota(jnp.int32, (n_rows, KV_DIM), 0) % N_KV_HEADS
    lane_kh = lax.broadcasted_iota(jnp.int32, (n_rows, KV_DIM), 1) // HEAD_DIM
    own = row_kh == lane_kh
    row_id = lax.broadcasted_iota(jnp.int32, (n_rows, 1), 0)
    sink = jnp.zeros((n_rows, 1), F32)
    for r in range(GQA_GROUP):
        for kh in range(N_KV_HEADS):
            sink = jnp.where(row_id == r * N_KV_HEADS + kh, sinks_ref[kh * GQA_GROUP + r], sink)

    is_last = lax.broadcasted_iota(jnp.int32, (KV_DIM, WINDOW), 1) == WINDOW - 1

    def scores(b):
        g = step * b_tile + b
        qrow = q_scr[pl.ds(g, 1), :]
        qr = [jnp.concatenate(
            [qrow[:, (kh * GQA_GROUP + r) * HEAD_DIM:(kh * GQA_GROUP + r + 1) * HEAD_DIM]
             for kh in range(N_KV_HEADS)], axis=1) for r in range(GQA_GROUP)]
        qexp = jnp.concatenate(
            [jnp.broadcast_to(qr[r], (N_KV_HEADS, KV_DIM)) for r in range(GQA_GROUP)], axis=0)
        qexp = jnp.where(own, qexp, 0.0)
        kct = ck_ref[b]
        knew = knew_scr[pl.ds(g, 1), :]
        s_old = _dot(qexp.astype(BF16), kct.astype(BF16))
        s_new = jnp.sum(qexp.astype(BF16).astype(F32) * knew.astype(BF16).astype(F32),
                        axis=1, keepdims=True)
        return s_old, s_new

    def finish(b, s_old, s_new, s_ahead):
        g = step * b_tile + b
        kct = ck_ref[b]
        vct = cv_ref[b]
        vnew = vnew_scr[pl.ds(g, 1), :]
        m = jnp.maximum(jnp.maximum(jnp.max(s_old, axis=1, keepdims=True), s_new), sink)
        p_old = jnp.exp(s_old - m)
        p_new = jnp.exp(s_new - m)
        denom = jnp.sum(p_old, axis=1, keepdims=True) + p_new + jnp.exp(sink - m)
        if s_ahead is not None:
            p_old = p_old + _zero_of(s_ahead)
        o = (_dot_nt(p_old.astype(BF16), vct.astype(BF16))
             + p_new.astype(BF16).astype(F32) * vnew.astype(BF16).astype(F32)) / denom
        o = jnp.where(own, o, 0.0)
        orow = [jnp.sum(o[r * N_KV_HEADS:(r + 1) * N_KV_HEADS, :], axis=0, keepdims=True)
                for r in range(GQA_GROUP)]
        o_scr[pl.ds(g, 1), :] = jnp.concatenate(
            [orow[r][:, kh * HEAD_DIM:(kh + 1) * HEAD_DIM]
             for kh in range(N_KV_HEADS) for r in range(GQA_GROUP)], axis=1)
        blk = pl.ds(pl.multiple_of((g // LANES) * LANES, LANES), LANES)
        to_last = LANES - 1 - g % LANES
        kout_ref[b] = jnp.where(is_last, pltpu.roll(knewt_scr[:, blk], to_last, 1),
                                pltpu.roll(kct, WINDOW - 1, 1))
        vout_ref[b] = jnp.where(is_last, pltpu.roll(vnewt_scr[:, blk], to_last, 1),
                                pltpu.roll(vct, WINDOW - 1, 1))

    def one_sequence(b, carry):
        finish(b, *scores(b), None)
        return carry

    lax.fori_loop(0, b_tile, one_sequence, 0, unroll=SAMPLE_GROUP)

    @pl.when(step == pl.num_programs(0) - 1)
    def _():
        og = (o_scr[...] * _silu(gate_scr[...])).astype(BF16)
        h2 = h_ref[...] + _dot(og, woutb_ref[...])
        y_ref[...] = _rms(h2, fg_ref[...])


def _layer_b_sample(h, sinks, kv_norm, norm_b, final_norm, w_kv, w_in_b, w_out_b, cache_k, cache_v):
    n_seq = h.shape[0]
    b_tile = SAMPLE_B_TILE
    cos, slo, shi = _rotary_tables(jnp.full((1,), PAST_LEN, F32))

    def const(shape):
        return pl.BlockSpec(shape, lambda *_: (0,) * len(shape))

    assert n_seq % LANES == 0 and cache_k.shape == (n_seq, KV_DIM, WINDOW)
    cache_spec = pl.BlockSpec((b_tile, KV_DIM, WINDOW), lambda i, *_: (i, 0, 0))
    return pl.pallas_call(
        functools.partial(_layer_b_sample_kernel, n_seq=n_seq, b_tile=b_tile),
        grid_spec=pltpu.PrefetchScalarGridSpec(
            num_scalar_prefetch=1,
            grid=(n_seq // b_tile,),
            in_specs=[const(h.shape), const(kv_norm.shape), const(norm_b.shape),
                      const(final_norm.shape), const(w_kv.shape), const(w_in_b.shape),
                      const(w_out_b.shape), const(cos.shape), const(slo.shape), const(shi.shape),
                      cache_spec, cache_spec],
            out_specs=[const((n_seq, D_MODEL)), cache_spec, cache_spec],
            scratch_shapes=[pltpu.VMEM((n_seq, D_MODEL), F32),
                            pltpu.VMEM((n_seq, D_MODEL), F32),
                            pltpu.VMEM((n_seq, KV_DIM), F32),
                            pltpu.VMEM((n_seq, KV_DIM), F32),
                            pltpu.VMEM((KV_DIM, n_seq), F32),
                            pltpu.VMEM((KV_DIM, n_seq), F32),
                            pltpu.VMEM((n_seq, D_MODEL), F32)]),
        out_shape=[jax.ShapeDtypeStruct((n_seq, D_MODEL), F32),
                   jax.ShapeDtypeStruct(cache_k.shape, F32),
                   jax.ShapeDtypeStruct(cache_v.shape, F32)],
        compiler_params=pltpu.CompilerParams(
            dimension_semantics=("arbitrary",), vmem_limit_bytes=VMEM_LIMIT_BYTES),
        name="layer_b_sample",
    )(sinks, h, kv_norm, norm_b, final_norm, w_kv, w_in_b, w_out_b, cos, slo, shi, cache_k, cache_v)


def kernel(x_prompt, x_sample, cache_k, cache_v, norm_a, w_in_a, v_norm_a, w_s_a, b_s_a, w_out_a,
           kv_norm, w_kv, norm_b, w_in_b, sinks_b, w_out_b, final_norm):
    batch, seq, _ = x_prompt.shape
    n_seq, dec_seq, _ = x_sample.shape
    assert dec_seq == 1 and seq % CHUNK == 0 and cache_k.shape[1] == WINDOW
    assert norm_a.shape[0] == 1 and norm_b.shape[0] == 1

    row = lambda g: g.reshape(1, -1)
    w_in_a16 = w_in_a[0].astype(BF16)
    w_out_a16 = w_out_a[0].astype(BF16)
    w_kv16 = w_kv.astype(BF16)
    w_in_b16 = w_in_b[0].astype(BF16)
    w_out_b16 = w_out_b[0].astype(BF16)
    bs_chunk = jnp.repeat(b_s_a[0].T, A_GROUP_DIM, axis=1)
    ws_one = jnp.repeat(w_s_a[0, :, 0, 0], A_GROUP_DIM).reshape(1, A_WIDTH)
    bs_one = jnp.repeat(b_s_a[0, :, 0], A_GROUP_DIM).reshape(1, A_WIDTH)

    h_p = _layer_a(x_prompt.reshape(batch * seq, D_MODEL), row(norm_a[0]), w_in_a16,
                   row(v_norm_a[0]), w_s_a[0], bs_chunk, w_out_a16, tile=A_TILE, chunked=True)[0]
    h_s, av_s = _layer_a(x_sample.reshape(n_seq, D_MODEL), row(norm_a[0]), w_in_a16,
                         row(v_norm_a[0]), ws_one, bs_one, w_out_a16, tile=n_seq, chunked=False)

    y_p, kt_p, vt_p = _layer_b_prompt(
        h_p, sinks_b[0], row(kv_norm), row(norm_b[0]), row(final_norm),
        w_kv16.T, w_in_b16[:, :D_MODEL].T, w_in_b16[:, D_MODEL:].T, w_out_b16.T,
        batch=batch, seq=seq)
    def to_window(x_t):
        n = x_t.shape[0]
        return x_t.reshape(n, N_KV_HEADS, HEAD_DIM, WINDOW).transpose(0, 3, 1, 2)

    def from_window(x):
        return x.transpose(0, 2, 3, 1).reshape(x.shape[0], KV_DIM, WINDOW)

    y_s, kt_s, vt_s = _layer_b_sample(h_s, sinks_b[0], row(kv_norm), row(norm_b[0]),
                                      row(final_norm), w_kv16, w_in_b16, w_out_b16,
                                      from_window(cache_k), from_window(cache_v))

    return (y_p.reshape(batch, seq, D_MODEL),
            y_s.reshape(n_seq, 1, D_MODEL),
            to_window(kt_p),
            to_window(vt_p),
            to_window(kt_s),
            to_window(vt_s),
            av_s.reshape(1, n_seq, 1, A_WIDTH))
```

```python
import functools

import jax
import jax.numpy as jnp
from jax import lax
from jax.experimental import pallas as pl
from jax.experimental.pallas import tpu as pltpu

D_MODEL = 1024
PAST_LEN = 8192
CHUNK = 128
A_WIDTH = 2 * D_MODEL
A_GROUPS = 8
A_GROUP_DIM = A_WIDTH // A_GROUPS
HEAD_DIM = 64
N_HEADS = D_MODEL // HEAD_DIM
N_KV_HEADS = 4
GQA_GROUP = N_HEADS // N_KV_HEADS
KV_DIM = N_KV_HEADS * HEAD_DIM
WINDOW = 128
Q_BLOCK = 128
ROT_DIM = HEAD_DIM // 4
ROPE_THETA = 500000.0
EPS = 1e-5

LANES = 128
BF16_SUBLANES = 16
LOG2_E = 1.4426950408889634
Q_SCALE_LOG2 = HEAD_DIM ** -0.5 * LOG2_E
VMEM_LIMIT_BYTES = 56 * 1024 * 1024

A_TILE = 512
B_TILE = 512
ATTN_AHEAD = 2
SAMPLE_B_TILE = 16
SAMPLE_GROUP = 4

F32 = jnp.float32
BF16 = jnp.bfloat16


def _rms(x, g):
    return x * lax.rsqrt(jnp.mean(x * x, axis=-1, keepdims=True) + EPS) * g


def _silu(x):
    return x * jax.nn.sigmoid(x)


def _dot(a, b):
    return jnp.dot(a, b, preferred_element_type=F32)


def _dot_nt(a, b):
    return lax.dot_general(a, b, (((1,), (1,)), ((), ())), preferred_element_type=F32)


def _zero_of(x):
    bits = pltpu.bitcast(x, jnp.uint32)
    return ((bits >> 16) >> 16).astype(F32)


def _rotate(x, cos, sin_lo, sin_hi):
    return (x * cos + pltpu.roll(x, LANES - ROT_DIM // 2, 1) * sin_lo
            + pltpu.roll(x, ROT_DIM // 2, 1) * sin_hi)


def _rotary_tables(positions):
    inv = ROPE_THETA ** (-jnp.arange(0, ROT_DIM, 2, dtype=F32) / ROT_DIM)
    ang = positions[:, None] * inv[None, :]
    cos8, sin8 = jnp.cos(ang), jnp.sin(ang)
    lane = jnp.arange(LANES) % HEAD_DIM
    freq = lane % (ROT_DIM // 2)
    first = lane < ROT_DIM // 2
    second = (lane >= ROT_DIM // 2) & (lane < ROT_DIM)
    cos = jnp.where((first | second)[None, :], cos8[:, freq], 1.0)
    sin_lo = jnp.where(first[None, :], -sin8[:, freq], 0.0)
    sin_hi = jnp.where(second[None, :], sin8[:, freq], 0.0)
    return cos, sin_lo, sin_hi


def _layer_a_kernel(x_ref, ng_ref, win_ref, vg_ref, ws_ref, bs_ref, wout_ref, *out_refs,
                    tile, chunked):
    if chunked:
        h_ref, y_scr = out_refs
    else:
        h_ref, av_ref, y_scr = out_refs
    x = x_ref[...]
    xn = _rms(x, ng_ref[...]).astype(BF16)
    v = _rms(_dot(xn, win_ref[:, A_WIDTH:2 * A_WIDTH]), vg_ref[...])
    if chunked:
        vb = v.astype(BF16)
        row = lax.broadcasted_iota(jnp.int32, (CHUNK, CHUNK), 0)
        col = lax.broadcasted_iota(jnp.int32, (CHUNK, CHUNK), 1)
        tri = row >= col
    else:
        av_ref[...] = v
    width = 2 * A_GROUP_DIM
    for pair in range(A_GROUPS // 2):
        cols = slice(pair * width, (pair + 1) * width)
        u = _dot(xn, win_ref[:, cols])
        gate = _dot(xn, win_ref[:, 2 * A_WIDTH + pair * width:2 * A_WIDTH + (pair + 1) * width])
        if chunked:
            ws = [jnp.where(tri, ws_ref[g], 0.0).astype(BF16) for g in (2 * pair, 2 * pair + 1)]
            z = jnp.concatenate(
                [jnp.concatenate(
                    [_dot(ws[i], vb[c * CHUNK:(c + 1) * CHUNK,
                                    (2 * pair + i) * A_GROUP_DIM:(2 * pair + i + 1) * A_GROUP_DIM])
                     for i in range(2)], axis=1) + bs_ref[:, cols]
                 for c in range(tile // CHUNK)], axis=0)
        else:
            z = v[:, cols] * ws_ref[:, cols] + bs_ref[:, cols]
        y_scr[:, cols] = (u * z * _silu(gate)).astype(BF16)
    h_ref[...] = x + _dot(y_scr[...], wout_ref[...])


def _const_spec(shape):
    return pl.BlockSpec(shape, lambda *_: (0,) * len(shape), pipeline_mode=pl.Buffered(1))


def _layer_a(x, norm_g, w_in, v_norm_g, ws, bs, w_out, *, tile, chunked):
    n_tok = x.shape[0]
    tok_spec = pl.BlockSpec((tile, D_MODEL), lambda i: (i, 0))
    out_shape = [jax.ShapeDtypeStruct((n_tok, D_MODEL), F32)]
    out_specs = [tok_spec]
    if not chunked:
        out_shape.append(jax.ShapeDtypeStruct((n_tok, A_WIDTH), F32))
        out_specs.append(pl.BlockSpec((tile, A_WIDTH), lambda i: (i, 0)))
    return pl.pallas_call(
        functools.partial(_layer_a_kernel, tile=tile, chunked=chunked),
        grid=(n_tok // tile,),
        in_specs=[tok_spec, _const_spec(norm_g.shape), _const_spec(w_in.shape),
                  _const_spec(v_norm_g.shape), _const_spec(ws.shape), _const_spec(bs.shape),
                  _const_spec(w_out.shape)],
        out_specs=out_specs,
        out_shape=out_shape,
        scratch_shapes=[pltpu.VMEM((tile, A_WIDTH), BF16)],
        compiler_params=pltpu.CompilerParams(
            dimension_semantics=("arbitrary",), vmem_limit_bytes=VMEM_LIMIT_BYTES),
        name="layer_a_prompt" if chunked else "layer_a_sample",
    )(x, norm_g, w_in, v_norm_g, ws, bs, w_out)


def _layer_b_prompt_kernel(sinks_ref, h_ref, kvg_ref, nbg_ref, fg_ref, wkvt_ref, wqt_ref,
                           wgt_ref, woutt_ref, cost_ref, sint_ref,
                           y_ref, kout_ref, vout_ref,
                           kext_scr, vtext_scr, qt_scr, ogt_scr, bias_scr, *, tile, n_t):
    t = pl.program_id(1)
    n_keys = WINDOW + Q_BLOCK

    @pl.when((pl.program_id(0) == 0) & (t == 0))
    def _():
        j = lax.broadcasted_iota(jnp.int32, (n_keys, Q_BLOCK), 0)
        i = lax.broadcasted_iota(jnp.int32, (n_keys, Q_BLOCK), 1)
        band = (j >= i) & (j <= WINDOW + i)
        bias_scr[0] = jnp.where(band & (j >= WINDOW), 0.0, -jnp.inf)
        bias_scr[1] = jnp.where(band, 0.0, -jnp.inf)

    @pl.when(t == 0)
    def _():
        kext_scr[0:WINDOW, :] = jnp.zeros((WINDOW, KV_DIM), BF16)
        vtext_scr[:, 0:WINDOW] = jnp.zeros((KV_DIM, WINDOW), BF16)

    h = h_ref[...]
    hn = h * lax.rsqrt(jnp.mean(h * h, axis=-1, keepdims=True) + EPS)
    xkv = (hn * kvg_ref[...]).astype(BF16)
    xb = (hn * nbg_ref[...]).astype(BF16)

    cost, sint = cost_ref[...], sint_ref[...]
    half = ROT_DIM // 2

    def rotate_head(rows):
        lo, hi = rows[0:half, :], rows[half:ROT_DIM, :]
        return jnp.concatenate(
            [lo * cost - hi * sint, hi * cost + lo * sint, rows[ROT_DIM:, :]], axis=0)

    kvt = _dot_nt(wkvt_ref[...], xkv)
    kt = jnp.concatenate(
        [rotate_head(kvt[kh * HEAD_DIM:(kh + 1) * HEAD_DIM, :]) for kh in range(N_KV_HEADS)],
        axis=0)
    vt = kvt[KV_DIM:, :]
    kext_scr[WINDOW:, :] = kt.T.astype(BF16)
    vtext_scr[:, WINDOW:] = vt.astype(BF16)

    @pl.when(t == n_t - 1)
    def _():
        kout_ref[0] = kt[:, tile - WINDOW:]
        vout_ref[0] = vt[:, tile - WINDOW:]

    qt = _dot_nt(wqt_ref[...], xb)
    for hd in range(N_HEADS):
        rot = rotate_head(qt[hd * HEAD_DIM:(hd + 1) * HEAD_DIM, :])
        qt_scr[hd * HEAD_DIM:(hd + 1) * HEAD_DIM, :] = (rot * Q_SCALE_LOG2).astype(BF16)

    lane_head = lax.broadcasted_iota(jnp.int32, (1, GQA_GROUP * Q_BLOCK), 1) // Q_BLOCK
    zeros_half = jnp.zeros((HEAD_DIM, GQA_GROUP * Q_BLOCK), BF16)
    ones_rows = jnp.ones((BF16_SUBLANES, n_keys), BF16)

    def scores(qb, kh):
        qcols = slice(qb * Q_BLOCK, (qb + 1) * Q_BLOCK)
        keys = slice(qb * Q_BLOCK, qb * Q_BLOCK + n_keys)
        q4 = jnp.concatenate(
            [qt_scr[(kh * GQA_GROUP + r) * HEAD_DIM:(kh * GQA_GROUP + r + 1) * HEAD_DIM, qcols]
             for r in range(GQA_GROUP)], axis=1)
        q4 = jnp.concatenate([q4, zeros_half] if kh % 2 == 0 else [zeros_half, q4], axis=0)
        kblk = kext_scr[keys, (kh // 2) * LANES:(kh // 2 + 1) * LANES]
        return _dot(kblk, q4)

    def finish(qb, kh, s, s_ahead):
        qcols = slice(qb * Q_BLOCK, (qb + 1) * Q_BLOCK)
        keys = slice(qb * Q_BLOCK, qb * Q_BLOCK + n_keys)
        bias = bias_scr[jnp.where(jnp.logical_or(t > 0, qb > 0), 1, 0)]
        s = s + jnp.concatenate([bias] * GQA_GROUP, axis=1)
        sink = jnp.zeros((1, GQA_GROUP * Q_BLOCK), F32)
        for r in range(GQA_GROUP):
            sink = jnp.where(lane_head == r, sinks_ref[kh * GQA_GROUP + r] * LOG2_E, sink)
        m = jnp.maximum(jnp.max(s, axis=0, keepdims=True), sink)
        p = jnp.exp2(s - m)
        if s_ahead is not None:
            p = jnp.concatenate(
                [p[:n_keys - 8, :], p[n_keys - 8:, :] + _zero_of(s_ahead[0:8, :])], axis=0)
        vt_ones = jnp.concatenate(
            [vtext_scr[kh * HEAD_DIM:(kh + 1) * HEAD_DIM, keys], ones_rows], axis=0)
        ot = _dot(vt_ones, p.astype(BF16))
        denom = ot[HEAD_DIM:HEAD_DIM + 1, :] + jnp.exp2(sink - m)
        ot = ot[0:HEAD_DIM, :] * (1.0 / denom)
        for r in range(GQA_GROUP):
            hd = kh * GQA_GROUP + r
            ogt_scr[hd * HEAD_DIM:(hd + 1) * HEAD_DIM, qcols] = ot[:, r * Q_BLOCK:(r + 1) * Q_BLOCK]

    blocks = [(qb, kh) for qb in range(tile // Q_BLOCK) for kh in range(N_KV_HEADS)]
    pending = [scores(*blk) for blk in blocks[:ATTN_AHEAD]]
    for n, blk in enumerate(blocks):
        if n + ATTN_AHEAD < len(blocks):
            pending.append(scores(*blocks[n + ATTN_AHEAD]))
        s_cur = pending.pop(0)
        finish(*blk, s_cur, pending[-1] if pending else None)

    kext_scr[0:WINDOW, :] = kext_scr[tile:tile + WINDOW, :]
    vtext_scr[:, 0:WINDOW] = vtext_scr[:, tile:tile + WINDOW]

    gt = _dot_nt(wgt_ref[...], xb)
    ogt = (ogt_scr[...] * _silu(gt)).astype(BF16)
    out_t = _dot(woutt_ref[...], ogt)
    h2 = h + out_t.T
    y_ref[...] = _rms(h2, fg_ref[...])


def _layer_b_prompt(h, sinks, kv_norm, norm_b, final_norm, w_kv_t, w_q_t, w_g_t, w_out_t,
                    *, batch, seq):
    tile = B_TILE
    n_t = seq // tile
    pos = jnp.arange(seq, dtype=F32)
    inv = ROPE_THETA ** (-jnp.arange(0, ROT_DIM, 2, dtype=F32) / ROT_DIM)
    ang_t = inv[:, None] * pos[None, :]
    cos_t, sin_t = jnp.cos(ang_t), jnp.sin(ang_t)
    tok_spec = pl.BlockSpec((tile, D_MODEL), lambda b, t, *_: (b * n_t + t, 0))
    rot_t_spec = pl.BlockSpec((ROT_DIM // 2, tile), lambda b, t, *_: (0, t))
    last_spec = pl.BlockSpec((1, KV_DIM, WINDOW), lambda b, t, *_: (b, 0, 0))

    def const(shape):
        return pl.BlockSpec(shape, lambda *_: (0,) * len(shape), pipeline_mode=pl.Buffered(1))

    return pl.pallas_call(
        functools.partial(_layer_b_prompt_kernel, tile=tile, n_t=n_t),
        grid_spec=pltpu.PrefetchScalarGridSpec(
            num_scalar_prefetch=1,
            grid=(batch, n_t),
            in_specs=[tok_spec, const(kv_norm.shape), const(norm_b.shape), const(final_norm.shape),
                      const(w_kv_t.shape), const(w_q_t.shape), const(w_g_t.shape),
                      const(w_out_t.shape), rot_t_spec, rot_t_spec],
            out_specs=[tok_spec, last_spec, last_spec],
            scratch_shapes=[pltpu.VMEM((WINDOW + tile, KV_DIM), BF16),
                            pltpu.VMEM((KV_DIM, WINDOW + tile), BF16),
                            pltpu.VMEM((D_MODEL, tile), BF16),
                            pltpu.VMEM((D_MODEL, tile), F32),
                            pltpu.VMEM((2, WINDOW + Q_BLOCK, Q_BLOCK), F32)]),
        out_shape=[jax.ShapeDtypeStruct((batch * seq, D_MODEL), F32),
                   jax.ShapeDtypeStruct((batch, KV_DIM, WINDOW), F32),
                   jax.ShapeDtypeStruct((batch, KV_DIM, WINDOW), F32)],
        compiler_params=pltpu.CompilerParams(
            dimension_semantics=("arbitrary", "arbitrary"), vmem_limit_bytes=VMEM_LIMIT_BYTES),
        name="layer_b_prompt",
    )(sinks, h, kv_norm, norm_b, final_norm, w_kv_t, w_q_t, w_g_t, w_out_t, cos_t, sin_t)


def _layer_b_sample_kernel(sinks_ref, h_ref, kvg_ref, nbg_ref, fg_ref, wkv_ref, winb_ref,
                           woutb_ref, cos_ref, slo_ref, shi_ref, ck_ref, cv_ref,
                           y_ref, kout_ref, vout_ref,
                           q_scr, gate_scr, knew_scr, vnew_scr, knewt_scr, vnewt_scr, o_scr,
                           *, n_seq, b_tile):
    step = pl.program_id(0)

    @pl.when(step == 0)
    def _():
        h = h_ref[...]
        hn = h * lax.rsqrt(jnp.mean(h * h, axis=-1, keepdims=True) + EPS)
        kv = _dot((hn * kvg_ref[...]).astype(BF16), wkv_ref[...])
        qg = _dot((hn * nbg_ref[...]).astype(BF16), winb_ref[...])
        cos, slo, shi = cos_ref[...], slo_ref[...], shi_ref[...]
        for c in range(KV_DIM // LANES):
            cols = slice(c * LANES, (c + 1) * LANES)
            knew_scr[:, cols] = _rotate(kv[:, cols], cos, slo, shi)
        vnew_scr[...] = kv[:, KV_DIM:]
        knewt_scr[...] = knew_scr[...].T
        vnewt_scr[...] = kv[:, KV_DIM:].T
        for c in range(D_MODEL // LANES):
            cols = slice(c * LANES, (c + 1) * LANES)
            q_scr[:, cols] = _rotate(qg[:, cols], cos, slo, shi) * HEAD_DIM ** -0.5
        gate_scr[...] = qg[:, D_MODEL:]

    n_rows = GQA_GROUP * N_KV_HEADS
    row_kh = lax.broadcasted_iota(jnp.int32, (n_rows, KV_DIM), 0) % N_KV_HEADS
    lane_kh = lax.broadcasted_iota(jnp.int32, (n_rows, KV_DIM), 1) // HEAD_DIM
    own = row_kh == lane_kh
    row_id = lax.broadcasted_iota(jnp.int32, (n_rows, 1), 0)
    sink = jnp.zeros((n_rows, 1), F32)
    for r in range(GQA_GROUP):
        for kh in range(N_KV_HEADS):
            sink = jnp.where(row_id == r * N_KV_HEADS + kh, sinks_ref[kh * GQA_GROUP + r], sink)

    is_last = lax.broadcasted_iota(jnp.int32, (KV_DIM, WINDOW), 1) == WINDOW - 1

    def scores(b):
        g = step * b_tile + b
        qrow = q_scr[pl.ds(g, 1), :]
        qr = [jnp.concatenate(
            [qrow[:, (kh * GQA_GROUP + r) * HEAD_DIM:(kh * GQA_GROUP + r + 1) * HEAD_DIM]
             for kh in range(N_KV_HEADS)], axis=1) for r in range(GQA_GROUP)]
        qexp = jnp.concatenate(
            [jnp.broadcast_to(qr[r], (N_KV_HEADS, KV_DIM)) for r in range(GQA_GROUP)], axis=0)
        qexp = jnp.where(own, qexp, 0.0)
        kct = ck_ref[b]
        knew = knew_scr[pl.ds(g, 1), :]
        s_old = _dot(qexp.astype(BF16), kct.astype(BF16))
        s_new = jnp.sum(qexp.astype(BF16).astype(F32) * knew.astype(BF16).astype(F32),
                        axis=1, keepdims=True)
        return s_old, s_new

    def finish(b, s_old, s_new, s_ahead):
        g = step * b_tile + b
        kct = ck_ref[b]
        vct = cv_ref[b]
        vnew = vnew_scr[pl.ds(g, 1), :]
        m = jnp.maximum(jnp.maximum(jnp.max(s_old, axis=1, keepdims=True), s_new), sink)
        p_old = jnp.exp(s_old - m)
        p_new = jnp.exp(s_new - m)
        denom = jnp.sum(p_old, axis=1, keepdims=True) + p_new + jnp.exp(sink - m)
        if s_ahead is not None:
            p_old = p_old + _zero_of(s_ahead)
        o = (_dot_nt(p_old.astype(BF16), vct.astype(BF16))
             + p_new.astype(BF16).astype(F32) * vnew.astype(BF16).astype(F32)) / denom
        o = jnp.where(own, o, 0.0)
        orow = [jnp.sum(o[r * N_KV_HEADS:(r + 1) * N_KV_HEADS, :], axis=0, keepdims=True)
                for r in range(GQA_GROUP)]
        o_scr[pl.ds(g, 1), :] = jnp.concatenate(
            [orow[r][:, kh * HEAD_DIM:(kh + 1) * HEAD_DIM]
             for kh in range(N_KV_HEADS) for r in range(GQA_GROUP)], axis=1)
        blk = pl.ds(pl.multiple_of((g // LANES) * LANES, LANES), LANES)
        to_last = LANES - 1 - g % LANES
        kout_ref[b] = jnp.where(is_last, pltpu.roll(knewt_scr[:, blk], to_last, 1),
                                pltpu.roll(kct, WINDOW - 1, 1))
        vout_ref[b] = jnp.where(is_last, pltpu.roll(vnewt_scr[:, blk], to_last, 1),
                                pltpu.roll(vct, WINDOW - 1, 1))

    def one_sequence(b, carry):
        finish(b, *scores(b), None)
        return carry

    lax.fori_loop(0, b_tile, one_sequence, 0, unroll=SAMPLE_GROUP)

    @pl.when(step == pl.num_programs(0) - 1)
    def _():
        og = (o_scr[...] * _silu(gate_scr[...])).astype(BF16)
        h2 = h_ref[...] + _dot(og, woutb_ref[...])
        y_ref[...] = _rms(h2, fg_ref[...])


def _layer_b_sample(h, sinks, kv_norm, norm_b, final_norm, w_kv, w_in_b, w_out_b, cache_k, cache_v):
    n_seq = h.shape[0]
    b_tile = SAMPLE_B_TILE
    cos, slo, shi = _rotary_tables(jnp.full((1,), PAST_LEN, F32))

    def const(shape):
        return pl.BlockSpec(shape, lambda *_: (0,) * len(shape))

    assert n_seq % LANES == 0 and cache_k.shape == (n_seq, KV_DIM, WINDOW)
    cache_spec = pl.BlockSpec((b_tile, KV_DIM, WINDOW), lambda i, *_: (i, 0, 0))
    return pl.pallas_call(
        functools.partial(_layer_b_sample_kernel, n_seq=n_seq, b_tile=b_tile),
        grid_spec=pltpu.PrefetchScalarGridSpec(
            num_scalar_prefetch=1,
            grid=(n_seq // b_tile,),
            in_specs=[const(h.shape), const(kv_norm.shape), const(norm_b.shape),
                      const(final_norm.shape), const(w_kv.shape), const(w_in_b.shape),
                      const(w_out_b.shape), const(cos.shape), const(slo.shape), const(shi.shape),
                      cache_spec, cache_spec],
            out_specs=[const((n_seq, D_MODEL)), cache_spec, cache_spec],
            scratch_shapes=[pltpu.VMEM((n_seq, D_MODEL), F32),
                            pltpu.VMEM((n_seq, D_MODEL), F32),
                            pltpu.VMEM((n_seq, KV_DIM), F32),
                            pltpu.VMEM((n_seq, KV_DIM), F32),
                            pltpu.VMEM((KV_DIM, n_seq), F32),
                            pltpu.VMEM((KV_DIM, n_seq), F32),
                            pltpu.VMEM((n_seq, D_MODEL), F32)]),
        out_shape=[jax.ShapeDtypeStruct((n_seq, D_MODEL), F32),
                   jax.ShapeDtypeStruct(cache_k.shape, F32),
                   jax.ShapeDtypeStruct(cache_v.shape, F32)],
        compiler_params=pltpu.CompilerParams(
            dimension_semantics=("arbitrary",), vmem_limit_bytes=VMEM_LIMIT_BYTES),
        name="layer_b_sample",
    )(sinks, h, kv_norm, norm_b, final_norm, w_kv, w_in_b, w_out_b, cos, slo, shi, cache_k, cache_v)


def kernel(x_prompt, x_sample, cache_k, cache_v, norm_a, w_in_a, v_norm_a, w_s_a, b_s_a, w_out_a,
           kv_norm, w_kv, norm_b, w_in_b, sinks_b, w_out_b, final_norm):
    batch, seq, _ = x_prompt.shape
    n_seq, dec_seq, _ = x_sample.shape
    assert dec_seq == 1 and seq % CHUNK == 0 and cache_k.shape[1] == WINDOW
    assert norm_a.shape[0] == 1 and norm_b.shape[0] == 1

    row = lambda g: g.reshape(1, -1)
    w_in_a16 = w_in_a[0].astype(BF16)
    w_out_a16 = w_out_a[0].astype(BF16)
    w_kv16 = w_kv.astype(BF16)
    w_in_b16 = w_in_b[0].astype(BF16)
    w_out_b16 = w_out_b[0].astype(BF16)
    bs_chunk = jnp.repeat(b_s_a[0].T, A_GROUP_DIM, axis=1)
    ws_one = jnp.repeat(w_s_a[0, :, 0, 0], A_GROUP_DIM).reshape(1, A_WIDTH)
    bs_one = jnp.repeat(b_s_a[0, :, 0], A_GROUP_DIM).reshape(1, A_WIDTH)

    h_p = _layer_a(x_prompt.reshape(batch * seq, D_MODEL), row(norm_a[0]), w_in_a16,
                   row(v_norm_a[0]), w_s_a[0], bs_chunk, w_out_a16, tile=A_TILE, chunked=True)[0]
    h_s, av_s = _layer_a(x_sample.reshape(n_seq, D_MODEL), row(norm_a[0]), w_in_a16,
                         row(v_norm_a[0]), ws_one, bs_one, w_out_a16, tile=n_seq, chunked=False)

    y_p, kt_p, vt_p = _layer_b_prompt(
        h_p, sinks_b[0], row(kv_norm), row(norm_b[0]), row(final_norm),
        w_kv16.T, w_in_b16[:, :D_MODEL].T, w_in_b16[:, D_MODEL:].T, w_out_b16.T,
        batch=batch, seq=seq)
    def to_window(x_t):
        n = x_t.shape[0]
        return x_t.reshape(n, N_KV_HEADS, HEAD_DIM, WINDOW).transpose(0, 3, 1, 2)

    def from_window(x):
        return x.transpose(0, 2, 3, 1).reshape(x.shape[0], KV_DIM, WINDOW)

    y_s, kt_s, vt_s = _layer_b_sample(h_s, sinks_b[0], row(kv_norm), row(norm_b[0]),
                                      row(final_norm), w_kv16, w_in_b16, w_out_b16,
                                      from_window(cache_k), from_window(cache_v))

    return (y_p.reshape(batch, seq, D_MODEL),
            y_s.reshape(n_seq, 1, D_MODEL),
            to_window(kt_p),
            to_window(vt_p),
            to_window(kt_s),
            to_window(vt_s),
            av_s.reshape(1, n_seq, 1, A_WIDTH))
```

```python
import functools

import jax
import jax.numpy as jnp
from jax import lax
from jax.experimental import pallas as pl
from jax.experimental.pallas import tpu as pltpu

D_MODEL = 1024
PAST_LEN = 8192
CHUNK = 128
A_WIDTH = 2 * D_MODEL
A_GROUPS = 8
A_GROUP_DIM = A_WIDTH // A_GROUPS
HEAD_DIM = 64
N_HEADS = D_MODEL // HEAD_DIM
N_KV_HEADS = 4
GQA_GROUP = N_HEADS // N_KV_HEADS
KV_DIM = N_KV_HEADS * HEAD_DIM
WINDOW = 128
Q_BLOCK = 128
ROT_DIM = HEAD_DIM // 4
ROPE_THETA = 500000.0
EPS = 1e-5

LANES = 128
BF16_SUBLANES = 16
LOG2_E = 1.4426950408889634
Q_SCALE_LOG2 = HEAD_DIM ** -0.5 * LOG2_E
VMEM_LIMIT_BYTES = 56 * 1024 * 1024

A_TILE = 512
B_TILE = 512
ATTN_AHEAD = 2
SAMPLE_B_TILE = 16
SAMPLE_GROUP = 8

F32 = jnp.float32
BF16 = jnp.bfloat16


def _rms(x, g):
    return x * lax.rsqrt(jnp.mean(x * x, axis=-1, keepdims=True) + EPS) * g


def _silu(x):
    return x * jax.nn.sigmoid(x)


def _dot(a, b):
    return jnp.dot(a, b, preferred_element_type=F32)


def _dot_nt(a, b):
    return lax.dot_general(a, b, (((1,), (1,)), ((), ())), preferred_element_type=F32)


def _zero_of(x):
    bits = pltpu.bitcast(x, jnp.uint32)
    return ((bits >> 16) >> 16).astype(F32)


def _member_major(head):
    kh, r = divmod(head, GQA_GROUP)
    start = (r * N_KV_HEADS + kh) * HEAD_DIM
    return slice(start, start + HEAD_DIM)


def _rotate(x, cos, sin_lo, sin_hi):
    return (x * cos + pltpu.roll(x, LANES - ROT_DIM // 2, 1) * sin_lo
            + pltpu.roll(x, ROT_DIM // 2, 1) * sin_hi)


def _rotary_tables(positions):
    inv = ROPE_THETA ** (-jnp.arange(0, ROT_DIM, 2, dtype=F32) / ROT_DIM)
    ang = positions[:, None] * inv[None, :]
    cos8, sin8 = jnp.cos(ang), jnp.sin(ang)
    lane = jnp.arange(LANES) % HEAD_DIM
    freq = lane % (ROT_DIM // 2)
    first = lane < ROT_DIM // 2
    second = (lane >= ROT_DIM // 2) & (lane < ROT_DIM)
    cos = jnp.where((first | second)[None, :], cos8[:, freq], 1.0)
    sin_lo = jnp.where(first[None, :], -sin8[:, freq], 0.0)
    sin_hi = jnp.where(second[None, :], sin8[:, freq], 0.0)
    return cos, sin_lo, sin_hi


def _layer_a_kernel(x_ref, ng_ref, win_ref, vg_ref, ws_ref, bs_ref, wout_ref, *out_refs,
                    tile, chunked):
    if chunked:
        h_ref, y_scr = out_refs
    else:
        h_ref, av_ref, y_scr = out_refs
    x = x_ref[...]
    xn = _rms(x, ng_ref[...]).astype(BF16)
    v = _rms(_dot(xn, win_ref[:, A_WIDTH:2 * A_WIDTH]), vg_ref[...])
    if chunked:
        vb = v.astype(BF16)
        row = lax.broadcasted_iota(jnp.int32, (CHUNK, CHUNK), 0)
        col = lax.broadcasted_iota(jnp.int32, (CHUNK, CHUNK), 1)
        tri = row >= col
    else:
        av_ref[...] = v
    width = 2 * A_GROUP_DIM
    for pair in range(A_GROUPS // 2):
        cols = slice(pair * width, (pair + 1) * width)
        u = _dot(xn, win_ref[:, cols])
        gate = _dot(xn, win_ref[:, 2 * A_WIDTH + pair * width:2 * A_WIDTH + (pair + 1) * width])
        if chunked:
            ws = [jnp.where(tri, ws_ref[g], 0.0).astype(BF16) for g in (2 * pair, 2 * pair + 1)]
            z = jnp.concatenate(
                [jnp.concatenate(
                    [_dot(ws[i], vb[c * CHUNK:(c + 1) * CHUNK,
                                    (2 * pair + i) * A_GROUP_DIM:(2 * pair + i + 1) * A_GROUP_DIM])
                     for i in range(2)], axis=1) + bs_ref[:, cols]
                 for c in range(tile // CHUNK)], axis=0)
        else:
            z = v[:, cols] * ws_ref[:, cols] + bs_ref[:, cols]
        y_scr[:, cols] = (u * z * _silu(gate)).astype(BF16)
    h_ref[...] = x + _dot(y_scr[...], wout_ref[...])


def _const_spec(shape):
    return pl.BlockSpec(shape, lambda *_: (0,) * len(shape), pipeline_mode=pl.Buffered(1))


def _layer_a(x, norm_g, w_in, v_norm_g, ws, bs, w_out, *, tile, chunked):
    n_tok = x.shape[0]
    tok_spec = pl.BlockSpec((tile, D_MODEL), lambda i: (i, 0))
    out_shape = [jax.ShapeDtypeStruct((n_tok, D_MODEL), F32)]
    out_specs = [tok_spec]
    if not chunked:
        out_shape.append(jax.ShapeDtypeStruct((n_tok, A_WIDTH), F32))
        out_specs.append(pl.BlockSpec((tile, A_WIDTH), lambda i: (i, 0)))
    return pl.pallas_call(
        functools.partial(_layer_a_kernel, tile=tile, chunked=chunked),
        grid=(n_tok // tile,),
        in_specs=[tok_spec, _const_spec(norm_g.shape), _const_spec(w_in.shape),
                  _const_spec(v_norm_g.shape), _const_spec(ws.shape), _const_spec(bs.shape),
                  _const_spec(w_out.shape)],
        out_specs=out_specs,
        out_shape=out_shape,
        scratch_shapes=[pltpu.VMEM((tile, A_WIDTH), BF16)],
        compiler_params=pltpu.CompilerParams(
            dimension_semantics=("arbitrary",), vmem_limit_bytes=VMEM_LIMIT_BYTES),
        name="layer_a_prompt" if chunked else "layer_a_sample",
    )(x, norm_g, w_in, v_norm_g, ws, bs, w_out)


def _layer_b_prompt_kernel(sinks_ref, h_ref, kvg_ref, nbg_ref, fg_ref, wkvt_ref, wqt_ref,
                           wgt_ref, woutt_ref, cost_ref, sint_ref,
                           y_ref, kout_ref, vout_ref,
                           kext_scr, vtext_scr, qt_scr, ogt_scr, bias_scr, *, tile, n_t):
    t = pl.program_id(1)
    n_keys = WINDOW + Q_BLOCK

    @pl.when((pl.program_id(0) == 0) & (t == 0))
    def _():
        j = lax.broadcasted_iota(jnp.int32, (n_keys, Q_BLOCK), 0)
        i = lax.broadcasted_iota(jnp.int32, (n_keys, Q_BLOCK), 1)
        band = (j >= i) & (j <= WINDOW + i)
        bias_scr[0] = jnp.where(band & (j >= WINDOW), 0.0, -jnp.inf)
        bias_scr[1] = jnp.where(band, 0.0, -jnp.inf)

    @pl.when(t == 0)
    def _():
        kext_scr[0:WINDOW, :] = jnp.zeros((WINDOW, KV_DIM), BF16)
        vtext_scr[:, 0:WINDOW] = jnp.zeros((KV_DIM, WINDOW), BF16)

    h = h_ref[...]
    hn = h * lax.rsqrt(jnp.mean(h * h, axis=-1, keepdims=True) + EPS)
    xkv = (hn * kvg_ref[...]).astype(BF16)
    xb = (hn * nbg_ref[...]).astype(BF16)

    cost, sint = cost_ref[...], sint_ref[...]
    half = ROT_DIM // 2

    def rotate_head(rows):
        lo, hi = rows[0:half, :], rows[half:ROT_DIM, :]
        return jnp.concatenate(
            [lo * cost - hi * sint, hi * cost + lo * sint, rows[ROT_DIM:, :]], axis=0)

    kvt = _dot_nt(wkvt_ref[...], xkv)
    kt = jnp.concatenate(
        [rotate_head(kvt[kh * HEAD_DIM:(kh + 1) * HEAD_DIM, :]) for kh in range(N_KV_HEADS)],
        axis=0)
    vt = kvt[KV_DIM:, :]
    kext_scr[WINDOW:, :] = kt.T.astype(BF16)
    vtext_scr[:, WINDOW:] = vt.astype(BF16)

    @pl.when(t == n_t - 1)
    def _():
        kout_ref[0] = kt[:, tile - WINDOW:]
        vout_ref[0] = vt[:, tile - WINDOW:]

    qt = _dot_nt(wqt_ref[...], xb)
    for hd in range(N_HEADS):
        rot = rotate_head(qt[hd * HEAD_DIM:(hd + 1) * HEAD_DIM, :])
        qt_scr[hd * HEAD_DIM:(hd + 1) * HEAD_DIM, :] = (rot * Q_SCALE_LOG2).astype(BF16)

    lane_head = lax.broadcasted_iota(jnp.int32, (1, GQA_GROUP * Q_BLOCK), 1) // Q_BLOCK
    zeros_half = jnp.zeros((HEAD_DIM, GQA_GROUP * Q_BLOCK), BF16)
    ones_rows = jnp.ones((BF16_SUBLANES, n_keys), BF16)

    def scores(qb, kh):
        qcols = slice(qb * Q_BLOCK, (qb + 1) * Q_BLOCK)
        keys = slice(qb * Q_BLOCK, qb * Q_BLOCK + n_keys)
        q4 = jnp.concatenate(
            [qt_scr[(kh * GQA_GROUP + r) * HEAD_DIM:(kh * GQA_GROUP + r + 1) * HEAD_DIM, qcols]
             for r in range(GQA_GROUP)], axis=1)
        q4 = jnp.concatenate([q4, zeros_half] if kh % 2 == 0 else [zeros_half, q4], axis=0)
        kblk = kext_scr[keys, (kh // 2) * LANES:(kh // 2 + 1) * LANES]
        return _dot(kblk, q4)

    def finish(qb, kh, s, s_ahead):
        qcols = slice(qb * Q_BLOCK, (qb + 1) * Q_BLOCK)
        keys = slice(qb * Q_BLOCK, qb * Q_BLOCK + n_keys)
        bias = bias_scr[jnp.where(jnp.logical_or(t > 0, qb > 0), 1, 0)]
        s = s + jnp.concatenate([bias] * GQA_GROUP, axis=1)
        sink = jnp.zeros((1, GQA_GROUP * Q_BLOCK), F32)
        for r in range(GQA_GROUP):
            sink = jnp.where(lane_head == r, sinks_ref[kh * GQA_GROUP + r] * LOG2_E, sink)
        m = jnp.maximum(jnp.max(s, axis=0, keepdims=True), sink)
        p = jnp.exp2(s - m)
        if s_ahead is not None:
            p = jnp.concatenate(
                [p[:n_keys - 8, :], p[n_keys - 8:, :] + _zero_of(s_ahead[0:8, :])], axis=0)
        vt_ones = jnp.concatenate(
            [vtext_scr[kh * HEAD_DIM:(kh + 1) * HEAD_DIM, keys], ones_rows], axis=0)
        ot = _dot(vt_ones, p.astype(BF16))
        denom = ot[HEAD_DIM:HEAD_DIM + 1, :] + jnp.exp2(sink - m)
        ot = ot[0:HEAD_DIM, :] * (1.0 / denom)
        for r in range(GQA_GROUP):
            hd = kh * GQA_GROUP + r
            ogt_scr[hd * HEAD_DIM:(hd + 1) * HEAD_DIM, qcols] = ot[:, r * Q_BLOCK:(r + 1) * Q_BLOCK]

    blocks = [(qb, kh) for qb in range(tile // Q_BLOCK) for kh in range(N_KV_HEADS)]
    pending = [scores(*blk) for blk in blocks[:ATTN_AHEAD]]
    for n, blk in enumerate(blocks):
        if n + ATTN_AHEAD < len(blocks):
            pending.append(scores(*blocks[n + ATTN_AHEAD]))
        s_cur = pending.pop(0)
        finish(*blk, s_cur, pending[-1] if pending else None)

    kext_scr[0:WINDOW, :] = kext_scr[tile:tile + WINDOW, :]
    vtext_scr[:, 0:WINDOW] = vtext_scr[:, tile:tile + WINDOW]

    gt = _dot_nt(wgt_ref[...], xb)
    ogt = (ogt_scr[...] * _silu(gt)).astype(BF16)
    out_t = _dot(woutt_ref[...], ogt)
    h2 = h + out_t.T
    y_ref[...] = _rms(h2, fg_ref[...])


def _layer_b_prompt(h, sinks, kv_norm, norm_b, final_norm, w_kv_t, w_q_t, w_g_t, w_out_t,
                    *, batch, seq):
    tile = B_TILE
    n_t = seq // tile
    pos = jnp.arange(seq, dtype=F32)
    inv = ROPE_THETA ** (-jnp.arange(0, ROT_DIM, 2, dtype=F32) / ROT_DIM)
    ang_t = inv[:, None] * pos[None, :]
    cos_t, sin_t = jnp.cos(ang_t), jnp.sin(ang_t)
    tok_spec = pl.BlockSpec((tile, D_MODEL), lambda b, t, *_: (b * n_t + t, 0))
    rot_t_spec = pl.BlockSpec((ROT_DIM // 2, tile), lambda b, t, *_: (0, t))
    last_spec = pl.BlockSpec((1, KV_DIM, WINDOW), lambda b, t, *_: (b, 0, 0))

    def const(shape):
        return pl.BlockSpec(shape, lambda *_: (0,) * len(shape), pipeline_mode=pl.Buffered(1))

    return pl.pallas_call(
        functools.partial(_layer_b_prompt_kernel, tile=tile, n_t=n_t),
        grid_spec=pltpu.PrefetchScalarGridSpec(
            num_scalar_prefetch=1,
            grid=(batch, n_t),
            in_specs=[tok_spec, const(kv_norm.shape), const(norm_b.shape), const(final_norm.shape),
                      const(w_kv_t.shape), const(w_q_t.shape), const(w_g_t.shape),
                      const(w_out_t.shape), rot_t_spec, rot_t_spec],
            out_specs=[tok_spec, last_spec, last_spec],
            scratch_shapes=[pltpu.VMEM((WINDOW + tile, KV_DIM), BF16),
                            pltpu.VMEM((KV_DIM, WINDOW + tile), BF16),
                            pltpu.VMEM((D_MODEL, tile), BF16),
                            pltpu.VMEM((D_MODEL, tile), F32),
                            pltpu.VMEM((2, WINDOW + Q_BLOCK, Q_BLOCK), F32)]),
        out_shape=[jax.ShapeDtypeStruct((batch * seq, D_MODEL), F32),
                   jax.ShapeDtypeStruct((batch, KV_DIM, WINDOW), F32),
                   jax.ShapeDtypeStruct((batch, KV_DIM, WINDOW), F32)],
        compiler_params=pltpu.CompilerParams(
            dimension_semantics=("arbitrary", "arbitrary"), vmem_limit_bytes=VMEM_LIMIT_BYTES),
        name="layer_b_prompt",
    )(sinks, h, kv_norm, norm_b, final_norm, w_kv_t, w_q_t, w_g_t, w_out_t, cos_t, sin_t)


def _layer_b_sample_kernel(sinks_ref, h_ref, kvg_ref, nbg_ref, fg_ref, wkv_ref, winb_ref,
                           woutb_ref, cos_ref, slo_ref, shi_ref, ck_ref, cv_ref,
                           y_ref, kout_ref, vout_ref,
                           q_scr, gate_scr, knew_scr, vnew_scr, knewt_scr, vnewt_scr, o_scr,
                           *, n_seq, b_tile):
    step = pl.program_id(0)

    @pl.when(step == 0)
    def _():
        h = h_ref[...]
        hn = h * lax.rsqrt(jnp.mean(h * h, axis=-1, keepdims=True) + EPS)
        kv = _dot((hn * kvg_ref[...]).astype(BF16), wkv_ref[...])
        qg = _dot((hn * nbg_ref[...]).astype(BF16), winb_ref[...])
        cos, slo, shi = cos_ref[...], slo_ref[...], shi_ref[...]
        for c in range(KV_DIM // LANES):
            cols = slice(c * LANES, (c + 1) * LANES)
            knew_scr[:, cols] = _rotate(kv[:, cols], cos, slo, shi)
        vnew_scr[...] = kv[:, KV_DIM:]
        knewt_scr[...] = knew_scr[...].T
        vnewt_scr[...] = kv[:, KV_DIM:].T
        for c in range(D_MODEL // LANES):
            cols = slice(c * LANES, (c + 1) * LANES)
            q2 = _rotate(qg[:, cols], cos, slo, shi) * HEAD_DIM ** -0.5
            g2 = qg[:, D_MODEL + c * LANES:D_MODEL + (c + 1) * LANES]
            for i in range(LANES // HEAD_DIM):
                dst = _member_major(c * (LANES // HEAD_DIM) + i)
                q_scr[:, dst] = q2[:, i * HEAD_DIM:(i + 1) * HEAD_DIM]
                gate_scr[:, dst] = g2[:, i * HEAD_DIM:(i + 1) * HEAD_DIM]

    n_rows = GQA_GROUP * N_KV_HEADS * SAMPLE_GROUP
    row = lax.broadcasted_iota(jnp.int32, (n_rows, 1), 0)
    row_kh = (row // SAMPLE_GROUP) % N_KV_HEADS
    row_seq = row % SAMPLE_GROUP
    lane_kh = lax.broadcasted_iota(jnp.int32, (1, KV_DIM), 1) // HEAD_DIM
    own = row_kh == lane_kh
    sink = jnp.zeros((n_rows, 1), F32)
    for r in range(GQA_GROUP):
        for kh in range(N_KV_HEADS):
            sink = jnp.where(row // SAMPLE_GROUP == r * N_KV_HEADS + kh,
                             sinks_ref[kh * GQA_GROUP + r], sink)
    is_last = lax.broadcasted_iota(jnp.int32, (KV_DIM, WINDOW), 1) == WINDOW - 1
    n_blk = GQA_GROUP * N_KV_HEADS

    def group(i, carry):
        b0 = i * SAMPLE_GROUP
        g0 = pl.multiple_of(step * b_tile + b0, SAMPLE_GROUP)
        seqs = pl.ds(g0, SAMPLE_GROUP)
        q8 = q_scr[seqs, :]
        qexp = jnp.concatenate(
            [q8[:, r * KV_DIM:(r + 1) * KV_DIM] for r in range(GQA_GROUP)
             for _ in range(N_KV_HEADS)], axis=0)
        qexp = jnp.where(own, qexp, 0.0).astype(BF16)
        knew8 = knew_scr[seqs, :].astype(BF16).astype(F32)
        vnew8 = vnew_scr[seqs, :].astype(BF16).astype(F32)
        s_new = jnp.sum(qexp.astype(F32) * jnp.concatenate([knew8] * n_blk, axis=0),
                        axis=1, keepdims=True)
        s_old = jnp.zeros((n_rows, WINDOW), F32)
        for b in range(SAMPLE_GROUP):
            s_b = _dot(qexp, ck_ref[b0 + b].astype(BF16))
            s_old = jnp.where(row_seq == b, s_b, s_old)
        m = jnp.maximum(jnp.maximum(jnp.max(s_old, axis=1, keepdims=True), s_new), sink)
        p_old = jnp.exp(s_old - m)
        p_new = jnp.exp(s_new - m)
        denom = jnp.sum(p_old, axis=1, keepdims=True) + p_new + jnp.exp(sink - m)
        p_old = p_old.astype(BF16)
        o = jnp.zeros((n_rows, KV_DIM), F32)
        for b in range(SAMPLE_GROUP):
            o_b = _dot_nt(p_old, cv_ref[b0 + b].astype(BF16))
            o = jnp.where(row_seq == b, o_b, o)
        o = (o + p_new.astype(BF16).astype(F32) * jnp.concatenate([vnew8] * n_blk, axis=0)) / denom
        o = jnp.where(own, o, 0.0)
        for r in range(GQA_GROUP):
            blks = [o[(r * N_KV_HEADS + kh) * SAMPLE_GROUP:(r * N_KV_HEADS + kh + 1) * SAMPLE_GROUP]
                    for kh in range(N_KV_HEADS)]
            o_scr[seqs, r * KV_DIM:(r + 1) * KV_DIM] = (blks[0] + blks[1]) + (blks[2] + blks[3])
        for b in range(SAMPLE_GROUP):
            g = g0 + b
            blk = pl.ds(pl.multiple_of((g // LANES) * LANES, LANES), LANES)
            to_last = LANES - 1 - g % LANES
            kout_ref[b0 + b] = jnp.where(is_last, pltpu.roll(knewt_scr[:, blk], to_last, 1),
                                         pltpu.roll(ck_ref[b0 + b], WINDOW - 1, 1))
            vout_ref[b0 + b] = jnp.where(is_last, pltpu.roll(vnewt_scr[:, blk], to_last, 1),
                                         pltpu.roll(cv_ref[b0 + b], WINDOW - 1, 1))
        return carry

    lax.fori_loop(0, b_tile // SAMPLE_GROUP, group, 0)

    @pl.when(step == pl.num_programs(0) - 1)
    def _():
        og_mm = o_scr[...] * _silu(gate_scr[...])
        og = jnp.concatenate(
            [og_mm[:, _member_major(hd)] for hd in range(N_HEADS)], axis=1).astype(BF16)
        h2 = h_ref[...] + _dot(og, woutb_ref[...])
        y_ref[...] = _rms(h2, fg_ref[...])


def _layer_b_sample(h, sinks, kv_norm, norm_b, final_norm, w_kv, w_in_b, w_out_b, cache_k, cache_v):
    n_seq = h.shape[0]
    b_tile = SAMPLE_B_TILE
    cos, slo, shi = _rotary_tables(jnp.full((1,), PAST_LEN, F32))

    def const(shape):
        return pl.BlockSpec(shape, lambda *_: (0,) * len(shape))

    assert n_seq % LANES == 0 and cache_k.shape == (n_seq, KV_DIM, WINDOW)
    cache_spec = pl.BlockSpec((b_tile, KV_DIM, WINDOW), lambda i, *_: (i, 0, 0))
    return pl.pallas_call(
        functools.partial(_layer_b_sample_kernel, n_seq=n_seq, b_tile=b_tile),
        grid_spec=pltpu.PrefetchScalarGridSpec(
            num_scalar_prefetch=1,
            grid=(n_seq // b_tile,),
            in_specs=[const(h.shape), const(kv_norm.shape), const(norm_b.shape),
                      const(final_norm.shape), const(w_kv.shape), const(w_in_b.shape),
                      const(w_out_b.shape), const(cos.shape), const(slo.shape), const(shi.shape),
                      cache_spec, cache_spec],
            out_specs=[const((n_seq, D_MODEL)), cache_spec, cache_spec],
            scratch_shapes=[pltpu.VMEM((n_seq, D_MODEL), F32),
                            pltpu.VMEM((n_seq, D_MODEL), F32),
                            pltpu.VMEM((n_seq, KV_DIM), F32),
                            pltpu.VMEM((n_seq, KV_DIM), F32),
                            pltpu.VMEM((KV_DIM, n_seq), F32),
                            pltpu.VMEM((KV_DIM, n_seq), F32),
                            pltpu.VMEM((n_seq, D_MODEL), F32)]),
        out_shape=[jax.ShapeDtypeStruct((n_seq, D_MODEL), F32),
                   jax.ShapeDtypeStruct(cache_k.shape, F32),
                   jax.ShapeDtypeStruct(cache_v.shape, F32)],
        compiler_params=pltpu.CompilerParams(
            dimension_semantics=("arbitrary",), vmem_limit_bytes=VMEM_LIMIT_BYTES),
        name="layer_b_sample",
    )(sinks, h, kv_norm, norm_b, final_norm, w_kv, w_in_b, w_out_b, cos, slo, shi, cache_k, cache_v)


def kernel(x_prompt, x_sample, cache_k, cache_v, norm_a, w_in_a, v_norm_a, w_s_a, b_s_a, w_out_a,
           kv_norm, w_kv, norm_b, w_in_b, sinks_b, w_out_b, final_norm):
    batch, seq, _ = x_prompt.shape
    n_seq, dec_seq, _ = x_sample.shape
    assert dec_seq == 1 and seq % CHUNK == 0 and cache_k.shape[1] == WINDOW
    assert norm_a.shape[0] == 1 and norm_b.shape[0] == 1

    row = lambda g: g.reshape(1, -1)
    w_in_a16 = w_in_a[0].astype(BF16)
    w_out_a16 = w_out_a[0].astype(BF16)
    w_kv16 = w_kv.astype(BF16)
    w_in_b16 = w_in_b[0].astype(BF16)
    w_out_b16 = w_out_b[0].astype(BF16)
    bs_chunk = jnp.repeat(b_s_a[0].T, A_GROUP_DIM, axis=1)
    ws_one = jnp.repeat(w_s_a[0, :, 0, 0], A_GROUP_DIM).reshape(1, A_WIDTH)
    bs_one = jnp.repeat(b_s_a[0, :, 0], A_GROUP_DIM).reshape(1, A_WIDTH)

    h_p = _layer_a(x_prompt.reshape(batch * seq, D_MODEL), row(norm_a[0]), w_in_a16,
                   row(v_norm_a[0]), w_s_a[0], bs_chunk, w_out_a16, tile=A_TILE, chunked=True)[0]
    h_s, av_s = _layer_a(x_sample.reshape(n_seq, D_MODEL), row(norm_a[0]), w_in_a16,
                         row(v_norm_a[0]), ws_one, bs_one, w_out_a16, tile=n_seq, chunked=False)

    y_p, kt_p, vt_p = _layer_b_prompt(
        h_p, sinks_b[0], row(kv_norm), row(norm_b[0]), row(final_norm),
        w_kv16.T, w_in_b16[:, :D_MODEL].T, w_in_b16[:, D_MODEL:].T, w_out_b16.T,
        batch=batch, seq=seq)
    def to_window(x_t):
        n = x_t.shape[0]
        return x_t.reshape(n, N_KV_HEADS, HEAD_DIM, WINDOW).transpose(0, 3, 1, 2)

    def from_window(x):
        return x.transpose(0, 2, 3, 1).reshape(x.shape[0], KV_DIM, WINDOW)

    y_s, kt_s, vt_s = _layer_b_sample(h_s, sinks_b[0], row(kv_norm), row(norm_b[0]),
                                      row(final_norm), w_kv16, w_in_b16, w_out_b16,
                                      from_window(cache_k), from_window(cache_v))

    return (y_p.reshape(batch, seq, D_MODEL),
            y_s.reshape(n_seq, 1, D_MODEL),
            to_window(kt_p),
            to_window(vt_p),
            to_window(kt_s),
            to_window(vt_s),
            av_s.reshape(1, n_seq, 1, A_WIDTH))
```

```python
import functools

import jax
import jax.numpy as jnp
from jax import lax
from jax.experimental import pallas as pl
from jax.experimental.pallas import tpu as pltpu

D_MODEL = 1024
PAST_LEN = 8192
CHUNK = 128
A_WIDTH = 2 * D_MODEL
A_GROUPS = 8
A_GROUP_DIM = A_WIDTH // A_GROUPS
HEAD_DIM = 64
N_HEADS = D_MODEL // HEAD_DIM
N_KV_HEADS = 4
GQA_GROUP = N_HEADS // N_KV_HEADS
KV_DIM = N_KV_HEADS * HEAD_DIM
WINDOW = 128
Q_BLOCK = 128
ROT_DIM = HEAD_DIM // 4
ROPE_THETA = 500000.0
EPS = 1e-5

LANES = 128
BF16_SUBLANES = 16
LOG2_E = 1.4426950408889634
Q_SCALE_LOG2 = HEAD_DIM ** -0.5 * LOG2_E
VMEM_LIMIT_BYTES = 56 * 1024 * 1024

A_TILE = 512
B_TILE = 512
PREP_COLS = 512
ATTN_AHEAD = 2
SAMPLE_B_TILE = 16
SAMPLE_GROUP = 8

F32 = jnp.float32
BF16 = jnp.bfloat16


def _rms(x, g):
    return x * lax.rsqrt(jnp.mean(x * x, axis=-1, keepdims=True) + EPS) * g


def _silu(x):
    return x * jax.nn.sigmoid(x)


def _dot(a, b):
    return jnp.dot(a, b, preferred_element_type=F32)


def _dot_nt(a, b):
    return lax.dot_general(a, b, (((1,), (1,)), ((), ())), preferred_element_type=F32)


def _zero_of(x):
    bits = pltpu.bitcast(x, jnp.uint32)
    return ((bits >> 16) >> 16).astype(F32)


def _member_major(head):
    kh, r = divmod(head, GQA_GROUP)
    start = (r * N_KV_HEADS + kh) * HEAD_DIM
    return slice(start, start + HEAD_DIM)


def _rotate(x, cos, sin_lo, sin_hi):
    return (x * cos + pltpu.roll(x, LANES - ROT_DIM // 2, 1) * sin_lo
            + pltpu.roll(x, ROT_DIM // 2, 1) * sin_hi)


def _rotary_tables(positions):
    lane = jnp.arange(LANES) % HEAD_DIM
    freq = (2 * (lane % (ROT_DIM // 2))).astype(F32)
    ang = positions[:, None] * (ROPE_THETA ** (-freq / ROT_DIM))[None, :]
    first = (lane < ROT_DIM // 2)[None, :]
    second = ((lane >= ROT_DIM // 2) & (lane < ROT_DIM))[None, :]
    cos = jnp.where(first | second, jnp.cos(ang), 1.0)
    sin_lo = jnp.where(first, -jnp.sin(ang), 0.0)
    sin_hi = jnp.where(second, jnp.sin(ang), 0.0)
    return cos, sin_lo, sin_hi


def _transpose_cast_kernel(w_ref, o_ref):
    o_ref[...] = w_ref[...].T.astype(BF16)


def _transpose_cast(w):
    k, n = w.shape
    return pl.pallas_call(
        _transpose_cast_kernel,
        grid=(n // PREP_COLS,),
        in_specs=[pl.BlockSpec((k, PREP_COLS), lambda j: (0, j))],
        out_specs=pl.BlockSpec((PREP_COLS, k), lambda j: (j, 0)),
        out_shape=jax.ShapeDtypeStruct((n, k), BF16),
        compiler_params=pltpu.CompilerParams(dimension_semantics=("arbitrary",)),
        name="transpose_cast",
    )(w)


def _layer_a_kernel(x_ref, ng_ref, win_ref, vg_ref, ws_ref, bs_ref, wout_ref, *out_refs,
                    tile, chunked):
    if chunked:
        h_ref, y_scr = out_refs
    else:
        h_ref, av_ref, y_scr = out_refs
    x = x_ref[...]
    xn = _rms(x, ng_ref[...]).astype(BF16)
    v = _rms(_dot(xn, win_ref[:, A_WIDTH:2 * A_WIDTH]), vg_ref[...])
    if chunked:
        vb = v.astype(BF16)
        row = lax.broadcasted_iota(jnp.int32, (CHUNK, CHUNK), 0)
        col = lax.broadcasted_iota(jnp.int32, (CHUNK, CHUNK), 1)
        tri = row >= col
    else:
        av_ref[...] = v
    width = 2 * A_GROUP_DIM
    for pair in range(A_GROUPS // 2):
        cols = slice(pair * width, (pair + 1) * width)
        u = _dot(xn, win_ref[:, cols])
        gate = _dot(xn, win_ref[:, 2 * A_WIDTH + pair * width:2 * A_WIDTH + (pair + 1) * width])
        if chunked:
            ws = [jnp.where(tri, ws_ref[g], 0.0).astype(BF16) for g in (2 * pair, 2 * pair + 1)]
            z = jnp.concatenate(
                [jnp.concatenate(
                    [_dot(ws[i], vb[c * CHUNK:(c + 1) * CHUNK,
                                    (2 * pair + i) * A_GROUP_DIM:(2 * pair + i + 1) * A_GROUP_DIM])
                     for i in range(2)], axis=1) + bs_ref[:, cols]
                 for c in range(tile // CHUNK)], axis=0)
        else:
            z = v[:, cols] * ws_ref[:, cols] + bs_ref[:, cols]
        y_scr[:, cols] = (u * z * _silu(gate)).astype(BF16)
    h_ref[...] = x + _dot(y_scr[...], wout_ref[...])


def _const_spec(shape):
    return pl.BlockSpec(shape, lambda *_: (0,) * len(shape), pipeline_mode=pl.Buffered(1))


def _layer_a(x, norm_g, w_in, v_norm_g, ws, bs, w_out, *, tile, chunked):
    n_tok = x.shape[0]
    tok_spec = pl.BlockSpec((tile, D_MODEL), lambda i: (i, 0))
    out_shape = [jax.ShapeDtypeStruct((n_tok, D_MODEL), F32)]
    out_specs = [tok_spec]
    if not chunked:
        out_shape.append(jax.ShapeDtypeStruct((n_tok, A_WIDTH), F32))
        out_specs.append(pl.BlockSpec((tile, A_WIDTH), lambda i: (i, 0)))
    return pl.pallas_call(
        functools.partial(_layer_a_kernel, tile=tile, chunked=chunked),
        grid=(n_tok // tile,),
        in_specs=[tok_spec, _const_spec(norm_g.shape), _const_spec(w_in.shape),
                  _const_spec(v_norm_g.shape), _const_spec(ws.shape), _const_spec(bs.shape),
                  _const_spec(w_out.shape)],
        out_specs=out_specs,
        out_shape=out_shape,
        scratch_shapes=[pltpu.VMEM((tile, A_WIDTH), BF16)],
        compiler_params=pltpu.CompilerParams(
            dimension_semantics=("arbitrary",), vmem_limit_bytes=VMEM_LIMIT_BYTES),
        name="layer_a_prompt" if chunked else "layer_a_sample",
    )(x, norm_g, w_in, v_norm_g, ws, bs, w_out)


def _layer_b_prompt_kernel(sinks_ref, h_ref, kvg_ref, nbg_ref, fg_ref, wkvt_ref, wqt_ref,
                           wgt_ref, woutt_ref, cost_ref, sint_ref,
                           y_ref, kout_ref, vout_ref,
                           kext_scr, vtext_scr, qt_scr, ogt_scr, bias_scr, *, tile, n_t):
    t = pl.program_id(1)
    n_keys = WINDOW + Q_BLOCK

    @pl.when((pl.program_id(0) == 0) & (t == 0))
    def _():
        j = lax.broadcasted_iota(jnp.int32, (n_keys, Q_BLOCK), 0)
        i = lax.broadcasted_iota(jnp.int32, (n_keys, Q_BLOCK), 1)
        band = (j >= i) & (j <= WINDOW + i)
        bias_scr[0] = jnp.where(band & (j >= WINDOW), 0.0, -jnp.inf)
        bias_scr[1] = jnp.where(band, 0.0, -jnp.inf)

    @pl.when(t == 0)
    def _():
        kext_scr[0:WINDOW, :] = jnp.zeros((WINDOW, KV_DIM), BF16)
        vtext_scr[:, 0:WINDOW] = jnp.zeros((KV_DIM, WINDOW), BF16)

    h = h_ref[...]
    hn = h * lax.rsqrt(jnp.mean(h * h, axis=-1, keepdims=True) + EPS)
    xkv = (hn * kvg_ref[...]).astype(BF16)
    xb = (hn * nbg_ref[...]).astype(BF16)

    cost, sint = cost_ref[...], sint_ref[...]
    half = ROT_DIM // 2

    def rotate_head(rows):
        lo, hi = rows[0:half, :], rows[half:ROT_DIM, :]
        return jnp.concatenate(
            [lo * cost - hi * sint, hi * cost + lo * sint, rows[ROT_DIM:, :]], axis=0)

    kvt = _dot_nt(wkvt_ref[...], xkv)
    kt = jnp.concatenate(
        [rotate_head(kvt[kh * HEAD_DIM:(kh + 1) * HEAD_DIM, :]) for kh in range(N_KV_HEADS)],
        axis=0)
    vt = kvt[KV_DIM:, :]
    kext_scr[WINDOW:, :] = kt.T.astype(BF16)
    vtext_scr[:, WINDOW:] = vt.astype(BF16)

    @pl.when(t == n_t - 1)
    def _():
        kout_ref[0] = kt[:, tile - WINDOW:]
        vout_ref[0] = vt[:, tile - WINDOW:]

    qt = _dot_nt(wqt_ref[...], xb)
    for hd in range(N_HEADS):
        rot = rotate_head(qt[hd * HEAD_DIM:(hd + 1) * HEAD_DIM, :])
        qt_scr[hd * HEAD_DIM:(hd + 1) * HEAD_DIM, :] = (rot * Q_SCALE_LOG2).astype(BF16)

    lane_head = lax.broadcasted_iota(jnp.int32, (1, GQA_GROUP * Q_BLOCK), 1) // Q_BLOCK
    zeros_half = jnp.zeros((HEAD_DIM, GQA_GROUP * Q_BLOCK), BF16)
    ones_rows = jnp.ones((BF16_SUBLANES, n_keys), BF16)

    def scores(qb, kh):
        qcols = slice(qb * Q_BLOCK, (qb + 1) * Q_BLOCK)
        keys = slice(qb * Q_BLOCK, qb * Q_BLOCK + n_keys)
        q4 = jnp.concatenate(
            [qt_scr[(kh * GQA_GROUP + r) * HEAD_DIM:(kh * GQA_GROUP + r + 1) * HEAD_DIM, qcols]
             for r in range(GQA_GROUP)], axis=1)
        q4 = jnp.concatenate([q4, zeros_half] if kh % 2 == 0 else [zeros_half, q4], axis=0)
        kblk = kext_scr[keys, (kh // 2) * LANES:(kh // 2 + 1) * LANES]
        return _dot(kblk, q4)

    def finish(qb, kh, s, s_ahead):
        qcols = slice(qb * Q_BLOCK, (qb + 1) * Q_BLOCK)
        keys = slice(qb * Q_BLOCK, qb * Q_BLOCK + n_keys)
        bias = bias_scr[jnp.where(jnp.logical_or(t > 0, qb > 0), 1, 0)]
        s = s + jnp.concatenate([bias] * GQA_GROUP, axis=1)
        sink = jnp.zeros((1, GQA_GROUP * Q_BLOCK), F32)
        for r in range(GQA_GROUP):
            sink = jnp.where(lane_head == r, sinks_ref[kh * GQA_GROUP + r] * LOG2_E, sink)
        m = jnp.maximum(jnp.max(s, axis=0, keepdims=True), sink)
        p = jnp.exp2(s - m)
        if s_ahead is not None:
            p = jnp.concatenate(
                [p[:n_keys - 8, :], p[n_keys - 8:, :] + _zero_of(s_ahead[0:8, :])], axis=0)
        vt_ones = jnp.concatenate(
            [vtext_scr[kh * HEAD_DIM:(kh + 1) * HEAD_DIM, keys], ones_rows], axis=0)
        ot = _dot(vt_ones, p.astype(BF16))
        denom = ot[HEAD_DIM:HEAD_DIM + 1, :] + jnp.exp2(sink - m)
        ot = ot[0:HEAD_DIM, :] * (1.0 / denom)
        for r in range(GQA_GROUP):
            hd = kh * GQA_GROUP + r
            ogt_scr[hd * HEAD_DIM:(hd + 1) * HEAD_DIM, qcols] = ot[:, r * Q_BLOCK:(r + 1) * Q_BLOCK]

    blocks = [(qb, kh) for qb in range(tile // Q_BLOCK) for kh in range(N_KV_HEADS)]
    pending = [scores(*blk) for blk in blocks[:ATTN_AHEAD]]
    for n, blk in enumerate(blocks):
        if n + ATTN_AHEAD < len(blocks):
            pending.append(scores(*blocks[n + ATTN_AHEAD]))
        s_cur = pending.pop(0)
        finish(*blk, s_cur, pending[-1] if pending else None)

    kext_scr[0:WINDOW, :] = kext_scr[tile:tile + WINDOW, :]
    vtext_scr[:, 0:WINDOW] = vtext_scr[:, tile:tile + WINDOW]

    gt = _dot_nt(wgt_ref[...], xb)
    ogt = (ogt_scr[...] * _silu(gt)).astype(BF16)
    out_t = _dot(woutt_ref[...], ogt)
    h2 = h + out_t.T
    y_ref[...] = _rms(h2, fg_ref[...])


def _layer_b_prompt(h, sinks, kv_norm, norm_b, final_norm, w_kv_t, w_qg_t, w_out_t, *, batch, seq):
    tile = B_TILE
    n_t = seq // tile
    pos = jnp.arange(seq, dtype=F32)
    inv = ROPE_THETA ** (-jnp.arange(0, ROT_DIM, 2, dtype=F32) / ROT_DIM)
    ang_t = inv[:, None] * pos[None, :]
    cos_t, sin_t = jnp.cos(ang_t), jnp.sin(ang_t)
    tok_spec = pl.BlockSpec((tile, D_MODEL), lambda b, t, *_: (b * n_t + t, 0))
    rot_t_spec = pl.BlockSpec((ROT_DIM // 2, tile), lambda b, t, *_: (0, t))
    last_spec = pl.BlockSpec((1, KV_DIM, WINDOW), lambda b, t, *_: (b, 0, 0))

    def const(shape):
        return pl.BlockSpec(shape, lambda *_: (0,) * len(shape), pipeline_mode=pl.Buffered(1))

    def half(i):
        return pl.BlockSpec((D_MODEL, D_MODEL), lambda *_: (i, 0), pipeline_mode=pl.Buffered(1))

    return pl.pallas_call(
        functools.partial(_layer_b_prompt_kernel, tile=tile, n_t=n_t),
        grid_spec=pltpu.PrefetchScalarGridSpec(
            num_scalar_prefetch=1,
            grid=(batch, n_t),
            in_specs=[tok_spec, const(kv_norm.shape), const(norm_b.shape), const(final_norm.shape),
                      const(w_kv_t.shape), half(0), half(1), const(w_out_t.shape),
                      rot_t_spec, rot_t_spec],
            out_specs=[tok_spec, last_spec, last_spec],
            scratch_shapes=[pltpu.VMEM((WINDOW + tile, KV_DIM), BF16),
                            pltpu.VMEM((KV_DIM, WINDOW + tile), BF16),
                            pltpu.VMEM((D_MODEL, tile), BF16),
                            pltpu.VMEM((D_MODEL, tile), F32),
                            pltpu.VMEM((2, WINDOW + Q_BLOCK, Q_BLOCK), F32)]),
        out_shape=[jax.ShapeDtypeStruct((batch * seq, D_MODEL), F32),
                   jax.ShapeDtypeStruct((batch, KV_DIM, WINDOW), F32),
                   jax.ShapeDtypeStruct((batch, KV_DIM, WINDOW), F32)],
        compiler_params=pltpu.CompilerParams(
            dimension_semantics=("arbitrary", "arbitrary"), vmem_limit_bytes=VMEM_LIMIT_BYTES),
        name="layer_b_prompt",
    )(sinks, h, kv_norm, norm_b, final_norm, w_kv_t, w_qg_t, w_qg_t, w_out_t, cos_t, sin_t)


def _layer_b_sample_kernel(sinks_ref, h_ref, kvg_ref, nbg_ref, fg_ref, wkvt_ref, wqgt_ref,
                           woutt_ref, cos_ref, slo_ref, shi_ref, ck_ref, cv_ref,
                           y_ref, kout_ref, vout_ref,
                           q_scr, gate_scr, knew_scr, vnew_scr, knewt_scr, vnewt_scr, o_scr,
                           *, n_seq, b_tile):
    step = pl.program_id(0)

    @pl.when(step == 0)
    def _():
        h = h_ref[...]
        hn = h * lax.rsqrt(jnp.mean(h * h, axis=-1, keepdims=True) + EPS)
        kv = _dot_nt((hn * kvg_ref[...]).astype(BF16), wkvt_ref[...])
        qg = _dot_nt((hn * nbg_ref[...]).astype(BF16), wqgt_ref[...])
        cos, slo, shi = cos_ref[...], slo_ref[...], shi_ref[...]
        for c in range(KV_DIM // LANES):
            cols = slice(c * LANES, (c + 1) * LANES)
            knew_scr[:, cols] = _rotate(kv[:, cols], cos, slo, shi)
        vnew_scr[...] = kv[:, KV_DIM:]
        knewt_scr[...] = knew_scr[...].T
        vnewt_scr[...] = kv[:, KV_DIM:].T
        for c in range(D_MODEL // LANES):
            cols = slice(c * LANES, (c + 1) * LANES)
            q2 = _rotate(qg[:, cols], cos, slo, shi) * HEAD_DIM ** -0.5
            g2 = qg[:, D_MODEL + c * LANES:D_MODEL + (c + 1) * LANES]
            for i in range(LANES // HEAD_DIM):
                dst = _member_major(c * (LANES // HEAD_DIM) + i)
                q_scr[:, dst] = q2[:, i * HEAD_DIM:(i + 1) * HEAD_DIM]
                gate_scr[:, dst] = g2[:, i * HEAD_DIM:(i + 1) * HEAD_DIM]

    n_rows = GQA_GROUP * N_KV_HEADS * SAMPLE_GROUP
    row = lax.broadcasted_iota(jnp.int32, (n_rows, 1), 0)
    row_kh = (row // SAMPLE_GROUP) % N_KV_HEADS
    row_seq = row % SAMPLE_GROUP
    lane_kh = lax.broadcasted_iota(jnp.int32, (1, KV_DIM), 1) // HEAD_DIM
    own = row_kh == lane_kh
    sink = jnp.zeros((n_rows, 1), F32)
    for r in range(GQA_GROUP):
        for kh in range(N_KV_HEADS):
            sink = jnp.where(row // SAMPLE_GROUP == r * N_KV_HEADS + kh,
                             sinks_ref[kh * GQA_GROUP + r], sink)
    is_last = lax.broadcasted_iota(jnp.int32, (KV_DIM, WINDOW), 1) == WINDOW - 1
    n_blk = GQA_GROUP * N_KV_HEADS

    def group(i, carry):
        b0 = i * SAMPLE_GROUP
        g0 = pl.multiple_of(step * b_tile + b0, SAMPLE_GROUP)
        seqs = pl.ds(g0, SAMPLE_GROUP)
        q8 = q_scr[seqs, :]
        qexp = jnp.concatenate(
            [q8[:, r * KV_DIM:(r + 1) * KV_DIM] for r in range(GQA_GROUP)
             for _ in range(N_KV_HEADS)], axis=0)
        qexp = jnp.where(own, qexp, 0.0).astype(BF16)
        knew8 = knew_scr[seqs, :].astype(BF16).astype(F32)
        vnew8 = vnew_scr[seqs, :].astype(BF16).astype(F32)
        s_new = jnp.sum(qexp.astype(F32) * jnp.concatenate([knew8] * n_blk, axis=0),
                        axis=1, keepdims=True)
        s_old = jnp.zeros((n_rows, WINDOW), F32)
        for b in range(SAMPLE_GROUP):
            s_b = _dot(qexp, ck_ref[b0 + b].astype(BF16))
            s_old = jnp.where(row_seq == b, s_b, s_old)
        m = jnp.maximum(jnp.maximum(jnp.max(s_old, axis=1, keepdims=True), s_new), sink)
        p_old = jnp.exp(s_old - m)
        p_new = jnp.exp(s_new - m)
        denom = jnp.sum(p_old, axis=1, keepdims=True) + p_new + jnp.exp(sink - m)
        p_old = p_old.astype(BF16)
        o = jnp.zeros((n_rows, KV_DIM), F32)
        for b in range(SAMPLE_GROUP):
            o_b = _dot_nt(p_old, cv_ref[b0 + b].astype(BF16))
            o = jnp.where(row_seq == b, o_b, o)
        o = (o + p_new.astype(BF16).astype(F32) * jnp.concatenate([vnew8] * n_blk, axis=0)) / denom
        o = jnp.where(own, o, 0.0)
        for r in range(GQA_GROUP):
            blks = [o[(r * N_KV_HEADS + kh) * SAMPLE_GROUP:(r * N_KV_HEADS + kh + 1) * SAMPLE_GROUP]
                    for kh in range(N_KV_HEADS)]
            o_scr[seqs, r * KV_DIM:(r + 1) * KV_DIM] = (blks[0] + blks[1]) + (blks[2] + blks[3])
        for b in range(SAMPLE_GROUP):
            g = g0 + b
            blk = pl.ds(pl.multiple_of((g // LANES) * LANES, LANES), LANES)
            to_last = LANES - 1 - g % LANES
            kout_ref[b0 + b] = jnp.where(is_last, pltpu.roll(knewt_scr[:, blk], to_last, 1),
                                         pltpu.roll(ck_ref[b0 + b], WINDOW - 1, 1))
            vout_ref[b0 + b] = jnp.where(is_last, pltpu.roll(vnewt_scr[:, blk], to_last, 1),
                                         pltpu.roll(cv_ref[b0 + b], WINDOW - 1, 1))
        return carry

    lax.fori_loop(0, b_tile // SAMPLE_GROUP, group, 0)

    @pl.when(step == pl.num_programs(0) - 1)
    def _():
        og_mm = o_scr[...] * _silu(gate_scr[...])
        og = jnp.concatenate(
            [og_mm[:, _member_major(hd)] for hd in range(N_HEADS)], axis=1).astype(BF16)
        h2 = h_ref[...] + _dot_nt(og, woutt_ref[...])
        y_ref[...] = _rms(h2, fg_ref[...])


def _layer_b_sample(h, sinks, kv_norm, norm_b, final_norm, w_kv_t, w_qg_t, w_out_t, cache_k, cache_v):
    n_seq = h.shape[0]
    b_tile = SAMPLE_B_TILE
    cos, slo, shi = _rotary_tables(jnp.full((1,), PAST_LEN, F32))

    def const(shape):
        return pl.BlockSpec(shape, lambda *_: (0,) * len(shape))

    assert n_seq % LANES == 0 and cache_k.shape == (n_seq, KV_DIM, WINDOW)
    cache_spec = pl.BlockSpec((b_tile, KV_DIM, WINDOW), lambda i, *_: (i, 0, 0))
    return pl.pallas_call(
        functools.partial(_layer_b_sample_kernel, n_seq=n_seq, b_tile=b_tile),
        grid_spec=pltpu.PrefetchScalarGridSpec(
            num_scalar_prefetch=1,
            grid=(n_seq // b_tile,),
            in_specs=[const(h.shape), const(kv_norm.shape), const(norm_b.shape),
                      const(final_norm.shape), const(w_kv_t.shape), const(w_qg_t.shape),
                      const(w_out_t.shape), const(cos.shape), const(slo.shape), const(shi.shape),
                      cache_spec, cache_spec],
            out_specs=[const((n_seq, D_MODEL)), cache_spec, cache_spec],
            scratch_shapes=[pltpu.VMEM((n_seq, D_MODEL), F32),
                            pltpu.VMEM((n_seq, D_MODEL), F32),
                            pltpu.VMEM((n_seq, KV_DIM), F32),
                            pltpu.VMEM((n_seq, KV_DIM), F32),
                            pltpu.VMEM((KV_DIM, n_seq), F32),
                            pltpu.VMEM((KV_DIM, n_seq), F32),
                            pltpu.VMEM((n_seq, D_MODEL), F32)]),
        out_shape=[jax.ShapeDtypeStruct((n_seq, D_MODEL), F32),
                   jax.ShapeDtypeStruct(cache_k.shape, F32),
                   jax.ShapeDtypeStruct(cache_v.shape, F32)],
        compiler_params=pltpu.CompilerParams(
            dimension_semantics=("arbitrary",), vmem_limit_bytes=VMEM_LIMIT_BYTES),
        name="layer_b_sample",
    )(sinks, h, kv_norm, norm_b, final_norm, w_kv_t, w_qg_t, w_out_t, cos, slo, shi, cache_k, cache_v)


def kernel(x_prompt, x_sample, cache_k, cache_v, norm_a, w_in_a, v_norm_a, w_s_a, b_s_a, w_out_a,
           kv_norm, w_kv, norm_b, w_in_b, sinks_b, w_out_b, final_norm):
    batch, seq, _ = x_prompt.shape
    n_seq, dec_seq, _ = x_sample.shape
    assert dec_seq == 1 and seq % CHUNK == 0 and cache_k.shape[1] == WINDOW
    assert norm_a.shape[0] == 1 and norm_b.shape[0] == 1

    row = lambda g: g.reshape(1, -1)
    w_in_a16 = w_in_a[0].astype(BF16)
    w_out_a16 = w_out_a[0].astype(BF16)
    w_kv_t = _transpose_cast(w_kv)
    w_qg_t = _transpose_cast(w_in_b[0])
    w_out_b_t = _transpose_cast(w_out_b[0])
    bs_chunk = jnp.repeat(b_s_a[0].T, A_GROUP_DIM, axis=1)
    ws_one = jnp.repeat(w_s_a[0, :, 0, 0], A_GROUP_DIM).reshape(1, A_WIDTH)
    bs_one = jnp.repeat(b_s_a[0, :, 0], A_GROUP_DIM).reshape(1, A_WIDTH)

    h_p = _layer_a(x_prompt.reshape(batch * seq, D_MODEL), row(norm_a[0]), w_in_a16,
                   row(v_norm_a[0]), w_s_a[0], bs_chunk, w_out_a16, tile=A_TILE, chunked=True)[0]
    h_s, av_s = _layer_a(x_sample.reshape(n_seq, D_MODEL), row(norm_a[0]), w_in_a16,
                         row(v_norm_a[0]), ws_one, bs_one, w_out_a16, tile=n_seq, chunked=False)

    y_p, kt_p, vt_p = _layer_b_prompt(
        h_p, sinks_b[0], row(kv_norm), row(norm_b[0]), row(final_norm),
        w_kv_t, w_qg_t, w_out_b_t, batch=batch, seq=seq)
    def to_window(x_t):
        n = x_t.shape[0]
        return x_t.reshape(n, N_KV_HEADS, HEAD_DIM, WINDOW).transpose(0, 3, 1, 2)

    def from_window(x):
        return x.transpose(0, 2, 3, 1).reshape(x.shape[0], KV_DIM, WINDOW)

    y_s, kt_s, vt_s = _layer_b_sample(h_s, sinks_b[0], row(kv_norm), row(norm_b[0]),
                                      row(final_norm), w_kv_t, w_qg_t, w_out_b_t,
                                      from_window(cache_k), from_window(cache_v))

    return (y_p.reshape(batch, seq, D_MODEL),
            y_s.reshape(n_seq, 1, D_MODEL),
            to_window(kt_p),
            to_window(vt_p),
            to_window(kt_s),
            to_window(vt_s),
            av_s.reshape(1, n_seq, 1, A_WIDTH))
```

```python
import functools

import jax
import jax.numpy as jnp
from jax import lax
from jax.experimental import pallas as pl
from jax.experimental.pallas import tpu as pltpu

D_MODEL = 1024
PAST_LEN = 8192
CHUNK = 128
A_WIDTH = 2 * D_MODEL
A_GROUPS = 8
A_GROUP_DIM = A_WIDTH // A_GROUPS
HEAD_DIM = 64
N_HEADS = D_MODEL // HEAD_DIM
N_KV_HEADS = 4
GQA_GROUP = N_HEADS // N_KV_HEADS
KV_DIM = N_KV_HEADS * HEAD_DIM
WINDOW = 128
Q_BLOCK = 128
ROT_DIM = HEAD_DIM // 4
ROPE_THETA = 500000.0
EPS = 1e-5

LANES = 128
BF16_SUBLANES = 16
LOG2_E = 1.4426950408889634
Q_SCALE_LOG2 = HEAD_DIM ** -0.5 * LOG2_E
VMEM_LIMIT_BYTES = 56 * 1024 * 1024

A_TILE = 512
B_TILE = 512
PREP_COLS = 512
OUT_ROWS = 256
ATTN_AHEAD = 2
SAMPLE_B_TILE = 16
SAMPLE_GROUP = 8

F32 = jnp.float32
BF16 = jnp.bfloat16


def _rms(x, g):
    return x * lax.rsqrt(jnp.mean(x * x, axis=-1, keepdims=True) + EPS) * g


def _silu(x):
    return x * jax.nn.sigmoid(x)


def _dot(a, b):
    return jnp.dot(a, b, preferred_element_type=F32)


def _dot_nt(a, b):
    return lax.dot_general(a, b, (((1,), (1,)), ((), ())), preferred_element_type=F32)


def _zero_of(x):
    bits = pltpu.bitcast(x, jnp.uint32)
    return ((bits >> 16) >> 16).astype(F32)


def _member_major(head):
    kh, r = divmod(head, GQA_GROUP)
    start = (r * N_KV_HEADS + kh) * HEAD_DIM
    return slice(start, start + HEAD_DIM)


def _rotate(x, cos, sin_lo, sin_hi):
    return (x * cos + pltpu.roll(x, LANES - ROT_DIM // 2, 1) * sin_lo
            + pltpu.roll(x, ROT_DIM // 2, 1) * sin_hi)


def _rotary_tables(positions):
    lane = jnp.arange(LANES) % HEAD_DIM
    freq = (2 * (lane % (ROT_DIM // 2))).astype(F32)
    ang = positions[:, None] * (ROPE_THETA ** (-freq / ROT_DIM))[None, :]
    first = (lane < ROT_DIM // 2)[None, :]
    second = ((lane >= ROT_DIM // 2) & (lane < ROT_DIM))[None, :]
    cos = jnp.where(first | second, jnp.cos(ang), 1.0)
    sin_lo = jnp.where(first, -jnp.sin(ang), 0.0)
    sin_hi = jnp.where(second, jnp.sin(ang), 0.0)
    return cos, sin_lo, sin_hi


def _transpose_cast_kernel(w_ref, o_ref):
    o_ref[...] = w_ref[...].T.astype(BF16)


def _transpose_cast(w):
    k, n = w.shape
    return pl.pallas_call(
        _transpose_cast_kernel,
        grid=(n // PREP_COLS,),
        in_specs=[pl.BlockSpec((k, PREP_COLS), lambda j: (0, j))],
        out_specs=pl.BlockSpec((PREP_COLS, k), lambda j: (j, 0)),
        out_shape=jax.ShapeDtypeStruct((n, k), BF16),
        compiler_params=pltpu.CompilerParams(dimension_semantics=("arbitrary",)),
        name="transpose_cast",
    )(w)


def _layer_a_kernel(x_ref, ng_ref, win_ref, vg_ref, ws_ref, bs_ref, wout_ref, *out_refs,
                    tile, chunked):
    if chunked:
        h_ref, y_scr = out_refs
    else:
        h_ref, av_ref, y_scr = out_refs
    x = x_ref[...]
    xn = _rms(x, ng_ref[...]).astype(BF16)
    v = _rms(_dot(xn, win_ref[:, A_WIDTH:2 * A_WIDTH]), vg_ref[...])
    if chunked:
        vb = v.astype(BF16)
        row = lax.broadcasted_iota(jnp.int32, (CHUNK, CHUNK), 0)
        col = lax.broadcasted_iota(jnp.int32, (CHUNK, CHUNK), 1)
        tri = row >= col
    else:
        av_ref[...] = v
    width = 2 * A_GROUP_DIM
    for pair in range(A_GROUPS // 2):
        cols = slice(pair * width, (pair + 1) * width)
        u = _dot(xn, win_ref[:, cols])
        gate = _dot(xn, win_ref[:, 2 * A_WIDTH + pair * width:2 * A_WIDTH + (pair + 1) * width])
        if chunked:
            ws = [jnp.where(tri, ws_ref[g], 0.0).astype(BF16) for g in (2 * pair, 2 * pair + 1)]
            z = jnp.concatenate(
                [jnp.concatenate(
                    [_dot(ws[i], vb[c * CHUNK:(c + 1) * CHUNK,
                                    (2 * pair + i) * A_GROUP_DIM:(2 * pair + i + 1) * A_GROUP_DIM])
                     for i in range(2)], axis=1) + bs_ref[:, cols]
                 for c in range(tile // CHUNK)], axis=0)
        else:
            z = v[:, cols] * ws_ref[:, cols] + bs_ref[:, cols]
        y_scr[:, cols] = (u * z * _silu(gate)).astype(BF16)
    h_ref[...] = x + _dot(y_scr[...], wout_ref[...])


def _const_spec(shape):
    return pl.BlockSpec(shape, lambda *_: (0,) * len(shape), pipeline_mode=pl.Buffered(1))


def _layer_a(x, norm_g, w_in, v_norm_g, ws, bs, w_out, *, tile, chunked):
    n_tok = x.shape[0]
    tok_spec = pl.BlockSpec((tile, D_MODEL), lambda i: (i, 0))
    out_shape = [jax.ShapeDtypeStruct((n_tok, D_MODEL), F32)]
    out_specs = [tok_spec]
    if not chunked:
        out_shape.append(jax.ShapeDtypeStruct((n_tok, A_WIDTH), F32))
        out_specs.append(pl.BlockSpec((tile, A_WIDTH), lambda i: (i, 0)))
    return pl.pallas_call(
        functools.partial(_layer_a_kernel, tile=tile, chunked=chunked),
        grid=(n_tok // tile,),
        in_specs=[tok_spec, _const_spec(norm_g.shape), _const_spec(w_in.shape),
                  _const_spec(v_norm_g.shape), _const_spec(ws.shape), _const_spec(bs.shape),
                  _const_spec(w_out.shape)],
        out_specs=out_specs,
        out_shape=out_shape,
        scratch_shapes=[pltpu.VMEM((tile, A_WIDTH), BF16)],
        compiler_params=pltpu.CompilerParams(
            dimension_semantics=("arbitrary",), vmem_limit_bytes=VMEM_LIMIT_BYTES),
        name="layer_a_prompt" if chunked else "layer_a_sample",
    )(x, norm_g, w_in, v_norm_g, ws, bs, w_out)


def _layer_b_prompt_kernel(sinks_ref, h_ref, kvg_ref, nbg_ref, fg_ref, wkvt_ref, wqt_ref,
                           wgt_ref, woutt_ref, cost_ref, sint_ref,
                           y_ref, kout_ref, vout_ref,
                           kext_scr, vtext_scr, qt_scr, ogt_scr, bias_scr, *, tile, n_t):
    t = pl.program_id(1)
    n_keys = WINDOW + Q_BLOCK

    @pl.when((pl.program_id(0) == 0) & (t == 0))
    def _():
        j = lax.broadcasted_iota(jnp.int32, (n_keys, Q_BLOCK), 0)
        i = lax.broadcasted_iota(jnp.int32, (n_keys, Q_BLOCK), 1)
        band = (j >= i) & (j <= WINDOW + i)
        bias_scr[0] = jnp.where(band & (j >= WINDOW), 0.0, -jnp.inf)
        bias_scr[1] = jnp.where(band, 0.0, -jnp.inf)

    @pl.when(t == 0)
    def _():
        kext_scr[0:WINDOW, :] = jnp.zeros((WINDOW, KV_DIM), BF16)
        vtext_scr[:, 0:WINDOW] = jnp.zeros((KV_DIM, WINDOW), BF16)

    h = h_ref[...]
    hn = h * lax.rsqrt(jnp.mean(h * h, axis=-1, keepdims=True) + EPS)
    xkv = (hn * kvg_ref[...]).astype(BF16)
    xb = (hn * nbg_ref[...]).astype(BF16)

    cost, sint = cost_ref[...], sint_ref[...]
    half = ROT_DIM // 2

    def rotate_head(rows):
        lo, hi = rows[0:half, :], rows[half:ROT_DIM, :]
        return jnp.concatenate(
            [lo * cost - hi * sint, hi * cost + lo * sint, rows[ROT_DIM:, :]], axis=0)

    kvt = _dot_nt(wkvt_ref[...], xkv)
    kt = jnp.concatenate(
        [rotate_head(kvt[kh * HEAD_DIM:(kh + 1) * HEAD_DIM, :]) for kh in range(N_KV_HEADS)],
        axis=0)
    vt = kvt[KV_DIM:, :]
    kext_scr[WINDOW:, :] = kt.T.astype(BF16)
    vtext_scr[:, WINDOW:] = vt.astype(BF16)

    @pl.when(t == n_t - 1)
    def _():
        kout_ref[0] = kt[:, tile - WINDOW:]
        vout_ref[0] = vt[:, tile - WINDOW:]

    qt = _dot_nt(wqt_ref[...], xb)
    for hd in range(N_HEADS):
        rot = rotate_head(qt[hd * HEAD_DIM:(hd + 1) * HEAD_DIM, :])
        qt_scr[hd * HEAD_DIM:(hd + 1) * HEAD_DIM, :] = (rot * Q_SCALE_LOG2).astype(BF16)

    lane_head = lax.broadcasted_iota(jnp.int32, (1, GQA_GROUP * Q_BLOCK), 1) // Q_BLOCK
    zeros_half = jnp.zeros((HEAD_DIM, GQA_GROUP * Q_BLOCK), BF16)
    ones_rows = jnp.ones((BF16_SUBLANES, n_keys), BF16)

    def scores(qb, kh):
        qcols = slice(qb * Q_BLOCK, (qb + 1) * Q_BLOCK)
        keys = slice(qb * Q_BLOCK, qb * Q_BLOCK + n_keys)
        q4 = jnp.concatenate(
            [qt_scr[(kh * GQA_GROUP + r) * HEAD_DIM:(kh * GQA_GROUP + r + 1) * HEAD_DIM, qcols]
             for r in range(GQA_GROUP)], axis=1)
        q4 = jnp.concatenate([q4, zeros_half] if kh % 2 == 0 else [zeros_half, q4], axis=0)
        kblk = kext_scr[keys, (kh // 2) * LANES:(kh // 2 + 1) * LANES]
        return _dot(kblk, q4)

    def finish(qb, kh, s, s_ahead):
        qcols = slice(qb * Q_BLOCK, (qb + 1) * Q_BLOCK)
        keys = slice(qb * Q_BLOCK, qb * Q_BLOCK + n_keys)
        bias = bias_scr[jnp.where(jnp.logical_or(t > 0, qb > 0), 1, 0)]
        s = s + jnp.concatenate([bias] * GQA_GROUP, axis=1)
        sink = jnp.zeros((1, GQA_GROUP * Q_BLOCK), F32)
        for r in range(GQA_GROUP):
            sink = jnp.where(lane_head == r, sinks_ref[kh * GQA_GROUP + r] * LOG2_E, sink)
        m = jnp.maximum(jnp.max(s, axis=0, keepdims=True), sink)
        p = jnp.exp2(s - m)
        if s_ahead is not None:
            p = jnp.concatenate(
                [p[:n_keys - 8, :], p[n_keys - 8:, :] + _zero_of(s_ahead[0:8, :])], axis=0)
        vt_ones = jnp.concatenate(
            [vtext_scr[kh * HEAD_DIM:(kh + 1) * HEAD_DIM, keys], ones_rows], axis=0)
        ot = _dot(vt_ones, p.astype(BF16))
        denom = ot[HEAD_DIM:HEAD_DIM + 1, :] + jnp.exp2(sink - m)
        ot = ot[0:HEAD_DIM, :] * (1.0 / denom)
        for r in range(GQA_GROUP):
            hd = kh * GQA_GROUP + r
            ogt_scr[hd * HEAD_DIM:(hd + 1) * HEAD_DIM, qcols] = ot[:, r * Q_BLOCK:(r + 1) * Q_BLOCK]

    blocks = [(qb, kh) for qb in range(tile // Q_BLOCK) for kh in range(N_KV_HEADS)]
    pending = [scores(*blk) for blk in blocks[:ATTN_AHEAD]]
    for n, blk in enumerate(blocks):
        if n + ATTN_AHEAD < len(blocks):
            pending.append(scores(*blocks[n + ATTN_AHEAD]))
        s_cur = pending.pop(0)
        finish(*blk, s_cur, pending[-1] if pending else None)

    kext_scr[0:WINDOW, :] = kext_scr[tile:tile + WINDOW, :]
    vtext_scr[:, 0:WINDOW] = vtext_scr[:, tile:tile + WINDOW]

    ogt = jnp.concatenate(
        [(ogt_scr[rc * OUT_ROWS:(rc + 1) * OUT_ROWS, :]
          * _silu(_dot_nt(wgt_ref[rc * OUT_ROWS:(rc + 1) * OUT_ROWS, :], xb))).astype(BF16)
         for rc in range(D_MODEL // OUT_ROWS)], axis=0)
    h2 = jnp.concatenate(
        [h[:, rc * OUT_ROWS:(rc + 1) * OUT_ROWS]
         + _dot(woutt_ref[rc * OUT_ROWS:(rc + 1) * OUT_ROWS, :], ogt).T
         for rc in range(D_MODEL // OUT_ROWS)], axis=1)
    y_ref[...] = _rms(h2, fg_ref[...])


def _layer_b_prompt(h, sinks, kv_norm, norm_b, final_norm, w_kv_t, w_qg_t, w_out_t, *, batch, seq):
    tile = B_TILE
    n_t = seq // tile
    pos = jnp.arange(seq, dtype=F32)
    inv = ROPE_THETA ** (-jnp.arange(0, ROT_DIM, 2, dtype=F32) / ROT_DIM)
    ang_t = inv[:, None] * pos[None, :]
    cos_t, sin_t = jnp.cos(ang_t), jnp.sin(ang_t)
    tok_spec = pl.BlockSpec((tile, D_MODEL), lambda b, t, *_: (b * n_t + t, 0))
    rot_t_spec = pl.BlockSpec((ROT_DIM // 2, tile), lambda b, t, *_: (0, t))
    last_spec = pl.BlockSpec((1, KV_DIM, WINDOW), lambda b, t, *_: (b, 0, 0))

    def const(shape):
        return pl.BlockSpec(shape, lambda *_: (0,) * len(shape), pipeline_mode=pl.Buffered(1))

    def half(i):
        return pl.BlockSpec((D_MODEL, D_MODEL), lambda *_: (i, 0), pipeline_mode=pl.Buffered(1))

    return pl.pallas_call(
        functools.partial(_layer_b_prompt_kernel, tile=tile, n_t=n_t),
        grid_spec=pltpu.PrefetchScalarGridSpec(
            num_scalar_prefetch=1,
            grid=(batch, n_t),
            in_specs=[tok_spec, const(kv_norm.shape), const(norm_b.shape), const(final_norm.shape),
                      const(w_kv_t.shape), half(0), half(1), const(w_out_t.shape),
                      rot_t_spec, rot_t_spec],
            out_specs=[tok_spec, last_spec, last_spec],
            scratch_shapes=[pltpu.VMEM((WINDOW + tile, KV_DIM), BF16),
                            pltpu.VMEM((KV_DIM, WINDOW + tile), BF16),
                            pltpu.VMEM((D_MODEL, tile), BF16),
                            pltpu.VMEM((D_MODEL, tile), F32),
                            pltpu.VMEM((2, WINDOW + Q_BLOCK, Q_BLOCK), F32)]),
        out_shape=[jax.ShapeDtypeStruct((batch * seq, D_MODEL), F32),
                   jax.ShapeDtypeStruct((batch, KV_DIM, WINDOW), F32),
                   jax.ShapeDtypeStruct((batch, KV_DIM, WINDOW), F32)],
        compiler_params=pltpu.CompilerParams(
            dimension_semantics=("arbitrary", "arbitrary"), vmem_limit_bytes=VMEM_LIMIT_BYTES),
        name="layer_b_prompt",
    )(sinks, h, kv_norm, norm_b, final_norm, w_kv_t, w_qg_t, w_qg_t, w_out_t, cos_t, sin_t)


def _layer_b_sample_kernel(sinks_ref, h_ref, kvg_ref, nbg_ref, fg_ref, wkvt_ref, wqgt_ref,
                           woutt_ref, cos_ref, slo_ref, shi_ref, ck_ref, cv_ref,
                           y_ref, kout_ref, vout_ref,
                           q_scr, gate_scr, knew_scr, vnew_scr, knewt_scr, vnewt_scr, o_scr,
                           *, n_seq, b_tile):
    step = pl.program_id(0)

    @pl.when(step == 0)
    def _():
        h = h_ref[...]
        hn = h * lax.rsqrt(jnp.mean(h * h, axis=-1, keepdims=True) + EPS)
        kv = _dot_nt((hn * kvg_ref[...]).astype(BF16), wkvt_ref[...])
        qg = _dot_nt((hn * nbg_ref[...]).astype(BF16), wqgt_ref[...])
        cos, slo, shi = cos_ref[...], slo_ref[...], shi_ref[...]
        for c in range(KV_DIM // LANES):
            cols = slice(c * LANES, (c + 1) * LANES)
            knew_scr[:, cols] = _rotate(kv[:, cols], cos, slo, shi)
        vnew_scr[...] = kv[:, KV_DIM:]
        knewt_scr[...] = knew_scr[...].T
        vnewt_scr[...] = kv[:, KV_DIM:].T
        for c in range(D_MODEL // LANES):
            cols = slice(c * LANES, (c + 1) * LANES)
            q2 = _rotate(qg[:, cols], cos, slo, shi) * HEAD_DIM ** -0.5
            g2 = qg[:, D_MODEL + c * LANES:D_MODEL + (c + 1) * LANES]
            for i in range(LANES // HEAD_DIM):
                dst = _member_major(c * (LANES // HEAD_DIM) + i)
                q_scr[:, dst] = q2[:, i * HEAD_DIM:(i + 1) * HEAD_DIM]
                gate_scr[:, dst] = g2[:, i * HEAD_DIM:(i + 1) * HEAD_DIM]

    n_rows = GQA_GROUP * N_KV_HEADS * SAMPLE_GROUP
    row = lax.broadcasted_iota(jnp.int32, (n_rows, 1), 0)
    row_kh = (row // SAMPLE_GROUP) % N_KV_HEADS
    row_seq = row % SAMPLE_GROUP
    lane_kh = lax.broadcasted_iota(jnp.int32, (1, KV_DIM), 1) // HEAD_DIM
    own = row_kh == lane_kh
    sink = jnp.zeros((n_rows, 1), F32)
    for r in range(GQA_GROUP):
        for kh in range(N_KV_HEADS):
            sink = jnp.where(row // SAMPLE_GROUP == r * N_KV_HEADS + kh,
                             sinks_ref[kh * GQA_GROUP + r], sink)
    is_last = lax.broadcasted_iota(jnp.int32, (KV_DIM, WINDOW), 1) == WINDOW - 1
    n_blk = GQA_GROUP * N_KV_HEADS

    def group(i, carry):
        b0 = i * SAMPLE_GROUP
        g0 = pl.multiple_of(step * b_tile + b0, SAMPLE_GROUP)
        seqs = pl.ds(g0, SAMPLE_GROUP)
        q8 = q_scr[seqs, :]
        qexp = jnp.concatenate(
            [q8[:, r * KV_DIM:(r + 1) * KV_DIM] for r in range(GQA_GROUP)
             for _ in range(N_KV_HEADS)], axis=0)
        qexp = jnp.where(own, qexp, 0.0).astype(BF16)
        knew8 = knew_scr[seqs, :].astype(BF16).astype(F32)
        vnew8 = vnew_scr[seqs, :].astype(BF16).astype(F32)
        s_new = jnp.sum(qexp.astype(F32) * jnp.concatenate([knew8] * n_blk, axis=0),
                        axis=1, keepdims=True)
        s_old = jnp.zeros((n_rows, WINDOW), F32)
        for b in range(SAMPLE_GROUP):
            s_b = _dot(qexp, ck_ref[b0 + b].astype(BF16))
            s_old = jnp.where(row_seq == b, s_b, s_old)
        m = jnp.maximum(jnp.maximum(jnp.max(s_old, axis=1, keepdims=True), s_new), sink)
        p_old = jnp.exp(s_old - m)
        p_new = jnp.exp(s_new - m)
        denom = jnp.sum(p_old, axis=1, keepdims=True) + p_new + jnp.exp(sink - m)
        p_old = p_old.astype(BF16)
        o = jnp.zeros((n_rows, KV_DIM), F32)
        for b in range(SAMPLE_GROUP):
            o_b = _dot_nt(p_old, cv_ref[b0 + b].astype(BF16))
            o = jnp.where(row_seq == b, o_b, o)
        o = (o + p_new.astype(BF16).astype(F32) * jnp.concatenate([vnew8] * n_blk, axis=0)) / denom
        o = jnp.where(own, o, 0.0)
        for r in range(GQA_GROUP):
            blks = [o[(r * N_KV_HEADS + kh) * SAMPLE_GROUP:(r * N_KV_HEADS + kh + 1) * SAMPLE_GROUP]
                    for kh in range(N_KV_HEADS)]
            o_scr[seqs, r * KV_DIM:(r + 1) * KV_DIM] = (blks[0] + blks[1]) + (blks[2] + blks[3])
        for b in range(SAMPLE_GROUP):
            g = g0 + b
            blk = pl.ds(pl.multiple_of((g // LANES) * LANES, LANES), LANES)
            to_last = LANES - 1 - g % LANES
            kout_ref[b0 + b] = jnp.where(is_last, pltpu.roll(knewt_scr[:, blk], to_last, 1),
                                         pltpu.roll(ck_ref[b0 + b], WINDOW - 1, 1))
            vout_ref[b0 + b] = jnp.where(is_last, pltpu.roll(vnewt_scr[:, blk], to_last, 1),
                                         pltpu.roll(cv_ref[b0 + b], WINDOW - 1, 1))
        return carry

    lax.fori_loop(0, b_tile // SAMPLE_GROUP, group, 0)

    @pl.when(step == pl.num_programs(0) - 1)
    def _():
        og_mm = o_scr[...] * _silu(gate_scr[...])
        og = jnp.concatenate(
            [og_mm[:, _member_major(hd)] for hd in range(N_HEADS)], axis=1).astype(BF16)
        h2 = h_ref[...] + _dot_nt(og, woutt_ref[...])
        y_ref[...] = _rms(h2, fg_ref[...])


def _layer_b_sample(h, sinks, kv_norm, norm_b, final_norm, w_kv_t, w_qg_t, w_out_t, cache_k, cache_v):
    n_seq = h.shape[0]
    b_tile = SAMPLE_B_TILE
    cos, slo, shi = _rotary_tables(jnp.full((1,), PAST_LEN, F32))

    def const(shape):
        return pl.BlockSpec(shape, lambda *_: (0,) * len(shape))

    assert n_seq % LANES == 0 and cache_k.shape == (n_seq, KV_DIM, WINDOW)
    cache_spec = pl.BlockSpec((b_tile, KV_DIM, WINDOW), lambda i, *_: (i, 0, 0))
    return pl.pallas_call(
        functools.partial(_layer_b_sample_kernel, n_seq=n_seq, b_tile=b_tile),
        grid_spec=pltpu.PrefetchScalarGridSpec(
            num_scalar_prefetch=1,
            grid=(n_seq // b_tile,),
            in_specs=[const(h.shape), const(kv_norm.shape), const(norm_b.shape),
                      const(final_norm.shape), const(w_kv_t.shape), const(w_qg_t.shape),
                      const(w_out_t.shape), const(cos.shape), const(slo.shape), const(shi.shape),
                      cache_spec, cache_spec],
            out_specs=[const((n_seq, D_MODEL)), cache_spec, cache_spec],
            scratch_shapes=[pltpu.VMEM((n_seq, D_MODEL), F32),
                            pltpu.VMEM((n_seq, D_MODEL), F32),
                            pltpu.VMEM((n_seq, KV_DIM), F32),
                            pltpu.VMEM((n_seq, KV_DIM), F32),
                            pltpu.VMEM((KV_DIM, n_seq), F32),
                            pltpu.VMEM((KV_DIM, n_seq), F32),
                            pltpu.VMEM((n_seq, D_MODEL), F32)]),
        out_shape=[jax.ShapeDtypeStruct((n_seq, D_MODEL), F32),
                   jax.ShapeDtypeStruct(cache_k.shape, F32),
                   jax.ShapeDtypeStruct(cache_v.shape, F32)],
        compiler_params=pltpu.CompilerParams(
            dimension_semantics=("arbitrary",), vmem_limit_bytes=VMEM_LIMIT_BYTES),
        name="layer_b_sample",
    )(sinks, h, kv_norm, norm_b, final_norm, w_kv_t, w_qg_t, w_out_t, cos, slo, shi, cache_k, cache_v)


def kernel(x_prompt, x_sample, cache_k, cache_v, norm_a, w_in_a, v_norm_a, w_s_a, b_s_a, w_out_a,
           kv_norm, w_kv, norm_b, w_in_b, sinks_b, w_out_b, final_norm):
    batch, seq, _ = x_prompt.shape
    n_seq, dec_seq, _ = x_sample.shape
    assert dec_seq == 1 and seq % CHUNK == 0 and cache_k.shape[1] == WINDOW
    assert norm_a.shape[0] == 1 and norm_b.shape[0] == 1

    row = lambda g: g.reshape(1, -1)
    w_in_a16 = w_in_a[0].astype(BF16)
    w_out_a16 = w_out_a[0].astype(BF16)
    w_kv_t = _transpose_cast(w_kv)
    w_qg_t = _transpose_cast(w_in_b[0])
    w_out_b_t = _transpose_cast(w_out_b[0])
    bs_chunk = jnp.repeat(b_s_a[0].T, A_GROUP_DIM, axis=1)
    ws_one = jnp.repeat(w_s_a[0, :, 0, 0], A_GROUP_DIM).reshape(1, A_WIDTH)
    bs_one = jnp.repeat(b_s_a[0, :, 0], A_GROUP_DIM).reshape(1, A_WIDTH)

    h_p = _layer_a(x_prompt.reshape(batch * seq, D_MODEL), row(norm_a[0]), w_in_a16,
                   row(v_norm_a[0]), w_s_a[0], bs_chunk, w_out_a16, tile=A_TILE, chunked=True)[0]
    h_s, av_s = _layer_a(x_sample.reshape(n_seq, D_MODEL), row(norm_a[0]), w_in_a16,
                         row(v_norm_a[0]), ws_one, bs_one, w_out_a16, tile=n_seq, chunked=False)

    y_p, kt_p, vt_p = _layer_b_prompt(
        h_p, sinks_b[0], row(kv_norm), row(norm_b[0]), row(final_norm),
        w_kv_t, w_qg_t, w_out_b_t, batch=batch, seq=seq)
    def to_window(x_t):
        n = x_t.shape[0]
        return x_t.reshape(n, N_KV_HEADS, HEAD_DIM, WINDOW).transpose(0, 3, 1, 2)

    def from_window(x):
        return x.transpose(0, 2, 3, 1).reshape(x.shape[0], KV_DIM, WINDOW)

    y_s, kt_s, vt_s = _layer_b_sample(h_s, sinks_b[0], row(kv_norm), row(norm_b[0]),
                                      row(final_norm), w_kv_t, w_qg_t, w_out_b_t,
                                      from_window(cache_k), from_window(cache_v))

    return (y_p.reshape(batch, seq, D_MODEL),
            y_s.reshape(n_seq, 1, D_MODEL),
            to_window(kt_p),
            to_window(vt_p),
            to_window(kt_s),
            to_window(vt_s),
            av_s.reshape(1, n_seq, 1, A_WIDTH))
```

```python
import functools

import jax
import jax.numpy as jnp
from jax import lax
from jax.experimental import pallas as pl
from jax.experimental.pallas import tpu as pltpu

D_MODEL = 1024
PAST_LEN = 8192
CHUNK = 128
A_WIDTH = 2 * D_MODEL
A_GROUPS = 8
A_GROUP_DIM = A_WIDTH // A_GROUPS
HEAD_DIM = 64
N_HEADS = D_MODEL // HEAD_DIM
N_KV_HEADS = 4
GQA_GROUP = N_HEADS // N_KV_HEADS
KV_DIM = N_KV_HEADS * HEAD_DIM
WINDOW = 128
Q_BLOCK = 128
ROT_DIM = HEAD_DIM // 4
ROPE_THETA = 500000.0
EPS = 1e-5

LANES = 128
BF16_SUBLANES = 16
LOG2_E = 1.4426950408889634
Q_SCALE_LOG2 = HEAD_DIM ** -0.5 * LOG2_E
VMEM_LIMIT_BYTES = 56 * 1024 * 1024

A_TILE = 512
B_TILE = 512
B_SUBTILES = 2
PREP_COLS = 512
OUT_ROWS = 256
ATTN_AHEAD = 2
SAMPLE_B_TILE = 16
SAMPLE_GROUP = 8

F32 = jnp.float32
BF16 = jnp.bfloat16


def _rms(x, g):
    return x * lax.rsqrt(jnp.mean(x * x, axis=-1, keepdims=True) + EPS) * g


def _silu(x):
    return x * jax.nn.sigmoid(x)


def _dot(a, b):
    return jnp.dot(a, b, preferred_element_type=F32)


def _dot_nt(a, b):
    return lax.dot_general(a, b, (((1,), (1,)), ((), ())), preferred_element_type=F32)


def _zero_of(x):
    bits = pltpu.bitcast(x, jnp.uint32)
    return ((bits >> 16) >> 16).astype(F32)


def _member_major(head):
    kh, r = divmod(head, GQA_GROUP)
    start = (r * N_KV_HEADS + kh) * HEAD_DIM
    return slice(start, start + HEAD_DIM)


def _rotate(x, cos, sin_lo, sin_hi):
    return (x * cos + pltpu.roll(x, LANES - ROT_DIM // 2, 1) * sin_lo
            + pltpu.roll(x, ROT_DIM // 2, 1) * sin_hi)


def _rotary_tables(positions):
    lane = jnp.arange(LANES) % HEAD_DIM
    freq = (2 * (lane % (ROT_DIM // 2))).astype(F32)
    ang = positions[:, None] * (ROPE_THETA ** (-freq / ROT_DIM))[None, :]
    first = (lane < ROT_DIM // 2)[None, :]
    second = ((lane >= ROT_DIM // 2) & (lane < ROT_DIM))[None, :]
    cos = jnp.where(first | second, jnp.cos(ang), 1.0)
    sin_lo = jnp.where(first, -jnp.sin(ang), 0.0)
    sin_hi = jnp.where(second, jnp.sin(ang), 0.0)
    return cos, sin_lo, sin_hi


def _transpose_cast_kernel(w_ref, o_ref):
    o_ref[...] = w_ref[...].T.astype(BF16)


def _transpose_cast(w):
    k, n = w.shape
    return pl.pallas_call(
        _transpose_cast_kernel,
        grid=(n // PREP_COLS,),
        in_specs=[pl.BlockSpec((k, PREP_COLS), lambda j: (0, j))],
        out_specs=pl.BlockSpec((PREP_COLS, k), lambda j: (j, 0)),
        out_shape=jax.ShapeDtypeStruct((n, k), BF16),
        compiler_params=pltpu.CompilerParams(dimension_semantics=("arbitrary",)),
        name="transpose_cast",
    )(w)


def _layer_a_kernel(x_ref, ng_ref, win_ref, vg_ref, ws_ref, bs_ref, wout_ref, *out_refs,
                    tile, chunked):
    if chunked:
        h_ref, y_scr = out_refs
    else:
        h_ref, av_ref, y_scr = out_refs
    x = x_ref[...]
    xn = _rms(x, ng_ref[...]).astype(BF16)
    v = _rms(_dot(xn, win_ref[:, A_WIDTH:2 * A_WIDTH]), vg_ref[...])
    if chunked:
        vb = v.astype(BF16)
        row = lax.broadcasted_iota(jnp.int32, (CHUNK, CHUNK), 0)
        col = lax.broadcasted_iota(jnp.int32, (CHUNK, CHUNK), 1)
        tri = row >= col
    else:
        av_ref[...] = v
    width = 2 * A_GROUP_DIM
    for pair in range(A_GROUPS // 2):
        cols = slice(pair * width, (pair + 1) * width)
        u = _dot(xn, win_ref[:, cols])
        gate = _dot(xn, win_ref[:, 2 * A_WIDTH + pair * width:2 * A_WIDTH + (pair + 1) * width])
        if chunked:
            ws = [jnp.where(tri, ws_ref[g], 0.0).astype(BF16) for g in (2 * pair, 2 * pair + 1)]
            z = jnp.concatenate(
                [jnp.concatenate(
                    [_dot(ws[i], vb[c * CHUNK:(c + 1) * CHUNK,
                                    (2 * pair + i) * A_GROUP_DIM:(2 * pair + i + 1) * A_GROUP_DIM])
                     for i in range(2)], axis=1) + bs_ref[:, cols]
                 for c in range(tile // CHUNK)], axis=0)
        else:
            z = v[:, cols] * ws_ref[:, cols] + bs_ref[:, cols]
        y_scr[:, cols] = (u * z * _silu(gate)).astype(BF16)
    h_ref[...] = x + _dot(y_scr[...], wout_ref[...])


def _const_spec(shape):
    return pl.BlockSpec(shape, lambda *_: (0,) * len(shape), pipeline_mode=pl.Buffered(1))


def _layer_a(x, norm_g, w_in, v_norm_g, ws, bs, w_out, *, tile, chunked):
    n_tok = x.shape[0]
    tok_spec = pl.BlockSpec((tile, D_MODEL), lambda i: (i, 0))
    out_shape = [jax.ShapeDtypeStruct((n_tok, D_MODEL), F32)]
    out_specs = [tok_spec]
    if not chunked:
        out_shape.append(jax.ShapeDtypeStruct((n_tok, A_WIDTH), F32))
        out_specs.append(pl.BlockSpec((tile, A_WIDTH), lambda i: (i, 0)))
    return pl.pallas_call(
        functools.partial(_layer_a_kernel, tile=tile, chunked=chunked),
        grid=(n_tok // tile,),
        in_specs=[tok_spec, _const_spec(norm_g.shape), _const_spec(w_in.shape),
                  _const_spec(v_norm_g.shape), _const_spec(ws.shape), _const_spec(bs.shape),
                  _const_spec(w_out.shape)],
        out_specs=out_specs,
        out_shape=out_shape,
        scratch_shapes=[pltpu.VMEM((tile, A_WIDTH), BF16)],
        compiler_params=pltpu.CompilerParams(
            dimension_semantics=("arbitrary",), vmem_limit_bytes=VMEM_LIMIT_BYTES),
        name="layer_a_prompt" if chunked else "layer_a_sample",
    )(x, norm_g, w_in, v_norm_g, ws, bs, w_out)


def _layer_b_prompt_kernel(sinks_ref, h_ref, kvg_ref, nbg_ref, fg_ref, wkvt_ref, wqt_ref,
                           wgt_ref, woutt_ref, cost_ref, sint_ref, y_ref, kout_ref, vout_ref,
                           *scratch, tile, n_t, n_sub):
    for sub in range(n_sub):
        rows = pl.ds(sub * tile, tile)
        _layer_b_prompt_tile(
            sinks_ref, h_ref.at[rows, :], kvg_ref, nbg_ref, fg_ref, wkvt_ref, wqt_ref, wgt_ref,
            woutt_ref, cost_ref.at[:, rows], sint_ref.at[:, rows], y_ref.at[rows, :], kout_ref,
            vout_ref, *scratch, tile=tile, n_t=n_t, t=pl.program_id(1) * n_sub + sub,
            first_possible=sub == 0, last_possible=sub == n_sub - 1)


def _layer_b_prompt_tile(sinks_ref, h_ref, kvg_ref, nbg_ref, fg_ref, wkvt_ref, wqt_ref,
                         wgt_ref, woutt_ref, cost_ref, sint_ref,
                         y_ref, kout_ref, vout_ref,
                         kext_scr, vtext_scr, qt_scr, ogt_scr, bias_scr,
                         *, tile, n_t, t, first_possible, last_possible):
    n_keys = WINDOW + Q_BLOCK

    if first_possible:
        @pl.when((pl.program_id(0) == 0) & (t == 0))
        def _():
            j = lax.broadcasted_iota(jnp.int32, (n_keys, Q_BLOCK), 0)
            i = lax.broadcasted_iota(jnp.int32, (n_keys, Q_BLOCK), 1)
            band = (j >= i) & (j <= WINDOW + i)
            bias_scr[0] = jnp.where(band & (j >= WINDOW), 0.0, -jnp.inf)
            bias_scr[1] = jnp.where(band, 0.0, -jnp.inf)

        @pl.when(t == 0)
        def _():
            kext_scr[0:WINDOW, :] = jnp.zeros((WINDOW, KV_DIM), BF16)
            vtext_scr[:, 0:WINDOW] = jnp.zeros((KV_DIM, WINDOW), BF16)

    h = h_ref[...]
    hn = h * lax.rsqrt(jnp.mean(h * h, axis=-1, keepdims=True) + EPS)
    xkv = (hn * kvg_ref[...]).astype(BF16)
    xb = (hn * nbg_ref[...]).astype(BF16)

    cost, sint = cost_ref[...], sint_ref[...]
    half = ROT_DIM // 2

    def rotate_head(rows):
        lo, hi = rows[0:half, :], rows[half:ROT_DIM, :]
        return jnp.concatenate(
            [lo * cost - hi * sint, hi * cost + lo * sint, rows[ROT_DIM:, :]], axis=0)

    kvt = _dot_nt(wkvt_ref[...], xkv)
    kt = jnp.concatenate(
        [rotate_head(kvt[kh * HEAD_DIM:(kh + 1) * HEAD_DIM, :]) for kh in range(N_KV_HEADS)],
        axis=0)
    vt = kvt[KV_DIM:, :]
    kext_scr[WINDOW:, :] = kt.T.astype(BF16)
    vtext_scr[:, WINDOW:] = vt.astype(BF16)

    if last_possible:
        @pl.when(t == n_t - 1)
        def _():
            kout_ref[0] = kt[:, tile - WINDOW:]
            vout_ref[0] = vt[:, tile - WINDOW:]

    qt = _dot_nt(wqt_ref[...], xb)
    for hd in range(N_HEADS):
        rot = rotate_head(qt[hd * HEAD_DIM:(hd + 1) * HEAD_DIM, :])
        qt_scr[hd * HEAD_DIM:(hd + 1) * HEAD_DIM, :] = (rot * Q_SCALE_LOG2).astype(BF16)

    lane_head = lax.broadcasted_iota(jnp.int32, (1, GQA_GROUP * Q_BLOCK), 1) // Q_BLOCK
    zeros_half = jnp.zeros((HEAD_DIM, GQA_GROUP * Q_BLOCK), BF16)
    ones_rows = jnp.ones((BF16_SUBLANES, n_keys), BF16)

    def scores(qb, kh):
        qcols = slice(qb * Q_BLOCK, (qb + 1) * Q_BLOCK)
        keys = slice(qb * Q_BLOCK, qb * Q_BLOCK + n_keys)
        q4 = jnp.concatenate(
            [qt_scr[(kh * GQA_GROUP + r) * HEAD_DIM:(kh * GQA_GROUP + r + 1) * HEAD_DIM, qcols]
             for r in range(GQA_GROUP)], axis=1)
        q4 = jnp.concatenate([q4, zeros_half] if kh % 2 == 0 else [zeros_half, q4], axis=0)
        kblk = kext_scr[keys, (kh // 2) * LANES:(kh // 2 + 1) * LANES]
        return _dot(kblk, q4)

    def finish(qb, kh, s, s_ahead):
        qcols = slice(qb * Q_BLOCK, (qb + 1) * Q_BLOCK)
        keys = slice(qb * Q_BLOCK, qb * Q_BLOCK + n_keys)
        if first_possible and qb == 0:
            bias = bias_scr[jnp.where(t > 0, 1, 0)]
        else:
            bias = bias_scr[1]
        s = s + jnp.concatenate([bias] * GQA_GROUP, axis=1)
        sink = jnp.zeros((1, GQA_GROUP * Q_BLOCK), F32)
        for r in range(GQA_GROUP):
            sink = jnp.where(lane_head == r, sinks_ref[kh * GQA_GROUP + r] * LOG2_E, sink)
        m = jnp.maximum(jnp.max(s, axis=0, keepdims=True), sink)
        p = jnp.exp2(s - m)
        if s_ahead is not None:
            p = jnp.concatenate(
                [p[:n_keys - 8, :], p[n_keys - 8:, :] + _zero_of(s_ahead[0:8, :])], axis=0)
        vt_ones = jnp.concatenate(
            [vtext_scr[kh * HEAD_DIM:(kh + 1) * HEAD_DIM, keys], ones_rows], axis=0)
        ot = _dot(vt_ones, p.astype(BF16))
        denom = ot[HEAD_DIM:HEAD_DIM + 1, :] + jnp.exp2(sink - m)
        ot = ot[0:HEAD_DIM, :] * (1.0 / denom)
        for r in range(GQA_GROUP):
            hd = kh * GQA_GROUP + r
            ogt_scr[hd * HEAD_DIM:(hd + 1) * HEAD_DIM, qcols] = ot[:, r * Q_BLOCK:(r + 1) * Q_BLOCK]

    blocks = [(qb, kh) for qb in range(tile // Q_BLOCK) for kh in range(N_KV_HEADS)]
    pending = [scores(*blk) for blk in blocks[:ATTN_AHEAD]]
    for n, blk in enumerate(blocks):
        if n + ATTN_AHEAD < len(blocks):
            pending.append(scores(*blocks[n + ATTN_AHEAD]))
        s_cur = pending.pop(0)
        finish(*blk, s_cur, pending[-1] if pending else None)

    kext_scr[0:WINDOW, :] = kext_scr[tile:tile + WINDOW, :]
    vtext_scr[:, 0:WINDOW] = vtext_scr[:, tile:tile + WINDOW]

    ogt = jnp.concatenate(
        [(ogt_scr[rc * OUT_ROWS:(rc + 1) * OUT_ROWS, :]
          * _silu(_dot_nt(wgt_ref[rc * OUT_ROWS:(rc + 1) * OUT_ROWS, :], xb))).astype(BF16)
         for rc in range(D_MODEL // OUT_ROWS)], axis=0)
    h2 = jnp.concatenate(
        [h[:, rc * OUT_ROWS:(rc + 1) * OUT_ROWS]
         + _dot(woutt_ref[rc * OUT_ROWS:(rc + 1) * OUT_ROWS, :], ogt).T
         for rc in range(D_MODEL // OUT_ROWS)], axis=1)
    y_ref[...] = _rms(h2, fg_ref[...])


def _layer_b_prompt(h, sinks, kv_norm, norm_b, final_norm, w_kv_t, w_qg_t, w_out_t, *, batch, seq):
    tile = B_TILE
    n_t = seq // tile
    n_sub = B_SUBTILES
    step = n_sub * tile
    n_steps = n_t // n_sub
    pos = jnp.arange(seq, dtype=F32)
    inv = ROPE_THETA ** (-jnp.arange(0, ROT_DIM, 2, dtype=F32) / ROT_DIM)
    ang_t = inv[:, None] * pos[None, :]
    cos_t, sin_t = jnp.cos(ang_t), jnp.sin(ang_t)
    tok_spec = pl.BlockSpec((step, D_MODEL), lambda b, t, *_: (b * n_steps + t, 0))
    rot_t_spec = pl.BlockSpec((ROT_DIM // 2, step), lambda b, t, *_: (0, t))
    last_spec = pl.BlockSpec((1, KV_DIM, WINDOW), lambda b, t, *_: (b, 0, 0))

    def const(shape):
        return pl.BlockSpec(shape, lambda *_: (0,) * len(shape), pipeline_mode=pl.Buffered(1))

    def half(i):
        return pl.BlockSpec((D_MODEL, D_MODEL), lambda *_: (i, 0), pipeline_mode=pl.Buffered(1))

    return pl.pallas_call(
        functools.partial(_layer_b_prompt_kernel, tile=tile, n_t=n_t, n_sub=n_sub),
        grid_spec=pltpu.PrefetchScalarGridSpec(
            num_scalar_prefetch=1,
            grid=(batch, n_steps),
            in_specs=[tok_spec, const(kv_norm.shape), const(norm_b.shape), const(final_norm.shape),
                      const(w_kv_t.shape), half(0), half(1), const(w_out_t.shape),
                      rot_t_spec, rot_t_spec],
            out_specs=[tok_spec, last_spec, last_spec],
            scratch_shapes=[pltpu.VMEM((WINDOW + tile, KV_DIM), BF16),
                            pltpu.VMEM((KV_DIM, WINDOW + tile), BF16),
                            pltpu.VMEM((D_MODEL, tile), BF16),
                            pltpu.VMEM((D_MODEL, tile), F32),
                            pltpu.VMEM((2, WINDOW + Q_BLOCK, Q_BLOCK), F32)]),
        out_shape=[jax.ShapeDtypeStruct((batch * seq, D_MODEL), F32),
                   jax.ShapeDtypeStruct((batch, KV_DIM, WINDOW), F32),
                   jax.ShapeDtypeStruct((batch, KV_DIM, WINDOW), F32)],
        compiler_params=pltpu.CompilerParams(
            dimension_semantics=("arbitrary", "arbitrary"), vmem_limit_bytes=VMEM_LIMIT_BYTES),
        name="layer_b_prompt",
    )(sinks, h, kv_norm, norm_b, final_norm, w_kv_t, w_qg_t, w_qg_t, w_out_t, cos_t, sin_t)


def _layer_b_sample_kernel(sinks_ref, h_ref, kvg_ref, nbg_ref, fg_ref, wkvt_ref, wqgt_ref,
                           woutt_ref, cos_ref, slo_ref, shi_ref, ck_ref, cv_ref,
                           y_ref, kout_ref, vout_ref,
                           q_scr, gate_scr, knew_scr, vnew_scr, knewt_scr, vnewt_scr, o_scr,
                           *, n_seq, b_tile):
    step = pl.program_id(0)

    @pl.when(step == 0)
    def _():
        h = h_ref[...]
        hn = h * lax.rsqrt(jnp.mean(h * h, axis=-1, keepdims=True) + EPS)
        kv = _dot_nt((hn * kvg_ref[...]).astype(BF16), wkvt_ref[...])
        qg = _dot_nt((hn * nbg_ref[...]).astype(BF16), wqgt_ref[...])
        cos, slo, shi = cos_ref[...], slo_ref[...], shi_ref[...]
        for c in range(KV_DIM // LANES):
            cols = slice(c * LANES, (c + 1) * LANES)
            knew_scr[:, cols] = _rotate(kv[:, cols], cos, slo, shi)
        vnew_scr[...] = kv[:, KV_DIM:]
        knewt_scr[...] = knew_scr[...].T
        vnewt_scr[...] = kv[:, KV_DIM:].T
        for c in range(D_MODEL // LANES):
            cols = slice(c * LANES, (c + 1) * LANES)
            q2 = _rotate(qg[:, cols], cos, slo, shi) * HEAD_DIM ** -0.5
            g2 = qg[:, D_MODEL + c * LANES:D_MODEL + (c + 1) * LANES]
            for i in range(LANES // HEAD_DIM):
                dst = _member_major(c * (LANES // HEAD_DIM) + i)
                q_scr[:, dst] = q2[:, i * HEAD_DIM:(i + 1) * HEAD_DIM]
                gate_scr[:, dst] = g2[:, i * HEAD_DIM:(i + 1) * HEAD_DIM]

    n_rows = GQA_GROUP * N_KV_HEADS * SAMPLE_GROUP
    row = lax.broadcasted_iota(jnp.int32, (n_rows, 1), 0)
    row_kh = (row // SAMPLE_GROUP) % N_KV_HEADS
    row_seq = row % SAMPLE_GROUP
    lane_kh = lax.broadcasted_iota(jnp.int32, (1, KV_DIM), 1) // HEAD_DIM
    own = row_kh == lane_kh
    sink = jnp.zeros((n_rows, 1), F32)
    for r in range(GQA_GROUP):
        for kh in range(N_KV_HEADS):
            sink = jnp.where(row // SAMPLE_GROUP == r * N_KV_HEADS + kh,
                             sinks_ref[kh * GQA_GROUP + r], sink)
    is_last = lax.broadcasted_iota(jnp.int32, (KV_DIM, WINDOW), 1) == WINDOW - 1
    n_blk = GQA_GROUP * N_KV_HEADS

    def group(i, carry):
        b0 = i * SAMPLE_GROUP
        g0 = pl.multiple_of(step * b_tile + b0, SAMPLE_GROUP)
        seqs = pl.ds(g0, SAMPLE_GROUP)
        q8 = q_scr[seqs, :]
        qexp = jnp.concatenate(
            [q8[:, r * KV_DIM:(r + 1) * KV_DIM] for r in range(GQA_GROUP)
             for _ in range(N_KV_HEADS)], axis=0)
        qexp = jnp.where(own, qexp, 0.0).astype(BF16)
        knew8 = knew_scr[seqs, :].astype(BF16).astype(F32)
        vnew8 = vnew_scr[seqs, :].astype(BF16).astype(F32)
        s_new = jnp.sum(qexp.astype(F32) * jnp.concatenate([knew8] * n_blk, axis=0),
                        axis=1, keepdims=True)
        s_old = jnp.zeros((n_rows, WINDOW), F32)
        for b in range(SAMPLE_GROUP):
            s_b = _dot(qexp, ck_ref[b0 + b].astype(BF16))
            s_old = jnp.where(row_seq == b, s_b, s_old)
        m = jnp.maximum(jnp.maximum(jnp.max(s_old, axis=1, keepdims=True), s_new), sink)
        p_old = jnp.exp(s_old - m)
        p_new = jnp.exp(s_new - m)
        denom = jnp.sum(p_old, axis=1, keepdims=True) + p_new + jnp.exp(sink - m)
        p_old = p_old.astype(BF16)
        o = jnp.zeros((n_rows, KV_DIM), F32)
        for b in range(SAMPLE_GROUP):
            o_b = _dot_nt(p_old, cv_ref[b0 + b].astype(BF16))
            o = jnp.where(row_seq == b, o_b, o)
        o = (o + p_new.astype(BF16).astype(F32) * jnp.concatenate([vnew8] * n_blk, axis=0)) / denom
        o = jnp.where(own, o, 0.0)
        for r in range(GQA_GROUP):
            blks = [o[(r * N_KV_HEADS + kh) * SAMPLE_GROUP:(r * N_KV_HEADS + kh + 1) * SAMPLE_GROUP]
                    for kh in range(N_KV_HEADS)]
            o_scr[seqs, r * KV_DIM:(r + 1) * KV_DIM] = (blks[0] + blks[1]) + (blks[2] + blks[3])
        for b in range(SAMPLE_GROUP):
            g = g0 + b
            blk = pl.ds(pl.multiple_of((g // LANES) * LANES, LANES), LANES)
            to_last = LANES - 1 - g % LANES
            kout_ref[b0 + b] = jnp.where(is_last, pltpu.roll(knewt_scr[:, blk], to_last, 1),
                                         pltpu.roll(ck_ref[b0 + b], WINDOW - 1, 1))
            vout_ref[b0 + b] = jnp.where(is_last, pltpu.roll(vnewt_scr[:, blk], to_last, 1),
                                         pltpu.roll(cv_ref[b0 + b], WINDOW - 1, 1))
        return carry

    lax.fori_loop(0, b_tile // SAMPLE_GROUP, group, 0)

    @pl.when(step == pl.num_programs(0) - 1)
    def _():
        og_mm = o_scr[...] * _silu(gate_scr[...])
        og = jnp.concatenate(
            [og_mm[:, _member_major(hd)] for hd in range(N_HEADS)], axis=1).astype(BF16)
        h2 = h_ref[...] + _dot_nt(og, woutt_ref[...])
        y_ref[...] = _rms(h2, fg_ref[...])


def _layer_b_sample(h, sinks, kv_norm, norm_b, final_norm, w_kv_t, w_qg_t, w_out_t, cache_k, cache_v):
    n_seq = h.shape[0]
    b_tile = SAMPLE_B_TILE
    cos, slo, shi = _rotary_tables(jnp.full((1,), PAST_LEN, F32))

    def const(shape):
        return pl.BlockSpec(shape, lambda *_: (0,) * len(shape))

    assert n_seq % LANES == 0 and cache_k.shape == (n_seq, KV_DIM, WINDOW)
    cache_spec = pl.BlockSpec((b_tile, KV_DIM, WINDOW), lambda i, *_: (i, 0, 0))
    return pl.pallas_call(
        functools.partial(_layer_b_sample_kernel, n_seq=n_seq, b_tile=b_tile),
        grid_spec=pltpu.PrefetchScalarGridSpec(
            num_scalar_prefetch=1,
            grid=(n_seq // b_tile,),
            in_specs=[const(h.shape), const(kv_norm.shape), const(norm_b.shape),
                      const(final_norm.shape), const(w_kv_t.shape), const(w_qg_t.shape),
                      const(w_out_t.shape), const(cos.shape), const(slo.shape), const(shi.shape),
                      cache_spec, cache_spec],
            out_specs=[const((n_seq, D_MODEL)), cache_spec, cache_spec],
            scratch_shapes=[pltpu.VMEM((n_seq, D_MODEL), F32),
                            pltpu.VMEM((n_seq, D_MODEL), F32),
                            pltpu.VMEM((n_seq, KV_DIM), F32),
                            pltpu.VMEM((n_seq, KV_DIM), F32),
                            pltpu.VMEM((KV_DIM, n_seq), F32),
                            pltpu.VMEM((KV_DIM, n_seq), F32),
                            pltpu.VMEM((n_seq, D_MODEL), F32)]),
        out_shape=[jax.ShapeDtypeStruct((n_seq, D_MODEL), F32),
                   jax.ShapeDtypeStruct(cache_k.shape, F32),
                   jax.ShapeDtypeStruct(cache_v.shape, F32)],
        compiler_params=pltpu.CompilerParams(
            dimension_semantics=("arbitrary",), vmem_limit_bytes=VMEM_LIMIT_BYTES),
        name="layer_b_sample",
    )(sinks, h, kv_norm, norm_b, final_norm, w_kv_t, w_qg_t, w_out_t, cos, slo, shi, cache_k, cache_v)


def kernel(x_prompt, x_sample, cache_k, cache_v, norm_a, w_in_a, v_norm_a, w_s_a, b_s_a, w_out_a,
           kv_norm, w_kv, norm_b, w_in_b, sinks_b, w_out_b, final_norm):
    batch, seq, _ = x_prompt.shape
    n_seq, dec_seq, _ = x_sample.shape
    assert dec_seq == 1 and seq % CHUNK == 0 and cache_k.shape[1] == WINDOW
    assert norm_a.shape[0] == 1 and norm_b.shape[0] == 1

    row = lambda g: g.reshape(1, -1)
    w_in_a16 = w_in_a[0].astype(BF16)
    w_out_a16 = w_out_a[0].astype(BF16)
    w_kv_t = _transpose_cast(w_kv)
    w_qg_t = _transpose_cast(w_in_b[0])
    w_out_b_t = _transpose_cast(w_out_b[0])
    bs_chunk = jnp.repeat(b_s_a[0].T, A_GROUP_DIM, axis=1)
    ws_one = jnp.repeat(w_s_a[0, :, 0, 0], A_GROUP_DIM).reshape(1, A_WIDTH)
    bs_one = jnp.repeat(b_s_a[0, :, 0], A_GROUP_DIM).reshape(1, A_WIDTH)

    h_p = _layer_a(x_prompt.reshape(batch * seq, D_MODEL), row(norm_a[0]), w_in_a16,
                   row(v_norm_a[0]), w_s_a[0], bs_chunk, w_out_a16, tile=A_TILE, chunked=True)[0]
    h_s, av_s = _layer_a(x_sample.reshape(n_seq, D_MODEL), row(norm_a[0]), w_in_a16,
                         row(v_norm_a[0]), ws_one, bs_one, w_out_a16, tile=n_seq, chunked=False)

    y_p, kt_p, vt_p = _layer_b_prompt(
        h_p, sinks_b[0], row(kv_norm), row(norm_b[0]), row(final_norm),
        w_kv_t, w_qg_t, w_out_b_t, batch=batch, seq=seq)
    def to_window(x_t):
        n = x_t.shape[0]
        return x_t.reshape(n, N_KV_HEADS, HEAD_DIM, WINDOW).transpose(0, 3, 1, 2)

    def from_window(x):
        return x.transpose(0, 2, 3, 1).reshape(x.shape[0], KV_DIM, WINDOW)

    y_s, kt_s, vt_s = _layer_b_sample(h_s, sinks_b[0], row(kv_norm), row(norm_b[0]),
                                      row(final_norm), w_kv_t, w_qg_t, w_out_b_t,
                                      from_window(cache_k), from_window(cache_v))

    return (y_p.reshape(batch, seq, D_MODEL),
            y_s.reshape(n_seq, 1, D_MODEL),
            to_window(kt_p),
            to_window(vt_p),
            to_window(kt_s),
            to_window(vt_s),
            av_s.reshape(1, n_seq, 1, A_WIDTH))
```

```python
import functools

import jax
import jax.numpy as jnp
from jax import lax
from jax.experimental import pallas as pl
from jax.experimental.pallas import tpu as pltpu

D_MODEL = 1024
PAST_LEN = 8192
CHUNK = 128
A_WIDTH = 2 * D_MODEL
A_GROUPS = 8
A_GROUP_DIM = A_WIDTH // A_GROUPS
HEAD_DIM = 64
N_HEADS = D_MODEL // HEAD_DIM
N_KV_HEADS = 4
GQA_GROUP = N_HEADS // N_KV_HEADS
KV_DIM = N_KV_HEADS * HEAD_DIM
WINDOW = 128
Q_BLOCK = 128
ROT_DIM = HEAD_DIM // 4
ROPE_THETA = 500000.0
EPS = 1e-5

LANES = 128
BF16_SUBLANES = 16
LOG2_E = 1.4426950408889634
Q_SCALE_LOG2 = HEAD_DIM ** -0.5 * LOG2_E
VMEM_LIMIT_BYTES = 56 * 1024 * 1024

A_TILE = 512
A_SUBTILES = 2
B_TILE = 512
B_SUBTILES = 2
PREP_COLS = 512
OUT_ROWS = 256
ATTN_AHEAD = 2
SAMPLE_B_TILE = 8
SAMPLE_GROUP = 8

F32 = jnp.float32
BF16 = jnp.bfloat16


def _rms(x, g):
    return x * lax.rsqrt(jnp.mean(x * x, axis=-1, keepdims=True) + EPS) * g


def _silu(x):
    return x * jax.nn.sigmoid(x)


def _dot(a, b):
    return jnp.dot(a, b, preferred_element_type=F32)


def _dot_nt(a, b):
    return lax.dot_general(a, b, (((1,), (1,)), ((), ())), preferred_element_type=F32)


def _zero_of(x):
    bits = pltpu.bitcast(x, jnp.uint32)
    return ((bits >> 16) >> 16).astype(F32)


def _member_major(head):
    kh, r = divmod(head, GQA_GROUP)
    start = (r * N_KV_HEADS + kh) * HEAD_DIM
    return slice(start, start + HEAD_DIM)


def _rotate(x, cos, sin_lo, sin_hi):
    return (x * cos + pltpu.roll(x, LANES - ROT_DIM // 2, 1) * sin_lo
            + pltpu.roll(x, ROT_DIM // 2, 1) * sin_hi)


def _rotary_tables(positions):
    lane = jnp.arange(LANES) % HEAD_DIM
    freq = (2 * (lane % (ROT_DIM // 2))).astype(F32)
    ang = positions[:, None] * (ROPE_THETA ** (-freq / ROT_DIM))[None, :]
    first = (lane < ROT_DIM // 2)[None, :]
    second = ((lane >= ROT_DIM // 2) & (lane < ROT_DIM))[None, :]
    cos = jnp.where(first | second, jnp.cos(ang), 1.0)
    sin_lo = jnp.where(first, -jnp.sin(ang), 0.0)
    sin_hi = jnp.where(second, jnp.sin(ang), 0.0)
    return cos, sin_lo, sin_hi


def _transpose_cast_kernel(w_ref, o_ref):
    o_ref[...] = w_ref[...].T.astype(BF16)


def _transpose_cast(w):
    k, n = w.shape
    return pl.pallas_call(
        _transpose_cast_kernel,
        grid=(n // PREP_COLS,),
        in_specs=[pl.BlockSpec((k, PREP_COLS), lambda j: (0, j))],
        out_specs=pl.BlockSpec((PREP_COLS, k), lambda j: (j, 0)),
        out_shape=jax.ShapeDtypeStruct((n, k), BF16),
        compiler_params=pltpu.CompilerParams(dimension_semantics=("arbitrary",)),
        name="transpose_cast",
    )(w)


def _layer_a_kernel(x_ref, ng_ref, win_ref, vg_ref, ws_ref, bs_ref, wout_ref, *out_refs,
                    tile, chunked, n_sub):
    for sub in range(n_sub):
        rows = pl.ds(sub * tile, tile)
        views = [ref.at[rows, :] for ref in out_refs[:-1]] + [out_refs[-1].at[sub]]
        _layer_a_tile(x_ref.at[rows, :], ng_ref, win_ref, vg_ref, ws_ref, bs_ref, wout_ref,
                      *views, tile=tile, chunked=chunked)


def _layer_a_tile(x_ref, ng_ref, win_ref, vg_ref, ws_ref, bs_ref, wout_ref, *out_refs,
                  tile, chunked):
    if chunked:
        h_ref, y_scr = out_refs
    else:
        h_ref, av_ref, y_scr = out_refs
    x = x_ref[...]
    xn = _rms(x, ng_ref[...]).astype(BF16)
    v = _rms(_dot(xn, win_ref[:, A_WIDTH:2 * A_WIDTH]), vg_ref[...])
    if chunked:
        vb = v.astype(BF16)
        row = lax.broadcasted_iota(jnp.int32, (CHUNK, CHUNK), 0)
        col = lax.broadcasted_iota(jnp.int32, (CHUNK, CHUNK), 1)
        tri = row >= col
    else:
        av_ref[...] = v
    width = 2 * A_GROUP_DIM
    for pair in range(A_GROUPS // 2):
        cols = slice(pair * width, (pair + 1) * width)
        u = _dot(xn, win_ref[:, cols])
        gate = _dot(xn, win_ref[:, 2 * A_WIDTH + pair * width:2 * A_WIDTH + (pair + 1) * width])
        if chunked:
            ws = [jnp.where(tri, ws_ref[g], 0.0).astype(BF16) for g in (2 * pair, 2 * pair + 1)]
            z = jnp.concatenate(
                [jnp.concatenate(
                    [_dot(ws[i], vb[c * CHUNK:(c + 1) * CHUNK,
                                    (2 * pair + i) * A_GROUP_DIM:(2 * pair + i + 1) * A_GROUP_DIM])
                     for i in range(2)], axis=1) + bs_ref[:, cols]
                 for c in range(tile // CHUNK)], axis=0)
        else:
            z = v[:, cols] * ws_ref[:, cols] + bs_ref[:, cols]
        y_scr[:, cols] = (u * z * _silu(gate)).astype(BF16)
    h_ref[...] = x + _dot(y_scr[...], wout_ref[...])


def _const_spec(shape):
    return pl.BlockSpec(shape, lambda *_: (0,) * len(shape), pipeline_mode=pl.Buffered(1))


def _layer_a(x, norm_g, w_in, v_norm_g, ws, bs, w_out, *, tile, n_sub, chunked):
    n_tok = x.shape[0]
    step = n_sub * tile
    tok_spec = pl.BlockSpec((step, D_MODEL), lambda i: (i, 0))
    out_shape = [jax.ShapeDtypeStruct((n_tok, D_MODEL), F32)]
    out_specs = [tok_spec]
    if not chunked:
        out_shape.append(jax.ShapeDtypeStruct((n_tok, A_WIDTH), F32))
        out_specs.append(pl.BlockSpec((step, A_WIDTH), lambda i: (i, 0)))
    return pl.pallas_call(
        functools.partial(_layer_a_kernel, tile=tile, chunked=chunked, n_sub=n_sub),
        grid=(n_tok // step,),
        in_specs=[tok_spec, _const_spec(norm_g.shape), _const_spec(w_in.shape),
                  _const_spec(v_norm_g.shape), _const_spec(ws.shape), _const_spec(bs.shape),
                  _const_spec(w_out.shape)],
        out_specs=out_specs,
        out_shape=out_shape,
        scratch_shapes=[pltpu.VMEM((n_sub, tile, A_WIDTH), BF16)],
        compiler_params=pltpu.CompilerParams(
            dimension_semantics=("arbitrary",), vmem_limit_bytes=VMEM_LIMIT_BYTES),
        name="layer_a_prompt" if chunked else "layer_a_sample",
    )(x, norm_g, w_in, v_norm_g, ws, bs, w_out)


def _layer_b_prompt_kernel(sinks_ref, h_ref, kvg_ref, nbg_ref, fg_ref, wkvt_ref, wqt_ref,
                           wgt_ref, woutt_ref, cost_ref, sint_ref, y_ref, kout_ref, vout_ref,
                           *scratch, tile, n_t, n_sub):
    for sub in range(n_sub):
        rows = pl.ds(sub * tile, tile)
        _layer_b_prompt_tile(
            sinks_ref, h_ref.at[rows, :], kvg_ref, nbg_ref, fg_ref, wkvt_ref, wqt_ref, wgt_ref,
            woutt_ref, cost_ref.at[:, rows], sint_ref.at[:, rows], y_ref.at[rows, :], kout_ref,
            vout_ref, *scratch, tile=tile, n_t=n_t, t=pl.program_id(1) * n_sub + sub,
            first_possible=sub == 0, last_possible=sub == n_sub - 1)


def _layer_b_prompt_tile(sinks_ref, h_ref, kvg_ref, nbg_ref, fg_ref, wkvt_ref, wqt_ref,
                         wgt_ref, woutt_ref, cost_ref, sint_ref,
                         y_ref, kout_ref, vout_ref,
                         kext_scr, vtext_scr, qt_scr, ogt_scr, bias_scr,
                         *, tile, n_t, t, first_possible, last_possible):
    n_keys = WINDOW + Q_BLOCK

    if first_possible:
        @pl.when((pl.program_id(0) == 0) & (t == 0))
        def _():
            j = lax.broadcasted_iota(jnp.int32, (n_keys, Q_BLOCK), 0)
            i = lax.broadcasted_iota(jnp.int32, (n_keys, Q_BLOCK), 1)
            band = (j >= i) & (j <= WINDOW + i)
            bias_scr[0] = jnp.where(band & (j >= WINDOW), 0.0, -jnp.inf)
            bias_scr[1] = jnp.where(band, 0.0, -jnp.inf)

        @pl.when(t == 0)
        def _():
            kext_scr[0:WINDOW, :] = jnp.zeros((WINDOW, KV_DIM), BF16)
            vtext_scr[:, 0:WINDOW] = jnp.zeros((KV_DIM, WINDOW), BF16)

    h = h_ref[...]
    hn = h * lax.rsqrt(jnp.mean(h * h, axis=-1, keepdims=True) + EPS)
    xkv = (hn * kvg_ref[...]).astype(BF16)
    xb = (hn * nbg_ref[...]).astype(BF16)

    cost, sint = cost_ref[...], sint_ref[...]
    half = ROT_DIM // 2

    def rotate_head(rows):
        lo, hi = rows[0:half, :], rows[half:ROT_DIM, :]
        return jnp.concatenate(
            [lo * cost - hi * sint, hi * cost + lo * sint, rows[ROT_DIM:, :]], axis=0)

    kvt = _dot_nt(wkvt_ref[...], xkv)
    kt = jnp.concatenate(
        [rotate_head(kvt[kh * HEAD_DIM:(kh + 1) * HEAD_DIM, :]) for kh in range(N_KV_HEADS)],
        axis=0)
    vt = kvt[KV_DIM:, :]
    kext_scr[WINDOW:, :] = kt.T.astype(BF16)
    vtext_scr[:, WINDOW:] = vt.astype(BF16)

    if last_possible:
        @pl.when(t == n_t - 1)
        def _():
            kout_ref[0] = kt[:, tile - WINDOW:]
            vout_ref[0] = vt[:, tile - WINDOW:]

    qt = _dot_nt(wqt_ref[...], xb)
    for hd in range(N_HEADS):
        rot = rotate_head(qt[hd * HEAD_DIM:(hd + 1) * HEAD_DIM, :])
        qt_scr[hd * HEAD_DIM:(hd + 1) * HEAD_DIM, :] = (rot * Q_SCALE_LOG2).astype(BF16)

    lane_head = lax.broadcasted_iota(jnp.int32, (1, GQA_GROUP * Q_BLOCK), 1) // Q_BLOCK
    zeros_half = jnp.zeros((HEAD_DIM, GQA_GROUP * Q_BLOCK), BF16)
    ones_rows = jnp.ones((BF16_SUBLANES, n_keys), BF16)

    def scores(qb, kh):
        qcols = slice(qb * Q_BLOCK, (qb + 1) * Q_BLOCK)
        keys = slice(qb * Q_BLOCK, qb * Q_BLOCK + n_keys)
        q4 = jnp.concatenate(
            [qt_scr[(kh * GQA_GROUP + r) * HEAD_DIM:(kh * GQA_GROUP + r + 1) * HEAD_DIM, qcols]
             for r in range(GQA_GROUP)], axis=1)
        q4 = jnp.concatenate([q4, zeros_half] if kh % 2 == 0 else [zeros_half, q4], axis=0)
        kblk = kext_scr[keys, (kh // 2) * LANES:(kh // 2 + 1) * LANES]
        return _dot(kblk, q4)

    def finish(qb, kh, s, s_ahead):
        qcols = slice(qb * Q_BLOCK, (qb + 1) * Q_BLOCK)
        keys = slice(qb * Q_BLOCK, qb * Q_BLOCK + n_keys)
        if first_possible and qb == 0:
            bias = bias_scr[jnp.where(t > 0, 1, 0)]
        else:
            bias = bias_scr[1]
        s = s + jnp.concatenate([bias] * GQA_GROUP, axis=1)
        sink = jnp.zeros((1, GQA_GROUP * Q_BLOCK), F32)
        for r in range(GQA_GROUP):
            sink = jnp.where(lane_head == r, sinks_ref[kh * GQA_GROUP + r] * LOG2_E, sink)
        m = jnp.maximum(jnp.max(s, axis=0, keepdims=True), sink)
        p = jnp.exp2(s - m)
        if s_ahead is not None:
            p = jnp.concatenate(
                [p[:n_keys - 8, :], p[n_keys - 8:, :] + _zero_of(s_ahead[0:8, :])], axis=0)
        vt_ones = jnp.concatenate(
            [vtext_scr[kh * HEAD_DIM:(kh + 1) * HEAD_DIM, keys], ones_rows], axis=0)
        ot = _dot(vt_ones, p.astype(BF16))
        denom = ot[HEAD_DIM:HEAD_DIM + 1, :] + jnp.exp2(sink - m)
        ot = ot[0:HEAD_DIM, :] * (1.0 / denom)
        for r in range(GQA_GROUP):
            hd = kh * GQA_GROUP + r
            ogt_scr[hd * HEAD_DIM:(hd + 1) * HEAD_DIM, qcols] = ot[:, r * Q_BLOCK:(r + 1) * Q_BLOCK]

    blocks = [(qb, kh) for qb in range(tile // Q_BLOCK) for kh in range(N_KV_HEADS)]
    pending = [scores(*blk) for blk in blocks[:ATTN_AHEAD]]
    for n, blk in enumerate(blocks):
        if n + ATTN_AHEAD < len(blocks):
            pending.append(scores(*blocks[n + ATTN_AHEAD]))
        s_cur = pending.pop(0)
        finish(*blk, s_cur, pending[-1] if pending else None)

    kext_scr[0:WINDOW, :] = kext_scr[tile:tile + WINDOW, :]
    vtext_scr[:, 0:WINDOW] = vtext_scr[:, tile:tile + WINDOW]

    ogt = jnp.concatenate(
        [(ogt_scr[rc * OUT_ROWS:(rc + 1) * OUT_ROWS, :]
          * _silu(_dot_nt(wgt_ref[rc * OUT_ROWS:(rc + 1) * OUT_ROWS, :], xb))).astype(BF16)
         for rc in range(D_MODEL // OUT_ROWS)], axis=0)
    h2 = jnp.concatenate(
        [h[:, rc * OUT_ROWS:(rc + 1) * OUT_ROWS]
         + _dot(woutt_ref[rc * OUT_ROWS:(rc + 1) * OUT_ROWS, :], ogt).T
         for rc in range(D_MODEL // OUT_ROWS)], axis=1)
    y_ref[...] = _rms(h2, fg_ref[...])


def _layer_b_prompt(h, sinks, kv_norm, norm_b, final_norm, w_kv_t, w_qg_t, w_out_t, *, batch, seq):
    tile = B_TILE
    n_t = seq // tile
    n_sub = B_SUBTILES
    step = n_sub * tile
    n_steps = n_t // n_sub
    pos = jnp.arange(seq, dtype=F32)
    inv = ROPE_THETA ** (-jnp.arange(0, ROT_DIM, 2, dtype=F32) / ROT_DIM)
    ang_t = inv[:, None] * pos[None, :]
    cos_t, sin_t = jnp.cos(ang_t), jnp.sin(ang_t)
    tok_spec = pl.BlockSpec((step, D_MODEL), lambda b, t, *_: (b * n_steps + t, 0))
    rot_t_spec = pl.BlockSpec((ROT_DIM // 2, step), lambda b, t, *_: (0, t))
    last_spec = pl.BlockSpec((1, KV_DIM, WINDOW), lambda b, t, *_: (b, 0, 0))

    def const(shape):
        return pl.BlockSpec(shape, lambda *_: (0,) * len(shape), pipeline_mode=pl.Buffered(1))

    def half(i):
        return pl.BlockSpec((D_MODEL, D_MODEL), lambda *_: (i, 0), pipeline_mode=pl.Buffered(1))

    return pl.pallas_call(
        functools.partial(_layer_b_prompt_kernel, tile=tile, n_t=n_t, n_sub=n_sub),
        grid_spec=pltpu.PrefetchScalarGridSpec(
            num_scalar_prefetch=1,
            grid=(batch, n_steps),
            in_specs=[tok_spec, const(kv_norm.shape), const(norm_b.shape), const(final_norm.shape),
                      const(w_kv_t.shape), half(0), half(1), const(w_out_t.shape),
                      rot_t_spec, rot_t_spec],
            out_specs=[tok_spec, last_spec, last_spec],
            scratch_shapes=[pltpu.VMEM((WINDOW + tile, KV_DIM), BF16),
                            pltpu.VMEM((KV_DIM, WINDOW + tile), BF16),
                            pltpu.VMEM((D_MODEL, tile), BF16),
                            pltpu.VMEM((D_MODEL, tile), F32),
                            pltpu.VMEM((2, WINDOW + Q_BLOCK, Q_BLOCK), F32)]),
        out_shape=[jax.ShapeDtypeStruct((batch * seq, D_MODEL), F32),
                   jax.ShapeDtypeStruct((batch, KV_DIM, WINDOW), F32),
                   jax.ShapeDtypeStruct((batch, KV_DIM, WINDOW), F32)],
        compiler_params=pltpu.CompilerParams(
            dimension_semantics=("arbitrary", "arbitrary"), vmem_limit_bytes=VMEM_LIMIT_BYTES),
        name="layer_b_prompt",
    )(sinks, h, kv_norm, norm_b, final_norm, w_kv_t, w_qg_t, w_qg_t, w_out_t, cos_t, sin_t)


def _layer_b_sample_kernel(sinks_ref, h_ref, kvg_ref, nbg_ref, fg_ref, wkvt_ref, wqgt_ref,
                           woutt_ref, cos_ref, slo_ref, shi_ref, ck_ref, cv_ref,
                           y_ref, kout_ref, vout_ref,
                           q_scr, gate_scr, knew_scr, vnew_scr, knewt_scr, vnewt_scr, o_scr,
                           *, n_seq, b_tile):
    step = pl.program_id(0)

    @pl.when(step == 0)
    def _():
        h = h_ref[...]
        hn = h * lax.rsqrt(jnp.mean(h * h, axis=-1, keepdims=True) + EPS)
        kv = _dot_nt((hn * kvg_ref[...]).astype(BF16), wkvt_ref[...])
        qg = _dot_nt((hn * nbg_ref[...]).astype(BF16), wqgt_ref[...])
        cos, slo, shi = cos_ref[...], slo_ref[...], shi_ref[...]
        for c in range(KV_DIM // LANES):
            cols = slice(c * LANES, (c + 1) * LANES)
            knew_scr[:, cols] = _rotate(kv[:, cols], cos, slo, shi)
        vnew_scr[...] = kv[:, KV_DIM:]
        knewt_scr[...] = knew_scr[...].T
        vnewt_scr[...] = kv[:, KV_DIM:].T
        for c in range(D_MODEL // LANES):
            cols = slice(c * LANES, (c + 1) * LANES)
            q2 = _rotate(qg[:, cols], cos, slo, shi) * HEAD_DIM ** -0.5
            g2 = qg[:, D_MODEL + c * LANES:D_MODEL + (c + 1) * LANES]
            for i in range(LANES // HEAD_DIM):
                dst = _member_major(c * (LANES // HEAD_DIM) + i)
                q_scr[:, dst] = q2[:, i * HEAD_DIM:(i + 1) * HEAD_DIM]
                gate_scr[:, dst] = g2[:, i * HEAD_DIM:(i + 1) * HEAD_DIM]

    n_rows = GQA_GROUP * N_KV_HEADS * SAMPLE_GROUP
    row = lax.broadcasted_iota(jnp.int32, (n_rows, 1), 0)
    row_kh = (row // SAMPLE_GROUP) % N_KV_HEADS
    row_seq = row % SAMPLE_GROUP
    lane_kh = lax.broadcasted_iota(jnp.int32, (1, KV_DIM), 1) // HEAD_DIM
    own = row_kh == lane_kh
    sink = jnp.zeros((n_rows, 1), F32)
    for r in range(GQA_GROUP):
        for kh in range(N_KV_HEADS):
            sink = jnp.where(row // SAMPLE_GROUP == r * N_KV_HEADS + kh,
                             sinks_ref[kh * GQA_GROUP + r], sink)
    is_last = lax.broadcasted_iota(jnp.int32, (KV_DIM, WINDOW), 1) == WINDOW - 1
    n_blk = GQA_GROUP * N_KV_HEADS

    def group(i, carry):
        b0 = i * SAMPLE_GROUP
        g0 = pl.multiple_of(step * b_tile + b0, SAMPLE_GROUP)
        seqs = pl.ds(g0, SAMPLE_GROUP)
        q8 = q_scr[seqs, :]
        qexp = jnp.concatenate(
            [q8[:, r * KV_DIM:(r + 1) * KV_DIM] for r in range(GQA_GROUP)
             for _ in range(N_KV_HEADS)], axis=0)
        qexp = jnp.where(own, qexp, 0.0).astype(BF16)
        knew8 = knew_scr[seqs, :].astype(BF16).astype(F32)
        vnew8 = vnew_scr[seqs, :].astype(BF16).astype(F32)
        s_new = jnp.sum(qexp.astype(F32) * jnp.concatenate([knew8] * n_blk, axis=0),
                        axis=1, keepdims=True)
        s_old = jnp.zeros((n_rows, WINDOW), F32)
        for b in range(SAMPLE_GROUP):
            s_b = _dot(qexp, ck_ref[b0 + b].astype(BF16))
            s_old = jnp.where(row_seq == b, s_b, s_old)
        m = jnp.maximum(jnp.maximum(jnp.max(s_old, axis=1, keepdims=True), s_new), sink)
        p_old = jnp.exp(s_old - m)
        p_new = jnp.exp(s_new - m)
        denom = jnp.sum(p_old, axis=1, keepdims=True) + p_new + jnp.exp(sink - m)
        p_old = p_old.astype(BF16)
        o = jnp.zeros((n_rows, KV_DIM), F32)
        for b in range(SAMPLE_GROUP):
            o_b = _dot_nt(p_old, cv_ref[b0 + b].astype(BF16))
            o = jnp.where(row_seq == b, o_b, o)
        o = (o + p_new.astype(BF16).astype(F32) * jnp.concatenate([vnew8] * n_blk, axis=0)) / denom
        o = jnp.where(own, o, 0.0)
        for r in range(GQA_GROUP):
            blks = [o[(r * N_KV_HEADS + kh) * SAMPLE_GROUP:(r * N_KV_HEADS + kh + 1) * SAMPLE_GROUP]
                    for kh in range(N_KV_HEADS)]
            o_scr[seqs, r * KV_DIM:(r + 1) * KV_DIM] = (blks[0] + blks[1]) + (blks[2] + blks[3])
        for b in range(SAMPLE_GROUP):
            g = g0 + b
            blk = pl.ds(pl.multiple_of((g // LANES) * LANES, LANES), LANES)
            to_last = LANES - 1 - g % LANES
            kout_ref[b0 + b] = jnp.where(is_last, pltpu.roll(knewt_scr[:, blk], to_last, 1),
                                         pltpu.roll(ck_ref[b0 + b], WINDOW - 1, 1))
            vout_ref[b0 + b] = jnp.where(is_last, pltpu.roll(vnewt_scr[:, blk], to_last, 1),
                                         pltpu.roll(cv_ref[b0 + b], WINDOW - 1, 1))
        return carry

    lax.fori_loop(0, b_tile // SAMPLE_GROUP, group, 0)

    @pl.when(step == pl.num_programs(0) - 1)
    def _():
        og_mm = o_scr[...] * _silu(gate_scr[...])
        og = jnp.concatenate(
            [og_mm[:, _member_major(hd)] for hd in range(N_HEADS)], axis=1).astype(BF16)
        h2 = h_ref[...] + _dot_nt(og, woutt_ref[...])
        y_ref[...] = _rms(h2, fg_ref[...])


def _layer_b_sample(h, sinks, kv_norm, norm_b, final_norm, w_kv_t, w_qg_t, w_out_t, cache_k, cache_v):
    n_seq = h.shape[0]
    b_tile = SAMPLE_B_TILE
    cos, slo, shi = _rotary_tables(jnp.full((1,), PAST_LEN, F32))

    def const(shape):
        return pl.BlockSpec(shape, lambda *_: (0,) * len(shape))

    assert n_seq % LANES == 0 and cache_k.shape == (n_seq, KV_DIM, WINDOW)
    cache_spec = pl.BlockSpec((b_tile, KV_DIM, WINDOW), lambda i, *_: (i, 0, 0))
    return pl.pallas_call(
        functools.partial(_layer_b_sample_kernel, n_seq=n_seq, b_tile=b_tile),
        grid_spec=pltpu.PrefetchScalarGridSpec(
            num_scalar_prefetch=1,
            grid=(n_seq // b_tile,),
            in_specs=[const(h.shape), const(kv_norm.shape), const(norm_b.shape),
                      const(final_norm.shape), const(w_kv_t.shape), const(w_qg_t.shape),
                      const(w_out_t.shape), const(cos.shape), const(slo.shape), const(shi.shape),
                      cache_spec, cache_spec],
            out_specs=[const((n_seq, D_MODEL)), cache_spec, cache_spec],
            scratch_shapes=[pltpu.VMEM((n_seq, D_MODEL), F32),
                            pltpu.VMEM((n_seq, D_MODEL), F32),
                            pltpu.VMEM((n_seq, KV_DIM), F32),
                            pltpu.VMEM((n_seq, KV_DIM), F32),
                            pltpu.VMEM((KV_DIM, n_seq), F32),
                            pltpu.VMEM((KV_DIM, n_seq), F32),
                            pltpu.VMEM((n_seq, D_MODEL), F32)]),
        out_shape=[jax.ShapeDtypeStruct((n_seq, D_MODEL), F32),
                   jax.ShapeDtypeStruct(cache_k.shape, F32),
                   jax.ShapeDtypeStruct(cache_v.shape, F32)],
        compiler_params=pltpu.CompilerParams(
            dimension_semantics=("arbitrary",), vmem_limit_bytes=VMEM_LIMIT_BYTES),
        name="layer_b_sample",
    )(sinks, h, kv_norm, norm_b, final_norm, w_kv_t, w_qg_t, w_out_t, cos, slo, shi, cache_k, cache_v)


def kernel(x_prompt, x_sample, cache_k, cache_v, norm_a, w_in_a, v_norm_a, w_s_a, b_s_a, w_out_a,
           kv_norm, w_kv, norm_b, w_in_b, sinks_b, w_out_b, final_norm):
    batch, seq, _ = x_prompt.shape
    n_seq, dec_seq, _ = x_sample.shape
    assert dec_seq == 1 and seq % CHUNK == 0 and cache_k.shape[1] == WINDOW
    assert norm_a.shape[0] == 1 and norm_b.shape[0] == 1

    row = lambda g: g.reshape(1, -1)
    w_in_a16 = w_in_a[0].astype(BF16)
    w_out_a16 = w_out_a[0].astype(BF16)
    w_kv_t = _transpose_cast(w_kv)
    w_qg_t = _transpose_cast(w_in_b[0])
    w_out_b_t = _transpose_cast(w_out_b[0])
    bs_chunk = jnp.repeat(b_s_a[0].T, A_GROUP_DIM, axis=1)
    ws_one = jnp.repeat(w_s_a[0, :, 0, 0], A_GROUP_DIM).reshape(1, A_WIDTH)
    bs_one = jnp.repeat(b_s_a[0, :, 0], A_GROUP_DIM).reshape(1, A_WIDTH)

    h_p = _layer_a(x_prompt.reshape(batch * seq, D_MODEL), row(norm_a[0]), w_in_a16,
                   row(v_norm_a[0]), w_s_a[0], bs_chunk, w_out_a16, tile=A_TILE, n_sub=A_SUBTILES,
                   chunked=True)[0]
    h_s, av_s = _layer_a(x_sample.reshape(n_seq, D_MODEL), row(norm_a[0]), w_in_a16,
                         row(v_norm_a[0]), ws_one, bs_one, w_out_a16, tile=n_seq, n_sub=1,
                         chunked=False)

    y_p, kt_p, vt_p = _layer_b_prompt(
        h_p, sinks_b[0], row(kv_norm), row(norm_b[0]), row(final_norm),
        w_kv_t, w_qg_t, w_out_b_t, batch=batch, seq=seq)
    def to_window(x_t):
        n = x_t.shape[0]
        return x_t.reshape(n, N_KV_HEADS, HEAD_DIM, WINDOW).transpose(0, 3, 1, 2)

    def from_window(x):
        return x.transpose(0, 2, 3, 1).reshape(x.shape[0], KV_DIM, WINDOW)

    y_s, kt_s, vt_s = _layer_b_sample(h_s, sinks_b[0], row(kv_norm), row(norm_b[0]),
                                      row(final_norm), w_kv_t, w_qg_t, w_out_b_t,
                                      from_window(cache_k), from_window(cache_v))

    return (y_p.reshape(batch, seq, D_MODEL),
            y_s.reshape(n_seq, 1, D_MODEL),
            to_window(kt_p),
            to_window(vt_p),
            to_window(kt_s),
            to_window(vt_s),
            av_s.reshape(1, n_seq, 1, A_WIDTH))
```

```python
import functools

import jax
import jax.numpy as jnp
from jax import lax
from jax.experimental import pallas as pl
from jax.experimental.pallas import tpu as pltpu

D_MODEL = 1024
PAST_LEN = 8192
CHUNK = 128
A_WIDTH = 2 * D_MODEL
A_GROUPS = 8
A_GROUP_DIM = A_WIDTH // A_GROUPS
HEAD_DIM = 64
N_HEADS = D_MODEL // HEAD_DIM
N_KV_HEADS = 4
GQA_GROUP = N_HEADS // N_KV_HEADS
KV_DIM = N_KV_HEADS * HEAD_DIM
WINDOW = 128
Q_BLOCK = 128
ROT_DIM = HEAD_DIM // 4
ROPE_THETA = 500000.0
EPS = 1e-5

LANES = 128
BF16_SUBLANES = 16
LOG2_E = 1.4426950408889634
Q_SCALE_LOG2 = HEAD_DIM ** -0.5 * LOG2_E
VMEM_LIMIT_BYTES = 56 * 1024 * 1024

A_TILE = 512
B_TILE = 512
B_SUBTILES = 2
PREP_COLS = 512
OUT_ROWS = 256
ATTN_AHEAD = 2
SAMPLE_B_TILE = 16
SAMPLE_GROUP = 8

F32 = jnp.float32
BF16 = jnp.bfloat16


def _rms(x, g):
    return x * lax.rsqrt(jnp.mean(x * x, axis=-1, keepdims=True) + EPS) * g


def _silu(x):
    return x * jax.nn.sigmoid(x)


def _dot(a, b):
    return jnp.dot(a, b, preferred_element_type=F32)


def _dot_nt(a, b):
    return lax.dot_general(a, b, (((1,), (1,)), ((), ())), preferred_element_type=F32)


def _zero_of(x):
    bits = pltpu.bitcast(x, jnp.uint32)
    return ((bits >> 16) >> 16).astype(F32)


def _member_major(head):
    kh, r = divmod(head, GQA_GROUP)
    start = (r * N_KV_HEADS + kh) * HEAD_DIM
    return slice(start, start + HEAD_DIM)


def _rotate(x, cos, sin_lo, sin_hi):
    return (x * cos + pltpu.roll(x, LANES - ROT_DIM // 2, 1) * sin_lo
            + pltpu.roll(x, ROT_DIM // 2, 1) * sin_hi)


def _rotary_tables(positions):
    lane = jnp.arange(LANES) % HEAD_DIM
    freq = (2 * (lane % (ROT_DIM // 2))).astype(F32)
    ang = positions[:, None] * (ROPE_THETA ** (-freq / ROT_DIM))[None, :]
    first = (lane < ROT_DIM // 2)[None, :]
    second = ((lane >= ROT_DIM // 2) & (lane < ROT_DIM))[None, :]
    cos = jnp.where(first | second, jnp.cos(ang), 1.0)
    sin_lo = jnp.where(first, -jnp.sin(ang), 0.0)
    sin_hi = jnp.where(second, jnp.sin(ang), 0.0)
    return cos, sin_lo, sin_hi


def _transpose_cast_kernel(w_ref, o_ref):
    o_ref[...] = w_ref[...].T.astype(BF16)


def _transpose_cast(w):
    k, n = w.shape
    return pl.pallas_call(
        _transpose_cast_kernel,
        grid=(n // PREP_COLS,),
        in_specs=[pl.BlockSpec((k, PREP_COLS), lambda j: (0, j))],
        out_specs=pl.BlockSpec((PREP_COLS, k), lambda j: (j, 0)),
        out_shape=jax.ShapeDtypeStruct((n, k), BF16),
        compiler_params=pltpu.CompilerParams(dimension_semantics=("arbitrary",)),
        name="transpose_cast",
    )(w)


def _layer_a_core(x, ng_ref, win_ref, vg_ref, ws_ref, bs_ref, wout_ref, y_scr, *, tile, chunked):
    xn = _rms(x, ng_ref[...]).astype(BF16)
    v = _rms(_dot(xn, win_ref[:, A_WIDTH:2 * A_WIDTH]), vg_ref[...])
    if chunked:
        vb = v.astype(BF16)
        row = lax.broadcasted_iota(jnp.int32, (CHUNK, CHUNK), 0)
        col = lax.broadcasted_iota(jnp.int32, (CHUNK, CHUNK), 1)
        tri = row >= col
    width = 2 * A_GROUP_DIM
    for pair in range(A_GROUPS // 2):
        cols = slice(pair * width, (pair + 1) * width)
        u = _dot(xn, win_ref[:, cols])
        gate = _dot(xn, win_ref[:, 2 * A_WIDTH + pair * width:2 * A_WIDTH + (pair + 1) * width])
        if chunked:
            ws = [jnp.where(tri, ws_ref[g], 0.0).astype(BF16) for g in (2 * pair, 2 * pair + 1)]
            z = jnp.concatenate(
                [jnp.concatenate(
                    [_dot(ws[i], vb[c * CHUNK:(c + 1) * CHUNK,
                                    (2 * pair + i) * A_GROUP_DIM:(2 * pair + i + 1) * A_GROUP_DIM])
                     for i in range(2)], axis=1) + bs_ref[:, cols]
                 for c in range(tile // CHUNK)], axis=0)
        else:
            z = v[:, cols] * ws_ref[:, cols] + bs_ref[:, cols]
        y_scr[:, cols] = (u * z * _silu(gate)).astype(BF16)
    return x + _dot(y_scr[...], wout_ref[...]), v


def _layer_a_prompt_kernel(x_ref, ng_ref, win_ref, vg_ref, ws_ref, bs_ref, wout_ref,
                           ck_ref, cv_ref, knewt_ref, vnewt_ref,
                           h_ref, kout_ref, vout_ref, y_scr, *, tile, n_roll):
    is_last = lax.broadcasted_iota(jnp.int32, (KV_DIM, WINDOW), 1) == WINDOW - 1
    for b in range(n_roll):
        g = pl.program_id(0) * n_roll + b
        blk = pl.ds(pl.multiple_of((g // LANES) * LANES, LANES), LANES)
        to_last = LANES - 1 - g % LANES
        kout_ref[b] = jnp.where(is_last, pltpu.roll(knewt_ref[:, blk], to_last, 1),
                                pltpu.roll(ck_ref[b], WINDOW - 1, 1))
        vout_ref[b] = jnp.where(is_last, pltpu.roll(vnewt_ref[:, blk], to_last, 1),
                                pltpu.roll(cv_ref[b], WINDOW - 1, 1))
    h, _ = _layer_a_core(x_ref[...], ng_ref, win_ref, vg_ref, ws_ref, bs_ref, wout_ref, y_scr,
                         tile=tile, chunked=True)
    h_ref[...] = h


def _layer_a_sample_kernel(x_ref, ng_ref, win_ref, vg_ref, ws_ref, bs_ref, wout_ref,
                           kvg_ref, wkvt_ref, cos_ref, slo_ref, shi_ref,
                           h_ref, av_ref, knew_ref, vnew_ref, knewt_ref, vnewt_ref, y_scr, *, tile):
    h, v = _layer_a_core(x_ref[...], ng_ref, win_ref, vg_ref, ws_ref, bs_ref, wout_ref, y_scr,
                         tile=tile, chunked=False)
    h_ref[...] = h
    av_ref[...] = v
    hn = h * lax.rsqrt(jnp.mean(h * h, axis=-1, keepdims=True) + EPS)
    kv = _dot_nt((hn * kvg_ref[...]).astype(BF16), wkvt_ref[...])
    cos, slo, shi = cos_ref[...], slo_ref[...], shi_ref[...]
    k = jnp.concatenate(
        [_rotate(kv[:, c * LANES:(c + 1) * LANES], cos, slo, shi) for c in range(KV_DIM // LANES)],
        axis=1)
    knew_ref[...] = k
    vnew_ref[...] = kv[:, KV_DIM:]
    knewt_ref[...] = k.T
    vnewt_ref[...] = kv[:, KV_DIM:].T


def _const_spec(shape):
    return pl.BlockSpec(shape, lambda *_: (0,) * len(shape), pipeline_mode=pl.Buffered(1))


def _layer_a_prompt(x, norm_g, w_in, v_norm_g, ws, bs, w_out, cache_k, cache_v, knew_t, vnew_t):
    n_tok = x.shape[0]
    tile = A_TILE
    n_steps = n_tok // tile
    n_seq = cache_k.shape[0]
    n_roll = n_seq // n_steps
    assert n_roll * n_steps == n_seq and n_seq % LANES == 0
    tok_spec = pl.BlockSpec((tile, D_MODEL), lambda i: (i, 0))
    cache_spec = pl.BlockSpec((n_roll, KV_DIM, WINDOW), lambda i: (i, 0, 0))
    consts = (norm_g, w_in, v_norm_g, ws, bs, w_out)
    return pl.pallas_call(
        functools.partial(_layer_a_prompt_kernel, tile=tile, n_roll=n_roll),
        grid=(n_steps,),
        in_specs=[tok_spec] + [_const_spec(c.shape) for c in consts]
        + [cache_spec, cache_spec, _const_spec(knew_t.shape), _const_spec(vnew_t.shape)],
        out_specs=[tok_spec, cache_spec, cache_spec],
        out_shape=[jax.ShapeDtypeStruct((n_tok, D_MODEL), F32),
                   jax.ShapeDtypeStruct(cache_k.shape, F32),
                   jax.ShapeDtypeStruct(cache_v.shape, F32)],
        scratch_shapes=[pltpu.VMEM((tile, A_WIDTH), BF16)],
        compiler_params=pltpu.CompilerParams(
            dimension_semantics=("arbitrary",), vmem_limit_bytes=VMEM_LIMIT_BYTES),
        name="layer_a_prompt",
    )(x, *consts, cache_k, cache_v, knew_t, vnew_t)


def _layer_a_sample(x, norm_g, w_in, v_norm_g, ws, bs, w_out, kv_norm, w_kv_t):
    n_seq = x.shape[0]
    cos, slo, shi = _rotary_tables(jnp.full((1,), PAST_LEN, F32))
    ins = (x, norm_g, w_in, v_norm_g, ws, bs, w_out, kv_norm, w_kv_t, cos, slo, shi)
    out_dims = [(n_seq, D_MODEL), (n_seq, A_WIDTH), (n_seq, KV_DIM), (n_seq, KV_DIM),
                (KV_DIM, n_seq), (KV_DIM, n_seq)]
    whole = lambda shape: pl.BlockSpec(shape, lambda i: (0,) * len(shape))
    return pl.pallas_call(
        functools.partial(_layer_a_sample_kernel, tile=n_seq),
        grid=(1,),
        in_specs=[whole(a.shape) for a in ins],
        out_specs=[whole(d) for d in out_dims],
        out_shape=[jax.ShapeDtypeStruct(d, F32) for d in out_dims],
        scratch_shapes=[pltpu.VMEM((n_seq, A_WIDTH), BF16)],
        compiler_params=pltpu.CompilerParams(
            dimension_semantics=("arbitrary",), vmem_limit_bytes=VMEM_LIMIT_BYTES),
        name="layer_a_sample",
    )(*ins)


def _layer_b_prompt_kernel(sinks_ref, h_ref, kvg_ref, nbg_ref, fg_ref, wkvt_ref, wqt_ref,
                           wgt_ref, woutt_ref, cost_ref, sint_ref, y_ref, kout_ref, vout_ref,
                           *scratch, tile, n_t, n_sub):
    for sub in range(n_sub):
        rows = pl.ds(sub * tile, tile)
        _layer_b_prompt_tile(
            sinks_ref, h_ref.at[rows, :], kvg_ref, nbg_ref, fg_ref, wkvt_ref, wqt_ref, wgt_ref,
            woutt_ref, cost_ref.at[:, rows], sint_ref.at[:, rows], y_ref.at[rows, :], kout_ref,
            vout_ref, *scratch, tile=tile, n_t=n_t, t=pl.program_id(1) * n_sub + sub,
            first_possible=sub == 0, last_possible=sub == n_sub - 1)


def _layer_b_prompt_tile(sinks_ref, h_ref, kvg_ref, nbg_ref, fg_ref, wkvt_ref, wqt_ref,
                         wgt_ref, woutt_ref, cost_ref, sint_ref,
                         y_ref, kout_ref, vout_ref,
                         kext_scr, vtext_scr, qt_scr, ogt_scr, bias_scr,
                         *, tile, n_t, t, first_possible, last_possible):
    n_keys = WINDOW + Q_BLOCK

    if first_possible:
        @pl.when((pl.program_id(0) == 0) & (t == 0))
        def _():
            j = lax.broadcasted_iota(jnp.int32, (n_keys, Q_BLOCK), 0)
            i = lax.broadcasted_iota(jnp.int32, (n_keys, Q_BLOCK), 1)
            band = (j >= i) & (j <= WINDOW + i)
            bias_scr[0] = jnp.where(band & (j >= WINDOW), 0.0, -jnp.inf)
            bias_scr[1] = jnp.where(band, 0.0, -jnp.inf)

        @pl.when(t == 0)
        def _():
            kext_scr[0:WINDOW, :] = jnp.zeros((WINDOW, KV_DIM), BF16)
            vtext_scr[:, 0:WINDOW] = jnp.zeros((KV_DIM, WINDOW), BF16)

    h = h_ref[...]
    hn = h * lax.rsqrt(jnp.mean(h * h, axis=-1, keepdims=True) + EPS)
    xkv = (hn * kvg_ref[...]).astype(BF16)
    xb = (hn * nbg_ref[...]).astype(BF16)

    cost, sint = cost_ref[...], sint_ref[...]
    half = ROT_DIM // 2

    def rotate_head(rows):
        lo, hi = rows[0:half, :], rows[half:ROT_DIM, :]
        return jnp.concatenate(
            [lo * cost - hi * sint, hi * cost + lo * sint, rows[ROT_DIM:, :]], axis=0)

    kvt = _dot_nt(wkvt_ref[...], xkv)
    kt = jnp.concatenate(
        [rotate_head(kvt[kh * HEAD_DIM:(kh + 1) * HEAD_DIM, :]) for kh in range(N_KV_HEADS)],
        axis=0)
    vt = kvt[KV_DIM:, :]
    kext_scr[WINDOW:, :] = kt.T.astype(BF16)
    vtext_scr[:, WINDOW:] = vt.astype(BF16)

    if last_possible:
        @pl.when(t == n_t - 1)
        def _():
            kout_ref[0] = kt[:, tile - WINDOW:]
            vout_ref[0] = vt[:, tile - WINDOW:]

    qt = _dot_nt(wqt_ref[...], xb)
    for hd in range(N_HEADS):
        rot = rotate_head(qt[hd * HEAD_DIM:(hd + 1) * HEAD_DIM, :])
        qt_scr[hd * HEAD_DIM:(hd + 1) * HEAD_DIM, :] = (rot * Q_SCALE_LOG2).astype(BF16)

    lane_head = lax.broadcasted_iota(jnp.int32, (1, GQA_GROUP * Q_BLOCK), 1) // Q_BLOCK
    zeros_half = jnp.zeros((HEAD_DIM, GQA_GROUP * Q_BLOCK), BF16)
    ones_rows = jnp.ones((BF16_SUBLANES, n_keys), BF16)

    def scores(qb, kh):
        qcols = slice(qb * Q_BLOCK, (qb + 1) * Q_BLOCK)
        keys = slice(qb * Q_BLOCK, qb * Q_BLOCK + n_keys)
        q4 = jnp.concatenate(
            [qt_scr[(kh * GQA_GROUP + r) * HEAD_DIM:(kh * GQA_GROUP + r + 1) * HEAD_DIM, qcols]
             for r in range(GQA_GROUP)], axis=1)
        q4 = jnp.concatenate([q4, zeros_half] if kh % 2 == 0 else [zeros_half, q4], axis=0)
        kblk = kext_scr[keys, (kh // 2) * LANES:(kh // 2 + 1) * LANES]
        return _dot(kblk, q4)

    def finish(qb, kh, s, s_ahead):
        qcols = slice(qb * Q_BLOCK, (qb + 1) * Q_BLOCK)
        keys = slice(qb * Q_BLOCK, qb * Q_BLOCK + n_keys)
        if first_possible and qb == 0:
            bias = bias_scr[jnp.where(t > 0, 1, 0)]
        else:
            bias = bias_scr[1]
        s = s + jnp.concatenate([bias] * GQA_GROUP, axis=1)
        sink = jnp.zeros((1, GQA_GROUP * Q_BLOCK), F32)
        for r in range(GQA_GROUP):
            sink = jnp.where(lane_head == r, sinks_ref[kh * GQA_GROUP + r] * LOG2_E, sink)
        m = jnp.maximum(jnp.max(s, axis=0, keepdims=True), sink)
        p = jnp.exp2(s - m)
        if s_ahead is not None:
            p = jnp.concatenate(
                [p[:n_keys - 8, :], p[n_keys - 8:, :] + _zero_of(s_ahead[0:8, :])], axis=0)
        vt_ones = jnp.concatenate(
            [vtext_scr[kh * HEAD_DIM:(kh + 1) * HEAD_DIM, keys], ones_rows], axis=0)
        ot = _dot(vt_ones, p.astype(BF16))
        denom = ot[HEAD_DIM:HEAD_DIM + 1, :] + jnp.exp2(sink - m)
        ot = ot[0:HEAD_DIM, :] * (1.0 / denom)
        for r in range(GQA_GROUP):
            hd = kh * GQA_GROUP + r
            ogt_scr[hd * HEAD_DIM:(hd + 1) * HEAD_DIM, qcols] = ot[:, r * Q_BLOCK:(r + 1) * Q_BLOCK]

    blocks = [(qb, kh) for qb in range(tile // Q_BLOCK) for kh in range(N_KV_HEADS)]
    pending = [scores(*blk) for blk in blocks[:ATTN_AHEAD]]
    for n, blk in enumerate(blocks):
        if n + ATTN_AHEAD < len(blocks):
            pending.append(scores(*blocks[n + ATTN_AHEAD]))
        s_cur = pending.pop(0)
        finish(*blk, s_cur, pending[-1] if pending else None)

    kext_scr[0:WINDOW, :] = kext_scr[tile:tile + WINDOW, :]
    vtext_scr[:, 0:WINDOW] = vtext_scr[:, tile:tile + WINDOW]

    ogt = jnp.concatenate(
        [(ogt_scr[rc * OUT_ROWS:(rc + 1) * OUT_ROWS, :]
          * _silu(_dot_nt(wgt_ref[rc * OUT_ROWS:(rc + 1) * OUT_ROWS, :], xb))).astype(BF16)
         for rc in range(D_MODEL // OUT_ROWS)], axis=0)
    h2 = jnp.concatenate(
        [h[:, rc * OUT_ROWS:(rc + 1) * OUT_ROWS]
         + _dot(woutt_ref[rc * OUT_ROWS:(rc + 1) * OUT_ROWS, :], ogt).T
         for rc in range(D_MODEL // OUT_ROWS)], axis=1)
    y_ref[...] = _rms(h2, fg_ref[...])


def _layer_b_prompt(h, sinks, kv_norm, norm_b, final_norm, w_kv_t, w_qg_t, w_out_t, *, batch, seq):
    tile = B_TILE
    n_t = seq // tile
    n_sub = B_SUBTILES
    step = n_sub * tile
    n_steps = n_t // n_sub
    pos = jnp.arange(seq, dtype=F32)
    inv = ROPE_THETA ** (-jnp.arange(0, ROT_DIM, 2, dtype=F32) / ROT_DIM)
    ang_t = inv[:, None] * pos[None, :]
    cos_t, sin_t = jnp.cos(ang_t), jnp.sin(ang_t)
    tok_spec = pl.BlockSpec((step, D_MODEL), lambda b, t, *_: (b * n_steps + t, 0))
    rot_t_spec = pl.BlockSpec((ROT_DIM // 2, step), lambda b, t, *_: (0, t))
    last_spec = pl.BlockSpec((1, KV_DIM, WINDOW), lambda b, t, *_: (b, 0, 0))

    def const(shape):
        return pl.BlockSpec(shape, lambda *_: (0,) * len(shape), pipeline_mode=pl.Buffered(1))

    def half(i):
        return pl.BlockSpec((D_MODEL, D_MODEL), lambda *_: (i, 0), pipeline_mode=pl.Buffered(1))

    return pl.pallas_call(
        functools.partial(_layer_b_prompt_kernel, tile=tile, n_t=n_t, n_sub=n_sub),
        grid_spec=pltpu.PrefetchScalarGridSpec(
            num_scalar_prefetch=1,
            grid=(batch, n_steps),
            in_specs=[tok_spec, const(kv_norm.shape), const(norm_b.shape), const(final_norm.shape),
                      const(w_kv_t.shape), half(0), half(1), const(w_out_t.shape),
                      rot_t_spec, rot_t_spec],
            out_specs=[tok_spec, last_spec, last_spec],
            scratch_shapes=[pltpu.VMEM((WINDOW + tile, KV_DIM), BF16),
                            pltpu.VMEM((KV_DIM, WINDOW + tile), BF16),
                            pltpu.VMEM((D_MODEL, tile), BF16),
                            pltpu.VMEM((D_MODEL, tile), F32),
                            pltpu.VMEM((2, WINDOW + Q_BLOCK, Q_BLOCK), F32)]),
        out_shape=[jax.ShapeDtypeStruct((batch * seq, D_MODEL), F32),
                   jax.ShapeDtypeStruct((batch, KV_DIM, WINDOW), F32),
                   jax.ShapeDtypeStruct((batch, KV_DIM, WINDOW), F32)],
        compiler_params=pltpu.CompilerParams(
            dimension_semantics=("arbitrary", "arbitrary"), vmem_limit_bytes=VMEM_LIMIT_BYTES),
        name="layer_b_prompt",
    )(sinks, h, kv_norm, norm_b, final_norm, w_kv_t, w_qg_t, w_qg_t, w_out_t, cos_t, sin_t)


def _layer_b_sample_kernel(sinks_ref, h_ref, nbg_ref, fg_ref, wqgt_ref, woutt_ref,
                           cos_ref, slo_ref, shi_ref, knew_ref, vnew_ref, ck_ref, cv_ref,
                           y_ref, q_scr, gate_scr, o_scr, *, n_seq, b_tile):
    step = pl.program_id(0)

    @pl.when(step == 0)
    def _():
        h = h_ref[...]
        hn = h * lax.rsqrt(jnp.mean(h * h, axis=-1, keepdims=True) + EPS)
        qg = _dot_nt((hn * nbg_ref[...]).astype(BF16), wqgt_ref[...])
        cos, slo, shi = cos_ref[...], slo_ref[...], shi_ref[...]
        for c in range(D_MODEL // LANES):
            cols = slice(c * LANES, (c + 1) * LANES)
            q2 = _rotate(qg[:, cols], cos, slo, shi) * HEAD_DIM ** -0.5
            g2 = qg[:, D_MODEL + c * LANES:D_MODEL + (c + 1) * LANES]
            for i in range(LANES // HEAD_DIM):
                dst = _member_major(c * (LANES // HEAD_DIM) + i)
                q_scr[:, dst] = q2[:, i * HEAD_DIM:(i + 1) * HEAD_DIM]
                gate_scr[:, dst] = g2[:, i * HEAD_DIM:(i + 1) * HEAD_DIM]

    n_rows = GQA_GROUP * N_KV_HEADS * SAMPLE_GROUP
    row = lax.broadcasted_iota(jnp.int32, (n_rows, 1), 0)
    row_kh = (row // SAMPLE_GROUP) % N_KV_HEADS
    row_seq = row % SAMPLE_GROUP
    lane_kh = lax.broadcasted_iota(jnp.int32, (1, KV_DIM), 1) // HEAD_DIM
    own = row_kh == lane_kh
    sink = jnp.zeros((n_rows, 1), F32)
    for r in range(GQA_GROUP):
        for kh in range(N_KV_HEADS):
            sink = jnp.where(row // SAMPLE_GROUP == r * N_KV_HEADS + kh,
                             sinks_ref[kh * GQA_GROUP + r], sink)
    n_blk = GQA_GROUP * N_KV_HEADS

    def group(i, carry):
        b0 = i * SAMPLE_GROUP
        g0 = pl.multiple_of(step * b_tile + b0, SAMPLE_GROUP)
        seqs = pl.ds(g0, SAMPLE_GROUP)
        q8 = q_scr[seqs, :]
        qexp = jnp.concatenate(
            [q8[:, r * KV_DIM:(r + 1) * KV_DIM] for r in range(GQA_GROUP)
             for _ in range(N_KV_HEADS)], axis=0)
        qexp = jnp.where(own, qexp, 0.0).astype(BF16)
        knew8 = knew_ref[seqs, :].astype(BF16).astype(F32)
        vnew8 = vnew_ref[seqs, :].astype(BF16).astype(F32)
        s_new = jnp.sum(qexp.astype(F32) * jnp.concatenate([knew8] * n_blk, axis=0),
                        axis=1, keepdims=True)
        s_old = jnp.zeros((n_rows, WINDOW), F32)
        for b in range(SAMPLE_GROUP):
            s_b = _dot(qexp, ck_ref[b0 + b].astype(BF16))
            s_old = jnp.where(row_seq == b, s_b, s_old)
        m = jnp.maximum(jnp.maximum(jnp.max(s_old, axis=1, keepdims=True), s_new), sink)
        p_old = jnp.exp(s_old - m)
        p_new = jnp.exp(s_new - m)
        denom = jnp.sum(p_old, axis=1, keepdims=True) + p_new + jnp.exp(sink - m)
        p_old = p_old.astype(BF16)
        o = jnp.zeros((n_rows, KV_DIM), F32)
        for b in range(SAMPLE_GROUP):
            o_b = _dot_nt(p_old, cv_ref[b0 + b].astype(BF16))
            o = jnp.where(row_seq == b, o_b, o)
        o = (o + p_new.astype(BF16).astype(F32) * jnp.concatenate([vnew8] * n_blk, axis=0)) / denom
        o = jnp.where(own, o, 0.0)
        for r in range(GQA_GROUP):
            blks = [o[(r * N_KV_HEADS + kh) * SAMPLE_GROUP:(r * N_KV_HEADS + kh + 1) * SAMPLE_GROUP]
                    for kh in range(N_KV_HEADS)]
            o_scr[seqs, r * KV_DIM:(r + 1) * KV_DIM] = (blks[0] + blks[1]) + (blks[2] + blks[3])
        return carry

    lax.fori_loop(0, b_tile // SAMPLE_GROUP, group, 0)

    @pl.when(step == pl.num_programs(0) - 1)
    def _():
        og_mm = o_scr[...] * _silu(gate_scr[...])
        og = jnp.concatenate(
            [og_mm[:, _member_major(hd)] for hd in range(N_HEADS)], axis=1).astype(BF16)
        h2 = h_ref[...] + _dot_nt(og, woutt_ref[...])
        y_ref[...] = _rms(h2, fg_ref[...])


def _layer_b_sample(h, sinks, norm_b, final_norm, w_qg_t, w_out_t, knew, vnew, cache_k, cache_v):
    n_seq = h.shape[0]
    b_tile = SAMPLE_B_TILE
    cos, slo, shi = _rotary_tables(jnp.full((1,), PAST_LEN, F32))

    def const(shape):
        return pl.BlockSpec(shape, lambda *_: (0,) * len(shape))

    assert cache_k.shape == (n_seq, KV_DIM, WINDOW)
    cache_spec = pl.BlockSpec((b_tile, KV_DIM, WINDOW), lambda i, *_: (i, 0, 0))
    consts = (h, norm_b, final_norm, w_qg_t, w_out_t, cos, slo, shi, knew, vnew)
    return pl.pallas_call(
        functools.partial(_layer_b_sample_kernel, n_seq=n_seq, b_tile=b_tile),
        grid_spec=pltpu.PrefetchScalarGridSpec(
            num_scalar_prefetch=1,
            grid=(n_seq // b_tile,),
            in_specs=[const(c.shape) for c in consts] + [cache_spec, cache_spec],
            out_specs=const((n_seq, D_MODEL)),
            scratch_shapes=[pltpu.VMEM((n_seq, D_MODEL), F32),
                            pltpu.VMEM((n_seq, D_MODEL), F32),
                            pltpu.VMEM((n_seq, D_MODEL), F32)]),
        out_shape=jax.ShapeDtypeStruct((n_seq, D_MODEL), F32),
        compiler_params=pltpu.CompilerParams(
            dimension_semantics=("arbitrary",), vmem_limit_bytes=VMEM_LIMIT_BYTES),
        name="layer_b_sample",
    )(sinks, *consts, cache_k, cache_v)


def kernel(x_prompt, x_sample, cache_k, cache_v, norm_a, w_in_a, v_norm_a, w_s_a, b_s_a, w_out_a,
           kv_norm, w_kv, norm_b, w_in_b, sinks_b, w_out_b, final_norm):
    batch, seq, _ = x_prompt.shape
    n_seq, dec_seq, _ = x_sample.shape
    assert dec_seq == 1 and seq % CHUNK == 0 and cache_k.shape[1] == WINDOW
    assert norm_a.shape[0] == 1 and norm_b.shape[0] == 1

    row = lambda g: g.reshape(1, -1)
    w_in_a16 = w_in_a[0].astype(BF16)
    w_out_a16 = w_out_a[0].astype(BF16)
    w_kv_t = _transpose_cast(w_kv)
    w_qg_t = _transpose_cast(w_in_b[0])
    w_out_b_t = _transpose_cast(w_out_b[0])
    bs_chunk = jnp.repeat(b_s_a[0].T, A_GROUP_DIM, axis=1)
    ws_one = jnp.repeat(w_s_a[0, :, 0, 0], A_GROUP_DIM).reshape(1, A_WIDTH)
    bs_one = jnp.repeat(b_s_a[0, :, 0], A_GROUP_DIM).reshape(1, A_WIDTH)

    def to_window(x_t):
        n = x_t.shape[0]
        return x_t.reshape(n, N_KV_HEADS, HEAD_DIM, WINDOW).transpose(0, 3, 1, 2)

    def from_window(x):
        return x.transpose(0, 2, 3, 1).reshape(x.shape[0], KV_DIM, WINDOW)

    cache_kt, cache_vt = from_window(cache_k), from_window(cache_v)

    h_s, av_s, knew, vnew, knew_t, vnew_t = _layer_a_sample(
        x_sample.reshape(n_seq, D_MODEL), row(norm_a[0]), w_in_a16, row(v_norm_a[0]), ws_one,
        bs_one, w_out_a16, row(kv_norm), w_kv_t)
    h_p, kt_s, vt_s = _layer_a_prompt(
        x_prompt.reshape(batch * seq, D_MODEL), row(norm_a[0]), w_in_a16, row(v_norm_a[0]),
        w_s_a[0], bs_chunk, w_out_a16, cache_kt, cache_vt, knew_t, vnew_t)

    y_p, kt_p, vt_p = _layer_b_prompt(
        h_p, sinks_b[0], row(kv_norm), row(norm_b[0]), row(final_norm),
        w_kv_t, w_qg_t, w_out_b_t, batch=batch, seq=seq)
    y_s = _layer_b_sample(h_s, sinks_b[0], row(norm_b[0]), row(final_norm), w_qg_t, w_out_b_t,
                          knew, vnew, cache_kt, cache_vt)

    return (y_p.reshape(batch, seq, D_MODEL),
            y_s.reshape(n_seq, 1, D_MODEL),
            to_window(kt_p),
            to_window(vt_p),
            to_window(kt_s),
            to_window(vt_s),
            av_s.reshape(1, n_seq, 1, A_WIDTH))
```

```python
import functools

import jax
import jax.numpy as jnp
from jax import lax
from jax.experimental import pallas as pl
from jax.experimental.pallas import tpu as pltpu

D_MODEL = 1024
PAST_LEN = 8192
CHUNK = 128
A_WIDTH = 2 * D_MODEL
A_GROUPS = 8
A_GROUP_DIM = A_WIDTH // A_GROUPS
HEAD_DIM = 64
N_HEADS = D_MODEL // HEAD_DIM
N_KV_HEADS = 4
GQA_GROUP = N_HEADS // N_KV_HEADS
KV_DIM = N_KV_HEADS * HEAD_DIM
WINDOW = 128
Q_BLOCK = 128
ROT_DIM = HEAD_DIM // 4
ROPE_THETA = 500000.0
EPS = 1e-5

LANES = 128
BF16_SUBLANES = 16
LOG2_E = 1.4426950408889634
Q_SCALE_LOG2 = HEAD_DIM ** -0.5 * LOG2_E
VMEM_LIMIT_BYTES = 56 * 1024 * 1024

A_TILE = 512
B_TILE = 512
B_SUBTILES = 2
PREP_COLS = 512
A_SAMPLE_COLS = 1024
A_SAMPLE_ROWS = 1024
OUT_ROWS = 256
ATTN_AHEAD = 2
SAMPLE_B_TILE = 16
SAMPLE_GROUP = 8

F32 = jnp.float32
BF16 = jnp.bfloat16


def _rms(x, g):
    return x * lax.rsqrt(jnp.mean(x * x, axis=-1, keepdims=True) + EPS) * g


def _silu(x):
    return x * jax.nn.sigmoid(x)


def _dot(a, b):
    return jnp.dot(a, b, preferred_element_type=F32)


def _dot_nt(a, b):
    return lax.dot_general(a, b, (((1,), (1,)), ((), ())), preferred_element_type=F32)


def _zero_of(x):
    bits = pltpu.bitcast(x, jnp.uint32)
    return ((bits >> 16) >> 16).astype(F32)


def _member_major(head):
    kh, r = divmod(head, GQA_GROUP)
    start = (r * N_KV_HEADS + kh) * HEAD_DIM
    return slice(start, start + HEAD_DIM)


def _rotate(x, cos, sin_lo, sin_hi):
    return (x * cos + pltpu.roll(x, LANES - ROT_DIM // 2, 1) * sin_lo
            + pltpu.roll(x, ROT_DIM // 2, 1) * sin_hi)


def _rotary_tables(positions):
    lane = jnp.arange(LANES) % HEAD_DIM
    freq = (2 * (lane % (ROT_DIM // 2))).astype(F32)
    ang = positions[:, None] * (ROPE_THETA ** (-freq / ROT_DIM))[None, :]
    first = (lane < ROT_DIM // 2)[None, :]
    second = ((lane >= ROT_DIM // 2) & (lane < ROT_DIM))[None, :]
    cos = jnp.where(first | second, jnp.cos(ang), 1.0)
    sin_lo = jnp.where(first, -jnp.sin(ang), 0.0)
    sin_hi = jnp.where(second, jnp.sin(ang), 0.0)
    return cos, sin_lo, sin_hi


def _transpose_cast_kernel(w_ref, o_ref):
    o_ref[...] = w_ref[...].T.astype(BF16)


def _transpose_cast(w):
    k, n = w.shape
    return pl.pallas_call(
        _transpose_cast_kernel,
        grid=(n // PREP_COLS,),
        in_specs=[pl.BlockSpec((k, PREP_COLS), lambda j: (0, j))],
        out_specs=pl.BlockSpec((PREP_COLS, k), lambda j: (j, 0)),
        out_shape=jax.ShapeDtypeStruct((n, k), BF16),
        compiler_params=pltpu.CompilerParams(dimension_semantics=("arbitrary",)),
        name="transpose_cast",
    )(w)


def _layer_a_tile(x, ng_ref, win_ref, vg_ref, ws_ref, bs_ref, wout_ref, y_scr, *, tile):
    xn = _rms(x, ng_ref[...]).astype(BF16)
    vb = _rms(_dot(xn, win_ref[:, A_WIDTH:2 * A_WIDTH]), vg_ref[...]).astype(BF16)
    row = lax.broadcasted_iota(jnp.int32, (CHUNK, CHUNK), 0)
    col = lax.broadcasted_iota(jnp.int32, (CHUNK, CHUNK), 1)
    tri = row >= col
    width = 2 * A_GROUP_DIM
    for pair in range(A_GROUPS // 2):
        cols = slice(pair * width, (pair + 1) * width)
        u = _dot(xn, win_ref[:, cols])
        gate = _dot(xn, win_ref[:, 2 * A_WIDTH + pair * width:2 * A_WIDTH + (pair + 1) * width])
        ws = [jnp.where(tri, ws_ref[g], 0.0).astype(BF16) for g in (2 * pair, 2 * pair + 1)]
        z = jnp.concatenate(
            [jnp.concatenate(
                [_dot(ws[i], vb[c * CHUNK:(c + 1) * CHUNK,
                                (2 * pair + i) * A_GROUP_DIM:(2 * pair + i + 1) * A_GROUP_DIM])
                 for i in range(2)], axis=1) + bs_ref[:, cols]
             for c in range(tile // CHUNK)], axis=0)
        y_scr[:, cols] = (u * z * _silu(gate)).astype(BF16)
    return x + _dot(y_scr[...], wout_ref[...])


def _layer_a_prompt_kernel(x_ref, ng_ref, win_ref, vg_ref, ws_ref, bs_ref, wout_ref,
                           ck_ref, cv_ref, knewt_ref, vnewt_ref,
                           h_ref, kout_ref, vout_ref, y_scr, *, tile, n_roll):
    is_last = lax.broadcasted_iota(jnp.int32, (KV_DIM, WINDOW), 1) == WINDOW - 1
    for b in range(n_roll):
        g = pl.program_id(0) * n_roll + b
        blk = pl.ds(pl.multiple_of((g // LANES) * LANES, LANES), LANES)
        to_last = LANES - 1 - g % LANES
        kout_ref[b] = jnp.where(is_last, pltpu.roll(knewt_ref[:, blk], to_last, 1),
                                pltpu.roll(ck_ref[b], WINDOW - 1, 1))
        vout_ref[b] = jnp.where(is_last, pltpu.roll(vnewt_ref[:, blk], to_last, 1),
                                pltpu.roll(cv_ref[b], WINDOW - 1, 1))
    h_ref[...] = _layer_a_tile(x_ref[...], ng_ref, win_ref, vg_ref, ws_ref, bs_ref, wout_ref,
                               y_scr, tile=tile)


def _layer_a_sample_kernel(x_ref, ng_ref, win_ref, vg_ref, ws_ref, bs_ref, wout_ref,
                           kvg_ref, wkvt_ref, cos_ref, slo_ref, shi_ref,
                           win16_ref, wout16_ref, h_ref, av_ref, knew_ref, vnew_ref, knewt_ref,
                           vnewt_ref, xn_scr, proj_scr, y_scr, acc_scr, *, n_in, n_out):
    step = pl.program_id(0)
    blocks_per_branch = n_in // 3

    @pl.when(step == 0)
    def _():
        xn_scr[...] = _rms(x_ref[...], ng_ref[...]).astype(BF16)

    @pl.when(step < n_in)
    def _():
        w = win_ref[...].astype(BF16)
        win16_ref[...] = w
        proj_scr[step] = _dot(xn_scr[...], w)

    @pl.when(step == n_in)
    def _():
        def branch(i):
            return jnp.concatenate(
                [proj_scr[i * blocks_per_branch + j] for j in range(blocks_per_branch)], axis=1)

        v = _rms(branch(1), vg_ref[...])
        av_ref[...] = v
        z = v * ws_ref[...] + bs_ref[...]
        y = (branch(0) * z * _silu(branch(2))).astype(BF16)
        rows = A_WIDTH // n_out
        for j in range(n_out):
            y_scr[j] = y[:, j * rows:(j + 1) * rows]

    @pl.when(step >= n_in)
    def _():
        w = wout_ref[...].astype(BF16)
        wout16_ref[...] = w
        part = _dot(y_scr[step - n_in], w)

        @pl.when(step == n_in)
        def _():
            acc_scr[...] = part

        @pl.when(step > n_in)
        def _():
            acc_scr[...] += part

    @pl.when(step == n_in + n_out - 1)
    def _():
        h = x_ref[...] + acc_scr[...]
        h_ref[...] = h
        hn = h * lax.rsqrt(jnp.mean(h * h, axis=-1, keepdims=True) + EPS)
        kv = _dot_nt((hn * kvg_ref[...]).astype(BF16), wkvt_ref[...])
        cos, slo, shi = cos_ref[...], slo_ref[...], shi_ref[...]
        k = jnp.concatenate(
            [_rotate(kv[:, c * LANES:(c + 1) * LANES], cos, slo, shi)
             for c in range(KV_DIM // LANES)], axis=1)
        knew_ref[...] = k
        vnew_ref[...] = kv[:, KV_DIM:]
        knewt_ref[...] = k.T
        vnewt_ref[...] = kv[:, KV_DIM:].T


def _const_spec(shape):
    return pl.BlockSpec(shape, lambda *_: (0,) * len(shape), pipeline_mode=pl.Buffered(1))


def _layer_a_prompt(x, norm_g, w_in, v_norm_g, ws, bs, w_out, cache_k, cache_v, knew_t, vnew_t):
    n_tok = x.shape[0]
    tile = A_TILE
    n_steps = n_tok // tile
    n_seq = cache_k.shape[0]
    n_roll = n_seq // n_steps
    assert n_roll * n_steps == n_seq and n_seq % LANES == 0
    tok_spec = pl.BlockSpec((tile, D_MODEL), lambda i: (i, 0))
    cache_spec = pl.BlockSpec((n_roll, KV_DIM, WINDOW), lambda i: (i, 0, 0))
    consts = (norm_g, w_in, v_norm_g, ws, bs, w_out)
    return pl.pallas_call(
        functools.partial(_layer_a_prompt_kernel, tile=tile, n_roll=n_roll),
        grid=(n_steps,),
        in_specs=[tok_spec] + [_const_spec(c.shape) for c in consts]
        + [cache_spec, cache_spec, _const_spec(knew_t.shape), _const_spec(vnew_t.shape)],
        out_specs=[tok_spec, cache_spec, cache_spec],
        out_shape=[jax.ShapeDtypeStruct((n_tok, D_MODEL), F32),
                   jax.ShapeDtypeStruct(cache_k.shape, F32),
                   jax.ShapeDtypeStruct(cache_v.shape, F32)],
        scratch_shapes=[pltpu.VMEM((tile, A_WIDTH), BF16)],
        compiler_params=pltpu.CompilerParams(
            dimension_semantics=("arbitrary",), vmem_limit_bytes=VMEM_LIMIT_BYTES),
        name="layer_a_prompt",
    )(x, *consts, cache_k, cache_v, knew_t, vnew_t)


def _layer_a_sample(x, norm_g, w_in, v_norm_g, ws, bs, w_out, kv_norm, w_kv_t):
    n_seq = x.shape[0]
    n_in = w_in.shape[1] // A_SAMPLE_COLS
    n_out = w_out.shape[0] // A_SAMPLE_ROWS
    assert n_in % 3 == 0
    cos, slo, shi = _rotary_tables(jnp.full((1,), PAST_LEN, F32))
    whole = lambda shape: pl.BlockSpec(shape, lambda i: (0,) * len(shape))
    win_block = lambda i: (0, jnp.minimum(i, n_in - 1))
    wout_block = lambda i: (jnp.maximum(i - n_in, 0), 0)
    small_dims = [(n_seq, D_MODEL), (n_seq, A_WIDTH), (n_seq, KV_DIM), (n_seq, KV_DIM),
                  (KV_DIM, n_seq), (KV_DIM, n_seq)]
    return pl.pallas_call(
        functools.partial(_layer_a_sample_kernel, n_in=n_in, n_out=n_out),
        grid=(n_in + n_out,),
        in_specs=[whole(x.shape), whole(norm_g.shape),
                  pl.BlockSpec((D_MODEL, A_SAMPLE_COLS), win_block),
                  whole(v_norm_g.shape), whole(ws.shape), whole(bs.shape),
                  pl.BlockSpec((A_SAMPLE_ROWS, D_MODEL), wout_block),
                  whole(kv_norm.shape), whole(w_kv_t.shape),
                  whole(cos.shape), whole(slo.shape), whole(shi.shape)],
        out_specs=[pl.BlockSpec((D_MODEL, A_SAMPLE_COLS), win_block),
                   pl.BlockSpec((A_SAMPLE_ROWS, D_MODEL), wout_block)]
        + [whole(d) for d in small_dims],
        out_shape=[jax.ShapeDtypeStruct(w_in.shape, BF16), jax.ShapeDtypeStruct(w_out.shape, BF16)]
        + [jax.ShapeDtypeStruct(d, F32) for d in small_dims],
        scratch_shapes=[pltpu.VMEM((n_seq, D_MODEL), BF16),
                        pltpu.VMEM((n_in, n_seq, A_SAMPLE_COLS), F32),
                        pltpu.VMEM((n_out, n_seq, A_SAMPLE_ROWS), BF16),
                        pltpu.VMEM((n_seq, D_MODEL), F32)],
        compiler_params=pltpu.CompilerParams(
            dimension_semantics=("arbitrary",), vmem_limit_bytes=VMEM_LIMIT_BYTES),
        name="layer_a_sample",
    )(x, norm_g, w_in, v_norm_g, ws, bs, w_out, kv_norm, w_kv_t, cos, slo, shi)


def _layer_b_prompt_kernel(sinks_ref, h_ref, kvg_ref, nbg_ref, fg_ref, wkvt_ref, wqt_ref,
                           wgt_ref, woutt_ref, cost_ref, sint_ref, y_ref, kout_ref, vout_ref,
                           *scratch, tile, n_t, n_sub):
    for sub in range(n_sub):
        rows = pl.ds(sub * tile, tile)
        _layer_b_prompt_tile(
            sinks_ref, h_ref.at[rows, :], kvg_ref, nbg_ref, fg_ref, wkvt_ref, wqt_ref, wgt_ref,
            woutt_ref, cost_ref.at[:, rows], sint_ref.at[:, rows], y_ref.at[rows, :], kout_ref,
            vout_ref, *scratch, tile=tile, n_t=n_t, t=pl.program_id(1) * n_sub + sub,
            first_possible=sub == 0, last_possible=sub == n_sub - 1)


def _layer_b_prompt_tile(sinks_ref, h_ref, kvg_ref, nbg_ref, fg_ref, wkvt_ref, wqt_ref,
                         wgt_ref, woutt_ref, cost_ref, sint_ref,
                         y_ref, kout_ref, vout_ref,
                         kext_scr, vtext_scr, qt_scr, ogt_scr, bias_scr,
                         *, tile, n_t, t, first_possible, last_possible):
    n_keys = WINDOW + Q_BLOCK

    if first_possible:
        @pl.when((pl.program_id(0) == 0) & (t == 0))
        def _():
            j = lax.broadcasted_iota(jnp.int32, (n_keys, Q_BLOCK), 0)
            i = lax.broadcasted_iota(jnp.int32, (n_keys, Q_BLOCK), 1)
            band = (j >= i) & (j <= WINDOW + i)
            bias_scr[0] = jnp.where(band & (j >= WINDOW), 0.0, -jnp.inf)
            bias_scr[1] = jnp.where(band, 0.0, -jnp.inf)

        @pl.when(t == 0)
        def _():
            kext_scr[0:WINDOW, :] = jnp.zeros((WINDOW, KV_DIM), BF16)
            vtext_scr[:, 0:WINDOW] = jnp.zeros((KV_DIM, WINDOW), BF16)

    h = h_ref[...]
    hn = h * lax.rsqrt(jnp.mean(h * h, axis=-1, keepdims=True) + EPS)
    xkv = (hn * kvg_ref[...]).astype(BF16)
    xb = (hn * nbg_ref[...]).astype(BF16)

    cost, sint = cost_ref[...], sint_ref[...]
    half = ROT_DIM // 2

    def rotate_head(rows):
        lo, hi = rows[0:half, :], rows[half:ROT_DIM, :]
        return jnp.concatenate(
            [lo * cost - hi * sint, hi * cost + lo * sint, rows[ROT_DIM:, :]], axis=0)

    kvt = _dot_nt(wkvt_ref[...], xkv)
    kt = jnp.concatenate(
        [rotate_head(kvt[kh * HEAD_DIM:(kh + 1) * HEAD_DIM, :]) for kh in range(N_KV_HEADS)],
        axis=0)
    vt = kvt[KV_DIM:, :]
    kext_scr[WINDOW:, :] = kt.T.astype(BF16)
    vtext_scr[:, WINDOW:] = vt.astype(BF16)

    if last_possible:
        @pl.when(t == n_t - 1)
        def _():
            kout_ref[0] = kt[:, tile - WINDOW:]
            vout_ref[0] = vt[:, tile - WINDOW:]

    qt = _dot_nt(wqt_ref[...], xb)
    for hd in range(N_HEADS):
        rot = rotate_head(qt[hd * HEAD_DIM:(hd + 1) * HEAD_DIM, :])
        qt_scr[hd * HEAD_DIM:(hd + 1) * HEAD_DIM, :] = (rot * Q_SCALE_LOG2).astype(BF16)

    lane_head = lax.broadcasted_iota(jnp.int32, (1, GQA_GROUP * Q_BLOCK), 1) // Q_BLOCK
    zeros_half = jnp.zeros((HEAD_DIM, GQA_GROUP * Q_BLOCK), BF16)
    ones_rows = jnp.ones((BF16_SUBLANES, n_keys), BF16)

    def scores(qb, kh):
        qcols = slice(qb * Q_BLOCK, (qb + 1) * Q_BLOCK)
        keys = slice(qb * Q_BLOCK, qb * Q_BLOCK + n_keys)
        q4 = jnp.concatenate(
            [qt_scr[(kh * GQA_GROUP + r) * HEAD_DIM:(kh * GQA_GROUP + r + 1) * HEAD_DIM, qcols]
             for r in range(GQA_GROUP)], axis=1)
        q4 = jnp.concatenate([q4, zeros_half] if kh % 2 == 0 else [zeros_half, q4], axis=0)
        kblk = kext_scr[keys, (kh // 2) * LANES:(kh // 2 + 1) * LANES]
        return _dot(kblk, q4)

    def finish(qb, kh, s, s_ahead):
        qcols = slice(qb * Q_BLOCK, (qb + 1) * Q_BLOCK)
        keys = slice(qb * Q_BLOCK, qb * Q_BLOCK + n_keys)
        if first_possible and qb == 0:
            bias = bias_scr[jnp.where(t > 0, 1, 0)]
        else:
            bias = bias_scr[1]
        s = s + jnp.concatenate([bias] * GQA_GROUP, axis=1)
        sink = jnp.zeros((1, GQA_GROUP * Q_BLOCK), F32)
        for r in range(GQA_GROUP):
            sink = jnp.where(lane_head == r, sinks_ref[kh * GQA_GROUP + r] * LOG2_E, sink)
        m = jnp.maximum(jnp.max(s, axis=0, keepdims=True), sink)
        p = jnp.exp2(s - m)
        if s_ahead is not None:
            p = jnp.concatenate(
                [p[:n_keys - 8, :], p[n_keys - 8:, :] + _zero_of(s_ahead[0:8, :])], axis=0)
        vt_ones = jnp.concatenate(
            [vtext_scr[kh * HEAD_DIM:(kh + 1) * HEAD_DIM, keys], ones_rows], axis=0)
        ot = _dot(vt_ones, p.astype(BF16))
        denom = ot[HEAD_DIM:HEAD_DIM + 1, :] + jnp.exp2(sink - m)
        ot = ot[0:HEAD_DIM, :] * (1.0 / denom)
        for r in range(GQA_GROUP):
            hd = kh * GQA_GROUP + r
            ogt_scr[hd * HEAD_DIM:(hd + 1) * HEAD_DIM, qcols] = ot[:, r * Q_BLOCK:(r + 1) * Q_BLOCK]

    blocks = [(qb, kh) for qb in range(tile // Q_BLOCK) for kh in range(N_KV_HEADS)]
    pending = [scores(*blk) for blk in blocks[:ATTN_AHEAD]]
    for n, blk in enumerate(blocks):
        if n + ATTN_AHEAD < len(blocks):
            pending.append(scores(*blocks[n + ATTN_AHEAD]))
        s_cur = pending.pop(0)
        finish(*blk, s_cur, pending[-1] if pending else None)

    kext_scr[0:WINDOW, :] = kext_scr[tile:tile + WINDOW, :]
    vtext_scr[:, 0:WINDOW] = vtext_scr[:, tile:tile + WINDOW]

    ogt = jnp.concatenate(
        [(ogt_scr[rc * OUT_ROWS:(rc + 1) * OUT_ROWS, :]
          * _silu(_dot_nt(wgt_ref[rc * OUT_ROWS:(rc + 1) * OUT_ROWS, :], xb))).astype(BF16)
         for rc in range(D_MODEL // OUT_ROWS)], axis=0)
    h2 = jnp.concatenate(
        [h[:, rc * OUT_ROWS:(rc + 1) * OUT_ROWS]
         + _dot(woutt_ref[rc * OUT_ROWS:(rc + 1) * OUT_ROWS, :], ogt).T
         for rc in range(D_MODEL // OUT_ROWS)], axis=1)
    y_ref[...] = _rms(h2, fg_ref[...])


def _layer_b_prompt(h, sinks, kv_norm, norm_b, final_norm, w_kv_t, w_qg_t, w_out_t, *, batch, seq):
    tile = B_TILE
    n_t = seq // tile
    n_sub = B_SUBTILES
    step = n_sub * tile
    n_steps = n_t // n_sub
    pos = jnp.arange(seq, dtype=F32)
    inv = ROPE_THETA ** (-jnp.arange(0, ROT_DIM, 2, dtype=F32) / ROT_DIM)
    ang_t = inv[:, None] * pos[None, :]
    cos_t, sin_t = jnp.cos(ang_t), jnp.sin(ang_t)
    tok_spec = pl.BlockSpec((step, D_MODEL), lambda b, t, *_: (b * n_steps + t, 0))
    rot_t_spec = pl.BlockSpec((ROT_DIM // 2, step), lambda b, t, *_: (0, t))
    last_spec = pl.BlockSpec((1, KV_DIM, WINDOW), lambda b, t, *_: (b, 0, 0))

    def const(shape):
        return pl.BlockSpec(shape, lambda *_: (0,) * len(shape), pipeline_mode=pl.Buffered(1))

    def half(i):
        return pl.BlockSpec((D_MODEL, D_MODEL), lambda *_: (i, 0), pipeline_mode=pl.Buffered(1))

    return pl.pallas_call(
        functools.partial(_layer_b_prompt_kernel, tile=tile, n_t=n_t, n_sub=n_sub),
        grid_spec=pltpu.PrefetchScalarGridSpec(
            num_scalar_prefetch=1,
            grid=(batch, n_steps),
            in_specs=[tok_spec, const(kv_norm.shape), const(norm_b.shape), const(final_norm.shape),
                      const(w_kv_t.shape), half(0), half(1), const(w_out_t.shape),
                      rot_t_spec, rot_t_spec],
            out_specs=[tok_spec, last_spec, last_spec],
            scratch_shapes=[pltpu.VMEM((WINDOW + tile, KV_DIM), BF16),
                            pltpu.VMEM((KV_DIM, WINDOW + tile), BF16),
                            pltpu.VMEM((D_MODEL, tile), BF16),
                            pltpu.VMEM((D_MODEL, tile), F32),
                            pltpu.VMEM((2, WINDOW + Q_BLOCK, Q_BLOCK), F32)]),
        out_shape=[jax.ShapeDtypeStruct((batch * seq, D_MODEL), F32),
                   jax.ShapeDtypeStruct((batch, KV_DIM, WINDOW), F32),
                   jax.ShapeDtypeStruct((batch, KV_DIM, WINDOW), F32)],
        compiler_params=pltpu.CompilerParams(
            dimension_semantics=("arbitrary", "arbitrary"), vmem_limit_bytes=VMEM_LIMIT_BYTES),
        name="layer_b_prompt",
    )(sinks, h, kv_norm, norm_b, final_norm, w_kv_t, w_qg_t, w_qg_t, w_out_t, cos_t, sin_t)


def _layer_b_sample_kernel(sinks_ref, h_ref, nbg_ref, fg_ref, wqgt_ref, woutt_ref,
                           cos_ref, slo_ref, shi_ref, knew_ref, vnew_ref, ck_ref, cv_ref,
                           y_ref, q_scr, gate_scr, o_scr, *, n_seq, b_tile):
    step = pl.program_id(0)

    @pl.when(step == 0)
    def _():
        h = h_ref[...]
        hn = h * lax.rsqrt(jnp.mean(h * h, axis=-1, keepdims=True) + EPS)
        qg = _dot_nt((hn * nbg_ref[...]).astype(BF16), wqgt_ref[...])
        cos, slo, shi = cos_ref[...], slo_ref[...], shi_ref[...]
        for c in range(D_MODEL // LANES):
            cols = slice(c * LANES, (c + 1) * LANES)
            q2 = _rotate(qg[:, cols], cos, slo, shi) * HEAD_DIM ** -0.5
            g2 = qg[:, D_MODEL + c * LANES:D_MODEL + (c + 1) * LANES]
            for i in range(LANES // HEAD_DIM):
                dst = _member_major(c * (LANES // HEAD_DIM) + i)
                q_scr[:, dst] = q2[:, i * HEAD_DIM:(i + 1) * HEAD_DIM]
                gate_scr[:, dst] = g2[:, i * HEAD_DIM:(i + 1) * HEAD_DIM]

    n_rows = GQA_GROUP * N_KV_HEADS * SAMPLE_GROUP
    row = lax.broadcasted_iota(jnp.int32, (n_rows, 1), 0)
    row_kh = (row // SAMPLE_GROUP) % N_KV_HEADS
    row_seq = row % SAMPLE_GROUP
    lane_kh = lax.broadcasted_iota(jnp.int32, (1, KV_DIM), 1) // HEAD_DIM
    own = row_kh == lane_kh
    sink = jnp.zeros((n_rows, 1), F32)
    for r in range(GQA_GROUP):
        for kh in range(N_KV_HEADS):
            sink = jnp.where(row // SAMPLE_GROUP == r * N_KV_HEADS + kh,
                             sinks_ref[kh * GQA_GROUP + r], sink)
    n_blk = GQA_GROUP * N_KV_HEADS

    def group(i, carry):
        b0 = i * SAMPLE_GROUP
        g0 = pl.multiple_of(step * b_tile + b0, SAMPLE_GROUP)
        seqs = pl.ds(g0, SAMPLE_GROUP)
        q8 = q_scr[seqs, :]
        qexp = jnp.concatenate(
            [q8[:, r * KV_DIM:(r + 1) * KV_DIM] for r in range(GQA_GROUP)
             for _ in range(N_KV_HEADS)], axis=0)
        qexp = jnp.where(own, qexp, 0.0).astype(BF16)
        knew8 = knew_ref[seqs, :].astype(BF16).astype(F32)
        vnew8 = vnew_ref[seqs, :].astype(BF16).astype(F32)
        s_new = jnp.sum(qexp.astype(F32) * jnp.concatenate([knew8] * n_blk, axis=0),
                        axis=1, keepdims=True)
        s_old = jnp.zeros((n_rows, WINDOW), F32)
        for b in range(SAMPLE_GROUP):
            s_b = _dot(qexp, ck_ref[b0 + b].astype(BF16))
            s_old = jnp.where(row_seq == b, s_b, s_old)
        m = jnp.maximum(jnp.maximum(jnp.max(s_old, axis=1, keepdims=True), s_new), sink)
        p_old = jnp.exp(s_old - m)
        p_new = jnp.exp(s_new - m)
        denom = jnp.sum(p_old, axis=1, keepdims=True) + p_new + jnp.exp(sink - m)
        p_old = p_old.astype(BF16)
        o = jnp.zeros((n_rows, KV_DIM), F32)
        for b in range(SAMPLE_GROUP):
            o_b = _dot_nt(p_old, cv_ref[b0 + b].astype(BF16))
            o = jnp.where(row_seq == b, o_b, o)
        o = (o + p_new.astype(BF16).astype(F32) * jnp.concatenate([vnew8] * n_blk, axis=0)) / denom
        o = jnp.where(own, o, 0.0)
        for r in range(GQA_GROUP):
            blks = [o[(r * N_KV_HEADS + kh) * SAMPLE_GROUP:(r * N_KV_HEADS + kh + 1) * SAMPLE_GROUP]
                    for kh in range(N_KV_HEADS)]
            o_scr[seqs, r * KV_DIM:(r + 1) * KV_DIM] = (blks[0] + blks[1]) + (blks[2] + blks[3])
        return carry

    lax.fori_loop(0, b_tile // SAMPLE_GROUP, group, 0)

    @pl.when(step == pl.num_programs(0) - 1)
    def _():
        og_mm = o_scr[...] * _silu(gate_scr[...])
        og = jnp.concatenate(
            [og_mm[:, _member_major(hd)] for hd in range(N_HEADS)], axis=1).astype(BF16)
        h2 = h_ref[...] + _dot_nt(og, woutt_ref[...])
        y_ref[...] = _rms(h2, fg_ref[...])


def _layer_b_sample(h, sinks, norm_b, final_norm, w_qg_t, w_out_t, knew, vnew, cache_k, cache_v):
    n_seq = h.shape[0]
    b_tile = SAMPLE_B_TILE
    cos, slo, shi = _rotary_tables(jnp.full((1,), PAST_LEN, F32))

    def const(shape):
        return pl.BlockSpec(shape, lambda *_: (0,) * len(shape))

    assert cache_k.shape == (n_seq, KV_DIM, WINDOW)
    cache_spec = pl.BlockSpec((b_tile, KV_DIM, WINDOW), lambda i, *_: (i, 0, 0))
    consts = (h, norm_b, final_norm, w_qg_t, w_out_t, cos, slo, shi, knew, vnew)
    return pl.pallas_call(
        functools.partial(_layer_b_sample_kernel, n_seq=n_seq, b_tile=b_tile),
        grid_spec=pltpu.PrefetchScalarGridSpec(
            num_scalar_prefetch=1,
            grid=(n_seq // b_tile,),
            in_specs=[const(c.shape) for c in consts] + [cache_spec, cache_spec],
            out_specs=const((n_seq, D_MODEL)),
            scratch_shapes=[pltpu.VMEM((n_seq, D_MODEL), F32),
                            pltpu.VMEM((n_seq, D_MODEL), F32),
                            pltpu.VMEM((n_seq, D_MODEL), F32)]),
        out_shape=jax.ShapeDtypeStruct((n_seq, D_MODEL), F32),
        compiler_params=pltpu.CompilerParams(
            dimension_semantics=("arbitrary",), vmem_limit_bytes=VMEM_LIMIT_BYTES),
        name="layer_b_sample",
    )(sinks, *consts, cache_k, cache_v)


def kernel(x_prompt, x_sample, cache_k, cache_v, norm_a, w_in_a, v_norm_a, w_s_a, b_s_a, w_out_a,
           kv_norm, w_kv, norm_b, w_in_b, sinks_b, w_out_b, final_norm):
    batch, seq, _ = x_prompt.shape
    n_seq, dec_seq, _ = x_sample.shape
    assert dec_seq == 1 and seq % CHUNK == 0 and cache_k.shape[1] == WINDOW
    assert norm_a.shape[0] == 1 and norm_b.shape[0] == 1

    row = lambda g: g.reshape(1, -1)
    w_kv_t =_transpose_cast(w_kv)
    w_qg_t = _transpose_cast(w_in_b[0])
    w_out_b_t = _transpose_cast(w_out_b[0])
    bs_chunk = jnp.repeat(b_s_a[0].T, A_GROUP_DIM, axis=1)
    ws_one = jnp.repeat(w_s_a[0, :, 0, 0], A_GROUP_DIM).reshape(1, A_WIDTH)
    bs_one = jnp.repeat(b_s_a[0, :, 0], A_GROUP_DIM).reshape(1, A_WIDTH)

    def to_window(x_t):
        n = x_t.shape[0]
        return x_t.reshape(n, N_KV_HEADS, HEAD_DIM, WINDOW).transpose(0, 3, 1, 2)

    def from_window(x):
        return x.transpose(0, 2, 3, 1).reshape(x.shape[0], KV_DIM, WINDOW)

    cache_kt, cache_vt = from_window(cache_k), from_window(cache_v)

    w_in_a16, w_out_a16, h_s, av_s, knew, vnew, knew_t, vnew_t = _layer_a_sample(
        x_sample.reshape(n_seq, D_MODEL), row(norm_a[0]), w_in_a[0], row(v_norm_a[0]), ws_one,
        bs_one, w_out_a[0], row(kv_norm), w_kv_t)
    h_p, kt_s, vt_s = _layer_a_prompt(
        x_prompt.reshape(batch * seq, D_MODEL), row(norm_a[0]), w_in_a16, row(v_norm_a[0]),
        w_s_a[0], bs_chunk, w_out_a16, cache_kt, cache_vt, knew_t, vnew_t)

    y_p, kt_p, vt_p = _layer_b_prompt(
        h_p, sinks_b[0], row(kv_norm), row(norm_b[0]), row(final_norm),
        w_kv_t, w_qg_t, w_out_b_t, batch=batch, seq=seq)
    y_s = _layer_b_sample(h_s, sinks_b[0], row(norm_b[0]), row(final_norm), w_qg_t, w_out_b_t,
                          knew, vnew, cache_kt, cache_vt)

    return (y_p.reshape(batch, seq, D_MODEL),
            y_s.reshape(n_seq, 1, D_MODEL),
            to_window(kt_p),
            to_window(vt_p),
            to_window(kt_s),
            to_window(vt_s),
            av_s.reshape(1, n_seq, 1, A_WIDTH))
```

```python
import functools

import jax
import jax.numpy as jnp
from jax import lax
from jax.experimental import pallas as pl
from jax.experimental.pallas import tpu as pltpu

D_MODEL = 1024
PAST_LEN = 8192
CHUNK = 128
A_WIDTH = 2 * D_MODEL
A_GROUPS = 8
A_GROUP_DIM = A_WIDTH // A_GROUPS
HEAD_DIM = 64
N_HEADS = D_MODEL // HEAD_DIM
N_KV_HEADS = 4
GQA_GROUP = N_HEADS // N_KV_HEADS
KV_DIM = N_KV_HEADS * HEAD_DIM
WINDOW = 128
Q_BLOCK = 128
ROT_DIM = HEAD_DIM // 4
ROPE_THETA = 500000.0
EPS = 1e-5

LANES = 128
BF16_SUBLANES = 16
LOG2_E = 1.4426950408889634
Q_SCALE_LOG2 = HEAD_DIM ** -0.5 * LOG2_E
VMEM_LIMIT_BYTES = 56 * 1024 * 1024

A_TILE = 512
B_TILE = 512
B_SUBTILES = 2
PREP_COLS = 512
A_SAMPLE_COLS = 1024
A_SAMPLE_ROWS = 1024
OUT_ROWS = 256
ATTN_AHEAD = 2
SAMPLE_B_TILE = 16
SAMPLE_GROUP = 8

F32 = jnp.float32
BF16 = jnp.bfloat16


def _rms(x, g):
    return x * lax.rsqrt(jnp.mean(x * x, axis=-1, keepdims=True) + EPS) * g


def _silu(x):
    return x * jax.nn.sigmoid(x)


def _dot(a, b):
    return jnp.dot(a, b, preferred_element_type=F32)


def _dot_nt(a, b):
    return lax.dot_general(a, b, (((1,), (1,)), ((), ())), preferred_element_type=F32)


def _zero_of(x):
    bits = pltpu.bitcast(x, jnp.uint32)
    return ((bits >> 16) >> 16).astype(F32)


def _member_major(head):
    kh, r = divmod(head, GQA_GROUP)
    start = (r * N_KV_HEADS + kh) * HEAD_DIM
    return slice(start, start + HEAD_DIM)


def _rotate(x, cos, sin_lo, sin_hi):
    return (x * cos + pltpu.roll(x, LANES - ROT_DIM // 2, 1) * sin_lo
            + pltpu.roll(x, ROT_DIM // 2, 1) * sin_hi)


def _rotary_tables(positions):
    lane = jnp.arange(LANES) % HEAD_DIM
    freq = (2 * (lane % (ROT_DIM // 2))).astype(F32)
    ang = positions[:, None] * (ROPE_THETA ** (-freq / ROT_DIM))[None, :]
    first = (lane < ROT_DIM // 2)[None, :]
    second = ((lane >= ROT_DIM // 2) & (lane < ROT_DIM))[None, :]
    cos = jnp.where(first | second, jnp.cos(ang), 1.0)
    sin_lo = jnp.where(first, -jnp.sin(ang), 0.0)
    sin_hi = jnp.where(second, jnp.sin(ang), 0.0)
    return cos, sin_lo, sin_hi


def _transpose_cast_kernel(w_ref, o_ref):
    o_ref[...] = w_ref[...].T.astype(BF16)


def _transpose_cast(w):
    k, n = w.shape
    return pl.pallas_call(
        _transpose_cast_kernel,
        grid=(n // PREP_COLS,),
        in_specs=[pl.BlockSpec((k, PREP_COLS), lambda j: (0, j))],
        out_specs=pl.BlockSpec((PREP_COLS, k), lambda j: (j, 0)),
        out_shape=jax.ShapeDtypeStruct((n, k), BF16),
        compiler_params=pltpu.CompilerParams(dimension_semantics=("arbitrary",)),
        name="transpose_cast",
    )(w)


def _layer_a_tile(x, ng_ref, win_ref, vg_ref, ws_ref, bs_ref, wout_ref, y_scr, *, tile):
    xn = _rms(x, ng_ref[...]).astype(BF16)
    vb = _rms(_dot(xn, win_ref[:, A_WIDTH:2 * A_WIDTH]), vg_ref[...]).astype(BF16)
    row = lax.broadcasted_iota(jnp.int32, (CHUNK, CHUNK), 0)
    col = lax.broadcasted_iota(jnp.int32, (CHUNK, CHUNK), 1)
    tri = row >= col
    width = 2 * A_GROUP_DIM
    for pair in range(A_GROUPS // 2):
        cols = slice(pair * width, (pair + 1) * width)
        u = _dot(xn, win_ref[:, cols])
        gate = _dot(xn, win_ref[:, 2 * A_WIDTH + pair * width:2 * A_WIDTH + (pair + 1) * width])
        ws = [jnp.where(tri, ws_ref[g], 0.0).astype(BF16) for g in (2 * pair, 2 * pair + 1)]
        z = jnp.concatenate(
            [jnp.concatenate(
                [_dot(ws[i], vb[c * CHUNK:(c + 1) * CHUNK,
                                (2 * pair + i) * A_GROUP_DIM:(2 * pair + i + 1) * A_GROUP_DIM])
                 for i in range(2)], axis=1) + bs_ref[:, cols]
             for c in range(tile // CHUNK)], axis=0)
        y_scr[:, cols] = (u * z * _silu(gate)).astype(BF16)
    return x + _dot(y_scr[...], wout_ref[...])


def _layer_a_prompt_kernel(x_ref, ng_ref, win_ref, vg_ref, ws_ref, bs_ref, wout_ref,
                           ck_ref, cv_ref, knewt_ref, vnewt_ref,
                           h_ref, kout_ref, vout_ref, ck16_ref, cv16_ref, y_scr, *, tile, n_roll):
    is_last = lax.broadcasted_iota(jnp.int32, (KV_DIM, WINDOW), 1) == WINDOW - 1
    for b in range(n_roll):
        g = pl.program_id(0) * n_roll + b
        blk = pl.ds(pl.multiple_of((g // LANES) * LANES, LANES), LANES)
        to_last = LANES - 1 - g % LANES
        kout_ref[b] = jnp.where(is_last, pltpu.roll(knewt_ref[:, blk], to_last, 1),
                                pltpu.roll(ck_ref[b], WINDOW - 1, 1))
        vout_ref[b] = jnp.where(is_last, pltpu.roll(vnewt_ref[:, blk], to_last, 1),
                                pltpu.roll(cv_ref[b], WINDOW - 1, 1))
        ck16_ref[b] = ck_ref[b].astype(BF16)
        cv16_ref[b] = cv_ref[b].astype(BF16)
    h_ref[...] = _layer_a_tile(x_ref[...], ng_ref, win_ref, vg_ref, ws_ref, bs_ref, wout_ref,
                               y_scr, tile=tile)


def _layer_a_sample_kernel(x_ref, ng_ref, win_ref, vg_ref, ws_ref, bs_ref, wout_ref,
                           kvg_ref, wkvt_ref, cos_ref, slo_ref, shi_ref,
                           win16_ref, wout16_ref, h_ref, av_ref, knew_ref, vnew_ref, knewt_ref,
                           vnewt_ref, xn_scr, proj_scr, y_scr, acc_scr, *, n_in, n_out):
    step = pl.program_id(0)
    blocks_per_branch = n_in // 3

    @pl.when(step == 0)
    def _():
        xn_scr[...] = _rms(x_ref[...], ng_ref[...]).astype(BF16)

    @pl.when(step < n_in)
    def _():
        w = win_ref[...].astype(BF16)
        win16_ref[...] = w
        proj_scr[step] = _dot(xn_scr[...], w)

    @pl.when(step == n_in)
    def _():
        def branch(i):
            return jnp.concatenate(
                [proj_scr[i * blocks_per_branch + j] for j in range(blocks_per_branch)], axis=1)

        v = _rms(branch(1), vg_ref[...])
        av_ref[...] = v
        z = v * ws_ref[...] + bs_ref[...]
        y = (branch(0) * z * _silu(branch(2))).astype(BF16)
        rows = A_WIDTH // n_out
        for j in range(n_out):
            y_scr[j] = y[:, j * rows:(j + 1) * rows]

    @pl.when(step >= n_in)
    def _():
        w = wout_ref[...].astype(BF16)
        wout16_ref[...] = w
        part = _dot(y_scr[step - n_in], w)

        @pl.when(step == n_in)
        def _():
            acc_scr[...] = part

        @pl.when(step > n_in)
        def _():
            acc_scr[...] += part

    @pl.when(step == n_in + n_out - 1)
    def _():
        h = x_ref[...] + acc_scr[...]
        h_ref[...] = h
        hn = h * lax.rsqrt(jnp.mean(h * h, axis=-1, keepdims=True) + EPS)
        kv = _dot_nt((hn * kvg_ref[...]).astype(BF16), wkvt_ref[...])
        cos, slo, shi = cos_ref[...], slo_ref[...], shi_ref[...]
        k = jnp.concatenate(
            [_rotate(kv[:, c * LANES:(c + 1) * LANES], cos, slo, shi)
             for c in range(KV_DIM // LANES)], axis=1)
        knew_ref[...] = k
        vnew_ref[...] = kv[:, KV_DIM:]
        knewt_ref[...] = k.T
        vnewt_ref[...] = kv[:, KV_DIM:].T


def _const_spec(shape):
    return pl.BlockSpec(shape, lambda *_: (0,) * len(shape), pipeline_mode=pl.Buffered(1))


def _layer_a_prompt(x, norm_g, w_in, v_norm_g, ws, bs, w_out, cache_k, cache_v, knew_t, vnew_t):
    n_tok = x.shape[0]
    tile = A_TILE
    n_steps = n_tok // tile
    n_seq = cache_k.shape[0]
    n_roll = n_seq // n_steps
    assert n_roll * n_steps == n_seq and n_seq % LANES == 0
    tok_spec = pl.BlockSpec((tile, D_MODEL), lambda i: (i, 0))
    cache_spec = pl.BlockSpec((n_roll, KV_DIM, WINDOW), lambda i: (i, 0, 0))
    consts = (norm_g, w_in, v_norm_g, ws, bs, w_out)
    return pl.pallas_call(
        functools.partial(_layer_a_prompt_kernel, tile=tile, n_roll=n_roll),
        grid=(n_steps,),
        in_specs=[tok_spec] + [_const_spec(c.shape) for c in consts]
        + [cache_spec, cache_spec, _const_spec(knew_t.shape), _const_spec(vnew_t.shape)],
        out_specs=[tok_spec, cache_spec, cache_spec, cache_spec, cache_spec],
        out_shape=[jax.ShapeDtypeStruct((n_tok, D_MODEL), F32),
                   jax.ShapeDtypeStruct(cache_k.shape, F32),
                   jax.ShapeDtypeStruct(cache_v.shape, F32),
                   jax.ShapeDtypeStruct(cache_k.shape, BF16),
                   jax.ShapeDtypeStruct(cache_v.shape, BF16)],
        scratch_shapes=[pltpu.VMEM((tile, A_WIDTH), BF16)],
        compiler_params=pltpu.CompilerParams(
            dimension_semantics=("arbitrary",), vmem_limit_bytes=VMEM_LIMIT_BYTES),
        name="layer_a_prompt",
    )(x, *consts, cache_k, cache_v, knew_t, vnew_t)


def _layer_a_sample(x, norm_g, w_in, v_norm_g, ws, bs, w_out, kv_norm, w_kv_t):
    n_seq = x.shape[0]
    n_in = w_in.shape[1] // A_SAMPLE_COLS
    n_out = w_out.shape[0] // A_SAMPLE_ROWS
    assert n_in % 3 == 0
    cos, slo, shi = _rotary_tables(jnp.full((1,), PAST_LEN, F32))
    whole = lambda shape: pl.BlockSpec(shape, lambda i: (0,) * len(shape))
    win_block = lambda i: (0, jnp.minimum(i, n_in - 1))
    wout_block = lambda i: (jnp.maximum(i - n_in, 0), 0)
    small_dims = [(n_seq, D_MODEL), (n_seq, A_WIDTH), (n_seq, KV_DIM), (n_seq, KV_DIM),
                  (KV_DIM, n_seq), (KV_DIM, n_seq)]
    return pl.pallas_call(
        functools.partial(_layer_a_sample_kernel, n_in=n_in, n_out=n_out),
        grid=(n_in + n_out,),
        in_specs=[whole(x.shape), whole(norm_g.shape),
                  pl.BlockSpec((D_MODEL, A_SAMPLE_COLS), win_block),
                  whole(v_norm_g.shape), whole(ws.shape), whole(bs.shape),
                  pl.BlockSpec((A_SAMPLE_ROWS, D_MODEL), wout_block),
                  whole(kv_norm.shape), whole(w_kv_t.shape),
                  whole(cos.shape), whole(slo.shape), whole(shi.shape)],
        out_specs=[pl.BlockSpec((D_MODEL, A_SAMPLE_COLS), win_block),
                   pl.BlockSpec((A_SAMPLE_ROWS, D_MODEL), wout_block)]
        + [whole(d) for d in small_dims],
        out_shape=[jax.ShapeDtypeStruct(w_in.shape, BF16), jax.ShapeDtypeStruct(w_out.shape, BF16)]
        + [jax.ShapeDtypeStruct(d, F32) for d in small_dims],
        scratch_shapes=[pltpu.VMEM((n_seq, D_MODEL), BF16),
                        pltpu.VMEM((n_in, n_seq, A_SAMPLE_COLS), F32),
                        pltpu.VMEM((n_out, n_seq, A_SAMPLE_ROWS), BF16),
                        pltpu.VMEM((n_seq, D_MODEL), F32)],
        compiler_params=pltpu.CompilerParams(
            dimension_semantics=("arbitrary",), vmem_limit_bytes=VMEM_LIMIT_BYTES),
        name="layer_a_sample",
    )(x, norm_g, w_in, v_norm_g, ws, bs, w_out, kv_norm, w_kv_t, cos, slo, shi)


def _layer_b_prompt_kernel(sinks_ref, h_ref, kvg_ref, nbg_ref, fg_ref, wkvt_ref, wqt_ref,
                           wgt_ref, woutt_ref, cost_ref, sint_ref, y_ref, kout_ref, vout_ref,
                           *scratch, tile, n_t, n_sub):
    for sub in range(n_sub):
        rows = pl.ds(sub * tile, tile)
        _layer_b_prompt_tile(
            sinks_ref, h_ref.at[rows, :], kvg_ref, nbg_ref, fg_ref, wkvt_ref, wqt_ref, wgt_ref,
            woutt_ref, cost_ref.at[:, rows], sint_ref.at[:, rows], y_ref.at[rows, :], kout_ref,
            vout_ref, *scratch, tile=tile, n_t=n_t, t=pl.program_id(1) * n_sub + sub,
            first_possible=sub == 0, last_possible=sub == n_sub - 1)


def _layer_b_prompt_tile(sinks_ref, h_ref, kvg_ref, nbg_ref, fg_ref, wkvt_ref, wqt_ref,
                         wgt_ref, woutt_ref, cost_ref, sint_ref,
                         y_ref, kout_ref, vout_ref,
                         kext_scr, vtext_scr, qt_scr, ogt_scr, bias_scr,
                         *, tile, n_t, t, first_possible, last_possible):
    n_keys = WINDOW + Q_BLOCK

    if first_possible:
        @pl.when((pl.program_id(0) == 0) & (t == 0))
        def _():
            j = lax.broadcasted_iota(jnp.int32, (n_keys, Q_BLOCK), 0)
            i = lax.broadcasted_iota(jnp.int32, (n_keys, Q_BLOCK), 1)
            band = (j >= i) & (j <= WINDOW + i)
            bias_scr[0] = jnp.where(band & (j >= WINDOW), 0.0, -jnp.inf)
            bias_scr[1] = jnp.where(band, 0.0, -jnp.inf)

        @pl.when(t == 0)
        def _():
            kext_scr[0:WINDOW, :] = jnp.zeros((WINDOW, KV_DIM), BF16)
            vtext_scr[:, 0:WINDOW] = jnp.zeros((KV_DIM, WINDOW), BF16)

    h = h_ref[...]
    hn = h * lax.rsqrt(jnp.mean(h * h, axis=-1, keepdims=True) + EPS)
    xkv = (hn * kvg_ref[...]).astype(BF16)
    xb = (hn * nbg_ref[...]).astype(BF16)

    cost, sint = cost_ref[...], sint_ref[...]
    half = ROT_DIM // 2

    def rotate_head(rows):
        lo, hi = rows[0:half, :], rows[half:ROT_DIM, :]
        return jnp.concatenate(
            [lo * cost - hi * sint, hi * cost + lo * sint, rows[ROT_DIM:, :]], axis=0)

    kvt = _dot_nt(wkvt_ref[...], xkv)
    kt = jnp.concatenate(
        [rotate_head(kvt[kh * HEAD_DIM:(kh + 1) * HEAD_DIM, :]) for kh in range(N_KV_HEADS)],
        axis=0)
    vt = kvt[KV_DIM:, :]
    kext_scr[WINDOW:, :] = kt.T.astype(BF16)
    vtext_scr[:, WINDOW:] = vt.astype(BF16)

    if last_possible:
        @pl.when(t == n_t - 1)
        def _():
            kout_ref[0] = kt[:, tile - WINDOW:]
            vout_ref[0] = vt[:, tile - WINDOW:]

    qt = _dot_nt(wqt_ref[...], xb)
    for hd in range(N_HEADS):
        rot = rotate_head(qt[hd * HEAD_DIM:(hd + 1) * HEAD_DIM, :])
        qt_scr[hd * HEAD_DIM:(hd + 1) * HEAD_DIM, :] = (rot * Q_SCALE_LOG2).astype(BF16)

    lane_head = lax.broadcasted_iota(jnp.int32, (1, GQA_GROUP * Q_BLOCK), 1) // Q_BLOCK
    zeros_half = jnp.zeros((HEAD_DIM, GQA_GROUP * Q_BLOCK), BF16)
    ones_rows = jnp.ones((BF16_SUBLANES, n_keys), BF16)

    def scores(qb, kh):
        qcols = slice(qb * Q_BLOCK, (qb + 1) * Q_BLOCK)
        keys = slice(qb * Q_BLOCK, qb * Q_BLOCK + n_keys)
        q4 = jnp.concatenate(
            [qt_scr[(kh * GQA_GROUP + r) * HEAD_DIM:(kh * GQA_GROUP + r + 1) * HEAD_DIM, qcols]
             for r in range(GQA_GROUP)], axis=1)
        q4 = jnp.concatenate([q4, zeros_half] if kh % 2 == 0 else [zeros_half, q4], axis=0)
        kblk = kext_scr[keys, (kh // 2) * LANES:(kh // 2 + 1) * LANES]
        return _dot(kblk, q4)

    def finish(qb, kh, s, s_ahead):
        qcols = slice(qb * Q_BLOCK, (qb + 1) * Q_BLOCK)
        keys = slice(qb * Q_BLOCK, qb * Q_BLOCK + n_keys)
        if first_possible and qb == 0:
            bias = bias_scr[jnp.where(t > 0, 1, 0)]
        else:
            bias = bias_scr[1]
        s = s + jnp.concatenate([bias] * GQA_GROUP, axis=1)
        sink = jnp.zeros((1, GQA_GROUP * Q_BLOCK), F32)
        for r in range(GQA_GROUP):
            sink = jnp.where(lane_head == r, sinks_ref[kh * GQA_GROUP + r] * LOG2_E, sink)
        m = jnp.maximum(jnp.max(s, axis=0, keepdims=True), sink)
        p = jnp.exp2(s - m)
        if s_ahead is not None:
            p = jnp.concatenate(
                [p[:n_keys - 8, :], p[n_keys - 8:, :] + _zero_of(s_ahead[0:8, :])], axis=0)
        vt_ones = jnp.concatenate(
            [vtext_scr[kh * HEAD_DIM:(kh + 1) * HEAD_DIM, keys], ones_rows], axis=0)
        ot = _dot(vt_ones, p.astype(BF16))
        denom = ot[HEAD_DIM:HEAD_DIM + 1, :] + jnp.exp2(sink - m)
        ot = ot[0:HEAD_DIM, :] * (1.0 / denom)
        for r in range(GQA_GROUP):
            hd = kh * GQA_GROUP + r
            ogt_scr[hd * HEAD_DIM:(hd + 1) * HEAD_DIM, qcols] = ot[:, r * Q_BLOCK:(r + 1) * Q_BLOCK]

    blocks = [(qb, kh) for qb in range(tile // Q_BLOCK) for kh in range(N_KV_HEADS)]
    pending = [scores(*blk) for blk in blocks[:ATTN_AHEAD]]
    for n, blk in enumerate(blocks):
        if n + ATTN_AHEAD < len(blocks):
            pending.append(scores(*blocks[n + ATTN_AHEAD]))
        s_cur = pending.pop(0)
        finish(*blk, s_cur, pending[-1] if pending else None)

    kext_scr[0:WINDOW, :] = kext_scr[tile:tile + WINDOW, :]
    vtext_scr[:, 0:WINDOW] = vtext_scr[:, tile:tile + WINDOW]

    ogt = jnp.concatenate(
        [(ogt_scr[rc * OUT_ROWS:(rc + 1) * OUT_ROWS, :]
          * _silu(_dot_nt(wgt_ref[rc * OUT_ROWS:(rc + 1) * OUT_ROWS, :], xb))).astype(BF16)
         for rc in range(D_MODEL // OUT_ROWS)], axis=0)
    h2 = jnp.concatenate(
        [h[:, rc * OUT_ROWS:(rc + 1) * OUT_ROWS]
         + _dot(woutt_ref[rc * OUT_ROWS:(rc + 1) * OUT_ROWS, :], ogt).T
         for rc in range(D_MODEL // OUT_ROWS)], axis=1)
    y_ref[...] = _rms(h2, fg_ref[...])


def _layer_b_prompt(h, sinks, kv_norm, norm_b, final_norm, w_kv_t, w_qg_t, w_out_t, *, batch, seq):
    tile = B_TILE
    n_t = seq // tile
    n_sub = B_SUBTILES
    step = n_sub * tile
    n_steps = n_t // n_sub
    pos = jnp.arange(seq, dtype=F32)
    inv = ROPE_THETA ** (-jnp.arange(0, ROT_DIM, 2, dtype=F32) / ROT_DIM)
    ang_t = inv[:, None] * pos[None, :]
    cos_t, sin_t = jnp.cos(ang_t), jnp.sin(ang_t)
    tok_spec = pl.BlockSpec((step, D_MODEL), lambda b, t, *_: (b * n_steps + t, 0))
    rot_t_spec = pl.BlockSpec((ROT_DIM // 2, step), lambda b, t, *_: (0, t))
    last_spec = pl.BlockSpec((1, KV_DIM, WINDOW), lambda b, t, *_: (b, 0, 0))

    def const(shape):
        return pl.BlockSpec(shape, lambda *_: (0,) * len(shape), pipeline_mode=pl.Buffered(1))

    def half(i):
        return pl.BlockSpec((D_MODEL, D_MODEL), lambda *_: (i, 0), pipeline_mode=pl.Buffered(1))

    return pl.pallas_call(
        functools.partial(_layer_b_prompt_kernel, tile=tile, n_t=n_t, n_sub=n_sub),
        grid_spec=pltpu.PrefetchScalarGridSpec(
            num_scalar_prefetch=1,
            grid=(batch, n_steps),
            in_specs=[tok_spec, const(kv_norm.shape), const(norm_b.shape), const(final_norm.shape),
                      const(w_kv_t.shape), half(0), half(1), const(w_out_t.shape),
                      rot_t_spec, rot_t_spec],
            out_specs=[tok_spec, last_spec, last_spec],
            scratch_shapes=[pltpu.VMEM((WINDOW + tile, KV_DIM), BF16),
                            pltpu.VMEM((KV_DIM, WINDOW + tile), BF16),
                            pltpu.VMEM((D_MODEL, tile), BF16),
                            pltpu.VMEM((D_MODEL, tile), F32),
                            pltpu.VMEM((2, WINDOW + Q_BLOCK, Q_BLOCK), F32)]),
        out_shape=[jax.ShapeDtypeStruct((batch * seq, D_MODEL), F32),
                   jax.ShapeDtypeStruct((batch, KV_DIM, WINDOW), F32),
                   jax.ShapeDtypeStruct((batch, KV_DIM, WINDOW), F32)],
        compiler_params=pltpu.CompilerParams(
            dimension_semantics=("arbitrary", "arbitrary"), vmem_limit_bytes=VMEM_LIMIT_BYTES),
        name="layer_b_prompt",
    )(sinks, h, kv_norm, norm_b, final_norm, w_kv_t, w_qg_t, w_qg_t, w_out_t, cos_t, sin_t)


def _layer_b_sample_kernel(sinks_ref, h_ref, nbg_ref, fg_ref, wqgt_ref, woutt_ref,
                           cos_ref, slo_ref, shi_ref, knew_ref, vnew_ref, ck_ref, cv_ref,
                           y_ref, q_scr, gate_scr, o_scr, *, n_seq, b_tile):
    step = pl.program_id(0)

    @pl.when(step == 0)
    def _():
        h = h_ref[...]
        hn = h * lax.rsqrt(jnp.mean(h * h, axis=-1, keepdims=True) + EPS)
        qg = _dot_nt((hn * nbg_ref[...]).astype(BF16), wqgt_ref[...])
        cos, slo, shi = cos_ref[...], slo_ref[...], shi_ref[...]
        for c in range(D_MODEL // LANES):
            cols = slice(c * LANES, (c + 1) * LANES)
            q2 = _rotate(qg[:, cols], cos, slo, shi) * HEAD_DIM ** -0.5
            g2 = qg[:, D_MODEL + c * LANES:D_MODEL + (c + 1) * LANES]
            for i in range(LANES // HEAD_DIM):
                dst = _member_major(c * (LANES // HEAD_DIM) + i)
                q_scr[:, dst] = q2[:, i * HEAD_DIM:(i + 1) * HEAD_DIM]
                gate_scr[:, dst] = g2[:, i * HEAD_DIM:(i + 1) * HEAD_DIM]

    n_rows = GQA_GROUP * N_KV_HEADS * SAMPLE_GROUP
    row = lax.broadcasted_iota(jnp.int32, (n_rows, 1), 0)
    row_kh = (row // SAMPLE_GROUP) % N_KV_HEADS
    row_seq = row % SAMPLE_GROUP
    lane_kh = lax.broadcasted_iota(jnp.int32, (1, KV_DIM), 1) // HEAD_DIM
    own = row_kh == lane_kh
    sink = jnp.zeros((n_rows, 1), F32)
    for r in range(GQA_GROUP):
        for kh in range(N_KV_HEADS):
            sink = jnp.where(row // SAMPLE_GROUP == r * N_KV_HEADS + kh,
                             sinks_ref[kh * GQA_GROUP + r], sink)
    n_blk = GQA_GROUP * N_KV_HEADS

    def group(i, carry):
        b0 = i * SAMPLE_GROUP
        g0 = pl.multiple_of(step * b_tile + b0, SAMPLE_GROUP)
        seqs = pl.ds(g0, SAMPLE_GROUP)
        q8 = q_scr[seqs, :]
        qexp = jnp.concatenate(
            [q8[:, r * KV_DIM:(r + 1) * KV_DIM] for r in range(GQA_GROUP)
             for _ in range(N_KV_HEADS)], axis=0)
        qexp = jnp.where(own, qexp, 0.0).astype(BF16)
        knew8 = knew_ref[seqs, :].astype(BF16).astype(F32)
        vnew8 = vnew_ref[seqs, :].astype(BF16).astype(F32)
        s_new = jnp.sum(qexp.astype(F32) * jnp.concatenate([knew8] * n_blk, axis=0),
                        axis=1, keepdims=True)
        s_old = jnp.zeros((n_rows, WINDOW), F32)
        for b in range(SAMPLE_GROUP):
            s_b = _dot(qexp, ck_ref[b0 + b])
            s_old = jnp.where(row_seq == b, s_b, s_old)
        m = jnp.maximum(jnp.maximum(jnp.max(s_old, axis=1, keepdims=True), s_new), sink)
        p_old = jnp.exp(s_old - m)
        p_new = jnp.exp(s_new - m)
        denom = jnp.sum(p_old, axis=1, keepdims=True) + p_new + jnp.exp(sink - m)
        p_old = p_old.astype(BF16)
        o = jnp.zeros((n_rows, KV_DIM), F32)
        for b in range(SAMPLE_GROUP):
            o_b = _dot_nt(p_old, cv_ref[b0 + b])
            o = jnp.where(row_seq == b, o_b, o)
        o = (o + p_new.astype(BF16).astype(F32) * jnp.concatenate([vnew8] * n_blk, axis=0)) / denom
        o = jnp.where(own, o, 0.0)
        for r in range(GQA_GROUP):
            blks = [o[(r * N_KV_HEADS + kh) * SAMPLE_GROUP:(r * N_KV_HEADS + kh + 1) * SAMPLE_GROUP]
                    for kh in range(N_KV_HEADS)]
            o_scr[seqs, r * KV_DIM:(r + 1) * KV_DIM] = (blks[0] + blks[1]) + (blks[2] + blks[3])
        return carry

    lax.fori_loop(0, b_tile // SAMPLE_GROUP, group, 0, unroll=True)

    @pl.when(step == pl.num_programs(0) - 1)
    def _():
        og_mm = o_scr[...] * _silu(gate_scr[...])
        og = jnp.concatenate(
            [og_mm[:, _member_major(hd)] for hd in range(N_HEADS)], axis=1).astype(BF16)
        h2 = h_ref[...] + _dot_nt(og, woutt_ref[...])
        y_ref[...] = _rms(h2, fg_ref[...])


def _layer_b_sample(h, sinks, norm_b, final_norm, w_qg_t, w_out_t, knew, vnew, cache_k, cache_v):
    n_seq = h.shape[0]
    b_tile = SAMPLE_B_TILE
    cos, slo, shi = _rotary_tables(jnp.full((1,), PAST_LEN, F32))

    def const(shape):
        return pl.BlockSpec(shape, lambda *_: (0,) * len(shape))

    assert cache_k.shape == (n_seq, KV_DIM, WINDOW)
    cache_spec = pl.BlockSpec((b_tile, KV_DIM, WINDOW), lambda i, *_: (i, 0, 0))
    consts = (h, norm_b, final_norm, w_qg_t, w_out_t, cos, slo, shi, knew, vnew)
    return pl.pallas_call(
        functools.partial(_layer_b_sample_kernel, n_seq=n_seq, b_tile=b_tile),
        grid_spec=pltpu.PrefetchScalarGridSpec(
            num_scalar_prefetch=1,
            grid=(n_seq // b_tile,),
            in_specs=[const(c.shape) for c in consts] + [cache_spec, cache_spec],
            out_specs=const((n_seq, D_MODEL)),
            scratch_shapes=[pltpu.VMEM((n_seq, D_MODEL), F32),
                            pltpu.VMEM((n_seq, D_MODEL), F32),
                            pltpu.VMEM((n_seq, D_MODEL), F32)]),
        out_shape=jax.ShapeDtypeStruct((n_seq, D_MODEL), F32),
        compiler_params=pltpu.CompilerParams(
            dimension_semantics=("arbitrary",), vmem_limit_bytes=VMEM_LIMIT_BYTES),
        name="layer_b_sample",
    )(sinks, *consts, cache_k, cache_v)


def kernel(x_prompt, x_sample, cache_k, cache_v, norm_a, w_in_a, v_norm_a, w_s_a, b_s_a, w_out_a,
           kv_norm, w_kv, norm_b, w_in_b, sinks_b, w_out_b, final_norm):
    batch, seq, _ = x_prompt.shape
    n_seq, dec_seq, _ = x_sample.shape
    assert dec_seq == 1 and seq % CHUNK == 0 and cache_k.shape[1] == WINDOW
    assert norm_a.shape[0] == 1 and norm_b.shape[0] == 1

    row = lambda g: g.reshape(1, -1)
    w_kv_t =_transpose_cast(w_kv)
    w_qg_t = _transpose_cast(w_in_b[0])
    w_out_b_t = _transpose_cast(w_out_b[0])
    bs_chunk = jnp.repeat(b_s_a[0].T, A_GROUP_DIM, axis=1)
    ws_one = jnp.repeat(w_s_a[0, :, 0, 0], A_GROUP_DIM).reshape(1, A_WIDTH)
    bs_one = jnp.repeat(b_s_a[0, :, 0], A_GROUP_DIM).reshape(1, A_WIDTH)

    def to_window(x_t):
        n = x_t.shape[0]
        return x_t.reshape(n, N_KV_HEADS, HEAD_DIM, WINDOW).transpose(0, 3, 1, 2)

    def from_window(x):
        return x.transpose(0, 2, 3, 1).reshape(x.shape[0], KV_DIM, WINDOW)

    cache_kt, cache_vt = from_window(cache_k), from_window(cache_v)

    w_in_a16, w_out_a16, h_s, av_s, knew, vnew, knew_t, vnew_t = _layer_a_sample(
        x_sample.reshape(n_seq, D_MODEL), row(norm_a[0]), w_in_a[0], row(v_norm_a[0]), ws_one,
        bs_one, w_out_a[0], row(kv_norm), w_kv_t)
    h_p, kt_s, vt_s, cache_kt16, cache_vt16 = _layer_a_prompt(
        x_prompt.reshape(batch * seq, D_MODEL), row(norm_a[0]), w_in_a16, row(v_norm_a[0]),
        w_s_a[0], bs_chunk, w_out_a16, cache_kt, cache_vt, knew_t, vnew_t)

    y_p, kt_p, vt_p = _layer_b_prompt(
        h_p, sinks_b[0], row(kv_norm), row(norm_b[0]), row(final_norm),
        w_kv_t, w_qg_t, w_out_b_t, batch=batch, seq=seq)
    y_s = _layer_b_sample(h_s, sinks_b[0], row(norm_b[0]), row(final_norm), w_qg_t, w_out_b_t,
                          knew, vnew, cache_kt16, cache_vt16)

    return (y_p.reshape(batch, seq, D_MODEL),
            y_s.reshape(n_seq, 1, D_MODEL),
            to_window(kt_p),
            to_window(vt_p),
            to_window(kt_s),
            to_window(vt_s),
            av_s.reshape(1, n_seq, 1, A_WIDTH))
```

```python
import functools

import jax
import jax.numpy as jnp
from jax import lax
from jax.experimental import pallas as pl
from jax.experimental.pallas import tpu as pltpu

D_MODEL = 1024
PAST_LEN = 8192
CHUNK = 128
A_WIDTH = 2 * D_MODEL
A_GROUPS = 8
A_GROUP_DIM = A_WIDTH // A_GROUPS
HEAD_DIM = 64
N_HEADS = D_MODEL // HEAD_DIM
N_KV_HEADS = 4
GQA_GROUP = N_HEADS // N_KV_HEADS
KV_DIM = N_KV_HEADS * HEAD_DIM
WINDOW = 128
Q_BLOCK = 128
ROT_DIM = HEAD_DIM // 4
ROPE_THETA = 500000.0
EPS = 1e-5

LANES = 128
BF16_SUBLANES = 16
LOG2_E = 1.4426950408889634
Q_SCALE_LOG2 = HEAD_DIM ** -0.5 * LOG2_E
VMEM_LIMIT_BYTES = 56 * 1024 * 1024

A_TILE = 512
B_TILE = 512
B_SUBTILES = 2
PREP_COLS = 512
A_SAMPLE_COLS = 1024
A_SAMPLE_ROWS = 1024
OUT_ROWS = 256
ATTN_AHEAD = 2
SAMPLE_B_TILE = 16
SAMPLE_GROUP = 8

F32 = jnp.float32
BF16 = jnp.bfloat16


def _rms(x, g):
    return x * lax.rsqrt(jnp.mean(x * x, axis=-1, keepdims=True) + EPS) * g


def _silu(x):
    return x * jax.nn.sigmoid(x)


def _dot(a, b):
    return jnp.dot(a, b, preferred_element_type=F32)


def _dot_nt(a, b):
    return lax.dot_general(a, b, (((1,), (1,)), ((), ())), preferred_element_type=F32)


def _zero_of(x):
    bits = pltpu.bitcast(x, jnp.uint32)
    return ((bits >> 16) >> 16).astype(F32)


def _member_major(head):
    kh, r = divmod(head, GQA_GROUP)
    start = (r * N_KV_HEADS + kh) * HEAD_DIM
    return slice(start, start + HEAD_DIM)


def _rotate(x, cos, sin_lo, sin_hi):
    return (x * cos + pltpu.roll(x, LANES - ROT_DIM // 2, 1) * sin_lo
            + pltpu.roll(x, ROT_DIM // 2, 1) * sin_hi)


def _rotary_tables(positions):
    lane = jnp.arange(LANES) % HEAD_DIM
    freq = (2 * (lane % (ROT_DIM // 2))).astype(F32)
    ang = positions[:, None] * (ROPE_THETA ** (-freq / ROT_DIM))[None, :]
    first = (lane < ROT_DIM // 2)[None, :]
    second = ((lane >= ROT_DIM // 2) & (lane < ROT_DIM))[None, :]
    cos = jnp.where(first | second, jnp.cos(ang), 1.0)
    sin_lo = jnp.where(first, -jnp.sin(ang), 0.0)
    sin_hi = jnp.where(second, jnp.sin(ang), 0.0)
    return cos, sin_lo, sin_hi


def _transpose_cast_kernel(w_ref, o_ref):
    o_ref[...] = w_ref[...].T.astype(BF16)


def _transpose_cast(w):
    k, n = w.shape
    return pl.pallas_call(
        _transpose_cast_kernel,
        grid=(n // PREP_COLS,),
        in_specs=[pl.BlockSpec((k, PREP_COLS), lambda j: (0, j))],
        out_specs=pl.BlockSpec((PREP_COLS, k), lambda j: (j, 0)),
        out_shape=jax.ShapeDtypeStruct((n, k), BF16),
        compiler_params=pltpu.CompilerParams(dimension_semantics=("arbitrary",)),
        name="transpose_cast",
    )(w)


def _layer_a_tile(x, ng_ref, win_ref, vg_ref, ws_ref, bs_ref, wout_ref, y_scr, *, tile):
    xn = _rms(x, ng_ref[...]).astype(BF16)
    vb = _rms(_dot(xn, win_ref[:, A_WIDTH:2 * A_WIDTH]), vg_ref[...]).astype(BF16)
    row = lax.broadcasted_iota(jnp.int32, (CHUNK, CHUNK), 0)
    col = lax.broadcasted_iota(jnp.int32, (CHUNK, CHUNK), 1)
    tri = row >= col
    width = 2 * A_GROUP_DIM
    for pair in range(A_GROUPS // 2):
        cols = slice(pair * width, (pair + 1) * width)
        u = _dot(xn, win_ref[:, cols])
        gate = _dot(xn, win_ref[:, 2 * A_WIDTH + pair * width:2 * A_WIDTH + (pair + 1) * width])
        ws = [jnp.where(tri, ws_ref[g], 0.0).astype(BF16) for g in (2 * pair, 2 * pair + 1)]
        z = jnp.concatenate(
            [jnp.concatenate(
                [_dot(ws[i], vb[c * CHUNK:(c + 1) * CHUNK,
                                (2 * pair + i) * A_GROUP_DIM:(2 * pair + i + 1) * A_GROUP_DIM])
                 for i in range(2)], axis=1) + bs_ref[:, cols]
             for c in range(tile // CHUNK)], axis=0)
        y_scr[:, cols] = (u * z * _silu(gate)).astype(BF16)
    return x + _dot(y_scr[...], wout_ref[...])


def _layer_a_prompt_kernel(x_ref, ng_ref, win_ref, vg_ref, ws_ref, bs_ref, wout_ref,
                           ck_ref, cv_ref, knewt_ref, vnewt_ref, winb_ref, woutb_ref,
                           h_ref, kout_ref, vout_ref, wqgt_ref, woutbt_ref, y_scr, *, tile, n_roll):
    wqgt_ref[...] = winb_ref[...].T.astype(BF16)
    woutbt_ref[...] = woutb_ref[...].T.astype(BF16)
    is_last = lax.broadcasted_iota(jnp.int32, (KV_DIM, WINDOW), 1) == WINDOW - 1
    for b in range(n_roll):
        g = pl.program_id(0) * n_roll + b
        blk = pl.ds(pl.multiple_of((g // LANES) * LANES, LANES), LANES)
        to_last = LANES - 1 - g % LANES
        kout_ref[b] = jnp.where(is_last, pltpu.roll(knewt_ref[:, blk], to_last, 1),
                                pltpu.roll(ck_ref[b], WINDOW - 1, 1))
        vout_ref[b] = jnp.where(is_last, pltpu.roll(vnewt_ref[:, blk], to_last, 1),
                                pltpu.roll(cv_ref[b], WINDOW - 1, 1))
    h_ref[...] = _layer_a_tile(x_ref[...], ng_ref, win_ref, vg_ref, ws_ref, bs_ref, wout_ref,
                               y_scr, tile=tile)


def _layer_a_sample_kernel(x_ref, ng_ref, win_ref, vg_ref, ws_ref, bs_ref, wout_ref,
                           kvg_ref, wkvt_ref, cos_ref, slo_ref, shi_ref,
                           win16_ref, wout16_ref, h_ref, av_ref, knew_ref, vnew_ref, knewt_ref,
                           vnewt_ref, xn_scr, proj_scr, y_scr, acc_scr, *, n_in, n_out):
    step = pl.program_id(0)
    blocks_per_branch = n_in // 3

    @pl.when(step == 0)
    def _():
        xn_scr[...] = _rms(x_ref[...], ng_ref[...]).astype(BF16)

    @pl.when(step < n_in)
    def _():
        w = win_ref[...].astype(BF16)
        win16_ref[...] = w
        proj_scr[step] = _dot(xn_scr[...], w)

    @pl.when(step == n_in)
    def _():
        def branch(i):
            return jnp.concatenate(
                [proj_scr[i * blocks_per_branch + j] for j in range(blocks_per_branch)], axis=1)

        v = _rms(branch(1), vg_ref[...])
        av_ref[...] = v
        z = v * ws_ref[...] + bs_ref[...]
        y = (branch(0) * z * _silu(branch(2))).astype(BF16)
        rows = A_WIDTH // n_out
        for j in range(n_out):
            y_scr[j] = y[:, j * rows:(j + 1) * rows]

    @pl.when(step >= n_in)
    def _():
        w = wout_ref[...].astype(BF16)
        wout16_ref[...] = w
        part = _dot(y_scr[step - n_in], w)

        @pl.when(step == n_in)
        def _():
            acc_scr[...] = part

        @pl.when(step > n_in)
        def _():
            acc_scr[...] += part

    @pl.when(step == n_in + n_out - 1)
    def _():
        h = x_ref[...] + acc_scr[...]
        h_ref[...] = h
        hn = h * lax.rsqrt(jnp.mean(h * h, axis=-1, keepdims=True) + EPS)
        kv = _dot_nt((hn * kvg_ref[...]).astype(BF16), wkvt_ref[...])
        cos, slo, shi = cos_ref[...], slo_ref[...], shi_ref[...]
        k = jnp.concatenate(
            [_rotate(kv[:, c * LANES:(c + 1) * LANES], cos, slo, shi)
             for c in range(KV_DIM // LANES)], axis=1)
        knew_ref[...] = k
        vnew_ref[...] = kv[:, KV_DIM:]
        knewt_ref[...] = k.T
        vnewt_ref[...] = kv[:, KV_DIM:].T


def _const_spec(shape):
    return pl.BlockSpec(shape, lambda *_: (0,) * len(shape), pipeline_mode=pl.Buffered(1))


def _layer_a_prompt(x, norm_g, w_in, v_norm_g, ws, bs, w_out, cache_k, cache_v, knew_t, vnew_t,
                    w_in_b, w_out_b):
    n_tok = x.shape[0]
    tile = A_TILE
    n_steps = n_tok // tile
    n_seq = cache_k.shape[0]
    n_roll = n_seq // n_steps
    assert n_roll * n_steps == n_seq and n_seq % LANES == 0
    assert max(w_in_b.shape[1], w_out_b.shape[1]) <= n_steps * LANES
    tok_spec = pl.BlockSpec((tile, D_MODEL), lambda i: (i, 0))
    cache_spec = pl.BlockSpec((n_roll, KV_DIM, WINDOW), lambda i: (i, 0, 0))

    def column_block(w):
        last = w.shape[1] // LANES - 1
        return pl.BlockSpec((w.shape[0], LANES), lambda i: (0, jnp.minimum(i, last)))

    def row_block(w):
        last = w.shape[1] // LANES - 1
        return pl.BlockSpec((LANES, w.shape[0]), lambda i: (jnp.minimum(i, last), 0))

    consts = (norm_g, w_in, v_norm_g, ws, bs, w_out)
    return pl.pallas_call(
        functools.partial(_layer_a_prompt_kernel, tile=tile, n_roll=n_roll),
        grid=(n_steps,),
        in_specs=[tok_spec] + [_const_spec(c.shape) for c in consts]
        + [cache_spec, cache_spec, _const_spec(knew_t.shape), _const_spec(vnew_t.shape),
           column_block(w_in_b), column_block(w_out_b)],
        out_specs=[tok_spec, cache_spec, cache_spec, row_block(w_in_b), row_block(w_out_b)],
        out_shape=[jax.ShapeDtypeStruct((n_tok, D_MODEL), F32),
                   jax.ShapeDtypeStruct(cache_k.shape, F32),
                   jax.ShapeDtypeStruct(cache_v.shape, F32),
                   jax.ShapeDtypeStruct(w_in_b.shape[::-1], BF16),
                   jax.ShapeDtypeStruct(w_out_b.shape[::-1], BF16)],
        scratch_shapes=[pltpu.VMEM((tile, A_WIDTH), BF16)],
        compiler_params=pltpu.CompilerParams(
            dimension_semantics=("arbitrary",), vmem_limit_bytes=VMEM_LIMIT_BYTES),
        name="layer_a_prompt",
    )(x, *consts, cache_k, cache_v, knew_t, vnew_t, w_in_b, w_out_b)


def _layer_a_sample(x, norm_g, w_in, v_norm_g, ws, bs, w_out, kv_norm, w_kv_t):
    n_seq = x.shape[0]
    n_in = w_in.shape[1] // A_SAMPLE_COLS
    n_out = w_out.shape[0] // A_SAMPLE_ROWS
    assert n_in % 3 == 0
    cos, slo, shi = _rotary_tables(jnp.full((1,), PAST_LEN, F32))
    whole = lambda shape: pl.BlockSpec(shape, lambda i: (0,) * len(shape))
    win_block = lambda i: (0, jnp.minimum(i, n_in - 1))
    wout_block = lambda i: (jnp.maximum(i - n_in, 0), 0)
    small_dims = [(n_seq, D_MODEL), (n_seq, A_WIDTH), (n_seq, KV_DIM), (n_seq, KV_DIM),
                  (KV_DIM, n_seq), (KV_DIM, n_seq)]
    return pl.pallas_call(
        functools.partial(_layer_a_sample_kernel, n_in=n_in, n_out=n_out),
        grid=(n_in + n_out,),
        in_specs=[whole(x.shape), whole(norm_g.shape),
                  pl.BlockSpec((D_MODEL, A_SAMPLE_COLS), win_block),
                  whole(v_norm_g.shape), whole(ws.shape), whole(bs.shape),
                  pl.BlockSpec((A_SAMPLE_ROWS, D_MODEL), wout_block),
                  whole(kv_norm.shape), whole(w_kv_t.shape),
                  whole(cos.shape), whole(slo.shape), whole(shi.shape)],
        out_specs=[pl.BlockSpec((D_MODEL, A_SAMPLE_COLS), win_block),
                   pl.BlockSpec((A_SAMPLE_ROWS, D_MODEL), wout_block)]
        + [whole(d) for d in small_dims],
        out_shape=[jax.ShapeDtypeStruct(w_in.shape, BF16), jax.ShapeDtypeStruct(w_out.shape, BF16)]
        + [jax.ShapeDtypeStruct(d, F32) for d in small_dims],
        scratch_shapes=[pltpu.VMEM((n_seq, D_MODEL), BF16),
                        pltpu.VMEM((n_in, n_seq, A_SAMPLE_COLS), F32),
                        pltpu.VMEM((n_out, n_seq, A_SAMPLE_ROWS), BF16),
                        pltpu.VMEM((n_seq, D_MODEL), F32)],
        compiler_params=pltpu.CompilerParams(
            dimension_semantics=("arbitrary",), vmem_limit_bytes=VMEM_LIMIT_BYTES),
        name="layer_a_sample",
    )(x, norm_g, w_in, v_norm_g, ws, bs, w_out, kv_norm, w_kv_t, cos, slo, shi)


def _layer_b_prompt_kernel(sinks_ref, h_ref, kvg_ref, nbg_ref, fg_ref, wkvt_ref, wqt_ref,
                           wgt_ref, woutt_ref, cost_ref, sint_ref, y_ref, kout_ref, vout_ref,
                           *scratch, tile, n_t, n_sub):
    for sub in range(n_sub):
        rows = pl.ds(sub * tile, tile)
        _layer_b_prompt_tile(
            sinks_ref, h_ref.at[rows, :], kvg_ref, nbg_ref, fg_ref, wkvt_ref, wqt_ref, wgt_ref,
            woutt_ref, cost_ref.at[:, rows], sint_ref.at[:, rows], y_ref.at[rows, :], kout_ref,
            vout_ref, *scratch, tile=tile, n_t=n_t, t=pl.program_id(1) * n_sub + sub,
            first_possible=sub == 0, last_possible=sub == n_sub - 1)


def _layer_b_prompt_tile(sinks_ref, h_ref, kvg_ref, nbg_ref, fg_ref, wkvt_ref, wqt_ref,
                         wgt_ref, woutt_ref, cost_ref, sint_ref,
                         y_ref, kout_ref, vout_ref,
                         kext_scr, vtext_scr, qt_scr, ogt_scr, bias_scr,
                         *, tile, n_t, t, first_possible, last_possible):
    n_keys = WINDOW + Q_BLOCK

    if first_possible:
        @pl.when((pl.program_id(0) == 0) & (t == 0))
        def _():
            j = lax.broadcasted_iota(jnp.int32, (n_keys, Q_BLOCK), 0)
            i = lax.broadcasted_iota(jnp.int32, (n_keys, Q_BLOCK), 1)
            band = (j >= i) & (j <= WINDOW + i)
            bias_scr[0] = jnp.where(band & (j >= WINDOW), 0.0, -jnp.inf)
            bias_scr[1] = jnp.where(band, 0.0, -jnp.inf)

        @pl.when(t == 0)
        def _():
            kext_scr[0:WINDOW, :] = jnp.zeros((WINDOW, KV_DIM), BF16)
            vtext_scr[:, 0:WINDOW] = jnp.zeros((KV_DIM, WINDOW), BF16)

    h = h_ref[...]
    hn = h * lax.rsqrt(jnp.mean(h * h, axis=-1, keepdims=True) + EPS)
    xkv = (hn * kvg_ref[...]).astype(BF16)
    xb = (hn * nbg_ref[...]).astype(BF16)

    cost, sint = cost_ref[...], sint_ref[...]
    half = ROT_DIM // 2

    def rotate_head(rows):
        lo, hi = rows[0:half, :], rows[half:ROT_DIM, :]
        return jnp.concatenate(
            [lo * cost - hi * sint, hi * cost + lo * sint, rows[ROT_DIM:, :]], axis=0)

    kvt = _dot_nt(wkvt_ref[...], xkv)
    kt = jnp.concatenate(
        [rotate_head(kvt[kh * HEAD_DIM:(kh + 1) * HEAD_DIM, :]) for kh in range(N_KV_HEADS)],
        axis=0)
    vt = kvt[KV_DIM:, :]
    kext_scr[WINDOW:, :] = kt.T.astype(BF16)
    vtext_scr[:, WINDOW:] = vt.astype(BF16)

    if last_possible:
        @pl.when(t == n_t - 1)
        def _():
            kout_ref[0] = kt[:, tile - WINDOW:]
            vout_ref[0] = vt[:, tile - WINDOW:]

    qt = _dot_nt(wqt_ref[...], xb)
    for hd in range(N_HEADS):
        rot = rotate_head(qt[hd * HEAD_DIM:(hd + 1) * HEAD_DIM, :])
        qt_scr[hd * HEAD_DIM:(hd + 1) * HEAD_DIM, :] = (rot * Q_SCALE_LOG2).astype(BF16)

    lane_head = lax.broadcasted_iota(jnp.int32, (1, GQA_GROUP * Q_BLOCK), 1) // Q_BLOCK
    zeros_half = jnp.zeros((HEAD_DIM, GQA_GROUP * Q_BLOCK), BF16)
    ones_rows = jnp.ones((BF16_SUBLANES, n_keys), BF16)

    def scores(qb, kh):
        qcols = slice(qb * Q_BLOCK, (qb + 1) * Q_BLOCK)
        keys = slice(qb * Q_BLOCK, qb * Q_BLOCK + n_keys)
        q4 = jnp.concatenate(
            [qt_scr[(kh * GQA_GROUP + r) * HEAD_DIM:(kh * GQA_GROUP + r + 1) * HEAD_DIM, qcols]
             for r in range(GQA_GROUP)], axis=1)
        q4 = jnp.concatenate([q4, zeros_half] if kh % 2 == 0 else [zeros_half, q4], axis=0)
        kblk = kext_scr[keys, (kh // 2) * LANES:(kh // 2 + 1) * LANES]
        return _dot(kblk, q4)

    def finish(qb, kh, s, s_ahead):
        qcols = slice(qb * Q_BLOCK, (qb + 1) * Q_BLOCK)
        keys = slice(qb * Q_BLOCK, qb * Q_BLOCK + n_keys)
        if first_possible and qb == 0:
            bias = bias_scr[jnp.where(t > 0, 1, 0)]
        else:
            bias = bias_scr[1]
        s = s + jnp.concatenate([bias] * GQA_GROUP, axis=1)
        sink = jnp.zeros((1, GQA_GROUP * Q_BLOCK), F32)
        for r in range(GQA_GROUP):
            sink = jnp.where(lane_head == r, sinks_ref[kh * GQA_GROUP + r] * LOG2_E, sink)
        m = jnp.maximum(jnp.max(s, axis=0, keepdims=True), sink)
        p = jnp.exp2(s - m)
        if s_ahead is not None:
            p = jnp.concatenate(
                [p[:n_keys - 8, :], p[n_keys - 8:, :] + _zero_of(s_ahead[0:8, :])], axis=0)
        vt_ones = jnp.concatenate(
            [vtext_scr[kh * HEAD_DIM:(kh + 1) * HEAD_DIM, keys], ones_rows], axis=0)
        ot = _dot(vt_ones, p.astype(BF16))
        denom = ot[HEAD_DIM:HEAD_DIM + 1, :] + jnp.exp2(sink - m)
        ot = ot[0:HEAD_DIM, :] * (1.0 / denom)
        for r in range(GQA_GROUP):
            hd = kh * GQA_GROUP + r
            ogt_scr[hd * HEAD_DIM:(hd + 1) * HEAD_DIM, qcols] = ot[:, r * Q_BLOCK:(r + 1) * Q_BLOCK]

    blocks = [(qb, kh) for qb in range(tile // Q_BLOCK) for kh in range(N_KV_HEADS)]
    pending = [scores(*blk) for blk in blocks[:ATTN_AHEAD]]
    for n, blk in enumerate(blocks):
        if n + ATTN_AHEAD < len(blocks):
            pending.append(scores(*blocks[n + ATTN_AHEAD]))
        s_cur = pending.pop(0)
        finish(*blk, s_cur, pending[-1] if pending else None)

    kext_scr[0:WINDOW, :] = kext_scr[tile:tile + WINDOW, :]
    vtext_scr[:, 0:WINDOW] = vtext_scr[:, tile:tile + WINDOW]

    ogt = jnp.concatenate(
        [(ogt_scr[rc * OUT_ROWS:(rc + 1) * OUT_ROWS, :]
          * _silu(_dot_nt(wgt_ref[rc * OUT_ROWS:(rc + 1) * OUT_ROWS, :], xb))).astype(BF16)
         for rc in range(D_MODEL // OUT_ROWS)], axis=0)
    h2 = jnp.concatenate(
        [h[:, rc * OUT_ROWS:(rc + 1) * OUT_ROWS]
         + _dot(woutt_ref[rc * OUT_ROWS:(rc + 1) * OUT_ROWS, :], ogt).T
         for rc in range(D_MODEL // OUT_ROWS)], axis=1)
    y_ref[...] = _rms(h2, fg_ref[...])


def _layer_b_prompt(h, sinks, kv_norm, norm_b, final_norm, w_kv_t, w_qg_t, w_out_t, *, batch, seq):
    tile = B_TILE
    n_t = seq // tile
    n_sub = B_SUBTILES
    step = n_sub * tile
    n_steps = n_t // n_sub
    pos = jnp.arange(seq, dtype=F32)
    inv = ROPE_THETA ** (-jnp.arange(0, ROT_DIM, 2, dtype=F32) / ROT_DIM)
    ang_t = inv[:, None] * pos[None, :]
    cos_t, sin_t = jnp.cos(ang_t), jnp.sin(ang_t)
    tok_spec = pl.BlockSpec((step, D_MODEL), lambda b, t, *_: (b * n_steps + t, 0))
    rot_t_spec = pl.BlockSpec((ROT_DIM // 2, step), lambda b, t, *_: (0, t))
    last_spec = pl.BlockSpec((1, KV_DIM, WINDOW), lambda b, t, *_: (b, 0, 0))

    def const(shape):
        return pl.BlockSpec(shape, lambda *_: (0,) * len(shape), pipeline_mode=pl.Buffered(1))

    def half(i):
        return pl.BlockSpec((D_MODEL, D_MODEL), lambda *_: (i, 0), pipeline_mode=pl.Buffered(1))

    return pl.pallas_call(
        functools.partial(_layer_b_prompt_kernel, tile=tile, n_t=n_t, n_sub=n_sub),
        grid_spec=pltpu.PrefetchScalarGridSpec(
            num_scalar_prefetch=1,
            grid=(batch, n_steps),
            in_specs=[tok_spec, const(kv_norm.shape), const(norm_b.shape), const(final_norm.shape),
                      const(w_kv_t.shape), half(0), half(1), const(w_out_t.shape),
                      rot_t_spec, rot_t_spec],
            out_specs=[tok_spec, last_spec, last_spec],
            scratch_shapes=[pltpu.VMEM((WINDOW + tile, KV_DIM), BF16),
                            pltpu.VMEM((KV_DIM, WINDOW + tile), BF16),
                            pltpu.VMEM((D_MODEL, tile), BF16),
                            pltpu.VMEM((D_MODEL, tile), F32),
                            pltpu.VMEM((2, WINDOW + Q_BLOCK, Q_BLOCK), F32)]),
        out_shape=[jax.ShapeDtypeStruct((batch * seq, D_MODEL), F32),
                   jax.ShapeDtypeStruct((batch, KV_DIM, WINDOW), F32),
                   jax.ShapeDtypeStruct((batch, KV_DIM, WINDOW), F32)],
        compiler_params=pltpu.CompilerParams(
            dimension_semantics=("arbitrary", "arbitrary"), vmem_limit_bytes=VMEM_LIMIT_BYTES),
        name="layer_b_prompt",
    )(sinks, h, kv_norm, norm_b, final_norm, w_kv_t, w_qg_t, w_qg_t, w_out_t, cos_t, sin_t)


def _layer_b_sample_kernel(sinks_ref, h_ref, nbg_ref, fg_ref, wqgt_ref, woutt_ref,
                           cos_ref, slo_ref, shi_ref, knew_ref, vnew_ref, ck_ref, cv_ref,
                           y_ref, q_scr, gate_scr, o_scr, *, n_seq, b_tile):
    step = pl.program_id(0)

    @pl.when(step == 0)
    def _():
        h = h_ref[...]
        hn = h * lax.rsqrt(jnp.mean(h * h, axis=-1, keepdims=True) + EPS)
        qg = _dot_nt((hn * nbg_ref[...]).astype(BF16), wqgt_ref[...])
        cos, slo, shi = cos_ref[...], slo_ref[...], shi_ref[...]
        for c in range(D_MODEL // LANES):
            cols = slice(c * LANES, (c + 1) * LANES)
            q2 = _rotate(qg[:, cols], cos, slo, shi) * HEAD_DIM ** -0.5
            g2 = qg[:, D_MODEL + c * LANES:D_MODEL + (c + 1) * LANES]
            for i in range(LANES // HEAD_DIM):
                dst = _member_major(c * (LANES // HEAD_DIM) + i)
                q_scr[:, dst] = q2[:, i * HEAD_DIM:(i + 1) * HEAD_DIM]
                gate_scr[:, dst] = g2[:, i * HEAD_DIM:(i + 1) * HEAD_DIM]

    n_rows = GQA_GROUP * N_KV_HEADS * SAMPLE_GROUP
    row = lax.broadcasted_iota(jnp.int32, (n_rows, 1), 0)
    row_kh = (row // SAMPLE_GROUP) % N_KV_HEADS
    row_seq = row % SAMPLE_GROUP
    lane_kh = lax.broadcasted_iota(jnp.int32, (1, KV_DIM), 1) // HEAD_DIM
    own = row_kh == lane_kh
    sink = jnp.zeros((n_rows, 1), F32)
    for r in range(GQA_GROUP):
        for kh in range(N_KV_HEADS):
            sink = jnp.where(row // SAMPLE_GROUP == r * N_KV_HEADS + kh,
                             sinks_ref[kh * GQA_GROUP + r], sink)
    n_blk = GQA_GROUP * N_KV_HEADS

    def group(i, carry):
        b0 = i * SAMPLE_GROUP
        g0 = pl.multiple_of(step * b_tile + b0, SAMPLE_GROUP)
        seqs = pl.ds(g0, SAMPLE_GROUP)
        q8 = q_scr[seqs, :]
        qexp = jnp.concatenate(
            [q8[:, r * KV_DIM:(r + 1) * KV_DIM] for r in range(GQA_GROUP)
             for _ in range(N_KV_HEADS)], axis=0)
        qexp = jnp.where(own, qexp, 0.0).astype(BF16)
        knew8 = knew_ref[seqs, :].astype(BF16).astype(F32)
        vnew8 = vnew_ref[seqs, :].astype(BF16).astype(F32)
        s_new = jnp.sum(qexp.astype(F32) * jnp.concatenate([knew8] * n_blk, axis=0),
                        axis=1, keepdims=True)
        s_old = jnp.zeros((n_rows, WINDOW), F32)
        for b in range(SAMPLE_GROUP):
            s_b = _dot(qexp, ck_ref[b0 + b].astype(BF16))
            s_old = jnp.where(row_seq == b, s_b, s_old)
        m = jnp.maximum(jnp.maximum(jnp.max(s_old, axis=1, keepdims=True), s_new), sink)
        p_old = jnp.exp(s_old - m)
        p_new = jnp.exp(s_new - m)
        denom = jnp.sum(p_old, axis=1, keepdims=True) + p_new + jnp.exp(sink - m)
        p_old = p_old.astype(BF16)
        o = jnp.zeros((n_rows, KV_DIM), F32)
        for b in range(SAMPLE_GROUP):
            o_b = _dot_nt(p_old, cv_ref[b0 + b].astype(BF16))
            o = jnp.where(row_seq == b, o_b, o)
        o = (o + p_new.astype(BF16).astype(F32) * jnp.concatenate([vnew8] * n_blk, axis=0)) / denom
        o = jnp.where(own, o, 0.0)
        for r in range(GQA_GROUP):
            blks = [o[(r * N_KV_HEADS + kh) * SAMPLE_GROUP:(r * N_KV_HEADS + kh + 1) * SAMPLE_GROUP]
                    for kh in range(N_KV_HEADS)]
            o_scr[seqs, r * KV_DIM:(r + 1) * KV_DIM] = (blks[0] + blks[1]) + (blks[2] + blks[3])
        return carry

    lax.fori_loop(0, b_tile // SAMPLE_GROUP, group, 0, unroll=True)

    @pl.when(step == pl.num_programs(0) - 1)
    def _():
        og_mm = o_scr[...] * _silu(gate_scr[...])
        og = jnp.concatenate(
            [og_mm[:, _member_major(hd)] for hd in range(N_HEADS)], axis=1).astype(BF16)
        h2 = h_ref[...] + _dot_nt(og, woutt_ref[...])
        y_ref[...] = _rms(h2, fg_ref[...])


def _layer_b_sample(h, sinks, norm_b, final_norm, w_qg_t, w_out_t, knew, vnew, cache_k, cache_v):
    n_seq = h.shape[0]
    b_tile = SAMPLE_B_TILE
    cos, slo, shi = _rotary_tables(jnp.full((1,), PAST_LEN, F32))

    def const(shape):
        return pl.BlockSpec(shape, lambda *_: (0,) * len(shape))

    assert cache_k.shape == (n_seq, KV_DIM, WINDOW)
    cache_spec = pl.BlockSpec((b_tile, KV_DIM, WINDOW), lambda i, *_: (i, 0, 0))
    consts = (h, norm_b, final_norm, w_qg_t, w_out_t, cos, slo, shi, knew, vnew)
    return pl.pallas_call(
        functools.partial(_layer_b_sample_kernel, n_seq=n_seq, b_tile=b_tile),
        grid_spec=pltpu.PrefetchScalarGridSpec(
            num_scalar_prefetch=1,
            grid=(n_seq // b_tile,),
            in_specs=[const(c.shape) for c in consts] + [cache_spec, cache_spec],
            out_specs=const((n_seq, D_MODEL)),
            scratch_shapes=[pltpu.VMEM((n_seq, D_MODEL), F32),
                            pltpu.VMEM((n_seq, D_MODEL), F32),
                            pltpu.VMEM((n_seq, D_MODEL), F32)]),
        out_shape=jax.ShapeDtypeStruct((n_seq, D_MODEL), F32),
        compiler_params=pltpu.CompilerParams(
            dimension_semantics=("arbitrary",), vmem_limit_bytes=VMEM_LIMIT_BYTES),
        name="layer_b_sample",
    )(sinks, *consts, cache_k, cache_v)


def kernel(x_prompt, x_sample, cache_k, cache_v, norm_a, w_in_a, v_norm_a, w_s_a, b_s_a, w_out_a,
           kv_norm, w_kv, norm_b, w_in_b, sinks_b, w_out_b, final_norm):
    batch, seq, _ = x_prompt.shape
    n_seq, dec_seq, _ = x_sample.shape
    assert dec_seq == 1 and seq % CHUNK == 0 and cache_k.shape[1] == WINDOW
    assert norm_a.shape[0] == 1 and norm_b.shape[0] == 1

    row = lambda g: g.reshape(1, -1)
    w_kv_t =_transpose_cast(w_kv)
    bs_chunk = jnp.repeat(b_s_a[0].T, A_GROUP_DIM, axis=1)
    ws_one = jnp.repeat(w_s_a[0, :, 0, 0], A_GROUP_DIM).reshape(1, A_WIDTH)
    bs_one = jnp.repeat(b_s_a[0, :, 0], A_GROUP_DIM).reshape(1, A_WIDTH)

    def to_window(x_t):
        n = x_t.shape[0]
        return x_t.reshape(n, N_KV_HEADS, HEAD_DIM, WINDOW).transpose(0, 3, 1, 2)

    def from_window(x):
        return x.transpose(0, 2, 3, 1).reshape(x.shape[0], KV_DIM, WINDOW)

    cache_kt, cache_vt = from_window(cache_k), from_window(cache_v)

    w_in_a16, w_out_a16, h_s, av_s, knew, vnew, knew_t, vnew_t = _layer_a_sample(
        x_sample.reshape(n_seq, D_MODEL), row(norm_a[0]), w_in_a[0], row(v_norm_a[0]), ws_one,
        bs_one, w_out_a[0], row(kv_norm), w_kv_t)
    h_p, kt_s, vt_s, w_qg_t, w_out_b_t = _layer_a_prompt(
        x_prompt.reshape(batch * seq, D_MODEL), row(norm_a[0]), w_in_a16, row(v_norm_a[0]),
        w_s_a[0], bs_chunk, w_out_a16, cache_kt, cache_vt, knew_t, vnew_t, w_in_b[0], w_out_b[0])

    y_p, kt_p, vt_p = _layer_b_prompt(
        h_p, sinks_b[0], row(kv_norm), row(norm_b[0]), row(final_norm),
        w_kv_t, w_qg_t, w_out_b_t, batch=batch, seq=seq)
    y_s = _layer_b_sample(h_s, sinks_b[0], row(norm_b[0]), row(final_norm), w_qg_t, w_out_b_t,
                          knew, vnew, cache_kt, cache_vt)

    return (y_p.reshape(batch, seq, D_MODEL),
            y_s.reshape(n_seq, 1, D_MODEL),
            to_window(kt_p),
            to_window(vt_p),
            to_window(kt_s),
            to_window(vt_s),
            av_s.reshape(1, n_seq, 1, A_WIDTH))
```

```python
import functools

import jax
import jax.numpy as jnp
from jax import lax
from jax.experimental import pallas as pl
from jax.experimental.pallas import tpu as pltpu

D_MODEL = 1024
PAST_LEN = 8192
CHUNK = 128
A_WIDTH = 2 * D_MODEL
A_GROUPS = 8
A_GROUP_DIM = A_WIDTH // A_GROUPS
HEAD_DIM = 64
N_HEADS = D_MODEL // HEAD_DIM
N_KV_HEADS = 4
GQA_GROUP = N_HEADS // N_KV_HEADS
KV_DIM = N_KV_HEADS * HEAD_DIM
WINDOW = 128
Q_BLOCK = 128
ROT_DIM = HEAD_DIM // 4
ROPE_THETA = 500000.0
EPS = 1e-5

LANES = 128
BF16_SUBLANES = 16
LOG2_E = 1.4426950408889634
Q_SCALE_LOG2 = HEAD_DIM ** -0.5 * LOG2_E
VMEM_LIMIT_BYTES = 56 * 1024 * 1024

A_TILE = 512
B_TILE = 512
B_SUBTILES = 2
PREP_COLS = 512
A_SAMPLE_COLS = 1024
A_SAMPLE_ROWS = 1024
OUT_ROWS = 256
ATTN_AHEAD = 2
SAMPLE_B_TILE = 16
SAMPLE_GROUP = 8

F32 = jnp.float32
BF16 = jnp.bfloat16


def _rms(x, g):
    return x * lax.rsqrt(jnp.mean(x * x, axis=-1, keepdims=True) + EPS) * g


def _silu(x):
    return x * jax.nn.sigmoid(x)


def _dot(a, b):
    return jnp.dot(a, b, preferred_element_type=F32)


def _dot_nt(a, b):
    return lax.dot_general(a, b, (((1,), (1,)), ((), ())), preferred_element_type=F32)


def _zero_of(x):
    bits = pltpu.bitcast(x, jnp.uint32)
    return ((bits >> 16) >> 16).astype(F32)


def _member_major(head):
    kh, r = divmod(head, GQA_GROUP)
    start = (r * N_KV_HEADS + kh) * HEAD_DIM
    return slice(start, start + HEAD_DIM)


def _rotate(x, cos, sin_lo, sin_hi):
    return (x * cos + pltpu.roll(x, LANES - ROT_DIM // 2, 1) * sin_lo
            + pltpu.roll(x, ROT_DIM // 2, 1) * sin_hi)


def _rotary_tables(positions):
    lane = jnp.arange(LANES) % HEAD_DIM
    freq = (2 * (lane % (ROT_DIM // 2))).astype(F32)
    ang = positions[:, None] * (ROPE_THETA ** (-freq / ROT_DIM))[None, :]
    first = (lane < ROT_DIM // 2)[None, :]
    second = ((lane >= ROT_DIM // 2) & (lane < ROT_DIM))[None, :]
    cos = jnp.where(first | second, jnp.cos(ang), 1.0)
    sin_lo = jnp.where(first, -jnp.sin(ang), 0.0)
    sin_hi = jnp.where(second, jnp.sin(ang), 0.0)
    return cos, sin_lo, sin_hi


def _transpose_cast_kernel(w_ref, o_ref):
    o_ref[...] = w_ref[...].T.astype(BF16)


def _transpose_cast(w):
    k, n = w.shape
    return pl.pallas_call(
        _transpose_cast_kernel,
        grid=(n // PREP_COLS,),
        in_specs=[pl.BlockSpec((k, PREP_COLS), lambda j: (0, j))],
        out_specs=pl.BlockSpec((PREP_COLS, k), lambda j: (j, 0)),
        out_shape=jax.ShapeDtypeStruct((n, k), BF16),
        compiler_params=pltpu.CompilerParams(dimension_semantics=("arbitrary",)),
        name="transpose_cast",
    )(w)


def _layer_a_tile(x, ng_ref, win_ref, vg_ref, ws_ref, bs_ref, wout_ref, y_scr, *, tile):
    xn = _rms(x, ng_ref[...]).astype(BF16)
    vb = _rms(_dot(xn, win_ref[:, A_WIDTH:2 * A_WIDTH]), vg_ref[...]).astype(BF16)
    row = lax.broadcasted_iota(jnp.int32, (CHUNK, CHUNK), 0)
    col = lax.broadcasted_iota(jnp.int32, (CHUNK, CHUNK), 1)
    tri = row >= col
    width = 2 * A_GROUP_DIM
    for pair in range(A_GROUPS // 2):
        cols = slice(pair * width, (pair + 1) * width)
        u = _dot(xn, win_ref[:, cols])
        gate = _dot(xn, win_ref[:, 2 * A_WIDTH + pair * width:2 * A_WIDTH + (pair + 1) * width])
        ws = [jnp.where(tri, ws_ref[g], 0.0).astype(BF16) for g in (2 * pair, 2 * pair + 1)]
        z = jnp.concatenate(
            [jnp.concatenate(
                [_dot(ws[i], vb[c * CHUNK:(c + 1) * CHUNK,
                                (2 * pair + i) * A_GROUP_DIM:(2 * pair + i + 1) * A_GROUP_DIM])
                 for i in range(2)], axis=1) + bs_ref[:, cols]
             for c in range(tile // CHUNK)], axis=0)
        y_scr[:, cols] = (u * z * _silu(gate)).astype(BF16)
    return x + _dot(y_scr[...], wout_ref[...])


def _layer_a_prompt_kernel(x_ref, ng_ref, win_ref, vg_ref, ws_ref, bs_ref, wout_ref,
                           ck_ref, cv_ref, knewt_ref, vnewt_ref, winb_ref, woutb_ref,
                           h_ref, kout_ref, vout_ref, wqgt_ref, woutbt_ref, y_scr, *, tile, n_roll):
    wqgt_ref[...] = winb_ref[...].T.astype(BF16)
    woutbt_ref[...] = woutb_ref[...].T.astype(BF16)
    is_last = lax.broadcasted_iota(jnp.int32, (KV_DIM, WINDOW), 1) == WINDOW - 1
    for b in range(n_roll):
        g = pl.program_id(0) * n_roll + b
        blk = pl.ds(pl.multiple_of((g // LANES) * LANES, LANES), LANES)
        to_last = LANES - 1 - g % LANES
        kout_ref[b] = jnp.where(is_last, pltpu.roll(knewt_ref[:, blk], to_last, 1),
                                pltpu.roll(ck_ref[b], WINDOW - 1, 1))
        vout_ref[b] = jnp.where(is_last, pltpu.roll(vnewt_ref[:, blk], to_last, 1),
                                pltpu.roll(cv_ref[b], WINDOW - 1, 1))
    h_ref[...] = _layer_a_tile(x_ref[...], ng_ref, win_ref, vg_ref, ws_ref, bs_ref, wout_ref,
                               y_scr, tile=tile)


def _layer_a_sample_kernel(x_ref, ng_ref, win_ref, vg_ref, ws_ref, bs_ref, wout_ref,
                           kvg_ref, wkvt_ref, cos_ref, slo_ref, shi_ref,
                           win16_ref, wout16_ref, h_ref, av_ref, knew_ref, vnew_ref, knewt_ref,
                           vnewt_ref, xn_scr, proj_scr, y_scr, acc_scr, *, n_in, n_out):
    step = pl.program_id(0)
    blocks_per_branch = n_in // 3

    @pl.when(step == 0)
    def _():
        xn_scr[...] = _rms(x_ref[:, 0, :], ng_ref[...]).astype(BF16)

    @pl.when(step < n_in)
    def _():
        w = win_ref[...].astype(BF16)
        win16_ref[...] = w
        proj_scr[step] = _dot(xn_scr[...], w)

    @pl.when(step == n_in)
    def _():
        def branch(i):
            return jnp.concatenate(
                [proj_scr[i * blocks_per_branch + j] for j in range(blocks_per_branch)], axis=1)

        v = _rms(branch(1), vg_ref[...])
        av_ref[:, 0, :] = v
        z = v * ws_ref[...] + bs_ref[...]
        y = (branch(0) * z * _silu(branch(2))).astype(BF16)
        rows = A_WIDTH // n_out
        for j in range(n_out):
            y_scr[j] = y[:, j * rows:(j + 1) * rows]

    @pl.when(step >= n_in)
    def _():
        w = wout_ref[...].astype(BF16)
        wout16_ref[...] = w
        part = _dot(y_scr[step - n_in], w)

        @pl.when(step == n_in)
        def _():
            acc_scr[...] = part

        @pl.when(step > n_in)
        def _():
            acc_scr[...] += part

    @pl.when(step == n_in + n_out - 1)
    def _():
        h = x_ref[:, 0, :] + acc_scr[...]
        h_ref[...] = h
        hn = h * lax.rsqrt(jnp.mean(h * h, axis=-1, keepdims=True) + EPS)
        kv = _dot_nt((hn * kvg_ref[...]).astype(BF16), wkvt_ref[...])
        cos, slo, shi = cos_ref[...], slo_ref[...], shi_ref[...]
        k = jnp.concatenate(
            [_rotate(kv[:, c * LANES:(c + 1) * LANES], cos, slo, shi)
             for c in range(KV_DIM // LANES)], axis=1)
        knew_ref[...] = k
        vnew_ref[...] = kv[:, KV_DIM:]
        knewt_ref[...] = k.T
        vnewt_ref[...] = kv[:, KV_DIM:].T


def _const_spec(shape):
    return pl.BlockSpec(shape, lambda *_: (0,) * len(shape), pipeline_mode=pl.Buffered(1))


def _layer_a_prompt(x, norm_g, w_in, v_norm_g, ws, bs, w_out, cache_k, cache_v, knew_t, vnew_t,
                    w_in_b, w_out_b):
    n_tok = x.shape[0]
    tile = A_TILE
    n_steps = n_tok // tile
    n_seq = cache_k.shape[0]
    n_roll = n_seq // n_steps
    assert n_roll * n_steps == n_seq and n_seq % LANES == 0
    assert max(w_in_b.shape[1], w_out_b.shape[1]) <= n_steps * LANES
    tok_spec = pl.BlockSpec((tile, D_MODEL), lambda i: (i, 0))
    cache_spec = pl.BlockSpec((n_roll, KV_DIM, WINDOW), lambda i: (i, 0, 0))

    def column_block(w):
        last = w.shape[1] // LANES - 1
        return pl.BlockSpec((w.shape[0], LANES), lambda i: (0, jnp.minimum(i, last)))

    def row_block(w):
        last = w.shape[1] // LANES - 1
        return pl.BlockSpec((LANES, w.shape[0]), lambda i: (jnp.minimum(i, last), 0))

    consts = (norm_g, w_in, v_norm_g, ws, bs, w_out)
    return pl.pallas_call(
        functools.partial(_layer_a_prompt_kernel, tile=tile, n_roll=n_roll),
        grid=(n_steps,),
        in_specs=[tok_spec] + [_const_spec(c.shape) for c in consts]
        + [cache_spec, cache_spec, _const_spec(knew_t.shape), _const_spec(vnew_t.shape),
           column_block(w_in_b), column_block(w_out_b)],
        out_specs=[tok_spec, cache_spec, cache_spec, row_block(w_in_b), row_block(w_out_b)],
        out_shape=[jax.ShapeDtypeStruct((n_tok, D_MODEL), F32),
                   jax.ShapeDtypeStruct(cache_k.shape, F32),
                   jax.ShapeDtypeStruct(cache_v.shape, F32),
                   jax.ShapeDtypeStruct(w_in_b.shape[::-1], BF16),
                   jax.ShapeDtypeStruct(w_out_b.shape[::-1], BF16)],
        scratch_shapes=[pltpu.VMEM((tile, A_WIDTH), BF16)],
        compiler_params=pltpu.CompilerParams(
            dimension_semantics=("arbitrary",), vmem_limit_bytes=VMEM_LIMIT_BYTES),
        name="layer_a_prompt",
    )(x, *consts, cache_k, cache_v, knew_t, vnew_t, w_in_b, w_out_b)


def _layer_a_sample(x, norm_g, w_in, v_norm_g, ws, bs, w_out, kv_norm, w_kv_t):
    n_seq = x.shape[0]
    n_in = w_in.shape[1] // A_SAMPLE_COLS
    n_out = w_out.shape[0] // A_SAMPLE_ROWS
    assert n_in % 3 == 0
    cos, slo, shi = _rotary_tables(jnp.full((1,), PAST_LEN, F32))
    whole = lambda shape: pl.BlockSpec(shape, lambda i: (0,) * len(shape))
    win_block = lambda i: (0, jnp.minimum(i, n_in - 1))
    wout_block = lambda i: (jnp.maximum(i - n_in, 0), 0)
    small_dims = [(n_seq, D_MODEL), (n_seq, 1, A_WIDTH), (n_seq, KV_DIM), (n_seq, KV_DIM),
                  (KV_DIM, n_seq), (KV_DIM, n_seq)]
    return pl.pallas_call(
        functools.partial(_layer_a_sample_kernel, n_in=n_in, n_out=n_out),
        grid=(n_in + n_out,),
        in_specs=[whole(x.shape), whole(norm_g.shape),
                  pl.BlockSpec((D_MODEL, A_SAMPLE_COLS), win_block),
                  whole(v_norm_g.shape), whole(ws.shape), whole(bs.shape),
                  pl.BlockSpec((A_SAMPLE_ROWS, D_MODEL), wout_block),
                  whole(kv_norm.shape), whole(w_kv_t.shape),
                  whole(cos.shape), whole(slo.shape), whole(shi.shape)],
        out_specs=[pl.BlockSpec((D_MODEL, A_SAMPLE_COLS), win_block),
                   pl.BlockSpec((A_SAMPLE_ROWS, D_MODEL), wout_block)]
        + [whole(d) for d in small_dims],
        out_shape=[jax.ShapeDtypeStruct(w_in.shape, BF16), jax.ShapeDtypeStruct(w_out.shape, BF16)]
        + [jax.ShapeDtypeStruct(d, F32) for d in small_dims],
        scratch_shapes=[pltpu.VMEM((n_seq, D_MODEL), BF16),
                        pltpu.VMEM((n_in, n_seq, A_SAMPLE_COLS), F32),
                        pltpu.VMEM((n_out, n_seq, A_SAMPLE_ROWS), BF16),
                        pltpu.VMEM((n_seq, D_MODEL), F32)],
        compiler_params=pltpu.CompilerParams(
            dimension_semantics=("arbitrary",), vmem_limit_bytes=VMEM_LIMIT_BYTES),
        name="layer_a_sample",
    )(x, norm_g, w_in, v_norm_g, ws, bs, w_out, kv_norm, w_kv_t, cos, slo, shi)


def _layer_b_prompt_kernel(sinks_ref, h_ref, kvg_ref, nbg_ref, fg_ref, wkvt_ref, wqt_ref,
                           wgt_ref, woutt_ref, cost_ref, sint_ref, y_ref, kout_ref, vout_ref,
                           *scratch, tile, n_t, n_sub):
    for sub in range(n_sub):
        rows = pl.ds(sub * tile, tile)
        _layer_b_prompt_tile(
            sinks_ref, h_ref.at[rows, :], kvg_ref, nbg_ref, fg_ref, wkvt_ref, wqt_ref, wgt_ref,
            woutt_ref, cost_ref.at[:, rows], sint_ref.at[:, rows], y_ref.at[rows, :], kout_ref,
            vout_ref, *scratch, tile=tile, n_t=n_t, t=pl.program_id(1) * n_sub + sub,
            first_possible=sub == 0, last_possible=sub == n_sub - 1)


def _layer_b_prompt_tile(sinks_ref, h_ref, kvg_ref, nbg_ref, fg_ref, wkvt_ref, wqt_ref,
                         wgt_ref, woutt_ref, cost_ref, sint_ref,
                         y_ref, kout_ref, vout_ref,
                         kext_scr, vtext_scr, qt_scr, ogt_scr, bias_scr,
                         *, tile, n_t, t, first_possible, last_possible):
    n_keys = WINDOW + Q_BLOCK

    if first_possible:
        @pl.when((pl.program_id(0) == 0) & (t == 0))
        def _():
            j = lax.broadcasted_iota(jnp.int32, (n_keys, Q_BLOCK), 0)
            i = lax.broadcasted_iota(jnp.int32, (n_keys, Q_BLOCK), 1)
            band = (j >= i) & (j <= WINDOW + i)
            bias_scr[0] = jnp.where(band & (j >= WINDOW), 0.0, -jnp.inf)
            bias_scr[1] = jnp.where(band, 0.0, -jnp.inf)

        @pl.when(t == 0)
        def _():
            kext_scr[0:WINDOW, :] = jnp.zeros((WINDOW, KV_DIM), BF16)
            vtext_scr[:, 0:WINDOW] = jnp.zeros((KV_DIM, WINDOW), BF16)

    h = h_ref[...]
    hn = h * lax.rsqrt(jnp.mean(h * h, axis=-1, keepdims=True) + EPS)
    xkv = (hn * kvg_ref[...]).astype(BF16)
    xb = (hn * nbg_ref[...]).astype(BF16)

    cost, sint = cost_ref[...], sint_ref[...]
    half = ROT_DIM // 2

    def rotate_head(rows):
        lo, hi = rows[0:half, :], rows[half:ROT_DIM, :]
        return jnp.concatenate(
            [lo * cost - hi * sint, hi * cost + lo * sint, rows[ROT_DIM:, :]], axis=0)

    kvt = _dot_nt(wkvt_ref[...], xkv)
    kt = jnp.concatenate(
        [rotate_head(kvt[kh * HEAD_DIM:(kh + 1) * HEAD_DIM, :]) for kh in range(N_KV_HEADS)],
        axis=0)
    vt = kvt[KV_DIM:, :]
    kext_scr[WINDOW:, :] = kt.T.astype(BF16)
    vtext_scr[:, WINDOW:] = vt.astype(BF16)

    if last_possible:
        @pl.when(t == n_t - 1)
        def _():
            kout_ref[0] = kt[:, tile - WINDOW:]
            vout_ref[0] = vt[:, tile - WINDOW:]

    qt = _dot_nt(wqt_ref[...], xb)
    for hd in range(N_HEADS):
        rot = rotate_head(qt[hd * HEAD_DIM:(hd + 1) * HEAD_DIM, :])
        qt_scr[hd * HEAD_DIM:(hd + 1) * HEAD_DIM, :] = (rot * Q_SCALE_LOG2).astype(BF16)

    lane_head = lax.broadcasted_iota(jnp.int32, (1, GQA_GROUP * Q_BLOCK), 1) // Q_BLOCK
    zeros_half = jnp.zeros((HEAD_DIM, GQA_GROUP * Q_BLOCK), BF16)
    ones_rows = jnp.ones((BF16_SUBLANES, n_keys), BF16)

    def scores(qb, kh):
        qcols = slice(qb * Q_BLOCK, (qb + 1) * Q_BLOCK)
        keys = slice(qb * Q_BLOCK, qb * Q_BLOCK + n_keys)
        q4 = jnp.concatenate(
            [qt_scr[(kh * GQA_GROUP + r) * HEAD_DIM:(kh * GQA_GROUP + r + 1) * HEAD_DIM, qcols]
             for r in range(GQA_GROUP)], axis=1)
        q4 = jnp.concatenate([q4, zeros_half] if kh % 2 == 0 else [zeros_half, q4], axis=0)
        kblk = kext_scr[keys, (kh // 2) * LANES:(kh // 2 + 1) * LANES]
        return _dot(kblk, q4)

    def finish(qb, kh, s, s_ahead):
        qcols = slice(qb * Q_BLOCK, (qb + 1) * Q_BLOCK)
        keys = slice(qb * Q_BLOCK, qb * Q_BLOCK + n_keys)
        if first_possible and qb == 0:
            bias = bias_scr[jnp.where(t > 0, 1, 0)]
        else:
            bias = bias_scr[1]
        s = s + jnp.concatenate([bias] * GQA_GROUP, axis=1)
        sink = jnp.zeros((1, GQA_GROUP * Q_BLOCK), F32)
        for r in range(GQA_GROUP):
            sink = jnp.where(lane_head == r, sinks_ref[kh * GQA_GROUP + r] * LOG2_E, sink)
        m = jnp.maximum(jnp.max(s, axis=0, keepdims=True), sink)
        p = jnp.exp2(s - m)
        if s_ahead is not None:
            p = jnp.concatenate(
                [p[:n_keys - 8, :], p[n_keys - 8:, :] + _zero_of(s_ahead[0:8, :])], axis=0)
        vt_ones = jnp.concatenate(
            [vtext_scr[kh * HEAD_DIM:(kh + 1) * HEAD_DIM, keys], ones_rows], axis=0)
        ot = _dot(vt_ones, p.astype(BF16))
        denom = ot[HEAD_DIM:HEAD_DIM + 1, :] + jnp.exp2(sink - m)
        ot = ot[0:HEAD_DIM, :] * (1.0 / denom)
        for r in range(GQA_GROUP):
            hd = kh * GQA_GROUP + r
            ogt_scr[hd * HEAD_DIM:(hd + 1) * HEAD_DIM, qcols] = ot[:, r * Q_BLOCK:(r + 1) * Q_BLOCK]

    blocks = [(qb, kh) for qb in range(tile // Q_BLOCK) for kh in range(N_KV_HEADS)]
    pending = [scores(*blk) for blk in blocks[:ATTN_AHEAD]]
    for n, blk in enumerate(blocks):
        if n + ATTN_AHEAD < len(blocks):
            pending.append(scores(*blocks[n + ATTN_AHEAD]))
        s_cur = pending.pop(0)
        finish(*blk, s_cur, pending[-1] if pending else None)

    kext_scr[0:WINDOW, :] = kext_scr[tile:tile + WINDOW, :]
    vtext_scr[:, 0:WINDOW] = vtext_scr[:, tile:tile + WINDOW]

    ogt = jnp.concatenate(
        [(ogt_scr[rc * OUT_ROWS:(rc + 1) * OUT_ROWS, :]
          * _silu(_dot_nt(wgt_ref[rc * OUT_ROWS:(rc + 1) * OUT_ROWS, :], xb))).astype(BF16)
         for rc in range(D_MODEL // OUT_ROWS)], axis=0)
    h2 = jnp.concatenate(
        [h[:, rc * OUT_ROWS:(rc + 1) * OUT_ROWS]
         + _dot(woutt_ref[rc * OUT_ROWS:(rc + 1) * OUT_ROWS, :], ogt).T
         for rc in range(D_MODEL // OUT_ROWS)], axis=1)
    y_ref[...] = _rms(h2, fg_ref[...])


def _layer_b_prompt(h, sinks, kv_norm, norm_b, final_norm, w_kv_t, w_qg_t, w_out_t, *, batch, seq):
    tile = B_TILE
    n_t = seq // tile
    n_sub = B_SUBTILES
    step = n_sub * tile
    n_steps = n_t // n_sub
    pos = jnp.arange(seq, dtype=F32)
    inv = ROPE_THETA ** (-jnp.arange(0, ROT_DIM, 2, dtype=F32) / ROT_DIM)
    ang_t = inv[:, None] * pos[None, :]
    cos_t, sin_t = jnp.cos(ang_t), jnp.sin(ang_t)
    tok_spec = pl.BlockSpec((step, D_MODEL), lambda b, t, *_: (b * n_steps + t, 0))
    rot_t_spec = pl.BlockSpec((ROT_DIM // 2, step), lambda b, t, *_: (0, t))
    last_spec = pl.BlockSpec((1, KV_DIM, WINDOW), lambda b, t, *_: (b, 0, 0))

    def const(shape):
        return pl.BlockSpec(shape, lambda *_: (0,) * len(shape), pipeline_mode=pl.Buffered(1))

    def half(i):
        return pl.BlockSpec((D_MODEL, D_MODEL), lambda *_: (i, 0), pipeline_mode=pl.Buffered(1))

    return pl.pallas_call(
        functools.partial(_layer_b_prompt_kernel, tile=tile, n_t=n_t, n_sub=n_sub),
        grid_spec=pltpu.PrefetchScalarGridSpec(
            num_scalar_prefetch=1,
            grid=(batch, n_steps),
            in_specs=[tok_spec, const(kv_norm.shape), const(norm_b.shape), const(final_norm.shape),
                      const(w_kv_t.shape), half(0), half(1), const(w_out_t.shape),
                      rot_t_spec, rot_t_spec],
            out_specs=[tok_spec, last_spec, last_spec],
            scratch_shapes=[pltpu.VMEM((WINDOW + tile, KV_DIM), BF16),
                            pltpu.VMEM((KV_DIM, WINDOW + tile), BF16),
                            pltpu.VMEM((D_MODEL, tile), BF16),
                            pltpu.VMEM((D_MODEL, tile), F32),
                            pltpu.VMEM((2, WINDOW + Q_BLOCK, Q_BLOCK), F32)]),
        out_shape=[jax.ShapeDtypeStruct((batch * seq, D_MODEL), F32),
                   jax.ShapeDtypeStruct((batch, KV_DIM, WINDOW), F32),
                   jax.ShapeDtypeStruct((batch, KV_DIM, WINDOW), F32)],
        compiler_params=pltpu.CompilerParams(
            dimension_semantics=("arbitrary", "arbitrary"), vmem_limit_bytes=VMEM_LIMIT_BYTES),
        name="layer_b_prompt",
    )(sinks, h, kv_norm, norm_b, final_norm, w_kv_t, w_qg_t, w_qg_t, w_out_t, cos_t, sin_t)


def _layer_b_sample_kernel(sinks_ref, h_ref, nbg_ref, fg_ref, wqgt_ref, woutt_ref,
                           cos_ref, slo_ref, shi_ref, knew_ref, vnew_ref, ck_ref, cv_ref,
                           y_ref, q_scr, gate_scr, o_scr, *, n_seq, b_tile):
    step = pl.program_id(0)

    @pl.when(step == 0)
    def _():
        h = h_ref[...]
        hn = h * lax.rsqrt(jnp.mean(h * h, axis=-1, keepdims=True) + EPS)
        qg = _dot_nt((hn * nbg_ref[...]).astype(BF16), wqgt_ref[...])
        cos, slo, shi = cos_ref[...], slo_ref[...], shi_ref[...]
        for c in range(D_MODEL // LANES):
            cols = slice(c * LANES, (c + 1) * LANES)
            q2 = _rotate(qg[:, cols], cos, slo, shi) * HEAD_DIM ** -0.5
            g2 = qg[:, D_MODEL + c * LANES:D_MODEL + (c + 1) * LANES]
            for i in range(LANES // HEAD_DIM):
                dst = _member_major(c * (LANES // HEAD_DIM) + i)
                q_scr[:, dst] = q2[:, i * HEAD_DIM:(i + 1) * HEAD_DIM]
                gate_scr[:, dst] = g2[:, i * HEAD_DIM:(i + 1) * HEAD_DIM]

    n_rows = GQA_GROUP * N_KV_HEADS * SAMPLE_GROUP
    row = lax.broadcasted_iota(jnp.int32, (n_rows, 1), 0)
    row_kh = (row // SAMPLE_GROUP) % N_KV_HEADS
    row_seq = row % SAMPLE_GROUP
    lane_kh = lax.broadcasted_iota(jnp.int32, (1, KV_DIM), 1) // HEAD_DIM
    own = row_kh == lane_kh
    sink = jnp.zeros((n_rows, 1), F32)
    for r in range(GQA_GROUP):
        for kh in range(N_KV_HEADS):
            sink = jnp.where(row // SAMPLE_GROUP == r * N_KV_HEADS + kh,
                             sinks_ref[kh * GQA_GROUP + r], sink)
    n_blk = GQA_GROUP * N_KV_HEADS

    def group(i, carry):
        b0 = i * SAMPLE_GROUP
        g0 = pl.multiple_of(step * b_tile + b0, SAMPLE_GROUP)
        seqs = pl.ds(g0, SAMPLE_GROUP)
        q8 = q_scr[seqs, :]
        qexp = jnp.concatenate(
            [q8[:, r * KV_DIM:(r + 1) * KV_DIM] for r in range(GQA_GROUP)
             for _ in range(N_KV_HEADS)], axis=0)
        qexp = jnp.where(own, qexp, 0.0).astype(BF16)
        knew8 = knew_ref[seqs, :].astype(BF16).astype(F32)
        vnew8 = vnew_ref[seqs, :].astype(BF16).astype(F32)
        s_new = jnp.sum(qexp.astype(F32) * jnp.concatenate([knew8] * n_blk, axis=0),
                        axis=1, keepdims=True)
        s_old = jnp.zeros((n_rows, WINDOW), F32)
        for b in range(SAMPLE_GROUP):
            s_b = _dot(qexp, ck_ref[b0 + b].astype(BF16))
            s_old = jnp.where(row_seq == b, s_b, s_old)
        m = jnp.maximum(jnp.maximum(jnp.max(s_old, axis=1, keepdims=True), s_new), sink)
        p_old = jnp.exp(s_old - m)
        p_new = jnp.exp(s_new - m)
        denom = jnp.sum(p_old, axis=1, keepdims=True) + p_new + jnp.exp(sink - m)
        p_old = p_old.astype(BF16)
        o = jnp.zeros((n_rows, KV_DIM), F32)
        for b in range(SAMPLE_GROUP):
            o_b = _dot_nt(p_old, cv_ref[b0 + b].astype(BF16))
            o = jnp.where(row_seq == b, o_b, o)
        o = (o + p_new.astype(BF16).astype(F32) * jnp.concatenate([vnew8] * n_blk, axis=0)) / denom
        o = jnp.where(own, o, 0.0)
        for r in range(GQA_GROUP):
            blks = [o[(r * N_KV_HEADS + kh) * SAMPLE_GROUP:(r * N_KV_HEADS + kh + 1) * SAMPLE_GROUP]
                    for kh in range(N_KV_HEADS)]
            o_scr[seqs, r * KV_DIM:(r + 1) * KV_DIM] = (blks[0] + blks[1]) + (blks[2] + blks[3])
        return carry

    lax.fori_loop(0, b_tile // SAMPLE_GROUP, group, 0, unroll=True)

    @pl.when(step == pl.num_programs(0) - 1)
    def _():
        og_mm = o_scr[...] * _silu(gate_scr[...])
        og = jnp.concatenate(
            [og_mm[:, _member_major(hd)] for hd in range(N_HEADS)], axis=1).astype(BF16)
        h2 = h_ref[...] + _dot_nt(og, woutt_ref[...])
        y_ref[:, 0, :] = _rms(h2, fg_ref[...])


def _layer_b_sample(h, sinks, norm_b, final_norm, w_qg_t, w_out_t, knew, vnew, cache_k, cache_v):
    n_seq = h.shape[0]
    b_tile = SAMPLE_B_TILE
    cos, slo, shi = _rotary_tables(jnp.full((1,), PAST_LEN, F32))

    def const(shape):
        return pl.BlockSpec(shape, lambda *_: (0,) * len(shape))

    assert cache_k.shape == (n_seq, KV_DIM, WINDOW)
    cache_spec = pl.BlockSpec((b_tile, KV_DIM, WINDOW), lambda i, *_: (i, 0, 0))
    consts = (h, norm_b, final_norm, w_qg_t, w_out_t, cos, slo, shi, knew, vnew)
    return pl.pallas_call(
        functools.partial(_layer_b_sample_kernel, n_seq=n_seq, b_tile=b_tile),
        grid_spec=pltpu.PrefetchScalarGridSpec(
            num_scalar_prefetch=1,
            grid=(n_seq // b_tile,),
            in_specs=[const(c.shape) for c in consts] + [cache_spec, cache_spec],
            out_specs=const((n_seq, 1, D_MODEL)),
            scratch_shapes=[pltpu.VMEM((n_seq, D_MODEL), F32),
                            pltpu.VMEM((n_seq, D_MODEL), F32),
                            pltpu.VMEM((n_seq, D_MODEL), F32)]),
        out_shape=jax.ShapeDtypeStruct((n_seq, 1, D_MODEL), F32),
        compiler_params=pltpu.CompilerParams(
            dimension_semantics=("arbitrary",), vmem_limit_bytes=VMEM_LIMIT_BYTES),
        name="layer_b_sample",
    )(sinks, *consts, cache_k, cache_v)


def kernel(x_prompt, x_sample, cache_k, cache_v, norm_a, w_in_a, v_norm_a, w_s_a, b_s_a, w_out_a,
           kv_norm, w_kv, norm_b, w_in_b, sinks_b, w_out_b, final_norm):
    batch, seq, _ = x_prompt.shape
    n_seq, dec_seq, _ = x_sample.shape
    assert dec_seq == 1 and seq % CHUNK == 0 and cache_k.shape[1] == WINDOW
    assert norm_a.shape[0] == 1 and norm_b.shape[0] == 1

    row = lambda g: g.reshape(1, -1)
    w_kv_t =_transpose_cast(w_kv)
    bs_chunk = jnp.repeat(b_s_a[0].T, A_GROUP_DIM, axis=1)
    ws_one = jnp.repeat(w_s_a[0, :, 0, 0], A_GROUP_DIM).reshape(1, A_WIDTH)
    bs_one = jnp.repeat(b_s_a[0, :, 0], A_GROUP_DIM).reshape(1, A_WIDTH)

    def to_window(x_t):
        n = x_t.shape[0]
        return x_t.reshape(n, N_KV_HEADS, HEAD_DIM, WINDOW).transpose(0, 3, 1, 2)

    def from_window(x):
        return x.transpose(0, 2, 3, 1).reshape(x.shape[0], KV_DIM, WINDOW)

    cache_kt, cache_vt = from_window(cache_k), from_window(cache_v)

    w_in_a16, w_out_a16, h_s, av_s, knew, vnew, knew_t, vnew_t = _layer_a_sample(
        x_sample, row(norm_a[0]), w_in_a[0], row(v_norm_a[0]), ws_one,
        bs_one, w_out_a[0], row(kv_norm), w_kv_t)
    h_p, kt_s, vt_s, w_qg_t, w_out_b_t = _layer_a_prompt(
        x_prompt.reshape(batch * seq, D_MODEL), row(norm_a[0]), w_in_a16, row(v_norm_a[0]),
        w_s_a[0], bs_chunk, w_out_a16, cache_kt, cache_vt, knew_t, vnew_t, w_in_b[0], w_out_b[0])

    y_p, kt_p, vt_p = _layer_b_prompt(
        h_p, sinks_b[0], row(kv_norm), row(norm_b[0]), row(final_norm),
        w_kv_t, w_qg_t, w_out_b_t, batch=batch, seq=seq)
    y_s = _layer_b_sample(h_s, sinks_b[0], row(norm_b[0]), row(final_norm), w_qg_t, w_out_b_t,
                          knew, vnew, cache_kt, cache_vt)

    return (y_p.reshape(batch, seq, D_MODEL),
            y_s,
            to_window(kt_p),
            to_window(vt_p),
            to_window(kt_s),
            to_window(vt_s),
            av_s.reshape(1, n_seq, 1, A_WIDTH))
```

```python
import functools

import jax
import jax.numpy as jnp
from jax import lax
from jax.experimental import pallas as pl
from jax.experimental.pallas import tpu as pltpu

D_MODEL = 1024
PAST_LEN = 8192
CHUNK = 128
A_WIDTH = 2 * D_MODEL
A_GROUPS = 8
A_GROUP_DIM = A_WIDTH // A_GROUPS
HEAD_DIM = 64
N_HEADS = D_MODEL // HEAD_DIM
N_KV_HEADS = 4
GQA_GROUP = N_HEADS // N_KV_HEADS
KV_DIM = N_KV_HEADS * HEAD_DIM
WINDOW = 128
Q_BLOCK = 128
ROT_DIM = HEAD_DIM // 4
ROPE_THETA = 500000.0
EPS = 1e-5

LANES = 128
BF16_SUBLANES = 16
LOG2_E = 1.4426950408889634
Q_SCALE_LOG2 = HEAD_DIM ** -0.5 * LOG2_E
VMEM_LIMIT_BYTES = 56 * 1024 * 1024

A_TILE = 512
B_TILE = 512
B_SUBTILES = 2
A_SAMPLE_COLS = 1024
A_SAMPLE_ROWS = 1024
OUT_ROWS = 256
ATTN_AHEAD = 2
SAMPLE_B_TILE = 16
SAMPLE_GROUP = 8

F32 = jnp.float32
BF16 = jnp.bfloat16


def _rms(x, g):
    return x * lax.rsqrt(jnp.mean(x * x, axis=-1, keepdims=True) + EPS) * g


def _silu(x):
    return x * jax.nn.sigmoid(x)


def _dot(a, b):
    return jnp.dot(a, b, preferred_element_type=F32)


def _dot_nt(a, b):
    return lax.dot_general(a, b, (((1,), (1,)), ((), ())), preferred_element_type=F32)


def _zero_of(x):
    bits = pltpu.bitcast(x, jnp.uint32)
    return ((bits >> 16) >> 16).astype(F32)


def _member_major(head):
    kh, r = divmod(head, GQA_GROUP)
    start = (r * N_KV_HEADS + kh) * HEAD_DIM
    return slice(start, start + HEAD_DIM)


def _rotate(x, cos, sin_lo, sin_hi):
    return (x * cos + pltpu.roll(x, LANES - ROT_DIM // 2, 1) * sin_lo
            + pltpu.roll(x, ROT_DIM // 2, 1) * sin_hi)


def _rotary_tables(positions):
    lane = jnp.arange(LANES) % HEAD_DIM
    freq = (2 * (lane % (ROT_DIM // 2))).astype(F32)
    ang = positions[:, None] * (ROPE_THETA ** (-freq / ROT_DIM))[None, :]
    first = (lane < ROT_DIM // 2)[None, :]
    second = ((lane >= ROT_DIM // 2) & (lane < ROT_DIM))[None, :]
    cos = jnp.where(first | second, jnp.cos(ang), 1.0)
    sin_lo = jnp.where(first, -jnp.sin(ang), 0.0)
    sin_hi = jnp.where(second, jnp.sin(ang), 0.0)
    return cos, sin_lo, sin_hi


def _layer_a_tile(x, ng_ref, win_ref, vg_ref, ws_ref, bs_ref, wout_ref, y_scr, *, tile):
    xn = _rms(x, ng_ref[...]).astype(BF16)
    vb = _rms(_dot(xn, win_ref[:, A_WIDTH:2 * A_WIDTH]), vg_ref[...]).astype(BF16)
    row = lax.broadcasted_iota(jnp.int32, (CHUNK, CHUNK), 0)
    col = lax.broadcasted_iota(jnp.int32, (CHUNK, CHUNK), 1)
    tri = row >= col
    width = 2 * A_GROUP_DIM
    for pair in range(A_GROUPS // 2):
        cols = slice(pair * width, (pair + 1) * width)
        u = _dot(xn, win_ref[:, cols])
        gate = _dot(xn, win_ref[:, 2 * A_WIDTH + pair * width:2 * A_WIDTH + (pair + 1) * width])
        ws = [jnp.where(tri, ws_ref[g], 0.0).astype(BF16) for g in (2 * pair, 2 * pair + 1)]
        z = jnp.concatenate(
            [jnp.concatenate(
                [_dot(ws[i], vb[c * CHUNK:(c + 1) * CHUNK,
                                (2 * pair + i) * A_GROUP_DIM:(2 * pair + i + 1) * A_GROUP_DIM])
                 for i in range(2)], axis=1) + bs_ref[:, cols]
             for c in range(tile // CHUNK)], axis=0)
        y_scr[:, cols] = (u * z * _silu(gate)).astype(BF16)
    return x + _dot(y_scr[...], wout_ref[...])


def _layer_a_prompt_kernel(x_ref, ng_ref, win_ref, vg_ref, ws_ref, bs_ref, wout_ref,
                           ck_ref, cv_ref, knewt_ref, vnewt_ref, winb_ref, woutb_ref,
                           h_ref, kout_ref, vout_ref, wqgt_ref, woutbt_ref, y_scr, *, tile, n_roll):
    wqgt_ref[...] = winb_ref[...].T.astype(BF16)
    woutbt_ref[...] = woutb_ref[...].T.astype(BF16)
    is_last = lax.broadcasted_iota(jnp.int32, (KV_DIM, WINDOW), 1) == WINDOW - 1
    for b in range(n_roll):
        g = pl.program_id(0) * n_roll + b
        blk = pl.ds(pl.multiple_of((g // LANES) * LANES, LANES), LANES)
        to_last = LANES - 1 - g % LANES
        kout_ref[b] = jnp.where(is_last, pltpu.roll(knewt_ref[:, blk], to_last, 1),
                                pltpu.roll(ck_ref[b], WINDOW - 1, 1))
        vout_ref[b] = jnp.where(is_last, pltpu.roll(vnewt_ref[:, blk], to_last, 1),
                                pltpu.roll(cv_ref[b], WINDOW - 1, 1))
    h_ref[...] = _layer_a_tile(x_ref[...], ng_ref, win_ref, vg_ref, ws_ref, bs_ref, wout_ref,
                               y_scr, tile=tile)


def _layer_a_sample_kernel(x_ref, ng_ref, win_ref, vg_ref, ws_ref, bs_ref, wout_ref,
                           kvg_ref, wkv_ref, cos_ref, slo_ref, shi_ref,
                           win16_ref, wout16_ref, wkvt_ref, h_ref, av_ref, knew_ref, vnew_ref,
                           knewt_ref, vnewt_ref, xn_scr, proj_scr, y_scr, acc_scr, *, n_in, n_out):
    step = pl.program_id(0)
    blocks_per_branch = n_in // 3

    @pl.when(step == 0)
    def _():
        xn_scr[...] = _rms(x_ref[:, 0, :], ng_ref[...]).astype(BF16)
        wkvt_ref[...] = wkv_ref[...].T.astype(BF16)

    @pl.when(step < n_in)
    def _():
        w = win_ref[...].astype(BF16)
        win16_ref[...] = w
        proj_scr[step] = _dot(xn_scr[...], w)

    @pl.when(step == n_in)
    def _():
        def branch(i):
            return jnp.concatenate(
                [proj_scr[i * blocks_per_branch + j] for j in range(blocks_per_branch)], axis=1)

        v = _rms(branch(1), vg_ref[...])
        av_ref[:, 0, :] = v
        z = v * ws_ref[...] + bs_ref[...]
        y = (branch(0) * z * _silu(branch(2))).astype(BF16)
        rows = A_WIDTH // n_out
        for j in range(n_out):
            y_scr[j] = y[:, j * rows:(j + 1) * rows]

    @pl.when(step >= n_in)
    def _():
        w = wout_ref[...].astype(BF16)
        wout16_ref[...] = w
        part = _dot(y_scr[step - n_in], w)

        @pl.when(step == n_in)
        def _():
            acc_scr[...] = part

        @pl.when(step > n_in)
        def _():
            acc_scr[...] += part

    @pl.when(step == n_in + n_out - 1)
    def _():
        h = x_ref[:, 0, :] + acc_scr[...]
        h_ref[...] = h
        hn = h * lax.rsqrt(jnp.mean(h * h, axis=-1, keepdims=True) + EPS)
        kv = _dot_nt((hn * kvg_ref[...]).astype(BF16), wkvt_ref[...])
        cos, slo, shi = cos_ref[...], slo_ref[...], shi_ref[...]
        k = jnp.concatenate(
            [_rotate(kv[:, c * LANES:(c + 1) * LANES], cos, slo, shi)
             for c in range(KV_DIM // LANES)], axis=1)
        knew_ref[...] = k
        vnew_ref[...] = kv[:, KV_DIM:]
        knewt_ref[...] = k.T
        vnewt_ref[...] = kv[:, KV_DIM:].T


def _const_spec(shape):
    return pl.BlockSpec(shape, lambda *_: (0,) * len(shape), pipeline_mode=pl.Buffered(1))


def _layer_a_prompt(x, norm_g, w_in, v_norm_g, ws, bs, w_out, cache_k, cache_v, knew_t, vnew_t,
                    w_in_b, w_out_b):
    n_tok = x.shape[0]
    tile = A_TILE
    n_steps = n_tok // tile
    n_seq = cache_k.shape[0]
    n_roll = n_seq // n_steps
    assert n_roll * n_steps == n_seq and n_seq % LANES == 0
    assert max(w_in_b.shape[1], w_out_b.shape[1]) <= n_steps * LANES
    tok_spec = pl.BlockSpec((tile, D_MODEL), lambda i: (i, 0))
    cache_spec = pl.BlockSpec((n_roll, KV_DIM, WINDOW), lambda i: (i, 0, 0))

    def column_block(w):
        last = w.shape[1] // LANES - 1
        return pl.BlockSpec((w.shape[0], LANES), lambda i: (0, jnp.minimum(i, last)))

    def row_block(w):
        last = w.shape[1] // LANES - 1
        return pl.BlockSpec((LANES, w.shape[0]), lambda i: (jnp.minimum(i, last), 0))

    consts = (norm_g, w_in, v_norm_g, ws, bs, w_out)
    return pl.pallas_call(
        functools.partial(_layer_a_prompt_kernel, tile=tile, n_roll=n_roll),
        grid=(n_steps,),
        in_specs=[tok_spec] + [_const_spec(c.shape) for c in consts]
        + [cache_spec, cache_spec, _const_spec(knew_t.shape), _const_spec(vnew_t.shape),
           column_block(w_in_b), column_block(w_out_b)],
        out_specs=[tok_spec, cache_spec, cache_spec, row_block(w_in_b), row_block(w_out_b)],
        out_shape=[jax.ShapeDtypeStruct((n_tok, D_MODEL), F32),
                   jax.ShapeDtypeStruct(cache_k.shape, F32),
                   jax.ShapeDtypeStruct(cache_v.shape, F32),
                   jax.ShapeDtypeStruct(w_in_b.shape[::-1], BF16),
                   jax.ShapeDtypeStruct(w_out_b.shape[::-1], BF16)],
        scratch_shapes=[pltpu.VMEM((tile, A_WIDTH), BF16)],
        compiler_params=pltpu.CompilerParams(
            dimension_semantics=("arbitrary",), vmem_limit_bytes=VMEM_LIMIT_BYTES),
        name="layer_a_prompt",
    )(x, *consts, cache_k, cache_v, knew_t, vnew_t, w_in_b, w_out_b)


def _layer_a_sample(x, norm_g, w_in, v_norm_g, ws, bs, w_out, kv_norm, w_kv):
    n_seq = x.shape[0]
    n_in = w_in.shape[1] // A_SAMPLE_COLS
    n_out = w_out.shape[0] // A_SAMPLE_ROWS
    assert n_in % 3 == 0
    cos, slo, shi = _rotary_tables(jnp.full((1,), PAST_LEN, F32))
    whole = lambda shape: pl.BlockSpec(shape, lambda i: (0,) * len(shape))
    win_block = lambda i: (0, jnp.minimum(i, n_in - 1))
    wout_block = lambda i: (jnp.maximum(i - n_in, 0), 0)
    small_dims = [(n_seq, D_MODEL), (n_seq, 1, A_WIDTH), (n_seq, KV_DIM), (n_seq, KV_DIM),
                  (KV_DIM, n_seq), (KV_DIM, n_seq)]
    return pl.pallas_call(
        functools.partial(_layer_a_sample_kernel, n_in=n_in, n_out=n_out),
        grid=(n_in + n_out,),
        in_specs=[whole(x.shape), whole(norm_g.shape),
                  pl.BlockSpec((D_MODEL, A_SAMPLE_COLS), win_block),
                  whole(v_norm_g.shape), whole(ws.shape), whole(bs.shape),
                  pl.BlockSpec((A_SAMPLE_ROWS, D_MODEL), wout_block),
                  whole(kv_norm.shape), whole(w_kv.shape),
                  whole(cos.shape), whole(slo.shape), whole(shi.shape)],
        out_specs=[pl.BlockSpec((D_MODEL, A_SAMPLE_COLS), win_block),
                   pl.BlockSpec((A_SAMPLE_ROWS, D_MODEL), wout_block), whole(w_kv.shape[::-1])]
        + [whole(d) for d in small_dims],
        out_shape=[jax.ShapeDtypeStruct(w_in.shape, BF16), jax.ShapeDtypeStruct(w_out.shape, BF16),
                   jax.ShapeDtypeStruct(w_kv.shape[::-1], BF16)]
        + [jax.ShapeDtypeStruct(d, F32) for d in small_dims],
        scratch_shapes=[pltpu.VMEM((n_seq, D_MODEL), BF16),
                        pltpu.VMEM((n_in, n_seq, A_SAMPLE_COLS), F32),
                        pltpu.VMEM((n_out, n_seq, A_SAMPLE_ROWS), BF16),
                        pltpu.VMEM((n_seq, D_MODEL), F32)],
        compiler_params=pltpu.CompilerParams(
            dimension_semantics=("arbitrary",), vmem_limit_bytes=VMEM_LIMIT_BYTES),
        name="layer_a_sample",
    )(x, norm_g, w_in, v_norm_g, ws, bs, w_out, kv_norm, w_kv, cos, slo, shi)


def _layer_b_prompt_kernel(sinks_ref, h_ref, kvg_ref, nbg_ref, fg_ref, wkvt_ref, wqt_ref,
                           wgt_ref, woutt_ref, cost_ref, sint_ref, y_ref, kout_ref, vout_ref,
                           *scratch, tile, n_t, n_sub):
    for sub in range(n_sub):
        rows = pl.ds(sub * tile, tile)
        _layer_b_prompt_tile(
            sinks_ref, h_ref.at[rows, :], kvg_ref, nbg_ref, fg_ref, wkvt_ref, wqt_ref, wgt_ref,
            woutt_ref, cost_ref.at[:, rows], sint_ref.at[:, rows], y_ref.at[rows, :], kout_ref,
            vout_ref, *scratch, tile=tile, n_t=n_t, t=pl.program_id(1) * n_sub + sub,
            first_possible=sub == 0, last_possible=sub == n_sub - 1)


def _layer_b_prompt_tile(sinks_ref, h_ref, kvg_ref, nbg_ref, fg_ref, wkvt_ref, wqt_ref,
                         wgt_ref, woutt_ref, cost_ref, sint_ref,
                         y_ref, kout_ref, vout_ref,
                         kext_scr, vtext_scr, qt_scr, ogt_scr, bias_scr,
                         *, tile, n_t, t, first_possible, last_possible):
    n_keys = WINDOW + Q_BLOCK

    if first_possible:
        @pl.when((pl.program_id(0) == 0) & (t == 0))
        def _():
            j = lax.broadcasted_iota(jnp.int32, (n_keys, Q_BLOCK), 0)
            i = lax.broadcasted_iota(jnp.int32, (n_keys, Q_BLOCK), 1)
            band = (j >= i) & (j <= WINDOW + i)
            bias_scr[0] = jnp.where(band & (j >= WINDOW), 0.0, -jnp.inf)
            bias_scr[1] = jnp.where(band, 0.0, -jnp.inf)

        @pl.when(t == 0)
        def _():
            kext_scr[0:WINDOW, :] = jnp.zeros((WINDOW, KV_DIM), BF16)
            vtext_scr[:, 0:WINDOW] = jnp.zeros((KV_DIM, WINDOW), BF16)

    h = h_ref[...]
    hn = h * lax.rsqrt(jnp.mean(h * h, axis=-1, keepdims=True) + EPS)
    xkv = (hn * kvg_ref[...]).astype(BF16)
    xb = (hn * nbg_ref[...]).astype(BF16)

    cost, sint = cost_ref[...], sint_ref[...]
    half = ROT_DIM // 2

    def rotate_head(rows):
        lo, hi = rows[0:half, :], rows[half:ROT_DIM, :]
        return jnp.concatenate(
            [lo * cost - hi * sint, hi * cost + lo * sint, rows[ROT_DIM:, :]], axis=0)

    kvt = _dot_nt(wkvt_ref[...], xkv)
    kt = jnp.concatenate(
        [rotate_head(kvt[kh * HEAD_DIM:(kh + 1) * HEAD_DIM, :]) for kh in range(N_KV_HEADS)],
        axis=0)
    vt = kvt[KV_DIM:, :]
    kext_scr[WINDOW:, :] = kt.T.astype(BF16)
    vtext_scr[:, WINDOW:] = vt.astype(BF16)

    if last_possible:
        @pl.when(t == n_t - 1)
        def _():
            kout_ref[0] = kt[:, tile - WINDOW:]
            vout_ref[0] = vt[:, tile - WINDOW:]

    qt = _dot_nt(wqt_ref[...], xb)
    for hd in range(N_HEADS):
        rot = rotate_head(qt[hd * HEAD_DIM:(hd + 1) * HEAD_DIM, :])
        qt_scr[hd * HEAD_DIM:(hd + 1) * HEAD_DIM, :] = (rot * Q_SCALE_LOG2).astype(BF16)

    lane_head = lax.broadcasted_iota(jnp.int32, (1, GQA_GROUP * Q_BLOCK), 1) // Q_BLOCK
    zeros_half = jnp.zeros((HEAD_DIM, GQA_GROUP * Q_BLOCK), BF16)
    ones_rows = jnp.ones((BF16_SUBLANES, n_keys), BF16)

    def scores(qb, kh):
        qcols = slice(qb * Q_BLOCK, (qb + 1) * Q_BLOCK)
        keys = slice(qb * Q_BLOCK, qb * Q_BLOCK + n_keys)
        q4 = jnp.concatenate(
            [qt_scr[(kh * GQA_GROUP + r) * HEAD_DIM:(kh * GQA_GROUP + r + 1) * HEAD_DIM, qcols]
             for r in range(GQA_GROUP)], axis=1)
        q4 = jnp.concatenate([q4, zeros_half] if kh % 2 == 0 else [zeros_half, q4], axis=0)
        kblk = kext_scr[keys, (kh // 2) * LANES:(kh // 2 + 1) * LANES]
        return _dot(kblk, q4)

    def finish(qb, kh, s, s_ahead):
        qcols = slice(qb * Q_BLOCK, (qb + 1) * Q_BLOCK)
        keys = slice(qb * Q_BLOCK, qb * Q_BLOCK + n_keys)
        if first_possible and qb == 0:
            bias = bias_scr[jnp.where(t > 0, 1, 0)]
        else:
            bias = bias_scr[1]
        s = s + jnp.concatenate([bias] * GQA_GROUP, axis=1)
        sink = jnp.zeros((1, GQA_GROUP * Q_BLOCK), F32)
        for r in range(GQA_GROUP):
            sink = jnp.where(lane_head == r, sinks_ref[kh * GQA_GROUP + r] * LOG2_E, sink)
        m = jnp.maximum(jnp.max(s, axis=0, keepdims=True), sink)
        p = jnp.exp2(s - m)
        if s_ahead is not None:
            p = jnp.concatenate(
                [p[:n_keys - 8, :], p[n_keys - 8:, :] + _zero_of(s_ahead[0:8, :])], axis=0)
        vt_ones = jnp.concatenate(
            [vtext_scr[kh * HEAD_DIM:(kh + 1) * HEAD_DIM, keys], ones_rows], axis=0)
        ot = _dot(vt_ones, p.astype(BF16))
        denom = ot[HEAD_DIM:HEAD_DIM + 1, :] + jnp.exp2(sink - m)
        ot = ot[0:HEAD_DIM, :] * (1.0 / denom)
        for r in range(GQA_GROUP):
            hd = kh * GQA_GROUP + r
            ogt_scr[hd * HEAD_DIM:(hd + 1) * HEAD_DIM, qcols] = ot[:, r * Q_BLOCK:(r + 1) * Q_BLOCK]

    blocks = [(qb, kh) for qb in range(tile // Q_BLOCK) for kh in range(N_KV_HEADS)]
    pending = [scores(*blk) for blk in blocks[:ATTN_AHEAD]]
    for n, blk in enumerate(blocks):
        if n + ATTN_AHEAD < len(blocks):
            pending.append(scores(*blocks[n + ATTN_AHEAD]))
        s_cur = pending.pop(0)
        finish(*blk, s_cur, pending[-1] if pending else None)

    kext_scr[0:WINDOW, :] = kext_scr[tile:tile + WINDOW, :]
    vtext_scr[:, 0:WINDOW] = vtext_scr[:, tile:tile + WINDOW]

    ogt = jnp.concatenate(
        [(ogt_scr[rc * OUT_ROWS:(rc + 1) * OUT_ROWS, :]
          * _silu(_dot_nt(wgt_ref[rc * OUT_ROWS:(rc + 1) * OUT_ROWS, :], xb))).astype(BF16)
         for rc in range(D_MODEL // OUT_ROWS)], axis=0)
    h2 = jnp.concatenate(
        [h[:, rc * OUT_ROWS:(rc + 1) * OUT_ROWS]
         + _dot(woutt_ref[rc * OUT_ROWS:(rc + 1) * OUT_ROWS, :], ogt).T
         for rc in range(D_MODEL // OUT_ROWS)], axis=1)
    y_ref[...] = _rms(h2, fg_ref[...])


def _layer_b_prompt(h, sinks, kv_norm, norm_b, final_norm, w_kv_t, w_qg_t, w_out_t, *, batch, seq):
    tile = B_TILE
    n_t = seq // tile
    n_sub = B_SUBTILES
    step = n_sub * tile
    n_steps = n_t // n_sub
    pos = jnp.arange(seq, dtype=F32)
    inv = ROPE_THETA ** (-jnp.arange(0, ROT_DIM, 2, dtype=F32) / ROT_DIM)
    ang_t = inv[:, None] * pos[None, :]
    cos_t, sin_t = jnp.cos(ang_t), jnp.sin(ang_t)
    tok_spec = pl.BlockSpec((step, D_MODEL), lambda b, t, *_: (b * n_steps + t, 0))
    rot_t_spec = pl.BlockSpec((ROT_DIM // 2, step), lambda b, t, *_: (0, t))
    last_spec = pl.BlockSpec((1, KV_DIM, WINDOW), lambda b, t, *_: (b, 0, 0))

    def const(shape):
        return pl.BlockSpec(shape, lambda *_: (0,) * len(shape), pipeline_mode=pl.Buffered(1))

    def half(i):
        return pl.BlockSpec((D_MODEL, D_MODEL), lambda *_: (i, 0), pipeline_mode=pl.Buffered(1))

    return pl.pallas_call(
        functools.partial(_layer_b_prompt_kernel, tile=tile, n_t=n_t, n_sub=n_sub),
        grid_spec=pltpu.PrefetchScalarGridSpec(
            num_scalar_prefetch=1,
            grid=(batch, n_steps),
            in_specs=[tok_spec, const(kv_norm.shape), const(norm_b.shape), const(final_norm.shape),
                      const(w_kv_t.shape), half(0), half(1), const(w_out_t.shape),
                      rot_t_spec, rot_t_spec],
            out_specs=[tok_spec, last_spec, last_spec],
            scratch_shapes=[pltpu.VMEM((WINDOW + tile, KV_DIM), BF16),
                            pltpu.VMEM((KV_DIM, WINDOW + tile), BF16),
                            pltpu.VMEM((D_MODEL, tile), BF16),
                            pltpu.VMEM((D_MODEL, tile), F32),
                            pltpu.VMEM((2, WINDOW + Q_BLOCK, Q_BLOCK), F32)]),
        out_shape=[jax.ShapeDtypeStruct((batch * seq, D_MODEL), F32),
                   jax.ShapeDtypeStruct((batch, KV_DIM, WINDOW), F32),
                   jax.ShapeDtypeStruct((batch, KV_DIM, WINDOW), F32)],
        compiler_params=pltpu.CompilerParams(
            dimension_semantics=("arbitrary", "arbitrary"), vmem_limit_bytes=VMEM_LIMIT_BYTES),
        name="layer_b_prompt",
    )(sinks, h, kv_norm, norm_b, final_norm, w_kv_t, w_qg_t, w_qg_t, w_out_t, cos_t, sin_t)


def _layer_b_sample_kernel(sinks_ref, h_ref, nbg_ref, fg_ref, wqgt_ref, woutt_ref,
                           cos_ref, slo_ref, shi_ref, knew_ref, vnew_ref, ck_ref, cv_ref,
                           y_ref, q_scr, gate_scr, o_scr, *, n_seq, b_tile):
    step = pl.program_id(0)

    @pl.when(step == 0)
    def _():
        h = h_ref[...]
        hn = h * lax.rsqrt(jnp.mean(h * h, axis=-1, keepdims=True) + EPS)
        qg = _dot_nt((hn * nbg_ref[...]).astype(BF16), wqgt_ref[...])
        cos, slo, shi = cos_ref[...], slo_ref[...], shi_ref[...]
        for c in range(D_MODEL // LANES):
            cols = slice(c * LANES, (c + 1) * LANES)
            q2 = _rotate(qg[:, cols], cos, slo, shi) * HEAD_DIM ** -0.5
            g2 = qg[:, D_MODEL + c * LANES:D_MODEL + (c + 1) * LANES]
            for i in range(LANES // HEAD_DIM):
                dst = _member_major(c * (LANES // HEAD_DIM) + i)
                q_scr[:, dst] = q2[:, i * HEAD_DIM:(i + 1) * HEAD_DIM]
                gate_scr[:, dst] = g2[:, i * HEAD_DIM:(i + 1) * HEAD_DIM]

    n_rows = GQA_GROUP * N_KV_HEADS * SAMPLE_GROUP
    row = lax.broadcasted_iota(jnp.int32, (n_rows, 1), 0)
    row_kh = (row // SAMPLE_GROUP) % N_KV_HEADS
    row_seq = row % SAMPLE_GROUP
    lane_kh = lax.broadcasted_iota(jnp.int32, (1, KV_DIM), 1) // HEAD_DIM
    own = row_kh == lane_kh
    sink = jnp.zeros((n_rows, 1), F32)
    for r in range(GQA_GROUP):
        for kh in range(N_KV_HEADS):
            sink = jnp.where(row // SAMPLE_GROUP == r * N_KV_HEADS + kh,
                             sinks_ref[kh * GQA_GROUP + r], sink)
    n_blk = GQA_GROUP * N_KV_HEADS

    def group(i, carry):
        b0 = i * SAMPLE_GROUP
        g0 = pl.multiple_of(step * b_tile + b0, SAMPLE_GROUP)
        seqs = pl.ds(g0, SAMPLE_GROUP)
        q8 = q_scr[seqs, :]
        qexp = jnp.concatenate(
            [q8[:, r * KV_DIM:(r + 1) * KV_DIM] for r in range(GQA_GROUP)
             for _ in range(N_KV_HEADS)], axis=0)
        qexp = jnp.where(own, qexp, 0.0).astype(BF16)
        knew8 = knew_ref[seqs, :].astype(BF16).astype(F32)
        vnew8 = vnew_ref[seqs, :].astype(BF16).astype(F32)
        s_new = jnp.sum(qexp.astype(F32) * jnp.concatenate([knew8] * n_blk, axis=0),
                        axis=1, keepdims=True)
        s_old = jnp.zeros((n_rows, WINDOW), F32)
        for b in range(SAMPLE_GROUP):
            s_b = _dot(qexp, ck_ref[b0 + b].astype(BF16))
            s_old = jnp.where(row_seq == b, s_b, s_old)
        m = jnp.maximum(jnp.maximum(jnp.max(s_old, axis=1, keepdims=True), s_new), sink)
        p_old = jnp.exp(s_old - m)
        p_new = jnp.exp(s_new - m)
        denom = jnp.sum(p_old, axis=1, keepdims=True) + p_new + jnp.exp(sink - m)
        p_old = p_old.astype(BF16)
        o = jnp.zeros((n_rows, KV_DIM), F32)
        for b in range(SAMPLE_GROUP):
            o_b = _dot_nt(p_old, cv_ref[b0 + b].astype(BF16))
            o = jnp.where(row_seq == b, o_b, o)
        o = (o + p_new.astype(BF16).astype(F32) * jnp.concatenate([vnew8] * n_blk, axis=0)) / denom
        o = jnp.where(own, o, 0.0)
        for r in range(GQA_GROUP):
            blks = [o[(r * N_KV_HEADS + kh) * SAMPLE_GROUP:(r * N_KV_HEADS + kh + 1) * SAMPLE_GROUP]
                    for kh in range(N_KV_HEADS)]
            o_scr[seqs, r * KV_DIM:(r + 1) * KV_DIM] = (blks[0] + blks[1]) + (blks[2] + blks[3])
        return carry

    lax.fori_loop(0, b_tile // SAMPLE_GROUP, group, 0, unroll=True)

    @pl.when(step == pl.num_programs(0) - 1)
    def _():
        og_mm = o_scr[...] * _silu(gate_scr[...])
        og = jnp.concatenate(
            [og_mm[:, _member_major(hd)] for hd in range(N_HEADS)], axis=1).astype(BF16)
        h2 = h_ref[...] + _dot_nt(og, woutt_ref[...])
        y_ref[:, 0, :] = _rms(h2, fg_ref[...])


def _layer_b_sample(h, sinks, norm_b, final_norm, w_qg_t, w_out_t, knew, vnew, cache_k, cache_v):
    n_seq = h.shape[0]
    b_tile = SAMPLE_B_TILE
    cos, slo, shi = _rotary_tables(jnp.full((1,), PAST_LEN, F32))

    def const(shape):
        return pl.BlockSpec(shape, lambda *_: (0,) * len(shape))

    assert cache_k.shape == (n_seq, KV_DIM, WINDOW)
    cache_spec = pl.BlockSpec((b_tile, KV_DIM, WINDOW), lambda i, *_: (i, 0, 0))
    consts = (h, norm_b, final_norm, w_qg_t, w_out_t, cos, slo, shi, knew, vnew)
    return pl.pallas_call(
        functools.partial(_layer_b_sample_kernel, n_seq=n_seq, b_tile=b_tile),
        grid_spec=pltpu.PrefetchScalarGridSpec(
            num_scalar_prefetch=1,
            grid=(n_seq // b_tile,),
            in_specs=[const(c.shape) for c in consts] + [cache_spec, cache_spec],
            out_specs=const((n_seq, 1, D_MODEL)),
            scratch_shapes=[pltpu.VMEM((n_seq, D_MODEL), F32),
                            pltpu.VMEM((n_seq, D_MODEL), F32),
                            pltpu.VMEM((n_seq, D_MODEL), F32)]),
        out_shape=jax.ShapeDtypeStruct((n_seq, 1, D_MODEL), F32),
        compiler_params=pltpu.CompilerParams(
            dimension_semantics=("arbitrary",), vmem_limit_bytes=VMEM_LIMIT_BYTES),
        name="layer_b_sample",
    )(sinks, *consts, cache_k, cache_v)


def kernel(x_prompt, x_sample, cache_k, cache_v, norm_a, w_in_a, v_norm_a, w_s_a, b_s_a, w_out_a,
           kv_norm, w_kv, norm_b, w_in_b, sinks_b, w_out_b, final_norm):
    batch, seq, _ = x_prompt.shape
    n_seq, dec_seq, _ = x_sample.shape
    assert dec_seq == 1 and seq % CHUNK == 0 and cache_k.shape[1] == WINDOW
    assert norm_a.shape[0] == 1 and norm_b.shape[0] == 1

    row = lambda g: g.reshape(1, -1)
    bs_chunk = jnp.repeat(b_s_a[0].T, A_GROUP_DIM, axis=1)
    ws_one = jnp.repeat(w_s_a[0, :, 0, 0], A_GROUP_DIM).reshape(1, A_WIDTH)
    bs_one = jnp.repeat(b_s_a[0, :, 0], A_GROUP_DIM).reshape(1, A_WIDTH)

    def to_window(x_t):
        n = x_t.shape[0]
        return x_t.reshape(n, N_KV_HEADS, HEAD_DIM, WINDOW).transpose(0, 3, 1, 2)

    def from_window(x):
        return x.transpose(0, 2, 3, 1).reshape(x.shape[0], KV_DIM, WINDOW)

    cache_kt, cache_vt = from_window(cache_k), from_window(cache_v)

    w_in_a16, w_out_a16, w_kv_t, h_s, av_s, knew, vnew, knew_t, vnew_t = _layer_a_sample(
        x_sample, row(norm_a[0]), w_in_a[0], row(v_norm_a[0]), ws_one,
        bs_one, w_out_a[0], row(kv_norm), w_kv)
    h_p, kt_s, vt_s, w_qg_t, w_out_b_t = _layer_a_prompt(
        x_prompt.reshape(batch * seq, D_MODEL), row(norm_a[0]), w_in_a16, row(v_norm_a[0]),
        w_s_a[0], bs_chunk, w_out_a16, cache_kt, cache_vt, knew_t, vnew_t, w_in_b[0], w_out_b[0])

    y_p, kt_p, vt_p = _layer_b_prompt(
        h_p, sinks_b[0], row(kv_norm), row(norm_b[0]), row(final_norm),
        w_kv_t, w_qg_t, w_out_b_t, batch=batch, seq=seq)
    y_s = _layer_b_sample(h_s, sinks_b[0], row(norm_b[0]), row(final_norm), w_qg_t, w_out_b_t,
                          knew, vnew, cache_kt, cache_vt)

    return (y_p.reshape(batch, seq, D_MODEL),
            y_s,
            to_window(kt_p),
            to_window(vt_p),
            to_window(kt_s),
            to_window(vt_s),
            av_s.reshape(1, n_seq, 1, A_WIDTH))
```

```python
import functools

import jax
import jax.numpy as jnp
from jax import lax
from jax.experimental import pallas as pl
from jax.experimental.pallas import tpu as pltpu

D_MODEL = 1024
PAST_LEN = 8192
CHUNK = 128
A_WIDTH = 2 * D_MODEL
A_GROUPS = 8
A_GROUP_DIM = A_WIDTH // A_GROUPS
HEAD_DIM = 64
N_HEADS = D_MODEL // HEAD_DIM
N_KV_HEADS = 4
GQA_GROUP = N_HEADS // N_KV_HEADS
KV_DIM = N_KV_HEADS * HEAD_DIM
WINDOW = 128
Q_BLOCK = 128
ROT_DIM = HEAD_DIM // 4
ROPE_THETA = 500000.0
EPS = 1e-5

LANES = 128
BF16_SUBLANES = 16
LOG2_E = 1.4426950408889634
Q_SCALE_LOG2 = HEAD_DIM ** -0.5 * LOG2_E
VMEM_LIMIT_BYTES = 56 * 1024 * 1024

A_TILE = 512
B_TILE = 512
B_SUBTILES = 2
A_SAMPLE_COLS = 1024
A_SAMPLE_ROWS = 1024
OUT_ROWS = 256
ATTN_AHEAD = 2
SAMPLE_B_TILE = 16
SAMPLE_GROUP = 8

F32 = jnp.float32
BF16 = jnp.bfloat16


def _rms(x, g):
    return x * lax.rsqrt(jnp.mean(x * x, axis=-1, keepdims=True) + EPS) * g


def _silu(x):
    return x * jax.nn.sigmoid(x)


def _dot(a, b):
    return jnp.dot(a, b, preferred_element_type=F32)


def _dot_nt(a, b):
    return lax.dot_general(a, b, (((1,), (1,)), ((), ())), preferred_element_type=F32)


def _zero_of(x):
    bits = pltpu.bitcast(x, jnp.uint32)
    return ((bits >> 16) >> 16).astype(F32)


def _member_major(head):
    kh, r = divmod(head, GQA_GROUP)
    start = (r * N_KV_HEADS + kh) * HEAD_DIM
    return slice(start, start + HEAD_DIM)


def _rotate(x, cos, sin_lo, sin_hi):
    return (x * cos + pltpu.roll(x, LANES - ROT_DIM // 2, 1) * sin_lo
            + pltpu.roll(x, ROT_DIM // 2, 1) * sin_hi)


def _rotary_tables(positions):
    lane = jnp.arange(LANES) % HEAD_DIM
    freq = (2 * (lane % (ROT_DIM // 2))).astype(F32)
    ang = positions[:, None] * (ROPE_THETA ** (-freq / ROT_DIM))[None, :]
    first = (lane < ROT_DIM // 2)[None, :]
    second = ((lane >= ROT_DIM // 2) & (lane < ROT_DIM))[None, :]
    cos = jnp.where(first | second, jnp.cos(ang), 1.0)
    sin_lo = jnp.where(first, -jnp.sin(ang), 0.0)
    sin_hi = jnp.where(second, jnp.sin(ang), 0.0)
    return cos, sin_lo, sin_hi


def _layer_a_tile(x, ng_ref, win_ref, vg_ref, ws_ref, bs_ref, wout_ref, y_scr, *, tile):
    xn = _rms(x, ng_ref[...]).astype(BF16)
    vb = _rms(_dot(xn, win_ref[:, A_WIDTH:2 * A_WIDTH]), vg_ref[...]).astype(BF16)
    row = lax.broadcasted_iota(jnp.int32, (CHUNK, CHUNK), 0)
    col = lax.broadcasted_iota(jnp.int32, (CHUNK, CHUNK), 1)
    tri = row >= col
    ws = [jnp.where(tri, ws_ref[g], 0.0).astype(BF16) for g in range(A_GROUPS)]
    z = jnp.concatenate(
        [jnp.concatenate(
            [_dot(ws[g], vb[c * CHUNK:(c + 1) * CHUNK, g * A_GROUP_DIM:(g + 1) * A_GROUP_DIM])
             for g in range(A_GROUPS)], axis=1) + bs_ref[...]
         for c in range(tile // CHUNK)], axis=0)
    uz = _dot(xn, win_ref[:, 0:A_WIDTH]) * z
    gate = _dot(xn, win_ref[:, 2 * A_WIDTH:3 * A_WIDTH])
    y_scr[...] = (uz * _silu(gate)).astype(BF16)
    return x + _dot(y_scr[...], wout_ref[...])


def _layer_a_prompt_kernel(x_ref, ng_ref, win_ref, vg_ref, ws_ref, bs_ref, wout_ref,
                           ck_ref, cv_ref, knewt_ref, vnewt_ref, winb_ref, woutb_ref,
                           h_ref, kout_ref, vout_ref, wqgt_ref, woutbt_ref, y_scr, *, tile, n_roll):
    wqgt_ref[...] = winb_ref[...].T.astype(BF16)
    woutbt_ref[...] = woutb_ref[...].T.astype(BF16)
    is_last = lax.broadcasted_iota(jnp.int32, (KV_DIM, WINDOW), 1) == WINDOW - 1
    for b in range(n_roll):
        g = pl.program_id(0) * n_roll + b
        blk = pl.ds(pl.multiple_of((g // LANES) * LANES, LANES), LANES)
        to_last = LANES - 1 - g % LANES
        kout_ref[b] = jnp.where(is_last, pltpu.roll(knewt_ref[:, blk], to_last, 1),
                                pltpu.roll(ck_ref[b], WINDOW - 1, 1))
        vout_ref[b] = jnp.where(is_last, pltpu.roll(vnewt_ref[:, blk], to_last, 1),
                                pltpu.roll(cv_ref[b], WINDOW - 1, 1))
    h_ref[...] = _layer_a_tile(x_ref[...], ng_ref, win_ref, vg_ref, ws_ref, bs_ref, wout_ref,
                               y_scr, tile=tile)


def _layer_a_sample_kernel(x_ref, ng_ref, win_ref, vg_ref, ws_ref, bs_ref, wout_ref,
                           kvg_ref, wkv_ref, cos_ref, slo_ref, shi_ref,
                           win16_ref, wout16_ref, wkvt_ref, h_ref, av_ref, knew_ref, vnew_ref,
                           knewt_ref, vnewt_ref, xn_scr, proj_scr, y_scr, acc_scr, *, n_in, n_out):
    step = pl.program_id(0)
    blocks_per_branch = n_in // 3

    @pl.when(step == 0)
    def _():
        xn_scr[...] = _rms(x_ref[:, 0, :], ng_ref[...]).astype(BF16)
        wkvt_ref[...] = wkv_ref[...].T.astype(BF16)

    @pl.when(step < n_in)
    def _():
        w = win_ref[...].astype(BF16)
        win16_ref[...] = w
        proj_scr[step] = _dot(xn_scr[...], w)

    @pl.when(step == n_in)
    def _():
        def branch(i):
            return jnp.concatenate(
                [proj_scr[i * blocks_per_branch + j] for j in range(blocks_per_branch)], axis=1)

        v = _rms(branch(1), vg_ref[...])
        av_ref[:, 0, :] = v
        z = v * ws_ref[...] + bs_ref[...]
        y = (branch(0) * z * _silu(branch(2))).astype(BF16)
        rows = A_WIDTH // n_out
        for j in range(n_out):
            y_scr[j] = y[:, j * rows:(j + 1) * rows]

    @pl.when(step >= n_in)
    def _():
        w = wout_ref[...].astype(BF16)
        wout16_ref[...] = w
        part = _dot(y_scr[step - n_in], w)

        @pl.when(step == n_in)
        def _():
            acc_scr[...] = part

        @pl.when(step > n_in)
        def _():
            acc_scr[...] += part

    @pl.when(step == n_in + n_out - 1)
    def _():
        h = x_ref[:, 0, :] + acc_scr[...]
        h_ref[...] = h
        hn = h * lax.rsqrt(jnp.mean(h * h, axis=-1, keepdims=True) + EPS)
        kv = _dot_nt((hn * kvg_ref[...]).astype(BF16), wkvt_ref[...])
        cos, slo, shi = cos_ref[...], slo_ref[...], shi_ref[...]
        k = jnp.concatenate(
            [_rotate(kv[:, c * LANES:(c + 1) * LANES], cos, slo, shi)
             for c in range(KV_DIM // LANES)], axis=1)
        knew_ref[...] = k
        vnew_ref[...] = kv[:, KV_DIM:]
        knewt_ref[...] = k.T
        vnewt_ref[...] = kv[:, KV_DIM:].T


def _const_spec(shape):
    return pl.BlockSpec(shape, lambda *_: (0,) * len(shape), pipeline_mode=pl.Buffered(1))


def _layer_a_prompt(x, norm_g, w_in, v_norm_g, ws, bs, w_out, cache_k, cache_v, knew_t, vnew_t,
                    w_in_b, w_out_b):
    n_tok = x.shape[0]
    tile = A_TILE
    n_steps = n_tok // tile
    n_seq = cache_k.shape[0]
    n_roll = n_seq // n_steps
    assert n_roll * n_steps == n_seq and n_seq % LANES == 0
    assert max(w_in_b.shape[1], w_out_b.shape[1]) <= n_steps * LANES
    tok_spec = pl.BlockSpec((tile, D_MODEL), lambda i: (i, 0))
    cache_spec = pl.BlockSpec((n_roll, KV_DIM, WINDOW), lambda i: (i, 0, 0))

    def column_block(w):
        last = w.shape[1] // LANES - 1
        return pl.BlockSpec((w.shape[0], LANES), lambda i: (0, jnp.minimum(i, last)))

    def row_block(w):
        last = w.shape[1] // LANES - 1
        return pl.BlockSpec((LANES, w.shape[0]), lambda i: (jnp.minimum(i, last), 0))

    consts = (norm_g, w_in, v_norm_g, ws, bs, w_out)
    return pl.pallas_call(
        functools.partial(_layer_a_prompt_kernel, tile=tile, n_roll=n_roll),
        grid=(n_steps,),
        in_specs=[tok_spec] + [_const_spec(c.shape) for c in consts]
        + [cache_spec, cache_spec, _const_spec(knew_t.shape), _const_spec(vnew_t.shape),
           column_block(w_in_b), column_block(w_out_b)],
        out_specs=[tok_spec, cache_spec, cache_spec, row_block(w_in_b), row_block(w_out_b)],
        out_shape=[jax.ShapeDtypeStruct((n_tok, D_MODEL), F32),
                   jax.ShapeDtypeStruct(cache_k.shape, F32),
                   jax.ShapeDtypeStruct(cache_v.shape, F32),
                   jax.ShapeDtypeStruct(w_in_b.shape[::-1], BF16),
                   jax.ShapeDtypeStruct(w_out_b.shape[::-1], BF16)],
        scratch_shapes=[pltpu.VMEM((tile, A_WIDTH), BF16)],
        compiler_params=pltpu.CompilerParams(
            dimension_semantics=("arbitrary",), vmem_limit_bytes=VMEM_LIMIT_BYTES),
        name="layer_a_prompt",
    )(x, *consts, cache_k, cache_v, knew_t, vnew_t, w_in_b, w_out_b)


def _layer_a_sample(x, norm_g, w_in, v_norm_g, ws, bs, w_out, kv_norm, w_kv):
    n_seq = x.shape[0]
    n_in = w_in.shape[1] // A_SAMPLE_COLS
    n_out = w_out.shape[0] // A_SAMPLE_ROWS
    assert n_in % 3 == 0
    cos, slo, shi = _rotary_tables(jnp.full((1,), PAST_LEN, F32))
    whole = lambda shape: pl.BlockSpec(shape, lambda i: (0,) * len(shape))
    win_block = lambda i: (0, jnp.minimum(i, n_in - 1))
    wout_block = lambda i: (jnp.maximum(i - n_in, 0), 0)
    small_dims = [(n_seq, D_MODEL), (n_seq, 1, A_WIDTH), (n_seq, KV_DIM), (n_seq, KV_DIM),
                  (KV_DIM, n_seq), (KV_DIM, n_seq)]
    return pl.pallas_call(
        functools.partial(_layer_a_sample_kernel, n_in=n_in, n_out=n_out),
        grid=(n_in + n_out,),
        in_specs=[whole(x.shape), whole(norm_g.shape),
                  pl.BlockSpec((D_MODEL, A_SAMPLE_COLS), win_block),
                  whole(v_norm_g.shape), whole(ws.shape), whole(bs.shape),
                  pl.BlockSpec((A_SAMPLE_ROWS, D_MODEL), wout_block),
                  whole(kv_norm.shape), whole(w_kv.shape),
                  whole(cos.shape), whole(slo.shape), whole(shi.shape)],
        out_specs=[pl.BlockSpec((D_MODEL, A_SAMPLE_COLS), win_block),
                   pl.BlockSpec((A_SAMPLE_ROWS, D_MODEL), wout_block), whole(w_kv.shape[::-1])]
        + [whole(d) for d in small_dims],
        out_shape=[jax.ShapeDtypeStruct(w_in.shape, BF16), jax.ShapeDtypeStruct(w_out.shape, BF16),
                   jax.ShapeDtypeStruct(w_kv.shape[::-1], BF16)]
        + [jax.ShapeDtypeStruct(d, F32) for d in small_dims],
        scratch_shapes=[pltpu.VMEM((n_seq, D_MODEL), BF16),
                        pltpu.VMEM((n_in, n_seq, A_SAMPLE_COLS), F32),
                        pltpu.VMEM((n_out, n_seq, A_SAMPLE_ROWS), BF16),
                        pltpu.VMEM((n_seq, D_MODEL), F32)],
        compiler_params=pltpu.CompilerParams(
            dimension_semantics=("arbitrary",), vmem_limit_bytes=VMEM_LIMIT_BYTES),
        name="layer_a_sample",
    )(x, norm_g, w_in, v_norm_g, ws, bs, w_out, kv_norm, w_kv, cos, slo, shi)


def _layer_b_prompt_kernel(sinks_ref, h_ref, kvg_ref, nbg_ref, fg_ref, wkvt_ref, wqt_ref,
                           wgt_ref, woutt_ref, cost_ref, sint_ref, y_ref, kout_ref, vout_ref,
                           *scratch, tile, n_t, n_sub):
    for sub in range(n_sub):
        rows = pl.ds(sub * tile, tile)
        _layer_b_prompt_tile(
            sinks_ref, h_ref.at[rows, :], kvg_ref, nbg_ref, fg_ref, wkvt_ref, wqt_ref, wgt_ref,
            woutt_ref, cost_ref.at[:, rows], sint_ref.at[:, rows], y_ref.at[rows, :], kout_ref,
            vout_ref, *scratch, tile=tile, n_t=n_t, t=pl.program_id(1) * n_sub + sub,
            first_possible=sub == 0, last_possible=sub == n_sub - 1)


def _layer_b_prompt_tile(sinks_ref, h_ref, kvg_ref, nbg_ref, fg_ref, wkvt_ref, wqt_ref,
                         wgt_ref, woutt_ref, cost_ref, sint_ref,
                         y_ref, kout_ref, vout_ref,
                         kext_scr, vtext_scr, qt_scr, ogt_scr, bias_scr,
                         *, tile, n_t, t, first_possible, last_possible):
    n_keys = WINDOW + Q_BLOCK

    if first_possible:
        @pl.when((pl.program_id(0) == 0) & (t == 0))
        def _():
            j = lax.broadcasted_iota(jnp.int32, (n_keys, Q_BLOCK), 0)
            i = lax.broadcasted_iota(jnp.int32, (n_keys, Q_BLOCK), 1)
            band = (j >= i) & (j <= WINDOW + i)
            bias_scr[0] = jnp.where(band & (j >= WINDOW), 0.0, -jnp.inf)
            bias_scr[1] = jnp.where(band, 0.0, -jnp.inf)

        @pl.when(t == 0)
        def _():
            kext_scr[0:WINDOW, :] = jnp.zeros((WINDOW, KV_DIM), BF16)
            vtext_scr[:, 0:WINDOW] = jnp.zeros((KV_DIM, WINDOW), BF16)

    h = h_ref[...]
    hn = h * lax.rsqrt(jnp.mean(h * h, axis=-1, keepdims=True) + EPS)
    xkv = (hn * kvg_ref[...]).astype(BF16)
    xb = (hn * nbg_ref[...]).astype(BF16)

    cost, sint = cost_ref[...], sint_ref[...]
    half = ROT_DIM // 2

    def rotate_head(rows):
        lo, hi = rows[0:half, :], rows[half:ROT_DIM, :]
        return jnp.concatenate(
            [lo * cost - hi * sint, hi * cost + lo * sint, rows[ROT_DIM:, :]], axis=0)

    kvt = _dot_nt(wkvt_ref[...], xkv)
    kt = jnp.concatenate(
        [rotate_head(kvt[kh * HEAD_DIM:(kh + 1) * HEAD_DIM, :]) for kh in range(N_KV_HEADS)],
        axis=0)
    vt = kvt[KV_DIM:, :]
    kext_scr[WINDOW:, :] = kt.T.astype(BF16)
    vtext_scr[:, WINDOW:] = vt.astype(BF16)

    if last_possible:
        @pl.when(t == n_t - 1)
        def _():
            kout_ref[0] = kt[:, tile - WINDOW:]
            vout_ref[0] = vt[:, tile - WINDOW:]

    qt = _dot_nt(wqt_ref[...], xb)
    for hd in range(N_HEADS):
        rot = rotate_head(qt[hd * HEAD_DIM:(hd + 1) * HEAD_DIM, :])
        qt_scr[hd * HEAD_DIM:(hd + 1) * HEAD_DIM, :] = (rot * Q_SCALE_LOG2).astype(BF16)

    lane_head = lax.broadcasted_iota(jnp.int32, (1, GQA_GROUP * Q_BLOCK), 1) // Q_BLOCK
    zeros_half = jnp.zeros((HEAD_DIM, GQA_GROUP * Q_BLOCK), BF16)
    ones_rows = jnp.ones((BF16_SUBLANES, n_keys), BF16)

    def scores(qb, kh):
        qcols = slice(qb * Q_BLOCK, (qb + 1) * Q_BLOCK)
        keys = slice(qb * Q_BLOCK, qb * Q_BLOCK + n_keys)
        q4 = jnp.concatenate(
            [qt_scr[(kh * GQA_GROUP + r) * HEAD_DIM:(kh * GQA_GROUP + r + 1) * HEAD_DIM, qcols]
             for r in range(GQA_GROUP)], axis=1)
        q4 = jnp.concatenate([q4, zeros_half] if kh % 2 == 0 else [zeros_half, q4], axis=0)
        kblk = kext_scr[keys, (kh // 2) * LANES:(kh // 2 + 1) * LANES]
        return _dot(kblk, q4)

    def finish(qb, kh, s, s_ahead):
        qcols = slice(qb * Q_BLOCK, (qb + 1) * Q_BLOCK)
        keys = slice(qb * Q_BLOCK, qb * Q_BLOCK + n_keys)
        if first_possible and qb == 0:
            bias = bias_scr[jnp.where(t > 0, 1, 0)]
        else:
            bias = bias_scr[1]
        s = s + jnp.concatenate([bias] * GQA_GROUP, axis=1)
        sink = jnp.zeros((1, GQA_GROUP * Q_BLOCK), F32)
        for r in range(GQA_GROUP):
            sink = jnp.where(lane_head == r, sinks_ref[kh * GQA_GROUP + r] * LOG2_E, sink)
        m = jnp.maximum(jnp.max(s, axis=0, keepdims=True), sink)
        p = jnp.exp2(s - m)
        if s_ahead is not None:
            p = jnp.concatenate(
                [p[:n_keys - 8, :], p[n_keys - 8:, :] + _zero_of(s_ahead[0:8, :])], axis=0)
        vt_ones = jnp.concatenate(
            [vtext_scr[kh * HEAD_DIM:(kh + 1) * HEAD_DIM, keys], ones_rows], axis=0)
        ot = _dot(vt_ones, p.astype(BF16))
        denom = ot[HEAD_DIM:HEAD_DIM + 1, :] + jnp.exp2(sink - m)
        ot = ot[0:HEAD_DIM, :] * (1.0 / denom)
        for r in range(GQA_GROUP):
            hd = kh * GQA_GROUP + r
            ogt_scr[hd * HEAD_DIM:(hd + 1) * HEAD_DIM, qcols] = ot[:, r * Q_BLOCK:(r + 1) * Q_BLOCK]

    blocks = [(qb, kh) for qb in range(tile // Q_BLOCK) for kh in range(N_KV_HEADS)]
    pending = [scores(*blk) for blk in blocks[:ATTN_AHEAD]]
    for n, blk in enumerate(blocks):
        if n + ATTN_AHEAD < len(blocks):
            pending.append(scores(*blocks[n + ATTN_AHEAD]))
        s_cur = pending.pop(0)
        finish(*blk, s_cur, pending[-1] if pending else None)

    kext_scr[0:WINDOW, :] = kext_scr[tile:tile + WINDOW, :]
    vtext_scr[:, 0:WINDOW] = vtext_scr[:, tile:tile + WINDOW]

    ogt = jnp.concatenate(
        [(ogt_scr[rc * OUT_ROWS:(rc + 1) * OUT_ROWS, :]
          * _silu(_dot_nt(wgt_ref[rc * OUT_ROWS:(rc + 1) * OUT_ROWS, :], xb))).astype(BF16)
         for rc in range(D_MODEL // OUT_ROWS)], axis=0)
    h2 = jnp.concatenate(
        [h[:, rc * OUT_ROWS:(rc + 1) * OUT_ROWS]
         + _dot(woutt_ref[rc * OUT_ROWS:(rc + 1) * OUT_ROWS, :], ogt).T
         for rc in range(D_MODEL // OUT_ROWS)], axis=1)
    y_ref[...] = _rms(h2, fg_ref[...])


def _layer_b_prompt(h, sinks, kv_norm, norm_b, final_norm, w_kv_t, w_qg_t, w_out_t, *, batch, seq):
    tile = B_TILE
    n_t = seq // tile
    n_sub = B_SUBTILES
    step = n_sub * tile
    n_steps = n_t // n_sub
    pos = jnp.arange(seq, dtype=F32)
    inv = ROPE_THETA ** (-jnp.arange(0, ROT_DIM, 2, dtype=F32) / ROT_DIM)
    ang_t = inv[:, None] * pos[None, :]
    cos_t, sin_t = jnp.cos(ang_t), jnp.sin(ang_t)
    tok_spec = pl.BlockSpec((step, D_MODEL), lambda b, t, *_: (b * n_steps + t, 0))
    rot_t_spec = pl.BlockSpec((ROT_DIM // 2, step), lambda b, t, *_: (0, t))
    last_spec = pl.BlockSpec((1, KV_DIM, WINDOW), lambda b, t, *_: (b, 0, 0))

    def const(shape):
        return pl.BlockSpec(shape, lambda *_: (0,) * len(shape), pipeline_mode=pl.Buffered(1))

    def half(i):
        return pl.BlockSpec((D_MODEL, D_MODEL), lambda *_: (i, 0), pipeline_mode=pl.Buffered(1))

    return pl.pallas_call(
        functools.partial(_layer_b_prompt_kernel, tile=tile, n_t=n_t, n_sub=n_sub),
        grid_spec=pltpu.PrefetchScalarGridSpec(
            num_scalar_prefetch=1,
            grid=(batch, n_steps),
            in_specs=[tok_spec, const(kv_norm.shape), const(norm_b.shape), const(final_norm.shape),
                      const(w_kv_t.shape), half(0), half(1), const(w_out_t.shape),
                      rot_t_spec, rot_t_spec],
            out_specs=[tok_spec, last_spec, last_spec],
            scratch_shapes=[pltpu.VMEM((WINDOW + tile, KV_DIM), BF16),
                            pltpu.VMEM((KV_DIM, WINDOW + tile), BF16),
                            pltpu.VMEM((D_MODEL, tile), BF16),
                            pltpu.VMEM((D_MODEL, tile), F32),
                            pltpu.VMEM((2, WINDOW + Q_BLOCK, Q_BLOCK), F32)]),
        out_shape=[jax.ShapeDtypeStruct((batch * seq, D_MODEL), F32),
                   jax.ShapeDtypeStruct((batch, KV_DIM, WINDOW), F32),
                   jax.ShapeDtypeStruct((batch, KV_DIM, WINDOW), F32)],
        compiler_params=pltpu.CompilerParams(
            dimension_semantics=("arbitrary", "arbitrary"), vmem_limit_bytes=VMEM_LIMIT_BYTES),
        name="layer_b_prompt",
    )(sinks, h, kv_norm, norm_b, final_norm, w_kv_t, w_qg_t, w_qg_t, w_out_t, cos_t, sin_t)


def _layer_b_sample_kernel(sinks_ref, h_ref, nbg_ref, fg_ref, wqgt_ref, woutt_ref,
                           cos_ref, slo_ref, shi_ref, knew_ref, vnew_ref, ck_ref, cv_ref,
                           y_ref, q_scr, gate_scr, o_scr, *, n_seq, b_tile):
    step = pl.program_id(0)

    @pl.when(step == 0)
    def _():
        h = h_ref[...]
        hn = h * lax.rsqrt(jnp.mean(h * h, axis=-1, keepdims=True) + EPS)
        qg = _dot_nt((hn * nbg_ref[...]).astype(BF16), wqgt_ref[...])
        cos, slo, shi = cos_ref[...], slo_ref[...], shi_ref[...]
        for c in range(D_MODEL // LANES):
            cols = slice(c * LANES, (c + 1) * LANES)
            q2 = _rotate(qg[:, cols], cos, slo, shi) * HEAD_DIM ** -0.5
            g2 = qg[:, D_MODEL + c * LANES:D_MODEL + (c + 1) * LANES]
            for i in range(LANES // HEAD_DIM):
                dst = _member_major(c * (LANES // HEAD_DIM) + i)
                q_scr[:, dst] = q2[:, i * HEAD_DIM:(i + 1) * HEAD_DIM]
                gate_scr[:, dst] = g2[:, i * HEAD_DIM:(i + 1) * HEAD_DIM]

    n_rows = GQA_GROUP * N_KV_HEADS * SAMPLE_GROUP
    row = lax.broadcasted_iota(jnp.int32, (n_rows, 1), 0)
    row_kh = (row // SAMPLE_GROUP) % N_KV_HEADS
    row_seq = row % SAMPLE_GROUP
    lane_kh = lax.broadcasted_iota(jnp.int32, (1, KV_DIM), 1) // HEAD_DIM
    own = row_kh == lane_kh
    sink = jnp.zeros((n_rows, 1), F32)
    for r in range(GQA_GROUP):
        for kh in range(N_KV_HEADS):
            sink = jnp.where(row // SAMPLE_GROUP == r * N_KV_HEADS + kh,
                             sinks_ref[kh * GQA_GROUP + r], sink)
    n_blk = GQA_GROUP * N_KV_HEADS

    def group(i, carry):
        b0 = i * SAMPLE_GROUP
        g0 = pl.multiple_of(step * b_tile + b0, SAMPLE_GROUP)
        seqs = pl.ds(g0, SAMPLE_GROUP)
        q8 = q_scr[seqs, :]
        qexp = jnp.concatenate(
            [q8[:, r * KV_DIM:(r + 1) * KV_DIM] for r in range(GQA_GROUP)
             for _ in range(N_KV_HEADS)], axis=0)
        qexp = jnp.where(own, qexp, 0.0).astype(BF16)
        knew8 = knew_ref[seqs, :].astype(BF16).astype(F32)
        vnew8 = vnew_ref[seqs, :].astype(BF16).astype(F32)
        s_new = jnp.sum(qexp.astype(F32) * jnp.concatenate([knew8] * n_blk, axis=0),
                        axis=1, keepdims=True)
        s_old = jnp.zeros((n_rows, WINDOW), F32)
        for b in range(SAMPLE_GROUP):
            s_b = _dot(qexp, ck_ref[b0 + b].astype(BF16))
            s_old = jnp.where(row_seq == b, s_b, s_old)
        m = jnp.maximum(jnp.maximum(jnp.max(s_old, axis=1, keepdims=True), s_new), sink)
        p_old = jnp.exp(s_old - m)
        p_new = jnp.exp(s_new - m)
        denom = jnp.sum(p_old, axis=1, keepdims=True) + p_new + jnp.exp(sink - m)
        p_old = p_old.astype(BF16)
        o = jnp.zeros((n_rows, KV_DIM), F32)
        for b in range(SAMPLE_GROUP):
            o_b = _dot_nt(p_old, cv_ref[b0 + b].astype(BF16))
            o = jnp.where(row_seq == b, o_b, o)
        o = (o + p_new.astype(BF16).astype(F32) * jnp.concatenate([vnew8] * n_blk, axis=0)) / denom
        o = jnp.where(own, o, 0.0)
        for r in range(GQA_GROUP):
            blks = [o[(r * N_KV_HEADS + kh) * SAMPLE_GROUP:(r * N_KV_HEADS + kh + 1) * SAMPLE_GROUP]
                    for kh in range(N_KV_HEADS)]
            o_scr[seqs, r * KV_DIM:(r + 1) * KV_DIM] = (blks[0] + blks[1]) + (blks[2] + blks[3])
        return carry

    lax.fori_loop(0, b_tile // SAMPLE_GROUP, group, 0, unroll=True)

    @pl.when(step == pl.num_programs(0) - 1)
    def _():
        og_mm = o_scr[...] * _silu(gate_scr[...])
        og = jnp.concatenate(
            [og_mm[:, _member_major(hd)] for hd in range(N_HEADS)], axis=1).astype(BF16)
        h2 = h_ref[...] + _dot_nt(og, woutt_ref[...])
        y_ref[:, 0, :] = _rms(h2, fg_ref[...])


def _layer_b_sample(h, sinks, norm_b, final_norm, w_qg_t, w_out_t, knew, vnew, cache_k, cache_v):
    n_seq = h.shape[0]
    b_tile = SAMPLE_B_TILE
    cos, slo, shi = _rotary_tables(jnp.full((1,), PAST_LEN, F32))

    def const(shape):
        return pl.BlockSpec(shape, lambda *_: (0,) * len(shape))

    assert cache_k.shape == (n_seq, KV_DIM, WINDOW)
    cache_spec = pl.BlockSpec((b_tile, KV_DIM, WINDOW), lambda i, *_: (i, 0, 0))
    consts = (h, norm_b, final_norm, w_qg_t, w_out_t, cos, slo, shi, knew, vnew)
    return pl.pallas_call(
        functools.partial(_layer_b_sample_kernel, n_seq=n_seq, b_tile=b_tile),
        grid_spec=pltpu.PrefetchScalarGridSpec(
            num_scalar_prefetch=1,
            grid=(n_seq // b_tile,),
            in_specs=[const(c.shape) for c in consts] + [cache_spec, cache_spec],
            out_specs=const((n_seq, 1, D_MODEL)),
            scratch_shapes=[pltpu.VMEM((n_seq, D_MODEL), F32),
                            pltpu.VMEM((n_seq, D_MODEL), F32),
                            pltpu.VMEM((n_seq, D_MODEL), F32)]),
        out_shape=jax.ShapeDtypeStruct((n_seq, 1, D_MODEL), F32),
        compiler_params=pltpu.CompilerParams(
            dimension_semantics=("arbitrary",), vmem_limit_bytes=VMEM_LIMIT_BYTES),
        name="layer_b_sample",
    )(sinks, *consts, cache_k, cache_v)


def kernel(x_prompt, x_sample, cache_k, cache_v, norm_a, w_in_a, v_norm_a, w_s_a, b_s_a, w_out_a,
           kv_norm, w_kv, norm_b, w_in_b, sinks_b, w_out_b, final_norm):
    batch, seq, _ = x_prompt.shape
    n_seq, dec_seq, _ = x_sample.shape
    assert dec_seq == 1 and seq % CHUNK == 0 and cache_k.shape[1] == WINDOW
    assert norm_a.shape[0] == 1 and norm_b.shape[0] == 1

    row = lambda g: g.reshape(1, -1)
    bs_chunk = jnp.repeat(b_s_a[0].T, A_GROUP_DIM, axis=1)
    ws_one = jnp.repeat(w_s_a[0, :, 0, 0], A_GROUP_DIM).reshape(1, A_WIDTH)
    bs_one = jnp.repeat(b_s_a[0, :, 0], A_GROUP_DIM).reshape(1, A_WIDTH)

    def to_window(x_t):
        n = x_t.shape[0]
        return x_t.reshape(n, N_KV_HEADS, HEAD_DIM, WINDOW).transpose(0, 3, 1, 2)

    def from_window(x):
        return x.transpose(0, 2, 3, 1).reshape(x.shape[0], KV_DIM, WINDOW)

    cache_kt, cache_vt = from_window(cache_k), from_window(cache_v)

    w_in_a16, w_out_a16, w_kv_t, h_s, av_s, knew, vnew, knew_t, vnew_t = _layer_a_sample(
        x_sample, row(norm_a[0]), w_in_a[0], row(v_norm_a[0]), ws_one,
        bs_one, w_out_a[0], row(kv_norm), w_kv)
    h_p, kt_s, vt_s, w_qg_t, w_out_b_t = _layer_a_prompt(
        x_prompt.reshape(batch * seq, D_MODEL), row(norm_a[0]), w_in_a16, row(v_norm_a[0]),
        w_s_a[0], bs_chunk, w_out_a16, cache_kt, cache_vt, knew_t, vnew_t, w_in_b[0], w_out_b[0])

    y_p, kt_p, vt_p = _layer_b_prompt(
        h_p, sinks_b[0], row(kv_norm), row(norm_b[0]), row(final_norm),
        w_kv_t, w_qg_t, w_out_b_t, batch=batch, seq=seq)
    y_s = _layer_b_sample(h_s, sinks_b[0], row(norm_b[0]), row(final_norm), w_qg_t, w_out_b_t,
                          knew, vnew, cache_kt, cache_vt)

    return (y_p.reshape(batch, seq, D_MODEL),
            y_s,
            to_window(kt_p),
            to_window(vt_p),
            to_window(kt_s),
            to_window(vt_s),
            av_s.reshape(1, n_seq, 1, A_WIDTH))
```

```python
import functools

import jax
import jax.numpy as jnp
from jax import lax
from jax.experimental import pallas as pl
from jax.experimental.pallas import tpu as pltpu

D_MODEL = 1024
PAST_LEN = 8192
CHUNK = 128
A_WIDTH = 2 * D_MODEL
A_GROUPS = 8
A_GROUP_DIM = A_WIDTH // A_GROUPS
HEAD_DIM = 64
N_HEADS = D_MODEL // HEAD_DIM
N_KV_HEADS = 4
GQA_GROUP = N_HEADS // N_KV_HEADS
KV_DIM = N_KV_HEADS * HEAD_DIM
WINDOW = 128
Q_BLOCK = 128
ROT_DIM = HEAD_DIM // 4
ROPE_THETA = 500000.0
EPS = 1e-5

LANES = 128
BF16_SUBLANES = 16
LOG2_E = 1.4426950408889634
Q_SCALE_LOG2 = HEAD_DIM ** -0.5 * LOG2_E
VMEM_LIMIT_BYTES = 56 * 1024 * 1024

A_TILE = 512
B_TILE = 512
B_SUBTILES = 2
A_SAMPLE_COLS = 1024
A_SAMPLE_ROWS = 1024
OUT_ROWS = 256
ATTN_AHEAD = 2
SAMPLE_B_TILE = 16
SAMPLE_GROUP = 8

F32 = jnp.float32
BF16 = jnp.bfloat16


def _rms(x, g):
    return x * lax.rsqrt(jnp.mean(x * x, axis=-1, keepdims=True) + EPS) * g


def _silu(x):
    return x * jax.nn.sigmoid(x)


def _dot(a, b):
    return jnp.dot(a, b, preferred_element_type=F32)


def _dot_nt(a, b):
    return lax.dot_general(a, b, (((1,), (1,)), ((), ())), preferred_element_type=F32)


def _zero_of(x):
    bits = pltpu.bitcast(x, jnp.uint32)
    return ((bits >> 16) >> 16).astype(F32)


def _member_major(head):
    kh, r = divmod(head, GQA_GROUP)
    start = (r * N_KV_HEADS + kh) * HEAD_DIM
    return slice(start, start + HEAD_DIM)


def _rotate(x, cos, sin_lo, sin_hi):
    return (x * cos + pltpu.roll(x, LANES - ROT_DIM // 2, 1) * sin_lo
            + pltpu.roll(x, ROT_DIM // 2, 1) * sin_hi)


def _rotary_tables(positions):
    lane = jnp.arange(LANES) % HEAD_DIM
    freq = (2 * (lane % (ROT_DIM // 2))).astype(F32)
    ang = positions[:, None] * (ROPE_THETA ** (-freq / ROT_DIM))[None, :]
    first = (lane < ROT_DIM // 2)[None, :]
    second = ((lane >= ROT_DIM // 2) & (lane < ROT_DIM))[None, :]
    cos = jnp.where(first | second, jnp.cos(ang), 1.0)
    sin_lo = jnp.where(first, -jnp.sin(ang), 0.0)
    sin_hi = jnp.where(second, jnp.sin(ang), 0.0)
    return cos, sin_lo, sin_hi


def _layer_a_tile(x, ng_ref, win_ref, vg_ref, ws_ref, bs_ref, wout_ref, y_scr, *, tile):
    xn = _rms(x, ng_ref[...]).astype(BF16)
    vb = _rms(_dot(xn, win_ref[:, A_WIDTH:2 * A_WIDTH]), vg_ref[...]).astype(BF16)
    row = lax.broadcasted_iota(jnp.int32, (CHUNK, CHUNK), 0)
    col = lax.broadcasted_iota(jnp.int32, (CHUNK, CHUNK), 1)
    tri = row >= col
    ws = [jnp.where(tri, ws_ref[g], 0.0).astype(BF16) for g in range(A_GROUPS)]
    z = jnp.concatenate(
        [jnp.concatenate(
            [_dot(ws[g], vb[c * CHUNK:(c + 1) * CHUNK, g * A_GROUP_DIM:(g + 1) * A_GROUP_DIM])
             for g in range(A_GROUPS)], axis=1) + bs_ref[...]
         for c in range(tile // CHUNK)], axis=0)
    uz = _dot(xn, win_ref[:, 0:A_WIDTH]) * z
    gate = _dot(xn, win_ref[:, 2 * A_WIDTH:3 * A_WIDTH])
    y_scr[...] = (uz * _silu(gate)).astype(BF16)
    return x + _dot(y_scr[...], wout_ref[...])


def _layer_a_prompt_kernel(x_ref, ng_ref, win_ref, vg_ref, ws_ref, b_ref, wout_ref,
                           ck_ref, cv_ref, knewt_ref, vnewt_ref, winb_ref, woutb_ref,
                           h_ref, kout_ref, vout_ref, wqgt_ref, woutbt_ref, y_scr, bs_scr,
                           *, tile, n_roll):
    @pl.when(pl.program_id(0) == 0)
    def _():
        b = jnp.concatenate(
            [b_ref[...], jnp.zeros((CHUNK - A_GROUPS, CHUNK), F32)], axis=0).T
        for g in range(A_GROUPS):
            bs_scr[:, g * A_GROUP_DIM:(g + 1) * A_GROUP_DIM] = jnp.broadcast_to(
                b[:, g:g + 1], (CHUNK, A_GROUP_DIM))

    wqgt_ref[...] = winb_ref[...].T.astype(BF16)
    woutbt_ref[...] = woutb_ref[...].T.astype(BF16)
    is_last = lax.broadcasted_iota(jnp.int32, (KV_DIM, WINDOW), 1) == WINDOW - 1
    for b in range(n_roll):
        g = pl.program_id(0) * n_roll + b
        blk = pl.ds(pl.multiple_of((g // LANES) * LANES, LANES), LANES)
        to_last = LANES - 1 - g % LANES
        kout_ref[b] = jnp.where(is_last, pltpu.roll(knewt_ref[:, blk], to_last, 1),
                                pltpu.roll(ck_ref[b], WINDOW - 1, 1))
        vout_ref[b] = jnp.where(is_last, pltpu.roll(vnewt_ref[:, blk], to_last, 1),
                                pltpu.roll(cv_ref[b], WINDOW - 1, 1))
    h_ref[...] = _layer_a_tile(x_ref[...], ng_ref, win_ref, vg_ref, ws_ref, bs_scr, wout_ref,
                               y_scr, tile=tile)


def _layer_a_sample_kernel(x_ref, ng_ref, win_ref, vg_ref, ws_ref, bs_ref, wout_ref,
                           kvg_ref, wkv_ref, cos_ref, slo_ref, shi_ref,
                           win16_ref, wout16_ref, wkvt_ref, h_ref, av_ref, knew_ref, vnew_ref,
                           knewt_ref, vnewt_ref, xn_scr, proj_scr, y_scr, acc_scr, *, n_in, n_out):
    step = pl.program_id(0)
    blocks_per_branch = n_in // 3

    @pl.when(step == 0)
    def _():
        xn_scr[...] = _rms(x_ref[:, 0, :], ng_ref[...]).astype(BF16)
        wkvt_ref[...] = wkv_ref[...].T.astype(BF16)

    @pl.when(step < n_in)
    def _():
        w = win_ref[...].astype(BF16)
        win16_ref[...] = w
        proj_scr[step] = _dot(xn_scr[...], w)

    @pl.when(step == n_in)
    def _():
        def branch(i):
            return jnp.concatenate(
                [proj_scr[i * blocks_per_branch + j] for j in range(blocks_per_branch)], axis=1)

        v = _rms(branch(1), vg_ref[...])
        av_ref[:, 0, :] = v
        lane_group = lax.broadcasted_iota(jnp.int32, (1, A_WIDTH), 1) // A_GROUP_DIM
        ws_row = jnp.zeros((1, A_WIDTH), F32)
        bs_row = jnp.zeros((1, A_WIDTH), F32)
        for g in range(A_GROUPS):
            ws_row = jnp.where(lane_group == g, ws_ref[g, 0:1, 0:1], ws_row)
            bs_row = jnp.where(lane_group == g, bs_ref[g:g + 1, 0:1], bs_row)
        z = v * ws_row + bs_row
        y = (branch(0) * z * _silu(branch(2))).astype(BF16)
        rows = A_WIDTH // n_out
        for j in range(n_out):
            y_scr[j] = y[:, j * rows:(j + 1) * rows]

    @pl.when(step >= n_in)
    def _():
        w = wout_ref[...].astype(BF16)
        wout16_ref[...] = w
        part = _dot(y_scr[step - n_in], w)

        @pl.when(step == n_in)
        def _():
            acc_scr[...] = part

        @pl.when(step > n_in)
        def _():
            acc_scr[...] += part

    @pl.when(step == n_in + n_out - 1)
    def _():
        h = x_ref[:, 0, :] + acc_scr[...]
        h_ref[...] = h
        hn = h * lax.rsqrt(jnp.mean(h * h, axis=-1, keepdims=True) + EPS)
        kv = _dot_nt((hn * kvg_ref[...]).astype(BF16), wkvt_ref[...])
        cos, slo, shi = cos_ref[...], slo_ref[...], shi_ref[...]
        k = jnp.concatenate(
            [_rotate(kv[:, c * LANES:(c + 1) * LANES], cos, slo, shi)
             for c in range(KV_DIM // LANES)], axis=1)
        knew_ref[...] = k
        vnew_ref[...] = kv[:, KV_DIM:]
        knewt_ref[...] = k.T
        vnewt_ref[...] = kv[:, KV_DIM:].T


def _const_spec(shape):
    return pl.BlockSpec(shape, lambda *_: (0,) * len(shape), pipeline_mode=pl.Buffered(1))


def _layer_a_prompt(x, norm_g, w_in, v_norm_g, ws, bs, w_out, cache_k, cache_v, knew_t, vnew_t,
                    w_in_b, w_out_b):
    n_tok = x.shape[0]
    tile = A_TILE
    n_steps = n_tok // tile
    n_seq = cache_k.shape[0]
    n_roll = n_seq // n_steps
    assert n_roll * n_steps == n_seq and n_seq % LANES == 0
    assert max(w_in_b.shape[1], w_out_b.shape[1]) <= n_steps * LANES
    tok_spec = pl.BlockSpec((tile, D_MODEL), lambda i: (i, 0))
    cache_spec = pl.BlockSpec((n_roll, KV_DIM, WINDOW), lambda i: (i, 0, 0))

    def column_block(w):
        last = w.shape[1] // LANES - 1
        return pl.BlockSpec((w.shape[0], LANES), lambda i: (0, jnp.minimum(i, last)))

    def row_block(w):
        last = w.shape[1] // LANES - 1
        return pl.BlockSpec((LANES, w.shape[0]), lambda i: (jnp.minimum(i, last), 0))

    consts = (norm_g, w_in, v_norm_g, ws, bs, w_out)
    return pl.pallas_call(
        functools.partial(_layer_a_prompt_kernel, tile=tile, n_roll=n_roll),
        grid=(n_steps,),
        in_specs=[tok_spec] + [_const_spec(c.shape) for c in consts]
        + [cache_spec, cache_spec, _const_spec(knew_t.shape), _const_spec(vnew_t.shape),
           column_block(w_in_b), column_block(w_out_b)],
        out_specs=[tok_spec, cache_spec, cache_spec, row_block(w_in_b), row_block(w_out_b)],
        out_shape=[jax.ShapeDtypeStruct((n_tok, D_MODEL), F32),
                   jax.ShapeDtypeStruct(cache_k.shape, F32),
                   jax.ShapeDtypeStruct(cache_v.shape, F32),
                   jax.ShapeDtypeStruct(w_in_b.shape[::-1], BF16),
                   jax.ShapeDtypeStruct(w_out_b.shape[::-1], BF16)],
        scratch_shapes=[pltpu.VMEM((tile, A_WIDTH), BF16), pltpu.VMEM((CHUNK, A_WIDTH), F32)],
        compiler_params=pltpu.CompilerParams(
            dimension_semantics=("arbitrary",), vmem_limit_bytes=VMEM_LIMIT_BYTES),
        name="layer_a_prompt",
    )(x, *consts, cache_k, cache_v, knew_t, vnew_t, w_in_b, w_out_b)


def _layer_a_sample(x, norm_g, w_in, v_norm_g, ws, bs, w_out, kv_norm, w_kv):
    n_seq = x.shape[0]
    n_in = w_in.shape[1] // A_SAMPLE_COLS
    n_out = w_out.shape[0] // A_SAMPLE_ROWS
    assert n_in % 3 == 0
    cos, slo, shi = _rotary_tables(jnp.full((1,), PAST_LEN, F32))
    whole = lambda shape: pl.BlockSpec(shape, lambda i: (0,) * len(shape))
    win_block = lambda i: (0, jnp.minimum(i, n_in - 1))
    wout_block = lambda i: (jnp.maximum(i - n_in, 0), 0)
    small_dims = [(n_seq, D_MODEL), (n_seq, 1, A_WIDTH), (n_seq, KV_DIM), (n_seq, KV_DIM),
                  (KV_DIM, n_seq), (KV_DIM, n_seq)]
    return pl.pallas_call(
        functools.partial(_layer_a_sample_kernel, n_in=n_in, n_out=n_out),
        grid=(n_in + n_out,),
        in_specs=[whole(x.shape), whole(norm_g.shape),
                  pl.BlockSpec((D_MODEL, A_SAMPLE_COLS), win_block),
                  whole(v_norm_g.shape),
                  pl.BlockSpec((A_GROUPS, 8, LANES), lambda i: (0, 0, 0)), whole(bs.shape),
                  pl.BlockSpec((A_SAMPLE_ROWS, D_MODEL), wout_block),
                  whole(kv_norm.shape), whole(w_kv.shape),
                  whole(cos.shape), whole(slo.shape), whole(shi.shape)],
        out_specs=[pl.BlockSpec((D_MODEL, A_SAMPLE_COLS), win_block),
                   pl.BlockSpec((A_SAMPLE_ROWS, D_MODEL), wout_block), whole(w_kv.shape[::-1])]
        + [whole(d) for d in small_dims],
        out_shape=[jax.ShapeDtypeStruct(w_in.shape, BF16), jax.ShapeDtypeStruct(w_out.shape, BF16),
                   jax.ShapeDtypeStruct(w_kv.shape[::-1], BF16)]
        + [jax.ShapeDtypeStruct(d, F32) for d in small_dims],
        scratch_shapes=[pltpu.VMEM((n_seq, D_MODEL), BF16),
                        pltpu.VMEM((n_in, n_seq, A_SAMPLE_COLS), F32),
                        pltpu.VMEM((n_out, n_seq, A_SAMPLE_ROWS), BF16),
                        pltpu.VMEM((n_seq, D_MODEL), F32)],
        compiler_params=pltpu.CompilerParams(
            dimension_semantics=("arbitrary",), vmem_limit_bytes=VMEM_LIMIT_BYTES),
        name="layer_a_sample",
    )(x, norm_g, w_in, v_norm_g, ws, bs, w_out, kv_norm, w_kv, cos, slo, shi)


def _layer_b_prompt_kernel(sinks_ref, h_ref, kvg_ref, nbg_ref, fg_ref, wkvt_ref, wqt_ref,
                           wgt_ref, woutt_ref, cost_ref, sint_ref, y_ref, kout_ref, vout_ref,
                           *scratch, tile, n_t, n_sub):
    for sub in range(n_sub):
        rows = pl.ds(sub * tile, tile)
        _layer_b_prompt_tile(
            sinks_ref, h_ref.at[rows, :], kvg_ref, nbg_ref, fg_ref, wkvt_ref, wqt_ref, wgt_ref,
            woutt_ref, cost_ref.at[:, rows], sint_ref.at[:, rows], y_ref.at[rows, :], kout_ref,
            vout_ref, *scratch, tile=tile, n_t=n_t, t=pl.program_id(1) * n_sub + sub,
            first_possible=sub == 0, last_possible=sub == n_sub - 1)


def _layer_b_prompt_tile(sinks_ref, h_ref, kvg_ref, nbg_ref, fg_ref, wkvt_ref, wqt_ref,
                         wgt_ref, woutt_ref, cost_ref, sint_ref,
                         y_ref, kout_ref, vout_ref,
                         kext_scr, vtext_scr, qt_scr, ogt_scr, bias_scr,
                         *, tile, n_t, t, first_possible, last_possible):
    n_keys = WINDOW + Q_BLOCK

    if first_possible:
        @pl.when((pl.program_id(0) == 0) & (t == 0))
        def _():
            j = lax.broadcasted_iota(jnp.int32, (n_keys, Q_BLOCK), 0)
            i = lax.broadcasted_iota(jnp.int32, (n_keys, Q_BLOCK), 1)
            band = (j >= i) & (j <= WINDOW + i)
            bias_scr[0] = jnp.where(band & (j >= WINDOW), 0.0, -jnp.inf)
            bias_scr[1] = jnp.where(band, 0.0, -jnp.inf)

        @pl.when(t == 0)
        def _():
            kext_scr[0:WINDOW, :] = jnp.zeros((WINDOW, KV_DIM), BF16)
            vtext_scr[:, 0:WINDOW] = jnp.zeros((KV_DIM, WINDOW), BF16)

    h = h_ref[...]
    hn = h * lax.rsqrt(jnp.mean(h * h, axis=-1, keepdims=True) + EPS)
    xkv = (hn * kvg_ref[...]).astype(BF16)
    xb = (hn * nbg_ref[...]).astype(BF16)

    cost, sint = cost_ref[...], sint_ref[...]
    half = ROT_DIM // 2

    def rotate_head(rows):
        lo, hi = rows[0:half, :], rows[half:ROT_DIM, :]
        return jnp.concatenate(
            [lo * cost - hi * sint, hi * cost + lo * sint, rows[ROT_DIM:, :]], axis=0)

    kvt = _dot_nt(wkvt_ref[...], xkv)
    kt = jnp.concatenate(
        [rotate_head(kvt[kh * HEAD_DIM:(kh + 1) * HEAD_DIM, :]) for kh in range(N_KV_HEADS)],
        axis=0)
    vt = kvt[KV_DIM:, :]
    kext_scr[WINDOW:, :] = kt.T.astype(BF16)
    vtext_scr[:, WINDOW:] = vt.astype(BF16)

    if last_possible:
        @pl.when(t == n_t - 1)
        def _():
            kout_ref[0] = kt[:, tile - WINDOW:]
            vout_ref[0] = vt[:, tile - WINDOW:]

    qt = _dot_nt(wqt_ref[...], xb)
    for hd in range(N_HEADS):
        rot = rotate_head(qt[hd * HEAD_DIM:(hd + 1) * HEAD_DIM, :])
        qt_scr[hd * HEAD_DIM:(hd + 1) * HEAD_DIM, :] = (rot * Q_SCALE_LOG2).astype(BF16)

    lane_head = lax.broadcasted_iota(jnp.int32, (1, GQA_GROUP * Q_BLOCK), 1) // Q_BLOCK
    zeros_half = jnp.zeros((HEAD_DIM, GQA_GROUP * Q_BLOCK), BF16)
    ones_rows = jnp.ones((BF16_SUBLANES, n_keys), BF16)

    def scores(qb, kh):
        qcols = slice(qb * Q_BLOCK, (qb + 1) * Q_BLOCK)
        keys = slice(qb * Q_BLOCK, qb * Q_BLOCK + n_keys)
        q4 = jnp.concatenate(
            [qt_scr[(kh * GQA_GROUP + r) * HEAD_DIM:(kh * GQA_GROUP + r + 1) * HEAD_DIM, qcols]
             for r in range(GQA_GROUP)], axis=1)
        q4 = jnp.concatenate([q4, zeros_half] if kh % 2 == 0 else [zeros_half, q4], axis=0)
        kblk = kext_scr[keys, (kh // 2) * LANES:(kh // 2 + 1) * LANES]
        return _dot(kblk, q4)

    def finish(qb, kh, s, s_ahead):
        qcols = slice(qb * Q_BLOCK, (qb + 1) * Q_BLOCK)
        keys = slice(qb * Q_BLOCK, qb * Q_BLOCK + n_keys)
        if first_possible and qb == 0:
            bias = bias_scr[jnp.where(t > 0, 1, 0)]
        else:
            bias = bias_scr[1]
        s = s + jnp.concatenate([bias] * GQA_GROUP, axis=1)
        sink = jnp.zeros((1, GQA_GROUP * Q_BLOCK), F32)
        for r in range(GQA_GROUP):
            sink = jnp.where(lane_head == r, sinks_ref[kh * GQA_GROUP + r] * LOG2_E, sink)
        m = jnp.maximum(jnp.max(s, axis=0, keepdims=True), sink)
        p = jnp.exp2(s - m)
        if s_ahead is not None:
            p = jnp.concatenate(
                [p[:n_keys - 8, :], p[n_keys - 8:, :] + _zero_of(s_ahead[0:8, :])], axis=0)
        vt_ones = jnp.concatenate(
            [vtext_scr[kh * HEAD_DIM:(kh + 1) * HEAD_DIM, keys], ones_rows], axis=0)
        ot = _dot(vt_ones, p.astype(BF16))
        denom = ot[HEAD_DIM:HEAD_DIM + 1, :] + jnp.exp2(sink - m)
        ot = ot[0:HEAD_DIM, :] * (1.0 / denom)
        for r in range(GQA_GROUP):
            hd = kh * GQA_GROUP + r
            ogt_scr[hd * HEAD_DIM:(hd + 1) * HEAD_DIM, qcols] = ot[:, r * Q_BLOCK:(r + 1) * Q_BLOCK]

    blocks = [(qb, kh) for qb in range(tile // Q_BLOCK) for kh in range(N_KV_HEADS)]
    pending = [scores(*blk) for blk in blocks[:ATTN_AHEAD]]
    for n, blk in enumerate(blocks):
        if n + ATTN_AHEAD < len(blocks):
            pending.append(scores(*blocks[n + ATTN_AHEAD]))
        s_cur = pending.pop(0)
        finish(*blk, s_cur, pending[-1] if pending else None)

    kext_scr[0:WINDOW, :] = kext_scr[tile:tile + WINDOW, :]
    vtext_scr[:, 0:WINDOW] = vtext_scr[:, tile:tile + WINDOW]

    ogt = jnp.concatenate(
        [(ogt_scr[rc * OUT_ROWS:(rc + 1) * OUT_ROWS, :]
          * _silu(_dot_nt(wgt_ref[rc * OUT_ROWS:(rc + 1) * OUT_ROWS, :], xb))).astype(BF16)
         for rc in range(D_MODEL // OUT_ROWS)], axis=0)
    h2 = jnp.concatenate(
        [h[:, rc * OUT_ROWS:(rc + 1) * OUT_ROWS]
         + _dot(woutt_ref[rc * OUT_ROWS:(rc + 1) * OUT_ROWS, :], ogt).T
         for rc in range(D_MODEL // OUT_ROWS)], axis=1)
    y_ref[...] = _rms(h2, fg_ref[...])


def _layer_b_prompt(h, sinks, kv_norm, norm_b, final_norm, w_kv_t, w_qg_t, w_out_t, *, batch, seq):
    tile = B_TILE
    n_t = seq // tile
    n_sub = B_SUBTILES
    step = n_sub * tile
    n_steps = n_t // n_sub
    pos = jnp.arange(seq, dtype=F32)
    inv = ROPE_THETA ** (-jnp.arange(0, ROT_DIM, 2, dtype=F32) / ROT_DIM)
    ang_t = inv[:, None] * pos[None, :]
    cos_t, sin_t = jnp.cos(ang_t), jnp.sin(ang_t)
    tok_spec = pl.BlockSpec((step, D_MODEL), lambda b, t, *_: (b * n_steps + t, 0))
    rot_t_spec = pl.BlockSpec((ROT_DIM // 2, step), lambda b, t, *_: (0, t))
    last_spec = pl.BlockSpec((1, KV_DIM, WINDOW), lambda b, t, *_: (b, 0, 0))

    def const(shape):
        return pl.BlockSpec(shape, lambda *_: (0,) * len(shape), pipeline_mode=pl.Buffered(1))

    def half(i):
        return pl.BlockSpec((D_MODEL, D_MODEL), lambda *_: (i, 0), pipeline_mode=pl.Buffered(1))

    return pl.pallas_call(
        functools.partial(_layer_b_prompt_kernel, tile=tile, n_t=n_t, n_sub=n_sub),
        grid_spec=pltpu.PrefetchScalarGridSpec(
            num_scalar_prefetch=1,
            grid=(batch, n_steps),
            in_specs=[tok_spec, const(kv_norm.shape), const(norm_b.shape), const(final_norm.shape),
                      const(w_kv_t.shape), half(0), half(1), const(w_out_t.shape),
                      rot_t_spec, rot_t_spec],
            out_specs=[tok_spec, last_spec, last_spec],
            scratch_shapes=[pltpu.VMEM((WINDOW + tile, KV_DIM), BF16),
                            pltpu.VMEM((KV_DIM, WINDOW + tile), BF16),
                            pltpu.VMEM((D_MODEL, tile), BF16),
                            pltpu.VMEM((D_MODEL, tile), F32),
                            pltpu.VMEM((2, WINDOW + Q_BLOCK, Q_BLOCK), F32)]),
        out_shape=[jax.ShapeDtypeStruct((batch * seq, D_MODEL), F32),
                   jax.ShapeDtypeStruct((batch, KV_DIM, WINDOW), F32),
                   jax.ShapeDtypeStruct((batch, KV_DIM, WINDOW), F32)],
        compiler_params=pltpu.CompilerParams(
            dimension_semantics=("arbitrary", "arbitrary"), vmem_limit_bytes=VMEM_LIMIT_BYTES),
        name="layer_b_prompt",
    )(sinks, h, kv_norm, norm_b, final_norm, w_kv_t, w_qg_t, w_qg_t, w_out_t, cos_t, sin_t)


def _layer_b_sample_kernel(sinks_ref, h_ref, nbg_ref, fg_ref, wqgt_ref, woutt_ref,
                           cos_ref, slo_ref, shi_ref, knew_ref, vnew_ref, ck_ref, cv_ref,
                           y_ref, q_scr, gate_scr, o_scr, *, n_seq, b_tile):
    step = pl.program_id(0)

    @pl.when(step == 0)
    def _():
        h = h_ref[...]
        hn = h * lax.rsqrt(jnp.mean(h * h, axis=-1, keepdims=True) + EPS)
        qg = _dot_nt((hn * nbg_ref[...]).astype(BF16), wqgt_ref[...])
        cos, slo, shi = cos_ref[...], slo_ref[...], shi_ref[...]
        for c in range(D_MODEL // LANES):
            cols = slice(c * LANES, (c + 1) * LANES)
            q2 = _rotate(qg[:, cols], cos, slo, shi) * HEAD_DIM ** -0.5
            g2 = qg[:, D_MODEL + c * LANES:D_MODEL + (c + 1) * LANES]
            for i in range(LANES // HEAD_DIM):
                dst = _member_major(c * (LANES // HEAD_DIM) + i)
                q_scr[:, dst] = q2[:, i * HEAD_DIM:(i + 1) * HEAD_DIM]
                gate_scr[:, dst] = g2[:, i * HEAD_DIM:(i + 1) * HEAD_DIM]

    n_rows = GQA_GROUP * N_KV_HEADS * SAMPLE_GROUP
    row = lax.broadcasted_iota(jnp.int32, (n_rows, 1), 0)
    row_kh = (row // SAMPLE_GROUP) % N_KV_HEADS
    row_seq = row % SAMPLE_GROUP
    lane_kh = lax.broadcasted_iota(jnp.int32, (1, KV_DIM), 1) // HEAD_DIM
    own = row_kh == lane_kh
    sink = jnp.zeros((n_rows, 1), F32)
    for r in range(GQA_GROUP):
        for kh in range(N_KV_HEADS):
            sink = jnp.where(row // SAMPLE_GROUP == r * N_KV_HEADS + kh,
                             sinks_ref[kh * GQA_GROUP + r], sink)
    n_blk = GQA_GROUP * N_KV_HEADS

    def group(i, carry):
        b0 = i * SAMPLE_GROUP
        g0 = pl.multiple_of(step * b_tile + b0, SAMPLE_GROUP)
        seqs = pl.ds(g0, SAMPLE_GROUP)
        q8 = q_scr[seqs, :]
        qexp = jnp.concatenate(
            [q8[:, r * KV_DIM:(r + 1) * KV_DIM] for r in range(GQA_GROUP)
             for _ in range(N_KV_HEADS)], axis=0)
        qexp = jnp.where(own, qexp, 0.0).astype(BF16)
        knew8 = knew_ref[seqs, :].astype(BF16).astype(F32)
        vnew8 = vnew_ref[seqs, :].astype(BF16).astype(F32)
        s_new = jnp.sum(qexp.astype(F32) * jnp.concatenate([knew8] * n_blk, axis=0),
                        axis=1, keepdims=True)
        s_old = jnp.zeros((n_rows, WINDOW), F32)
        for b in range(SAMPLE_GROUP):
            s_b = _dot(qexp, ck_ref[b0 + b].astype(BF16))
            s_old = jnp.where(row_seq == b, s_b, s_old)
        m = jnp.maximum(jnp.maximum(jnp.max(s_old, axis=1, keepdims=True), s_new), sink)
        p_old = jnp.exp(s_old - m)
        p_new = jnp.exp(s_new - m)
        denom = jnp.sum(p_old, axis=1, keepdims=True) + p_new + jnp.exp(sink - m)
        p_old = p_old.astype(BF16)
        o = jnp.zeros((n_rows, KV_DIM), F32)
        for b in range(SAMPLE_GROUP):
            o_b = _dot_nt(p_old, cv_ref[b0 + b].astype(BF16))
            o = jnp.where(row_seq == b, o_b, o)
        o = (o + p_new.astype(BF16).astype(F32) * jnp.concatenate([vnew8] * n_blk, axis=0)) / denom
        o = jnp.where(own, o, 0.0)
        for r in range(GQA_GROUP):
            blks = [o[(r * N_KV_HEADS + kh) * SAMPLE_GROUP:(r * N_KV_HEADS + kh + 1) * SAMPLE_GROUP]
                    for kh in range(N_KV_HEADS)]
            o_scr[seqs, r * KV_DIM:(r + 1) * KV_DIM] = (blks[0] + blks[1]) + (blks[2] + blks[3])
        return carry

    lax.fori_loop(0, b_tile // SAMPLE_GROUP, group, 0, unroll=True)

    @pl.when(step == pl.num_programs(0) - 1)
    def _():
        og_mm = o_scr[...] * _silu(gate_scr[...])
        og = jnp.concatenate(
            [og_mm[:, _member_major(hd)] for hd in range(N_HEADS)], axis=1).astype(BF16)
        h2 = h_ref[...] + _dot_nt(og, woutt_ref[...])
        y_ref[:, 0, :] = _rms(h2, fg_ref[...])


def _layer_b_sample(h, sinks, norm_b, final_norm, w_qg_t, w_out_t, knew, vnew, cache_k, cache_v):
    n_seq = h.shape[0]
    b_tile = SAMPLE_B_TILE
    cos, slo, shi = _rotary_tables(jnp.full((1,), PAST_LEN, F32))

    def const(shape):
        return pl.BlockSpec(shape, lambda *_: (0,) * len(shape))

    assert cache_k.shape == (n_seq, KV_DIM, WINDOW)
    cache_spec = pl.BlockSpec((b_tile, KV_DIM, WINDOW), lambda i, *_: (i, 0, 0))
    consts = (h, norm_b, final_norm, w_qg_t, w_out_t, cos, slo, shi, knew, vnew)
    return pl.pallas_call(
        functools.partial(_layer_b_sample_kernel, n_seq=n_seq, b_tile=b_tile),
        grid_spec=pltpu.PrefetchScalarGridSpec(
            num_scalar_prefetch=1,
            grid=(n_seq // b_tile,),
            in_specs=[const(c.shape) for c in consts] + [cache_spec, cache_spec],
            out_specs=const((n_seq, 1, D_MODEL)),
            scratch_shapes=[pltpu.VMEM((n_seq, D_MODEL), F32),
                            pltpu.VMEM((n_seq, D_MODEL), F32),
                            pltpu.VMEM((n_seq, D_MODEL), F32)]),
        out_shape=jax.ShapeDtypeStruct((n_seq, 1, D_MODEL), F32),
        compiler_params=pltpu.CompilerParams(
            dimension_semantics=("arbitrary",), vmem_limit_bytes=VMEM_LIMIT_BYTES),
        name="layer_b_sample",
    )(sinks, *consts, cache_k, cache_v)


def kernel(x_prompt, x_sample, cache_k, cache_v, norm_a, w_in_a, v_norm_a, w_s_a, b_s_a, w_out_a,
           kv_norm, w_kv, norm_b, w_in_b, sinks_b, w_out_b, final_norm):
    batch, seq, _ = x_prompt.shape
    n_seq, dec_seq, _ = x_sample.shape
    assert dec_seq == 1 and seq % CHUNK == 0 and cache_k.shape[1] == WINDOW
    assert norm_a.shape[0] == 1 and norm_b.shape[0] == 1

    row = lambda g: g.reshape(1, -1)

    def to_window(x_t):
        n = x_t.shape[0]
        return x_t.reshape(n, N_KV_HEADS, HEAD_DIM, WINDOW).transpose(0, 3, 1, 2)

    def from_window(x):
        return x.transpose(0, 2, 3, 1).reshape(x.shape[0], KV_DIM, WINDOW)

    cache_kt, cache_vt = from_window(cache_k), from_window(cache_v)

    w_in_a16, w_out_a16, w_kv_t, h_s, av_s, knew, vnew, knew_t, vnew_t = _layer_a_sample(
        x_sample, row(norm_a[0]), w_in_a[0], row(v_norm_a[0]), w_s_a[0],
        b_s_a[0], w_out_a[0], row(kv_norm), w_kv)
    h_p, kt_s, vt_s, w_qg_t, w_out_b_t = _layer_a_prompt(
        x_prompt.reshape(batch * seq, D_MODEL), row(norm_a[0]), w_in_a16, row(v_norm_a[0]),
        w_s_a[0], b_s_a[0], w_out_a16, cache_kt, cache_vt, knew_t, vnew_t, w_in_b[0], w_out_b[0])

    y_p, kt_p, vt_p = _layer_b_prompt(
        h_p, sinks_b[0], row(kv_norm), row(norm_b[0]), row(final_norm),
        w_kv_t, w_qg_t, w_out_b_t, batch=batch, seq=seq)
    y_s = _layer_b_sample(h_s, sinks_b[0], row(norm_b[0]), row(final_norm), w_qg_t, w_out_b_t,
                          knew, vnew, cache_kt, cache_vt)

    return (y_p.reshape(batch, seq, D_MODEL),
            y_s,
            to_window(kt_p),
            to_window(vt_p),
            to_window(kt_s),
            to_window(vt_s),
            av_s.reshape(1, n_seq, 1, A_WIDTH))
```

```python
import functools

import jax
import jax.numpy as jnp
from jax import lax
from jax.experimental import pallas as pl
from jax.experimental.pallas import tpu as pltpu

D_MODEL = 1024
PAST_LEN = 8192
CHUNK = 128
A_WIDTH = 2 * D_MODEL
A_GROUPS = 8
A_GROUP_DIM = A_WIDTH // A_GROUPS
HEAD_DIM = 64
N_HEADS = D_MODEL // HEAD_DIM
N_KV_HEADS = 4
GQA_GROUP = N_HEADS // N_KV_HEADS
KV_DIM = N_KV_HEADS * HEAD_DIM
WINDOW = 128
Q_BLOCK = 128
ROT_DIM = HEAD_DIM // 4
ROPE_THETA = 500000.0
EPS = 1e-5

LANES = 128
BF16_SUBLANES = 16
LOG2_E = 1.4426950408889634
Q_SCALE_LOG2 = HEAD_DIM ** -0.5 * LOG2_E
VMEM_LIMIT_BYTES = 56 * 1024 * 1024

A_TILE = 512
B_TILE = 512
B_SUBTILES = 2
A_SAMPLE_COLS = 1024
A_SAMPLE_ROWS = 1024
OUT_ROWS = 256
ATTN_AHEAD = 2
SAMPLE_B_TILE = 16
SAMPLE_GROUP = 8

F32 = jnp.float32
BF16 = jnp.bfloat16


def _rms(x, g):
    return x * lax.rsqrt(jnp.mean(x * x, axis=-1, keepdims=True) + EPS) * g


def _silu(x):
    return x * jax.nn.sigmoid(x)


def _dot(a, b):
    return jnp.dot(a, b, preferred_element_type=F32)


def _dot_nt(a, b):
    return lax.dot_general(a, b, (((1,), (1,)), ((), ())), preferred_element_type=F32)


def _zero_of(x):
    bits = pltpu.bitcast(x, jnp.uint32)
    return ((bits >> 16) >> 16).astype(F32)


def _member_major(head):
    kh, r = divmod(head, GQA_GROUP)
    start = (r * N_KV_HEADS + kh) * HEAD_DIM
    return slice(start, start + HEAD_DIM)


def _rotate(x, cos, sin_lo, sin_hi):
    return (x * cos + pltpu.roll(x, LANES - ROT_DIM // 2, 1) * sin_lo
            + pltpu.roll(x, ROT_DIM // 2, 1) * sin_hi)


def _rotary_table(seq):
    pos = jnp.concatenate([jnp.arange(seq, dtype=F32), jnp.full((LANES,), PAST_LEN, F32)])
    inv = ROPE_THETA ** (-jnp.arange(0, ROT_DIM, 2, dtype=F32) / ROT_DIM)
    ang = inv[:, None] * pos[None, :]
    return jnp.cos(ang), jnp.sin(ang)


def _sample_rotary_rows(cost_ref, sint_ref):
    half = ROT_DIM // 2
    lane = lax.broadcasted_iota(jnp.int32, (1, LANES), 1) % HEAD_DIM
    cos = jnp.ones((1, LANES), F32)
    sin = jnp.zeros((1, LANES), F32)
    for i in range(half):
        pick = (lane == i) | (lane == i + half)
        cos = jnp.where(pick, cost_ref[i:i + 1, 0:1], cos)
        sin = jnp.where(pick, sint_ref[i:i + 1, 0:1], sin)
    return (cos, jnp.where(lane < half, -sin, 0.0),
            jnp.where((lane >= half) & (lane < ROT_DIM), sin, 0.0))


def _sample_rot_specs(rot):
    last = rot[0].shape[1] // LANES - 1
    return [pl.BlockSpec((ROT_DIM // 2, LANES), lambda *_: (0, last)) for _ in rot]


def _layer_a_tile(x, ng_ref, win_ref, vg_ref, ws_ref, bs_ref, wout_ref, y_scr, *, tile):
    xn = _rms(x, ng_ref[...]).astype(BF16)
    vb = _rms(_dot(xn, win_ref[:, A_WIDTH:2 * A_WIDTH]), vg_ref[...]).astype(BF16)
    row = lax.broadcasted_iota(jnp.int32, (CHUNK, CHUNK), 0)
    col = lax.broadcasted_iota(jnp.int32, (CHUNK, CHUNK), 1)
    tri = row >= col
    ws = [jnp.where(tri, ws_ref[g], 0.0).astype(BF16) for g in range(A_GROUPS)]
    z = jnp.concatenate(
        [jnp.concatenate(
            [_dot(ws[g], vb[c * CHUNK:(c + 1) * CHUNK, g * A_GROUP_DIM:(g + 1) * A_GROUP_DIM])
             for g in range(A_GROUPS)], axis=1) + bs_ref[...]
         for c in range(tile // CHUNK)], axis=0)
    uz = _dot(xn, win_ref[:, 0:A_WIDTH]) * z
    gate = _dot(xn, win_ref[:, 2 * A_WIDTH:3 * A_WIDTH])
    y_scr[...] = (uz * _silu(gate)).astype(BF16)
    return x + _dot(y_scr[...], wout_ref[...])


def _layer_a_prompt_kernel(x_ref, ng_ref, win_ref, vg_ref, ws_ref, b_ref, wout_ref,
                           ck_ref, cv_ref, knewt_ref, vnewt_ref, winb_ref, woutb_ref,
                           h_ref, kout_ref, vout_ref, wqgt_ref, woutbt_ref, y_scr, bs_scr,
                           *, tile, n_roll):
    @pl.when(pl.program_id(0) == 0)
    def _():
        b = jnp.concatenate(
            [b_ref[...], jnp.zeros((CHUNK - A_GROUPS, CHUNK), F32)], axis=0).T
        for g in range(A_GROUPS):
            bs_scr[:, g * A_GROUP_DIM:(g + 1) * A_GROUP_DIM] = jnp.broadcast_to(
                b[:, g:g + 1], (CHUNK, A_GROUP_DIM))

    wqgt_ref[...] = winb_ref[...].T.astype(BF16)
    woutbt_ref[...] = woutb_ref[...].T.astype(BF16)
    is_last = lax.broadcasted_iota(jnp.int32, (KV_DIM, WINDOW), 1) == WINDOW - 1
    for b in range(n_roll):
        g = pl.program_id(0) * n_roll + b
        blk = pl.ds(pl.multiple_of((g // LANES) * LANES, LANES), LANES)
        to_last = LANES - 1 - g % LANES
        kout_ref[b] = jnp.where(is_last, pltpu.roll(knewt_ref[:, blk], to_last, 1),
                                pltpu.roll(ck_ref[b], WINDOW - 1, 1))
        vout_ref[b] = jnp.where(is_last, pltpu.roll(vnewt_ref[:, blk], to_last, 1),
                                pltpu.roll(cv_ref[b], WINDOW - 1, 1))
    h_ref[...] = _layer_a_tile(x_ref[...], ng_ref, win_ref, vg_ref, ws_ref, bs_scr, wout_ref,
                               y_scr, tile=tile)


def _layer_a_sample_kernel(x_ref, ng_ref, win_ref, vg_ref, ws_ref, bs_ref, wout_ref,
                           kvg_ref, wkv_ref, cost_ref, sint_ref,
                           win16_ref, wout16_ref, wkvt_ref, h_ref, av_ref, knew_ref, vnew_ref,
                           knewt_ref, vnewt_ref, xn_scr, proj_scr, y_scr, acc_scr, *, n_in, n_out):
    step = pl.program_id(0)
    blocks_per_branch = n_in // 3

    @pl.when(step == 0)
    def _():
        xn_scr[...] = _rms(x_ref[:, 0, :], ng_ref[...]).astype(BF16)
        wkvt_ref[...] = wkv_ref[...].T.astype(BF16)

    @pl.when(step < n_in)
    def _():
        w = win_ref[...].astype(BF16)
        win16_ref[...] = w
        proj_scr[step] = _dot(xn_scr[...], w)

    @pl.when(step == n_in)
    def _():
        def branch(i):
            return jnp.concatenate(
                [proj_scr[i * blocks_per_branch + j] for j in range(blocks_per_branch)], axis=1)

        v = _rms(branch(1), vg_ref[...])
        av_ref[:, 0, :] = v
        lane_group = lax.broadcasted_iota(jnp.int32, (1, A_WIDTH), 1) // A_GROUP_DIM
        ws_row = jnp.zeros((1, A_WIDTH), F32)
        bs_row = jnp.zeros((1, A_WIDTH), F32)
        for g in range(A_GROUPS):
            ws_row = jnp.where(lane_group == g, ws_ref[g, 0:1, 0:1], ws_row)
            bs_row = jnp.where(lane_group == g, bs_ref[g:g + 1, 0:1], bs_row)
        z = v * ws_row + bs_row
        y = (branch(0) * z * _silu(branch(2))).astype(BF16)
        rows = A_WIDTH // n_out
        for j in range(n_out):
            y_scr[j] = y[:, j * rows:(j + 1) * rows]

    @pl.when(step >= n_in)
    def _():
        w = wout_ref[...].astype(BF16)
        wout16_ref[...] = w
        part = _dot(y_scr[step - n_in], w)

        @pl.when(step == n_in)
        def _():
            acc_scr[...] = part

        @pl.when(step > n_in)
        def _():
            acc_scr[...] += part

    @pl.when(step == n_in + n_out - 1)
    def _():
        h = x_ref[:, 0, :] + acc_scr[...]
        h_ref[...] = h
        hn = h * lax.rsqrt(jnp.mean(h * h, axis=-1, keepdims=True) + EPS)
        kv = _dot_nt((hn * kvg_ref[...]).astype(BF16), wkvt_ref[...])
        cos, slo, shi = _sample_rotary_rows(cost_ref, sint_ref)
        k = jnp.concatenate(
            [_rotate(kv[:, c * LANES:(c + 1) * LANES], cos, slo, shi)
             for c in range(KV_DIM // LANES)], axis=1)
        knew_ref[...] = k
        vnew_ref[...] = kv[:, KV_DIM:]
        knewt_ref[...] = k.T
        vnewt_ref[...] = kv[:, KV_DIM:].T


def _const_spec(shape):
    return pl.BlockSpec(shape, lambda *_: (0,) * len(shape), pipeline_mode=pl.Buffered(1))


def _layer_a_prompt(x, norm_g, w_in, v_norm_g, ws, bs, w_out, cache_k, cache_v, knew_t, vnew_t,
                    w_in_b, w_out_b):
    n_tok = x.shape[0]
    tile = A_TILE
    n_steps = n_tok // tile
    n_seq = cache_k.shape[0]
    n_roll = n_seq // n_steps
    assert n_roll * n_steps == n_seq and n_seq % LANES == 0
    assert max(w_in_b.shape[1], w_out_b.shape[1]) <= n_steps * LANES
    tok_spec = pl.BlockSpec((tile, D_MODEL), lambda i: (i, 0))
    cache_spec = pl.BlockSpec((n_roll, KV_DIM, WINDOW), lambda i: (i, 0, 0))

    def column_block(w):
        last = w.shape[1] // LANES - 1
        return pl.BlockSpec((w.shape[0], LANES), lambda i: (0, jnp.minimum(i, last)))

    def row_block(w):
        last = w.shape[1] // LANES - 1
        return pl.BlockSpec((LANES, w.shape[0]), lambda i: (jnp.minimum(i, last), 0))

    consts = (norm_g, w_in, v_norm_g, ws, bs, w_out)
    return pl.pallas_call(
        functools.partial(_layer_a_prompt_kernel, tile=tile, n_roll=n_roll),
        grid=(n_steps,),
        in_specs=[tok_spec] + [_const_spec(c.shape) for c in consts]
        + [cache_spec, cache_spec, _const_spec(knew_t.shape), _const_spec(vnew_t.shape),
           column_block(w_in_b), column_block(w_out_b)],
        out_specs=[tok_spec, cache_spec, cache_spec, row_block(w_in_b), row_block(w_out_b)],
        out_shape=[jax.ShapeDtypeStruct((n_tok, D_MODEL), F32),
                   jax.ShapeDtypeStruct(cache_k.shape, F32),
                   jax.ShapeDtypeStruct(cache_v.shape, F32),
                   jax.ShapeDtypeStruct(w_in_b.shape[::-1], BF16),
                   jax.ShapeDtypeStruct(w_out_b.shape[::-1], BF16)],
        scratch_shapes=[pltpu.VMEM((tile, A_WIDTH), BF16), pltpu.VMEM((CHUNK, A_WIDTH), F32)],
        compiler_params=pltpu.CompilerParams(
            dimension_semantics=("arbitrary",), vmem_limit_bytes=VMEM_LIMIT_BYTES),
        name="layer_a_prompt",
    )(x, *consts, cache_k, cache_v, knew_t, vnew_t, w_in_b, w_out_b)


def _layer_a_sample(x, norm_g, w_in, v_norm_g, ws, bs, w_out, kv_norm, w_kv, rot):
    n_seq = x.shape[0]
    n_in = w_in.shape[1] // A_SAMPLE_COLS
    n_out = w_out.shape[0] // A_SAMPLE_ROWS
    assert n_in % 3 == 0
    whole = lambda shape: pl.BlockSpec(shape, lambda i: (0,) * len(shape))
    win_block = lambda i: (0, jnp.minimum(i, n_in - 1))
    wout_block = lambda i: (jnp.maximum(i - n_in, 0), 0)
    small_dims = [(n_seq, D_MODEL), (n_seq, 1, A_WIDTH), (n_seq, KV_DIM), (n_seq, KV_DIM),
                  (KV_DIM, n_seq), (KV_DIM, n_seq)]
    return pl.pallas_call(
        functools.partial(_layer_a_sample_kernel, n_in=n_in, n_out=n_out),
        grid=(n_in + n_out,),
        in_specs=[whole(x.shape), whole(norm_g.shape),
                  pl.BlockSpec((D_MODEL, A_SAMPLE_COLS), win_block),
                  whole(v_norm_g.shape),
                  pl.BlockSpec((A_GROUPS, 8, LANES), lambda i: (0, 0, 0)), whole(bs.shape),
                  pl.BlockSpec((A_SAMPLE_ROWS, D_MODEL), wout_block),
                  whole(kv_norm.shape), whole(w_kv.shape),
                  *_sample_rot_specs(rot)],
        out_specs=[pl.BlockSpec((D_MODEL, A_SAMPLE_COLS), win_block),
                   pl.BlockSpec((A_SAMPLE_ROWS, D_MODEL), wout_block), whole(w_kv.shape[::-1])]
        + [whole(d) for d in small_dims],
        out_shape=[jax.ShapeDtypeStruct(w_in.shape, BF16), jax.ShapeDtypeStruct(w_out.shape, BF16),
                   jax.ShapeDtypeStruct(w_kv.shape[::-1], BF16)]
        + [jax.ShapeDtypeStruct(d, F32) for d in small_dims],
        scratch_shapes=[pltpu.VMEM((n_seq, D_MODEL), BF16),
                        pltpu.VMEM((n_in, n_seq, A_SAMPLE_COLS), F32),
                        pltpu.VMEM((n_out, n_seq, A_SAMPLE_ROWS), BF16),
                        pltpu.VMEM((n_seq, D_MODEL), F32)],
        compiler_params=pltpu.CompilerParams(
            dimension_semantics=("arbitrary",), vmem_limit_bytes=VMEM_LIMIT_BYTES),
        name="layer_a_sample",
    )(x, norm_g, w_in, v_norm_g, ws, bs, w_out, kv_norm, w_kv, *rot)


def _layer_b_prompt_kernel(sinks_ref, h_ref, kvg_ref, nbg_ref, fg_ref, wkvt_ref, wqt_ref,
                           wgt_ref, woutt_ref, cost_ref, sint_ref, y_ref, kout_ref, vout_ref,
                           *scratch, tile, n_t, n_sub):
    for sub in range(n_sub):
        rows = pl.ds(sub * tile, tile)
        _layer_b_prompt_tile(
            sinks_ref, h_ref.at[rows, :], kvg_ref, nbg_ref, fg_ref, wkvt_ref, wqt_ref, wgt_ref,
            woutt_ref, cost_ref.at[:, rows], sint_ref.at[:, rows], y_ref.at[rows, :], kout_ref,
            vout_ref, *scratch, tile=tile, n_t=n_t, t=pl.program_id(1) * n_sub + sub,
            first_possible=sub == 0, last_possible=sub == n_sub - 1)


def _layer_b_prompt_tile(sinks_ref, h_ref, kvg_ref, nbg_ref, fg_ref, wkvt_ref, wqt_ref,
                         wgt_ref, woutt_ref, cost_ref, sint_ref,
                         y_ref, kout_ref, vout_ref,
                         kext_scr, vtext_scr, qt_scr, ogt_scr, bias_scr,
                         *, tile, n_t, t, first_possible, last_possible):
    n_keys = WINDOW + Q_BLOCK

    if first_possible:
        @pl.when((pl.program_id(0) == 0) & (t == 0))
        def _():
            j = lax.broadcasted_iota(jnp.int32, (n_keys, Q_BLOCK), 0)
            i = lax.broadcasted_iota(jnp.int32, (n_keys, Q_BLOCK), 1)
            band = (j >= i) & (j <= WINDOW + i)
            bias_scr[0] = jnp.where(band & (j >= WINDOW), 0.0, -jnp.inf)
            bias_scr[1] = jnp.where(band, 0.0, -jnp.inf)

        @pl.when(t == 0)
        def _():
            kext_scr[0:WINDOW, :] = jnp.zeros((WINDOW, KV_DIM), BF16)
            vtext_scr[:, 0:WINDOW] = jnp.zeros((KV_DIM, WINDOW), BF16)

    h = h_ref[...]
    hn = h * lax.rsqrt(jnp.mean(h * h, axis=-1, keepdims=True) + EPS)
    xkv = (hn * kvg_ref[...]).astype(BF16)
    xb = (hn * nbg_ref[...]).astype(BF16)

    cost, sint = cost_ref[...], sint_ref[...]
    half = ROT_DIM // 2

    def rotate_head(rows):
        lo, hi = rows[0:half, :], rows[half:ROT_DIM, :]
        return jnp.concatenate(
            [lo * cost - hi * sint, hi * cost + lo * sint, rows[ROT_DIM:, :]], axis=0)

    kvt = _dot_nt(wkvt_ref[...], xkv)
    kt = jnp.concatenate(
        [rotate_head(kvt[kh * HEAD_DIM:(kh + 1) * HEAD_DIM, :]) for kh in range(N_KV_HEADS)],
        axis=0)
    vt = kvt[KV_DIM:, :]
    kext_scr[WINDOW:, :] = kt.T.astype(BF16)
    vtext_scr[:, WINDOW:] = vt.astype(BF16)

    if last_possible:
        @pl.when(t == n_t - 1)
        def _():
            kout_ref[0] = kt[:, tile - WINDOW:]
            vout_ref[0] = vt[:, tile - WINDOW:]

    qt = _dot_nt(wqt_ref[...], xb)
    for hd in range(N_HEADS):
        rot = rotate_head(qt[hd * HEAD_DIM:(hd + 1) * HEAD_DIM, :])
        qt_scr[hd * HEAD_DIM:(hd + 1) * HEAD_DIM, :] = (rot * Q_SCALE_LOG2).astype(BF16)

    lane_head = lax.broadcasted_iota(jnp.int32, (1, GQA_GROUP * Q_BLOCK), 1) // Q_BLOCK
    zeros_half = jnp.zeros((HEAD_DIM, GQA_GROUP * Q_BLOCK), BF16)
    ones_rows = jnp.ones((BF16_SUBLANES, n_keys), BF16)

    def scores(qb, kh):
        qcols = slice(qb * Q_BLOCK, (qb + 1) * Q_BLOCK)
        keys = slice(qb * Q_BLOCK, qb * Q_BLOCK + n_keys)
        q4 = jnp.concatenate(
            [qt_scr[(kh * GQA_GROUP + r) * HEAD_DIM:(kh * GQA_GROUP + r + 1) * HEAD_DIM, qcols]
             for r in range(GQA_GROUP)], axis=1)
        q4 = jnp.concatenate([q4, zeros_half] if kh % 2 == 0 else [zeros_half, q4], axis=0)
        kblk = kext_scr[keys, (kh // 2) * LANES:(kh // 2 + 1) * LANES]
        return _dot(kblk, q4)

    def finish(qb, kh, s, s_ahead):
        qcols = slice(qb * Q_BLOCK, (qb + 1) * Q_BLOCK)
        keys = slice(qb * Q_BLOCK, qb * Q_BLOCK + n_keys)
        if first_possible and qb == 0:
            bias = bias_scr[jnp.where(t > 0, 1, 0)]
        else:
            bias = bias_scr[1]
        s = s + jnp.concatenate([bias] * GQA_GROUP, axis=1)
        sink = jnp.zeros((1, GQA_GROUP * Q_BLOCK), F32)
        for r in range(GQA_GROUP):
            sink = jnp.where(lane_head == r, sinks_ref[kh * GQA_GROUP + r] * LOG2_E, sink)
        m = jnp.maximum(jnp.max(s, axis=0, keepdims=True), sink)
        p = jnp.exp2(s - m)
        if s_ahead is not None:
            p = jnp.concatenate(
                [p[:n_keys - 8, :], p[n_keys - 8:, :] + _zero_of(s_ahead[0:8, :])], axis=0)
        vt_ones = jnp.concatenate(
            [vtext_scr[kh * HEAD_DIM:(kh + 1) * HEAD_DIM, keys], ones_rows], axis=0)
        ot = _dot(vt_ones, p.astype(BF16))
        denom = ot[HEAD_DIM:HEAD_DIM + 1, :] + jnp.exp2(sink - m)
        ot = ot[0:HEAD_DIM, :] * (1.0 / denom)
        for r in range(GQA_GROUP):
            hd = kh * GQA_GROUP + r
            ogt_scr[hd * HEAD_DIM:(hd + 1) * HEAD_DIM, qcols] = ot[:, r * Q_BLOCK:(r + 1) * Q_BLOCK]

    blocks = [(qb, kh) for qb in range(tile // Q_BLOCK) for kh in range(N_KV_HEADS)]
    pending = [scores(*blk) for blk in blocks[:ATTN_AHEAD]]
    for n, blk in enumerate(blocks):
        if n + ATTN_AHEAD < len(blocks):
            pending.append(scores(*blocks[n + ATTN_AHEAD]))
        s_cur = pending.pop(0)
        finish(*blk, s_cur, pending[-1] if pending else None)

    kext_scr[0:WINDOW, :] = kext_scr[tile:tile + WINDOW, :]
    vtext_scr[:, 0:WINDOW] = vtext_scr[:, tile:tile + WINDOW]

    ogt = jnp.concatenate(
        [(ogt_scr[rc * OUT_ROWS:(rc + 1) * OUT_ROWS, :]
          * _silu(_dot_nt(wgt_ref[rc * OUT_ROWS:(rc + 1) * OUT_ROWS, :], xb))).astype(BF16)
         for rc in range(D_MODEL // OUT_ROWS)], axis=0)
    h2 = jnp.concatenate(
        [h[:, rc * OUT_ROWS:(rc + 1) * OUT_ROWS]
         + _dot(woutt_ref[rc * OUT_ROWS:(rc + 1) * OUT_ROWS, :], ogt).T
         for rc in range(D_MODEL // OUT_ROWS)], axis=1)
    y_ref[...] = _rms(h2, fg_ref[...])


def _layer_b_prompt(h, sinks, kv_norm, norm_b, final_norm, w_kv_t, w_qg_t, w_out_t, rot,
                    *, batch, seq):
    tile = B_TILE
    n_t = seq // tile
    n_sub = B_SUBTILES
    step = n_sub * tile
    n_steps = n_t // n_sub
    tok_spec = pl.BlockSpec((step, D_MODEL), lambda b, t, *_: (b * n_steps + t, 0))
    rot_t_spec = pl.BlockSpec((ROT_DIM // 2, step), lambda b, t, *_: (0, t))
    last_spec = pl.BlockSpec((1, KV_DIM, WINDOW), lambda b, t, *_: (b, 0, 0))

    def const(shape):
        return pl.BlockSpec(shape, lambda *_: (0,) * len(shape), pipeline_mode=pl.Buffered(1))

    def half(i):
        return pl.BlockSpec((D_MODEL, D_MODEL), lambda *_: (i, 0), pipeline_mode=pl.Buffered(1))

    return pl.pallas_call(
        functools.partial(_layer_b_prompt_kernel, tile=tile, n_t=n_t, n_sub=n_sub),
        grid_spec=pltpu.PrefetchScalarGridSpec(
            num_scalar_prefetch=1,
            grid=(batch, n_steps),
            in_specs=[tok_spec, const(kv_norm.shape), const(norm_b.shape), const(final_norm.shape),
                      const(w_kv_t.shape), half(0), half(1), const(w_out_t.shape),
                      rot_t_spec, rot_t_spec],
            out_specs=[tok_spec, last_spec, last_spec],
            scratch_shapes=[pltpu.VMEM((WINDOW + tile, KV_DIM), BF16),
                            pltpu.VMEM((KV_DIM, WINDOW + tile), BF16),
                            pltpu.VMEM((D_MODEL, tile), BF16),
                            pltpu.VMEM((D_MODEL, tile), F32),
                            pltpu.VMEM((2, WINDOW + Q_BLOCK, Q_BLOCK), F32)]),
        out_shape=[jax.ShapeDtypeStruct((batch * seq, D_MODEL), F32),
                   jax.ShapeDtypeStruct((batch, KV_DIM, WINDOW), F32),
                   jax.ShapeDtypeStruct((batch, KV_DIM, WINDOW), F32)],
        compiler_params=pltpu.CompilerParams(
            dimension_semantics=("arbitrary", "arbitrary"), vmem_limit_bytes=VMEM_LIMIT_BYTES),
        name="layer_b_prompt",
    )(sinks, h, kv_norm, norm_b, final_norm, w_kv_t, w_qg_t, w_qg_t, w_out_t, *rot)


def _layer_b_sample_kernel(sinks_ref, h_ref, nbg_ref, fg_ref, wqgt_ref, woutt_ref,
                           cost_ref, sint_ref, knew_ref, vnew_ref, ck_ref, cv_ref,
                           y_ref, q_scr, gate_scr, o_scr, *, n_seq, b_tile):
    step = pl.program_id(0)

    @pl.when(step == 0)
    def _():
        h = h_ref[...]
        hn = h * lax.rsqrt(jnp.mean(h * h, axis=-1, keepdims=True) + EPS)
        qg = _dot_nt((hn * nbg_ref[...]).astype(BF16), wqgt_ref[...])
        cos, slo, shi = _sample_rotary_rows(cost_ref, sint_ref)
        for c in range(D_MODEL // LANES):
            cols = slice(c * LANES, (c + 1) * LANES)
            q2 = _rotate(qg[:, cols], cos, slo, shi) * HEAD_DIM ** -0.5
            g2 = qg[:, D_MODEL + c * LANES:D_MODEL + (c + 1) * LANES]
            for i in range(LANES // HEAD_DIM):
                dst = _member_major(c * (LANES // HEAD_DIM) + i)
                q_scr[:, dst] = q2[:, i * HEAD_DIM:(i + 1) * HEAD_DIM]
                gate_scr[:, dst] = g2[:, i * HEAD_DIM:(i + 1) * HEAD_DIM]

    n_rows = GQA_GROUP * N_KV_HEADS * SAMPLE_GROUP
    row = lax.broadcasted_iota(jnp.int32, (n_rows, 1), 0)
    row_kh = (row // SAMPLE_GROUP) % N_KV_HEADS
    row_seq = row % SAMPLE_GROUP
    lane_kh = lax.broadcasted_iota(jnp.int32, (1, KV_DIM), 1) // HEAD_DIM
    own = row_kh == lane_kh
    sink = jnp.zeros((n_rows, 1), F32)
    for r in range(GQA_GROUP):
        for kh in range(N_KV_HEADS):
            sink = jnp.where(row // SAMPLE_GROUP == r * N_KV_HEADS + kh,
                             sinks_ref[kh * GQA_GROUP + r], sink)
    n_blk = GQA_GROUP * N_KV_HEADS

    def group(i, carry):
        b0 = i * SAMPLE_GROUP
        g0 = pl.multiple_of(step * b_tile + b0, SAMPLE_GROUP)
        seqs = pl.ds(g0, SAMPLE_GROUP)
        q8 = q_scr[seqs, :]
        qexp = jnp.concatenate(
            [q8[:, r * KV_DIM:(r + 1) * KV_DIM] for r in range(GQA_GROUP)
             for _ in range(N_KV_HEADS)], axis=0)
        qexp = jnp.where(own, qexp, 0.0).astype(BF16)
        knew8 = knew_ref[seqs, :].astype(BF16).astype(F32)
        vnew8 = vnew_ref[seqs, :].astype(BF16).astype(F32)
        s_new = jnp.sum(qexp.astype(F32) * jnp.concatenate([knew8] * n_blk, axis=0),
                        axis=1, keepdims=True)
        s_old = jnp.zeros((n_rows, WINDOW), F32)
        for b in range(SAMPLE_GROUP):
            s_b = _dot(qexp, ck_ref[b0 + b].astype(BF16))
            s_old = jnp.where(row_seq == b, s_b, s_old)
        m = jnp.maximum(jnp.maximum(jnp.max(s_old, axis=1, keepdims=True), s_new), sink)
        p_old = jnp.exp(s_old - m)
        p_new = jnp.exp(s_new - m)
        denom = jnp.sum(p_old, axis=1, keepdims=True) + p_new + jnp.exp(sink - m)
        p_old = p_old.astype(BF16)
        o = jnp.zeros((n_rows, KV_DIM), F32)
        for b in range(SAMPLE_GROUP):
            o_b = _dot_nt(p_old, cv_ref[b0 + b].astype(BF16))
            o = jnp.where(row_seq == b, o_b, o)
        o = (o + p_new.astype(BF16).astype(F32) * jnp.concatenate([vnew8] * n_blk, axis=0)) / denom
        o = jnp.where(own, o, 0.0)
        for r in range(GQA_GROUP):
            blks = [o[(r * N_KV_HEADS + kh) * SAMPLE_GROUP:(r * N_KV_HEADS + kh + 1) * SAMPLE_GROUP]
                    for kh in range(N_KV_HEADS)]
            o_scr[seqs, r * KV_DIM:(r + 1) * KV_DIM] = (blks[0] + blks[1]) + (blks[2] + blks[3])
        return carry

    lax.fori_loop(0, b_tile // SAMPLE_GROUP, group, 0, unroll=True)

    @pl.when(step == pl.num_programs(0) - 1)
    def _():
        og_mm = o_scr[...] * _silu(gate_scr[...])
        og = jnp.concatenate(
            [og_mm[:, _member_major(hd)] for hd in range(N_HEADS)], axis=1).astype(BF16)
        h2 = h_ref[...] + _dot_nt(og, woutt_ref[...])
        y_ref[:, 0, :] = _rms(h2, fg_ref[...])


def _layer_b_sample(h, sinks, norm_b, final_norm, w_qg_t, w_out_t, rot, knew, vnew, cache_k,
                    cache_v):
    n_seq = h.shape[0]
    b_tile = SAMPLE_B_TILE

    def const(shape):
        return pl.BlockSpec(shape, lambda *_: (0,) * len(shape))

    assert cache_k.shape == (n_seq, KV_DIM, WINDOW)
    cache_spec = pl.BlockSpec((b_tile, KV_DIM, WINDOW), lambda i, *_: (i, 0, 0))
    weights = (h, norm_b, final_norm, w_qg_t, w_out_t)
    return pl.pallas_call(
        functools.partial(_layer_b_sample_kernel, n_seq=n_seq, b_tile=b_tile),
        grid_spec=pltpu.PrefetchScalarGridSpec(
            num_scalar_prefetch=1,
            grid=(n_seq // b_tile,),
            in_specs=[const(c.shape) for c in weights] + _sample_rot_specs(rot)
            + [const(knew.shape), const(vnew.shape), cache_spec, cache_spec],
            out_specs=const((n_seq, 1, D_MODEL)),
            scratch_shapes=[pltpu.VMEM((n_seq, D_MODEL), F32),
                            pltpu.VMEM((n_seq, D_MODEL), F32),
                            pltpu.VMEM((n_seq, D_MODEL), F32)]),
        out_shape=jax.ShapeDtypeStruct((n_seq, 1, D_MODEL), F32),
        compiler_params=pltpu.CompilerParams(
            dimension_semantics=("arbitrary",), vmem_limit_bytes=VMEM_LIMIT_BYTES),
        name="layer_b_sample",
    )(sinks, *weights, *rot, knew, vnew, cache_k, cache_v)


def kernel(x_prompt, x_sample, cache_k, cache_v, norm_a, w_in_a, v_norm_a, w_s_a, b_s_a, w_out_a,
           kv_norm, w_kv, norm_b, w_in_b, sinks_b, w_out_b, final_norm):
    batch, seq, _ = x_prompt.shape
    n_seq, dec_seq, _ = x_sample.shape
    assert dec_seq == 1 and seq % CHUNK == 0 and cache_k.shape[1] == WINDOW
    assert norm_a.shape[0] == 1 and norm_b.shape[0] == 1

    row = lambda g: g.reshape(1, -1)

    def to_window(x_t):
        n = x_t.shape[0]
        return x_t.reshape(n, N_KV_HEADS, HEAD_DIM, WINDOW).transpose(0, 3, 1, 2)

    def from_window(x):
        return x.transpose(0, 2, 3, 1).reshape(x.shape[0], KV_DIM, WINDOW)

    cache_kt, cache_vt = from_window(cache_k), from_window(cache_v)
    rot = _rotary_table(seq)

    w_in_a16, w_out_a16, w_kv_t, h_s, av_s, knew, vnew, knew_t, vnew_t = _layer_a_sample(
        x_sample, row(norm_a[0]), w_in_a[0], row(v_norm_a[0]), w_s_a[0],
        b_s_a[0], w_out_a[0], row(kv_norm), w_kv, rot)
    h_p, kt_s, vt_s, w_qg_t, w_out_b_t = _layer_a_prompt(
        x_prompt.reshape(batch * seq, D_MODEL), row(norm_a[0]), w_in_a16, row(v_norm_a[0]),
        w_s_a[0], b_s_a[0], w_out_a16, cache_kt, cache_vt, knew_t, vnew_t, w_in_b[0], w_out_b[0])

    y_p, kt_p, vt_p = _layer_b_prompt(
        h_p, sinks_b[0], row(kv_norm), row(norm_b[0]), row(final_norm),
        w_kv_t, w_qg_t, w_out_b_t, rot, batch=batch, seq=seq)
    y_s = _layer_b_sample(h_s, sinks_b[0], row(norm_b[0]), row(final_norm), w_qg_t, w_out_b_t,
                          rot, knew, vnew, cache_kt, cache_vt)

    return (y_p.reshape(batch, seq, D_MODEL),
            y_s,
            to_window(kt_p),
            to_window(vt_p),
            to_window(kt_s),
            to_window(vt_s),
            av_s.reshape(1, n_seq, 1, A_WIDTH))
```

```python
import functools

import jax
import jax.numpy as jnp
from jax import lax
from jax.experimental import pallas as pl
from jax.experimental.pallas import tpu as pltpu

D_MODEL = 1024
PAST_LEN = 8192
CHUNK = 128
A_WIDTH = 2 * D_MODEL
A_GROUPS = 8
A_GROUP_DIM = A_WIDTH // A_GROUPS
HEAD_DIM = 64
N_HEADS = D_MODEL // HEAD_DIM
N_KV_HEADS = 4
GQA_GROUP = N_HEADS // N_KV_HEADS
KV_DIM = N_KV_HEADS * HEAD_DIM
WINDOW = 128
Q_BLOCK = 128
ROT_DIM = HEAD_DIM // 4
ROPE_THETA = 500000.0
EPS = 1e-5

LANES = 128
BF16_SUBLANES = 16
LOG2_E = 1.4426950408889634
Q_SCALE_LOG2 = HEAD_DIM ** -0.5 * LOG2_E
VMEM_LIMIT_BYTES = 56 * 1024 * 1024

A_TILE = 512
B_TILE = 512
B_SUBTILES = 2
A_SAMPLE_COLS = 1024
A_SAMPLE_ROWS = 1024
OUT_ROWS = 256
ATTN_AHEAD = 2
SAMPLE_B_TILE = 16
SAMPLE_GROUP = 8
SAMPLE_SLOTS = 3

F32 = jnp.float32
BF16 = jnp.bfloat16


def _rms(x, g):
    return x * lax.rsqrt(jnp.mean(x * x, axis=-1, keepdims=True) + EPS) * g


def _silu(x):
    return x * jax.nn.sigmoid(x)


def _dot(a, b):
    return jnp.dot(a, b, preferred_element_type=F32)


def _dot_nt(a, b):
    return lax.dot_general(a, b, (((1,), (1,)), ((), ())), preferred_element_type=F32)


def _zero_of(x):
    bits = pltpu.bitcast(x, jnp.uint32)
    return ((bits >> 16) >> 16).astype(F32)


def _member_major(head):
    kh, r = divmod(head, GQA_GROUP)
    start = (r * N_KV_HEADS + kh) * HEAD_DIM
    return slice(start, start + HEAD_DIM)


def _rotate(x, cos, sin_lo, sin_hi):
    return (x * cos + pltpu.roll(x, LANES - ROT_DIM // 2, 1) * sin_lo
            + pltpu.roll(x, ROT_DIM // 2, 1) * sin_hi)


def _rotary_tables(positions):
    lane = jnp.arange(LANES) % HEAD_DIM
    freq = (2 * (lane % (ROT_DIM // 2))).astype(F32)
    ang = positions[:, None] * (ROPE_THETA ** (-freq / ROT_DIM))[None, :]
    first = (lane < ROT_DIM // 2)[None, :]
    second = ((lane >= ROT_DIM // 2) & (lane < ROT_DIM))[None, :]
    cos = jnp.where(first | second, jnp.cos(ang), 1.0)
    sin_lo = jnp.where(first, -jnp.sin(ang), 0.0)
    sin_hi = jnp.where(second, jnp.sin(ang), 0.0)
    return cos, sin_lo, sin_hi


def _layer_a_tile(x, ng_ref, win_ref, vg_ref, ws_ref, bs_ref, wout_ref, y_scr, *, tile):
    xn = _rms(x, ng_ref[...]).astype(BF16)
    vb = _rms(_dot(xn, win_ref[:, A_WIDTH:2 * A_WIDTH]), vg_ref[...]).astype(BF16)
    row = lax.broadcasted_iota(jnp.int32, (CHUNK, CHUNK), 0)
    col = lax.broadcasted_iota(jnp.int32, (CHUNK, CHUNK), 1)
    tri = row >= col
    ws = [jnp.where(tri, ws_ref[g], 0.0).astype(BF16) for g in range(A_GROUPS)]
    z = jnp.concatenate(
        [jnp.concatenate(
            [_dot(ws[g], vb[c * CHUNK:(c + 1) * CHUNK, g * A_GROUP_DIM:(g + 1) * A_GROUP_DIM])
             for g in range(A_GROUPS)], axis=1) + bs_ref[...]
         for c in range(tile // CHUNK)], axis=0)
    uz = _dot(xn, win_ref[:, 0:A_WIDTH]) * z
    gate = _dot(xn, win_ref[:, 2 * A_WIDTH:3 * A_WIDTH])
    y_scr[...] = (uz * _silu(gate)).astype(BF16)
    return x + _dot(y_scr[...], wout_ref[...])


def _layer_a_prompt_kernel(x_ref, ng_ref, win_ref, vg_ref, ws_ref, b_ref, wout_ref,
                           ck_ref, cv_ref, knewt_ref, vnewt_ref, winb_ref, woutb_ref,
                           h_ref, kout_ref, vout_ref, wqgt_ref, woutbt_ref, y_scr, bs_scr,
                           *, tile, n_roll):
    @pl.when(pl.program_id(0) == 0)
    def _():
        b = jnp.concatenate(
            [b_ref[...], jnp.zeros((CHUNK - A_GROUPS, CHUNK), F32)], axis=0).T
        for g in range(A_GROUPS):
            bs_scr[:, g * A_GROUP_DIM:(g + 1) * A_GROUP_DIM] = jnp.broadcast_to(
                b[:, g:g + 1], (CHUNK, A_GROUP_DIM))

    wqgt_ref[...] = winb_ref[...].T.astype(BF16)
    woutbt_ref[...] = woutb_ref[...].T.astype(BF16)
    is_last = lax.broadcasted_iota(jnp.int32, (KV_DIM, WINDOW), 1) == WINDOW - 1
    for b in range(n_roll):
        g = pl.program_id(0) * n_roll + b
        blk = pl.ds(pl.multiple_of((g // LANES) * LANES, LANES), LANES)
        to_last = LANES - 1 - g % LANES
        kout_ref[b] = jnp.where(is_last, pltpu.roll(knewt_ref[:, blk], to_last, 1),
                                pltpu.roll(ck_ref[b], WINDOW - 1, 1))
        vout_ref[b] = jnp.where(is_last, pltpu.roll(vnewt_ref[:, blk], to_last, 1),
                                pltpu.roll(cv_ref[b], WINDOW - 1, 1))
    h_ref[...] = _layer_a_tile(x_ref[...], ng_ref, win_ref, vg_ref, ws_ref, bs_scr, wout_ref,
                               y_scr, tile=tile)


def _layer_a_sample_kernel(x_ref, ng_ref, win_ref, vg_ref, ws_ref, bs_ref, wout_ref,
                           kvg_ref, wkv_ref, cos_ref, slo_ref, shi_ref,
                           win16_ref, wout16_ref, wkvt_ref, h_ref, av_ref, knew_ref, vnew_ref,
                           knewt_ref, vnewt_ref, xn_scr, proj_scr, y_scr, acc_scr, *, n_in, n_out):
    step = pl.program_id(0)
    blocks_per_branch = n_in // 3

    @pl.when(step == 0)
    def _():
        xn_scr[...] = _rms(x_ref[:, 0, :], ng_ref[...]).astype(BF16)
        wkvt_ref[...] = wkv_ref[...].T.astype(BF16)

    @pl.when(step < n_in)
    def _():
        w = win_ref[...].astype(BF16)
        win16_ref[...] = w
        proj_scr[step] = _dot(xn_scr[...], w)

    @pl.when(step == n_in)
    def _():
        def branch(i):
            return jnp.concatenate(
                [proj_scr[i * blocks_per_branch + j] for j in range(blocks_per_branch)], axis=1)

        v = _rms(branch(1), vg_ref[...])
        av_ref[:, 0, :] = v
        lane_group = lax.broadcasted_iota(jnp.int32, (1, A_WIDTH), 1) // A_GROUP_DIM
        ws_row = jnp.zeros((1, A_WIDTH), F32)
        bs_row = jnp.zeros((1, A_WIDTH), F32)
        for g in range(A_GROUPS):
            ws_row = jnp.where(lane_group == g, ws_ref[g, 0:1, 0:1], ws_row)
            bs_row = jnp.where(lane_group == g, bs_ref[g:g + 1, 0:1], bs_row)
        z = v * ws_row + bs_row
        y = (branch(0) * z * _silu(branch(2))).astype(BF16)
        rows = A_WIDTH // n_out
        for j in range(n_out):
            y_scr[j] = y[:, j * rows:(j + 1) * rows]

    @pl.when(step >= n_in)
    def _():
        w = wout_ref[...].astype(BF16)
        wout16_ref[...] = w
        part = _dot(y_scr[step - n_in], w)

        @pl.when(step == n_in)
        def _():
            acc_scr[...] = part

        @pl.when(step > n_in)
        def _():
            acc_scr[...] += part

    @pl.when(step == n_in + n_out - 1)
    def _():
        h = x_ref[:, 0, :] + acc_scr[...]
        h_ref[...] = h
        hn = h * lax.rsqrt(jnp.mean(h * h, axis=-1, keepdims=True) + EPS)
        kv = _dot_nt((hn * kvg_ref[...]).astype(BF16), wkvt_ref[...])
        cos, slo, shi = cos_ref[...], slo_ref[...], shi_ref[...]
        k = jnp.concatenate(
            [_rotate(kv[:, c * LANES:(c + 1) * LANES], cos, slo, shi)
             for c in range(KV_DIM // LANES)], axis=1)
        knew_ref[...] = k
        vnew_ref[...] = kv[:, KV_DIM:]
        knewt_ref[...] = k.T
        vnewt_ref[...] = kv[:, KV_DIM:].T


def _const_spec(shape):
    return pl.BlockSpec(shape, lambda *_: (0,) * len(shape), pipeline_mode=pl.Buffered(1))


def _layer_a_prompt(x, norm_g, w_in, v_norm_g, ws, bs, w_out, cache_k, cache_v, knew_t, vnew_t,
                    w_in_b, w_out_b):
    n_tok = x.shape[0]
    tile = A_TILE
    n_steps = n_tok // tile
    n_seq = cache_k.shape[0]
    n_roll = n_seq // n_steps
    assert n_roll * n_steps == n_seq and n_seq % LANES == 0
    assert max(w_in_b.shape[1], w_out_b.shape[1]) <= n_steps * LANES
    tok_spec = pl.BlockSpec((tile, D_MODEL), lambda i: (i, 0))
    cache_spec = pl.BlockSpec((n_roll, KV_DIM, WINDOW), lambda i: (i, 0, 0))

    def column_block(w):
        last = w.shape[1] // LANES - 1
        return pl.BlockSpec((w.shape[0], LANES), lambda i: (0, jnp.minimum(i, last)))

    def row_block(w):
        last = w.shape[1] // LANES - 1
        return pl.BlockSpec((LANES, w.shape[0]), lambda i: (jnp.minimum(i, last), 0))

    consts = (norm_g, w_in, v_norm_g, ws, bs, w_out)
    return pl.pallas_call(
        functools.partial(_layer_a_prompt_kernel, tile=tile, n_roll=n_roll),
        grid=(n_steps,),
        in_specs=[tok_spec] + [_const_spec(c.shape) for c in consts]
        + [cache_spec, cache_spec, _const_spec(knew_t.shape), _const_spec(vnew_t.shape),
           column_block(w_in_b), column_block(w_out_b)],
        out_specs=[tok_spec, cache_spec, cache_spec, row_block(w_in_b), row_block(w_out_b)],
        out_shape=[jax.ShapeDtypeStruct((n_tok, D_MODEL), F32),
                   jax.ShapeDtypeStruct(cache_k.shape, F32),
                   jax.ShapeDtypeStruct(cache_v.shape, F32),
                   jax.ShapeDtypeStruct(w_in_b.shape[::-1], BF16),
                   jax.ShapeDtypeStruct(w_out_b.shape[::-1], BF16)],
        scratch_shapes=[pltpu.VMEM((tile, A_WIDTH), BF16), pltpu.VMEM((CHUNK, A_WIDTH), F32)],
        compiler_params=pltpu.CompilerParams(
            dimension_semantics=("arbitrary",), vmem_limit_bytes=VMEM_LIMIT_BYTES),
        name="layer_a_prompt",
    )(x, *consts, cache_k, cache_v, knew_t, vnew_t, w_in_b, w_out_b)


def _layer_a_sample(x, norm_g, w_in, v_norm_g, ws, bs, w_out, kv_norm, w_kv):
    n_seq = x.shape[0]
    n_in = w_in.shape[1] // A_SAMPLE_COLS
    n_out = w_out.shape[0] // A_SAMPLE_ROWS
    assert n_in % 3 == 0
    cos, slo, shi = _rotary_tables(jnp.full((1,), PAST_LEN, F32))
    whole = lambda shape: pl.BlockSpec(shape, lambda i: (0,) * len(shape))
    win_block = lambda i: (0, jnp.minimum(i, n_in - 1))
    wout_block = lambda i: (jnp.maximum(i - n_in, 0), 0)
    small_dims = [(n_seq, D_MODEL), (n_seq, 1, A_WIDTH), (n_seq, KV_DIM), (n_seq, KV_DIM),
                  (KV_DIM, n_seq), (KV_DIM, n_seq)]
    return pl.pallas_call(
        functools.partial(_layer_a_sample_kernel, n_in=n_in, n_out=n_out),
        grid=(n_in + n_out,),
        in_specs=[whole(x.shape), whole(norm_g.shape),
                  pl.BlockSpec((D_MODEL, A_SAMPLE_COLS), win_block),
                  whole(v_norm_g.shape),
                  pl.BlockSpec((A_GROUPS, 8, LANES), lambda i: (0, 0, 0)), whole(bs.shape),
                  pl.BlockSpec((A_SAMPLE_ROWS, D_MODEL), wout_block),
                  whole(kv_norm.shape), whole(w_kv.shape),
                  whole(cos.shape), whole(slo.shape), whole(shi.shape)],
        out_specs=[pl.BlockSpec((D_MODEL, A_SAMPLE_COLS), win_block),
                   pl.BlockSpec((A_SAMPLE_ROWS, D_MODEL), wout_block), whole(w_kv.shape[::-1])]
        + [whole(d) for d in small_dims],
        out_shape=[jax.ShapeDtypeStruct(w_in.shape, BF16), jax.ShapeDtypeStruct(w_out.shape, BF16),
                   jax.ShapeDtypeStruct(w_kv.shape[::-1], BF16)]
        + [jax.ShapeDtypeStruct(d, F32) for d in small_dims],
        scratch_shapes=[pltpu.VMEM((n_seq, D_MODEL), BF16),
                        pltpu.VMEM((n_in, n_seq, A_SAMPLE_COLS), F32),
                        pltpu.VMEM((n_out, n_seq, A_SAMPLE_ROWS), BF16),
                        pltpu.VMEM((n_seq, D_MODEL), F32)],
        compiler_params=pltpu.CompilerParams(
            dimension_semantics=("arbitrary",), vmem_limit_bytes=VMEM_LIMIT_BYTES),
        name="layer_a_sample",
    )(x, norm_g, w_in, v_norm_g, ws, bs, w_out, kv_norm, w_kv, cos, slo, shi)


def _layer_b_prompt_kernel(sinks_ref, h_ref, kvg_ref, nbg_ref, fg_ref, wkvt_ref, wqt_ref,
                           wgt_ref, woutt_ref, cost_ref, sint_ref, y_ref, kout_ref, vout_ref,
                           *scratch, tile, n_t, n_sub):
    for sub in range(n_sub):
        rows = pl.ds(sub * tile, tile)
        _layer_b_prompt_tile(
            sinks_ref, h_ref.at[rows, :], kvg_ref, nbg_ref, fg_ref, wkvt_ref, wqt_ref, wgt_ref,
            woutt_ref, cost_ref.at[:, rows], sint_ref.at[:, rows], y_ref.at[rows, :], kout_ref,
            vout_ref, *scratch, tile=tile, n_t=n_t, t=pl.program_id(1) * n_sub + sub,
            first_possible=sub == 0, last_possible=sub == n_sub - 1)


def _layer_b_prompt_tile(sinks_ref, h_ref, kvg_ref, nbg_ref, fg_ref, wkvt_ref, wqt_ref,
                         wgt_ref, woutt_ref, cost_ref, sint_ref,
                         y_ref, kout_ref, vout_ref,
                         kext_scr, vtext_scr, qt_scr, ogt_scr, bias_scr,
                         *, tile, n_t, t, first_possible, last_possible):
    n_keys = WINDOW + Q_BLOCK

    if first_possible:
        @pl.when((pl.program_id(0) == 0) & (t == 0))
        def _():
            j = lax.broadcasted_iota(jnp.int32, (n_keys, Q_BLOCK), 0)
            i = lax.broadcasted_iota(jnp.int32, (n_keys, Q_BLOCK), 1)
            band = (j >= i) & (j <= WINDOW + i)
            bias_scr[0] = jnp.where(band & (j >= WINDOW), 0.0, -jnp.inf)
            bias_scr[1] = jnp.where(band, 0.0, -jnp.inf)

        @pl.when(t == 0)
        def _():
            kext_scr[0:WINDOW, :] = jnp.zeros((WINDOW, KV_DIM), BF16)
            vtext_scr[:, 0:WINDOW] = jnp.zeros((KV_DIM, WINDOW), BF16)

    h = h_ref[...]
    hn = h * lax.rsqrt(jnp.mean(h * h, axis=-1, keepdims=True) + EPS)
    xkv = (hn * kvg_ref[...]).astype(BF16)
    xb = (hn * nbg_ref[...]).astype(BF16)

    cost, sint = cost_ref[...], sint_ref[...]
    half = ROT_DIM // 2

    def rotate_head(rows):
        lo, hi = rows[0:half, :], rows[half:ROT_DIM, :]
        return jnp.concatenate(
            [lo * cost - hi * sint, hi * cost + lo * sint, rows[ROT_DIM:, :]], axis=0)

    kvt = _dot_nt(wkvt_ref[...], xkv)
    kt = jnp.concatenate(
        [rotate_head(kvt[kh * HEAD_DIM:(kh + 1) * HEAD_DIM, :]) for kh in range(N_KV_HEADS)],
        axis=0)
    vt = kvt[KV_DIM:, :]
    kext_scr[WINDOW:, :] = kt.T.astype(BF16)
    vtext_scr[:, WINDOW:] = vt.astype(BF16)

    if last_possible:
        @pl.when(t == n_t - 1)
        def _():
            kout_ref[0] = kt[:, tile - WINDOW:]
            vout_ref[0] = vt[:, tile - WINDOW:]

    qt = _dot_nt(wqt_ref[...], xb)
    for hd in range(N_HEADS):
        rot = rotate_head(qt[hd * HEAD_DIM:(hd + 1) * HEAD_DIM, :])
        qt_scr[hd * HEAD_DIM:(hd + 1) * HEAD_DIM, :] = (rot * Q_SCALE_LOG2).astype(BF16)

    lane_head = lax.broadcasted_iota(jnp.int32, (1, GQA_GROUP * Q_BLOCK), 1) // Q_BLOCK
    zeros_half = jnp.zeros((HEAD_DIM, GQA_GROUP * Q_BLOCK), BF16)
    ones_rows = jnp.ones((BF16_SUBLANES, n_keys), BF16)

    def scores(qb, kh):
        qcols = slice(qb * Q_BLOCK, (qb + 1) * Q_BLOCK)
        keys = slice(qb * Q_BLOCK, qb * Q_BLOCK + n_keys)
        q4 = jnp.concatenate(
            [qt_scr[(kh * GQA_GROUP + r) * HEAD_DIM:(kh * GQA_GROUP + r + 1) * HEAD_DIM, qcols]
             for r in range(GQA_GROUP)], axis=1)
        q4 = jnp.concatenate([q4, zeros_half] if kh % 2 == 0 else [zeros_half, q4], axis=0)
        kblk = kext_scr[keys, (kh // 2) * LANES:(kh // 2 + 1) * LANES]
        return _dot(kblk, q4)

    def finish(qb, kh, s, s_ahead):
        qcols = slice(qb * Q_BLOCK, (qb + 1) * Q_BLOCK)
        keys = slice(qb * Q_BLOCK, qb * Q_BLOCK + n_keys)
        if first_possible and qb == 0:
            bias = bias_scr[jnp.where(t > 0, 1, 0)]
        else:
            bias = bias_scr[1]
        s = s + jnp.concatenate([bias] * GQA_GROUP, axis=1)
        sink = jnp.zeros((1, GQA_GROUP * Q_BLOCK), F32)
        for r in range(GQA_GROUP):
            sink = jnp.where(lane_head == r, sinks_ref[kh * GQA_GROUP + r] * LOG2_E, sink)
        m = jnp.maximum(jnp.max(s, axis=0, keepdims=True), sink)
        p = jnp.exp2(s - m)
        if s_ahead is not None:
            p = jnp.concatenate(
                [p[:n_keys - 8, :], p[n_keys - 8:, :] + _zero_of(s_ahead[0:8, :])], axis=0)
        vt_ones = jnp.concatenate(
            [vtext_scr[kh * HEAD_DIM:(kh + 1) * HEAD_DIM, keys], ones_rows], axis=0)
        ot = _dot(vt_ones, p.astype(BF16))
        denom = ot[HEAD_DIM:HEAD_DIM + 1, :] + jnp.exp2(sink - m)
        ot = ot[0:HEAD_DIM, :] * (1.0 / denom)
        for r in range(GQA_GROUP):
            hd = kh * GQA_GROUP + r
            ogt_scr[hd * HEAD_DIM:(hd + 1) * HEAD_DIM, qcols] = ot[:, r * Q_BLOCK:(r + 1) * Q_BLOCK]

    blocks = [(qb, kh) for qb in range(tile // Q_BLOCK) for kh in range(N_KV_HEADS)]
    pending = [scores(*blk) for blk in blocks[:ATTN_AHEAD]]
    for n, blk in enumerate(blocks):
        if n + ATTN_AHEAD < len(blocks):
            pending.append(scores(*blocks[n + ATTN_AHEAD]))
        s_cur = pending.pop(0)
        finish(*blk, s_cur, pending[-1] if pending else None)

    kext_scr[0:WINDOW, :] = kext_scr[tile:tile + WINDOW, :]
    vtext_scr[:, 0:WINDOW] = vtext_scr[:, tile:tile + WINDOW]

    ogt = jnp.concatenate(
        [(ogt_scr[rc * OUT_ROWS:(rc + 1) * OUT_ROWS, :]
          * _silu(_dot_nt(wgt_ref[rc * OUT_ROWS:(rc + 1) * OUT_ROWS, :], xb))).astype(BF16)
         for rc in range(D_MODEL // OUT_ROWS)], axis=0)
    h2 = jnp.concatenate(
        [h[:, rc * OUT_ROWS:(rc + 1) * OUT_ROWS]
         + _dot(woutt_ref[rc * OUT_ROWS:(rc + 1) * OUT_ROWS, :], ogt).T
         for rc in range(D_MODEL // OUT_ROWS)], axis=1)
    y_ref[...] = _rms(h2, fg_ref[...])


def _layer_b_prompt(h, sinks, kv_norm, norm_b, final_norm, w_kv_t, w_qg_t, w_out_t, *, batch, seq):
    tile = B_TILE
    n_t = seq // tile
    n_sub = B_SUBTILES
    step = n_sub * tile
    n_steps = n_t // n_sub
    pos = jnp.arange(seq, dtype=F32)
    inv = ROPE_THETA ** (-jnp.arange(0, ROT_DIM, 2, dtype=F32) / ROT_DIM)
    ang_t = inv[:, None] * pos[None, :]
    cos_t, sin_t = jnp.cos(ang_t), jnp.sin(ang_t)
    tok_spec = pl.BlockSpec((step, D_MODEL), lambda b, t, *_: (b * n_steps + t, 0))
    rot_t_spec = pl.BlockSpec((ROT_DIM // 2, step), lambda b, t, *_: (0, t))
    last_spec = pl.BlockSpec((1, KV_DIM, WINDOW), lambda b, t, *_: (b, 0, 0))

    def const(shape):
        return pl.BlockSpec(shape, lambda *_: (0,) * len(shape), pipeline_mode=pl.Buffered(1))

    def half(i):
        return pl.BlockSpec((D_MODEL, D_MODEL), lambda *_: (i, 0), pipeline_mode=pl.Buffered(1))

    return pl.pallas_call(
        functools.partial(_layer_b_prompt_kernel, tile=tile, n_t=n_t, n_sub=n_sub),
        grid_spec=pltpu.PrefetchScalarGridSpec(
            num_scalar_prefetch=1,
            grid=(batch, n_steps),
            in_specs=[tok_spec, const(kv_norm.shape), const(norm_b.shape), const(final_norm.shape),
                      const(w_kv_t.shape), half(0), half(1), const(w_out_t.shape),
                      rot_t_spec, rot_t_spec],
            out_specs=[tok_spec, last_spec, last_spec],
            scratch_shapes=[pltpu.VMEM((WINDOW + tile, KV_DIM), BF16),
                            pltpu.VMEM((KV_DIM, WINDOW + tile), BF16),
                            pltpu.VMEM((D_MODEL, tile), BF16),
                            pltpu.VMEM((D_MODEL, tile), F32),
                            pltpu.VMEM((2, WINDOW + Q_BLOCK, Q_BLOCK), F32)]),
        out_shape=[jax.ShapeDtypeStruct((batch * seq, D_MODEL), F32),
                   jax.ShapeDtypeStruct((batch, KV_DIM, WINDOW), F32),
                   jax.ShapeDtypeStruct((batch, KV_DIM, WINDOW), F32)],
        compiler_params=pltpu.CompilerParams(
            dimension_semantics=("arbitrary", "arbitrary"), vmem_limit_bytes=VMEM_LIMIT_BYTES),
        name="layer_b_prompt",
    )(sinks, h, kv_norm, norm_b, final_norm, w_kv_t, w_qg_t, w_qg_t, w_out_t, cos_t, sin_t)


def _layer_b_sample_kernel(sinks_ref, h_ref, nbg_ref, fg_ref, wqt_ref, wqgt_hbm, woutt_hbm,
                           cos_ref, slo_ref, shi_ref, knew_ref, vnew_ref, ck_hbm, cv_hbm,
                           y_ref, q_scr, gate_scr, o_scr, wgt_scr, woutt_scr, ck_scr, cv_scr,
                           cache_sem, weight_sem, *, n_seq, b_tile):
    step = pl.program_id(0)
    n_steps = n_seq // b_tile

    def cache_copies(s, slot):
        rows = pl.ds(s * b_tile, b_tile)
        return (pltpu.make_async_copy(ck_hbm.at[rows], ck_scr.at[slot], cache_sem.at[0, slot]),
                pltpu.make_async_copy(cv_hbm.at[rows], cv_scr.at[slot], cache_sem.at[1, slot]))

    weight_copies = (
        pltpu.make_async_copy(wqgt_hbm.at[pl.ds(D_MODEL, D_MODEL), :], wgt_scr, weight_sem.at[0]),
        pltpu.make_async_copy(woutt_hbm, woutt_scr, weight_sem.at[1]))

    def normed_h():
        h = h_ref[...]
        hn = h * lax.rsqrt(jnp.mean(h * h, axis=-1, keepdims=True) + EPS)
        return (hn * nbg_ref[...]).astype(BF16)

    @pl.when(step == 0)
    def _():
        for s in range(2):
            for copy in cache_copies(s, s):
                copy.start()
        for copy in weight_copies:
            copy.start()
        q = _dot_nt(normed_h(), wqt_ref[...])
        cos, slo, shi = cos_ref[...], slo_ref[...], shi_ref[...]
        for c in range(D_MODEL // LANES):
            q2 = _rotate(q[:, c * LANES:(c + 1) * LANES], cos, slo, shi) * HEAD_DIM ** -0.5
            for i in range(LANES // HEAD_DIM):
                dst = _member_major(c * (LANES // HEAD_DIM) + i)
                q_scr[:, dst] = q2[:, i * HEAD_DIM:(i + 1) * HEAD_DIM]

    @pl.when(step + 2 < n_steps)
    def _():
        for copy in cache_copies(step + 2, (step + 2) % SAMPLE_SLOTS):
            copy.start()

    slot = step % SAMPLE_SLOTS
    for copy in cache_copies(step, slot):
        copy.wait()

    n_rows = GQA_GROUP * N_KV_HEADS * SAMPLE_GROUP
    row = lax.broadcasted_iota(jnp.int32, (n_rows, 1), 0)
    row_kh = (row // SAMPLE_GROUP) % N_KV_HEADS
    row_seq = row % SAMPLE_GROUP
    lane_kh = lax.broadcasted_iota(jnp.int32, (1, KV_DIM), 1) // HEAD_DIM
    own = row_kh == lane_kh
    sink = jnp.zeros((n_rows, 1), F32)
    for r in range(GQA_GROUP):
        for kh in range(N_KV_HEADS):
            sink = jnp.where(row // SAMPLE_GROUP == r * N_KV_HEADS + kh,
                             sinks_ref[kh * GQA_GROUP + r], sink)
    n_blk = GQA_GROUP * N_KV_HEADS

    def group(i, carry):
        b0 = i * SAMPLE_GROUP
        g0 = pl.multiple_of(step * b_tile + b0, SAMPLE_GROUP)
        seqs = pl.ds(g0, SAMPLE_GROUP)
        q8 = q_scr[seqs, :]
        qexp = jnp.concatenate(
            [q8[:, r * KV_DIM:(r + 1) * KV_DIM] for r in range(GQA_GROUP)
             for _ in range(N_KV_HEADS)], axis=0)
        qexp = jnp.where(own, qexp, 0.0).astype(BF16)
        knew8 = knew_ref[seqs, :].astype(BF16).astype(F32)
        vnew8 = vnew_ref[seqs, :].astype(BF16).astype(F32)
        s_new = jnp.sum(qexp.astype(F32) * jnp.concatenate([knew8] * n_blk, axis=0),
                        axis=1, keepdims=True)
        s_old = jnp.zeros((n_rows, WINDOW), F32)
        for b in range(SAMPLE_GROUP):
            s_b = _dot(qexp, ck_scr[slot, b0 + b].astype(BF16))
            s_old = jnp.where(row_seq == b, s_b, s_old)
        m = jnp.maximum(jnp.maximum(jnp.max(s_old, axis=1, keepdims=True), s_new), sink)
        p_old = jnp.exp(s_old - m)
        p_new = jnp.exp(s_new - m)
        denom = jnp.sum(p_old, axis=1, keepdims=True) + p_new + jnp.exp(sink - m)
        p_old = p_old.astype(BF16)
        o = jnp.zeros((n_rows, KV_DIM), F32)
        for b in range(SAMPLE_GROUP):
            o_b = _dot_nt(p_old, cv_scr[slot, b0 + b].astype(BF16))
            o = jnp.where(row_seq == b, o_b, o)
        o = (o + p_new.astype(BF16).astype(F32) * jnp.concatenate([vnew8] * n_blk, axis=0)) / denom
        o = jnp.where(own, o, 0.0)
        for r in range(GQA_GROUP):
            blks = [o[(r * N_KV_HEADS + kh) * SAMPLE_GROUP:(r * N_KV_HEADS + kh + 1) * SAMPLE_GROUP]
                    for kh in range(N_KV_HEADS)]
            o_scr[seqs, r * KV_DIM:(r + 1) * KV_DIM] = (blks[0] + blks[1]) + (blks[2] + blks[3])
        return carry

    lax.fori_loop(0, b_tile // SAMPLE_GROUP, group, 0, unroll=True)

    @pl.when(step == pl.num_programs(0) - 1)
    def _():
        for copy in weight_copies:
            copy.wait()
        gate = _dot_nt(normed_h(), wgt_scr[...])
        for hd in range(N_HEADS):
            gate_scr[:, _member_major(hd)] = gate[:, hd * HEAD_DIM:(hd + 1) * HEAD_DIM]
        og_mm = o_scr[...] * _silu(gate_scr[...])
        og = jnp.concatenate(
            [og_mm[:, _member_major(hd)] for hd in range(N_HEADS)], axis=1).astype(BF16)
        h2 = h_ref[...] + _dot_nt(og, woutt_scr[...])
        y_ref[:, 0, :] = _rms(h2, fg_ref[...])


def _layer_b_sample(h, sinks, norm_b, final_norm, w_qg_t, w_out_t, knew, vnew, cache_k, cache_v):
    n_seq = h.shape[0]
    b_tile = SAMPLE_B_TILE
    cos, slo, shi = _rotary_tables(jnp.full((1,), PAST_LEN, F32))

    def const(shape):
        return pl.BlockSpec(shape, lambda *_: (0,) * len(shape))

    assert cache_k.shape == (n_seq, KV_DIM, WINDOW) and n_seq // b_tile >= 2
    in_hbm = pl.BlockSpec(memory_space=pl.ANY)
    w_q_spec = pl.BlockSpec((D_MODEL, D_MODEL), lambda *_: (0, 0))
    tables = (cos, slo, shi, knew, vnew)
    return pl.pallas_call(
        functools.partial(_layer_b_sample_kernel, n_seq=n_seq, b_tile=b_tile),
        grid_spec=pltpu.PrefetchScalarGridSpec(
            num_scalar_prefetch=1,
            grid=(n_seq // b_tile,),
            in_specs=[const(h.shape), const(norm_b.shape), const(final_norm.shape),
                      w_q_spec, in_hbm, in_hbm] + [const(c.shape) for c in tables]
            + [in_hbm, in_hbm],
            out_specs=const((n_seq, 1, D_MODEL)),
            scratch_shapes=[pltpu.VMEM((n_seq, D_MODEL), F32),
                            pltpu.VMEM((n_seq, D_MODEL), F32),
                            pltpu.VMEM((n_seq, D_MODEL), F32),
                            pltpu.VMEM((D_MODEL, D_MODEL), BF16),
                            pltpu.VMEM((D_MODEL, D_MODEL), BF16),
                            pltpu.VMEM((SAMPLE_SLOTS, b_tile, KV_DIM, WINDOW), F32),
                            pltpu.VMEM((SAMPLE_SLOTS, b_tile, KV_DIM, WINDOW), F32),
                            pltpu.SemaphoreType.DMA((2, SAMPLE_SLOTS)),
                            pltpu.SemaphoreType.DMA((2,))]),
        out_shape=jax.ShapeDtypeStruct((n_seq, 1, D_MODEL), F32),
        compiler_params=pltpu.CompilerParams(
            dimension_semantics=("arbitrary",), vmem_limit_bytes=VMEM_LIMIT_BYTES),
        name="layer_b_sample",
    )(sinks, h, norm_b, final_norm, w_qg_t, w_qg_t, w_out_t, *tables, cache_k, cache_v)


def kernel(x_prompt, x_sample, cache_k, cache_v, norm_a, w_in_a, v_norm_a, w_s_a, b_s_a, w_out_a,
           kv_norm, w_kv, norm_b, w_in_b, sinks_b, w_out_b, final_norm):
    batch, seq, _ = x_prompt.shape
    n_seq, dec_seq, _ = x_sample.shape
    assert dec_seq == 1 and seq % CHUNK == 0 and cache_k.shape[1] == WINDOW
    assert norm_a.shape[0] == 1 and norm_b.shape[0] == 1

    row = lambda g: g.reshape(1, -1)

    def to_window(x_t):
        n = x_t.shape[0]
        return x_t.reshape(n, N_KV_HEADS, HEAD_DIM, WINDOW).transpose(0, 3, 1, 2)

    def from_window(x):
        return x.transpose(0, 2, 3, 1).reshape(x.shape[0], KV_DIM, WINDOW)

    cache_kt, cache_vt = from_window(cache_k), from_window(cache_v)

    w_in_a16, w_out_a16, w_kv_t, h_s, av_s, knew, vnew, knew_t, vnew_t = _layer_a_sample(
        x_sample, row(norm_a[0]), w_in_a[0], row(v_norm_a[0]), w_s_a[0],
        b_s_a[0], w_out_a[0], row(kv_norm), w_kv)
    h_p, kt_s, vt_s, w_qg_t, w_out_b_t = _layer_a_prompt(
        x_prompt.reshape(batch * seq, D_MODEL), row(norm_a[0]), w_in_a16, row(v_norm_a[0]),
        w_s_a[0], b_s_a[0], w_out_a16, cache_kt, cache_vt, knew_t, vnew_t, w_in_b[0], w_out_b[0])

    y_p, kt_p, vt_p = _layer_b_prompt(
        h_p, sinks_b[0], row(kv_norm), row(norm_b[0]), row(final_norm),
        w_kv_t, w_qg_t, w_out_b_t, batch=batch, seq=seq)
    y_s = _layer_b_sample(h_s, sinks_b[0], row(norm_b[0]), row(final_norm), w_qg_t, w_out_b_t,
                          knew, vnew, cache_kt, cache_vt)

    return (y_p.reshape(batch, seq, D_MODEL),
            y_s,
            to_window(kt_p),
            to_window(vt_p),
            to_window(kt_s),
            to_window(vt_s),
            av_s.reshape(1, n_seq, 1, A_WIDTH))
```

```python
import functools

import jax
import jax.numpy as jnp
from jax import lax
from jax.experimental import pallas as pl
from jax.experimental.pallas import tpu as pltpu

D_MODEL = 1024
PAST_LEN = 8192
CHUNK = 128
A_WIDTH = 2 * D_MODEL
A_GROUPS = 8
A_GROUP_DIM = A_WIDTH // A_GROUPS
HEAD_DIM = 64
N_HEADS = D_MODEL // HEAD_DIM
N_KV_HEADS = 4
GQA_GROUP = N_HEADS // N_KV_HEADS
KV_DIM = N_KV_HEADS * HEAD_DIM
WINDOW = 128
Q_BLOCK = 128
ROT_DIM = HEAD_DIM // 4
ROPE_THETA = 500000.0
EPS = 1e-5

LANES = 128
BF16_SUBLANES = 16
LOG2_E = 1.4426950408889634
Q_SCALE_LOG2 = HEAD_DIM ** -0.5 * LOG2_E
VMEM_LIMIT_BYTES = 56 * 1024 * 1024

A_TILE = 512
B_TILE = 512
B_SUBTILES = 2
A_SAMPLE_COLS = 1024
A_SAMPLE_ROWS = 1024
OUT_ROWS = 256
ATTN_AHEAD = 2
SAMPLE_B_TILE = 16
SAMPLE_GROUP = 8
SAMPLE_SLOTS = 3

F32 = jnp.float32
BF16 = jnp.bfloat16


def _rms(x, g):
    return x * lax.rsqrt(jnp.mean(x * x, axis=-1, keepdims=True) + EPS) * g


def _silu(x):
    return x * jax.nn.sigmoid(x)


def _dot(a, b):
    return jnp.dot(a, b, preferred_element_type=F32)


def _dot_nt(a, b):
    return lax.dot_general(a, b, (((1,), (1,)), ((), ())), preferred_element_type=F32)


def _zero_of(x):
    bits = pltpu.bitcast(x, jnp.uint32)
    return ((bits >> 16) >> 16).astype(F32)


def _member_major(head):
    kh, r = divmod(head, GQA_GROUP)
    start = (r * N_KV_HEADS + kh) * HEAD_DIM
    return slice(start, start + HEAD_DIM)


def _rotate(x, cos, sin_lo, sin_hi):
    return (x * cos + pltpu.roll(x, LANES - ROT_DIM // 2, 1) * sin_lo
            + pltpu.roll(x, ROT_DIM // 2, 1) * sin_hi)


def _rotary_tables(positions):
    lane = jnp.arange(LANES) % HEAD_DIM
    freq = (2 * (lane % (ROT_DIM // 2))).astype(F32)
    ang = positions[:, None] * (ROPE_THETA ** (-freq / ROT_DIM))[None, :]
    first = (lane < ROT_DIM // 2)[None, :]
    second = ((lane >= ROT_DIM // 2) & (lane < ROT_DIM))[None, :]
    cos = jnp.where(first | second, jnp.cos(ang), 1.0)
    sin_lo = jnp.where(first, -jnp.sin(ang), 0.0)
    sin_hi = jnp.where(second, jnp.sin(ang), 0.0)
    return cos, sin_lo, sin_hi


def _layer_a_tile(x, ng_ref, win_ref, vg_ref, ws_ref, bs_ref, wout_ref, y_scr, *, tile):
    xn = _rms(x, ng_ref[...]).astype(BF16)
    vb = _rms(_dot(xn, win_ref[:, A_WIDTH:2 * A_WIDTH]), vg_ref[...]).astype(BF16)
    row = lax.broadcasted_iota(jnp.int32, (CHUNK, CHUNK), 0)
    col = lax.broadcasted_iota(jnp.int32, (CHUNK, CHUNK), 1)
    tri = row >= col
    ws = [jnp.where(tri, ws_ref[g], 0.0).astype(BF16) for g in range(A_GROUPS)]
    z = jnp.concatenate(
        [jnp.concatenate(
            [_dot(ws[g], vb[c * CHUNK:(c + 1) * CHUNK, g * A_GROUP_DIM:(g + 1) * A_GROUP_DIM])
             for g in range(A_GROUPS)], axis=1) + bs_ref[...]
         for c in range(tile // CHUNK)], axis=0)
    uz = _dot(xn, win_ref[:, 0:A_WIDTH]) * z
    gate = _dot(xn, win_ref[:, 2 * A_WIDTH:3 * A_WIDTH])
    y_scr[...] = (uz * _silu(gate)).astype(BF16)
    return x + _dot(y_scr[...], wout_ref[...])


def _layer_a_prompt_kernel(x_ref, ng_ref, win_ref, vg_ref, ws_ref, b_ref, wout_ref,
                           ck_ref, cv_ref, knewt_ref, vnewt_ref, winb_ref, woutb_ref,
                           h_ref, kout_ref, vout_ref, k16_ref, v16_ref, wqgt_ref, woutbt_ref,
                           y_scr, bs_scr, *, tile, n_roll):
    @pl.when(pl.program_id(0) == 0)
    def _():
        b = jnp.concatenate(
            [b_ref[...], jnp.zeros((CHUNK - A_GROUPS, CHUNK), F32)], axis=0).T
        for g in range(A_GROUPS):
            bs_scr[:, g * A_GROUP_DIM:(g + 1) * A_GROUP_DIM] = jnp.broadcast_to(
                b[:, g:g + 1], (CHUNK, A_GROUP_DIM))

    wqgt_ref[...] = winb_ref[...].T.astype(BF16)
    woutbt_ref[...] = woutb_ref[...].T.astype(BF16)
    is_last = lax.broadcasted_iota(jnp.int32, (KV_DIM, WINDOW), 1) == WINDOW - 1
    for b in range(n_roll):
        g = pl.program_id(0) * n_roll + b
        blk = pl.ds(pl.multiple_of((g // LANES) * LANES, LANES), LANES)
        to_last = LANES - 1 - g % LANES
        kout_ref[b] = jnp.where(is_last, pltpu.roll(knewt_ref[:, blk], to_last, 1),
                                pltpu.roll(ck_ref[b], WINDOW - 1, 1))
        vout_ref[b] = jnp.where(is_last, pltpu.roll(vnewt_ref[:, blk], to_last, 1),
                                pltpu.roll(cv_ref[b], WINDOW - 1, 1))
        k16_ref[b] = ck_ref[b].astype(BF16)
        v16_ref[b] = cv_ref[b].astype(BF16)
    h_ref[...] = _layer_a_tile(x_ref[...], ng_ref, win_ref, vg_ref, ws_ref, bs_scr, wout_ref,
                               y_scr, tile=tile)


def _layer_a_sample_kernel(x_ref, ng_ref, win_ref, vg_ref, ws_ref, bs_ref, wout_ref,
                           kvg_ref, wkv_ref, cos_ref, slo_ref, shi_ref,
                           win16_ref, wout16_ref, wkvt_ref, h_ref, av_ref, knew_ref, vnew_ref,
                           knewt_ref, vnewt_ref, xn_scr, proj_scr, y_scr, acc_scr, *, n_in, n_out):
    step = pl.program_id(0)
    blocks_per_branch = n_in // 3

    @pl.when(step == 0)
    def _():
        xn_scr[...] = _rms(x_ref[:, 0, :], ng_ref[...]).astype(BF16)
        wkvt_ref[...] = wkv_ref[...].T.astype(BF16)

    @pl.when(step < n_in)
    def _():
        w = win_ref[...].astype(BF16)
        win16_ref[...] = w
        proj_scr[step] = _dot(xn_scr[...], w)

    @pl.when(step == n_in)
    def _():
        def branch(i):
            return jnp.concatenate(
                [proj_scr[i * blocks_per_branch + j] for j in range(blocks_per_branch)], axis=1)

        v = _rms(branch(1), vg_ref[...])
        av_ref[:, 0, :] = v
        lane_group = lax.broadcasted_iota(jnp.int32, (1, A_WIDTH), 1) // A_GROUP_DIM
        ws_row = jnp.zeros((1, A_WIDTH), F32)
        bs_row = jnp.zeros((1, A_WIDTH), F32)
        for g in range(A_GROUPS):
            ws_row = jnp.where(lane_group == g, ws_ref[g, 0:1, 0:1], ws_row)
            bs_row = jnp.where(lane_group == g, bs_ref[g:g + 1, 0:1], bs_row)
        z = v * ws_row + bs_row
        y = (branch(0) * z * _silu(branch(2))).astype(BF16)
        rows = A_WIDTH // n_out
        for j in range(n_out):
            y_scr[j] = y[:, j * rows:(j + 1) * rows]

    @pl.when(step >= n_in)
    def _():
        w = wout_ref[...].astype(BF16)
        wout16_ref[...] = w
        part = _dot(y_scr[step - n_in], w)

        @pl.when(step == n_in)
        def _():
            acc_scr[...] = part

        @pl.when(step > n_in)
        def _():
            acc_scr[...] += part

    @pl.when(step == n_in + n_out - 1)
    def _():
        h = x_ref[:, 0, :] + acc_scr[...]
        h_ref[...] = h
        hn = h * lax.rsqrt(jnp.mean(h * h, axis=-1, keepdims=True) + EPS)
        kv = _dot_nt((hn * kvg_ref[...]).astype(BF16), wkvt_ref[...])
        cos, slo, shi = cos_ref[...], slo_ref[...], shi_ref[...]
        k = jnp.concatenate(
            [_rotate(kv[:, c * LANES:(c + 1) * LANES], cos, slo, shi)
             for c in range(KV_DIM // LANES)], axis=1)
        knew_ref[...] = k
        vnew_ref[...] = kv[:, KV_DIM:]
        knewt_ref[...] = k.T
        vnewt_ref[...] = kv[:, KV_DIM:].T


def _const_spec(shape):
    return pl.BlockSpec(shape, lambda *_: (0,) * len(shape), pipeline_mode=pl.Buffered(1))


def _layer_a_prompt(x, norm_g, w_in, v_norm_g, ws, bs, w_out, cache_k, cache_v, knew_t, vnew_t,
                    w_in_b, w_out_b):
    n_tok = x.shape[0]
    tile = A_TILE
    n_steps = n_tok // tile
    n_seq = cache_k.shape[0]
    n_roll = n_seq // n_steps
    assert n_roll * n_steps == n_seq and n_seq % LANES == 0
    assert max(w_in_b.shape[1], w_out_b.shape[1]) <= n_steps * LANES
    tok_spec = pl.BlockSpec((tile, D_MODEL), lambda i: (i, 0))
    cache_spec = pl.BlockSpec((n_roll, KV_DIM, WINDOW), lambda i: (i, 0, 0))

    def column_block(w):
        last = w.shape[1] // LANES - 1
        return pl.BlockSpec((w.shape[0], LANES), lambda i: (0, jnp.minimum(i, last)))

    def row_block(w):
        last = w.shape[1] // LANES - 1
        return pl.BlockSpec((LANES, w.shape[0]), lambda i: (jnp.minimum(i, last), 0))

    consts = (norm_g, w_in, v_norm_g, ws, bs, w_out)
    return pl.pallas_call(
        functools.partial(_layer_a_prompt_kernel, tile=tile, n_roll=n_roll),
        grid=(n_steps,),
        in_specs=[tok_spec] + [_const_spec(c.shape) for c in consts]
        + [cache_spec, cache_spec, _const_spec(knew_t.shape), _const_spec(vnew_t.shape),
           column_block(w_in_b), column_block(w_out_b)],
        out_specs=[tok_spec, cache_spec, cache_spec, cache_spec, cache_spec,
                   row_block(w_in_b), row_block(w_out_b)],
        out_shape=[jax.ShapeDtypeStruct((n_tok, D_MODEL), F32),
                   jax.ShapeDtypeStruct(cache_k.shape, F32),
                   jax.ShapeDtypeStruct(cache_v.shape, F32),
                   jax.ShapeDtypeStruct(cache_k.shape, BF16),
                   jax.ShapeDtypeStruct(cache_v.shape, BF16),
                   jax.ShapeDtypeStruct(w_in_b.shape[::-1], BF16),
                   jax.ShapeDtypeStruct(w_out_b.shape[::-1], BF16)],
        scratch_shapes=[pltpu.VMEM((tile, A_WIDTH), BF16), pltpu.VMEM((CHUNK, A_WIDTH), F32)],
        compiler_params=pltpu.CompilerParams(
            dimension_semantics=("arbitrary",), vmem_limit_bytes=VMEM_LIMIT_BYTES),
        name="layer_a_prompt",
    )(x, *consts, cache_k, cache_v, knew_t, vnew_t, w_in_b, w_out_b)


def _layer_a_sample(x, norm_g, w_in, v_norm_g, ws, bs, w_out, kv_norm, w_kv):
    n_seq = x.shape[0]
    n_in = w_in.shape[1] // A_SAMPLE_COLS
    n_out = w_out.shape[0] // A_SAMPLE_ROWS
    assert n_in % 3 == 0
    cos, slo, shi = _rotary_tables(jnp.full((1,), PAST_LEN, F32))
    whole = lambda shape: pl.BlockSpec(shape, lambda i: (0,) * len(shape))
    win_block = lambda i: (0, jnp.minimum(i, n_in - 1))
    wout_block = lambda i: (jnp.maximum(i - n_in, 0), 0)
    small_dims = [(n_seq, D_MODEL), (n_seq, 1, A_WIDTH), (n_seq, KV_DIM), (n_seq, KV_DIM),
                  (KV_DIM, n_seq), (KV_DIM, n_seq)]
    return pl.pallas_call(
        functools.partial(_layer_a_sample_kernel, n_in=n_in, n_out=n_out),
        grid=(n_in + n_out,),
        in_specs=[whole(x.shape), whole(norm_g.shape),
                  pl.BlockSpec((D_MODEL, A_SAMPLE_COLS), win_block),
                  whole(v_norm_g.shape),
                  pl.BlockSpec((A_GROUPS, 8, LANES), lambda i: (0, 0, 0)), whole(bs.shape),
                  pl.BlockSpec((A_SAMPLE_ROWS, D_MODEL), wout_block),
                  whole(kv_norm.shape), whole(w_kv.shape),
                  whole(cos.shape), whole(slo.shape), whole(shi.shape)],
        out_specs=[pl.BlockSpec((D_MODEL, A_SAMPLE_COLS), win_block),
                   pl.BlockSpec((A_SAMPLE_ROWS, D_MODEL), wout_block), whole(w_kv.shape[::-1])]
        + [whole(d) for d in small_dims],
        out_shape=[jax.ShapeDtypeStruct(w_in.shape, BF16), jax.ShapeDtypeStruct(w_out.shape, BF16),
                   jax.ShapeDtypeStruct(w_kv.shape[::-1], BF16)]
        + [jax.ShapeDtypeStruct(d, F32) for d in small_dims],
        scratch_shapes=[pltpu.VMEM((n_seq, D_MODEL), BF16),
                        pltpu.VMEM((n_in, n_seq, A_SAMPLE_COLS), F32),
                        pltpu.VMEM((n_out, n_seq, A_SAMPLE_ROWS), BF16),
                        pltpu.VMEM((n_seq, D_MODEL), F32)],
        compiler_params=pltpu.CompilerParams(
            dimension_semantics=("arbitrary",), vmem_limit_bytes=VMEM_LIMIT_BYTES),
        name="layer_a_sample",
    )(x, norm_g, w_in, v_norm_g, ws, bs, w_out, kv_norm, w_kv, cos, slo, shi)


def _layer_b_prompt_kernel(sinks_ref, h_ref, kvg_ref, nbg_ref, fg_ref, wkvt_ref, wqt_ref,
                           wgt_ref, woutt_ref, cost_ref, sint_ref, y_ref, kout_ref, vout_ref,
                           *scratch, tile, n_t, n_sub):
    for sub in range(n_sub):
        rows = pl.ds(sub * tile, tile)
        _layer_b_prompt_tile(
            sinks_ref, h_ref.at[rows, :], kvg_ref, nbg_ref, fg_ref, wkvt_ref, wqt_ref, wgt_ref,
            woutt_ref, cost_ref.at[:, rows], sint_ref.at[:, rows], y_ref.at[rows, :], kout_ref,
            vout_ref, *scratch, tile=tile, n_t=n_t, t=pl.program_id(1) * n_sub + sub,
            first_possible=sub == 0, last_possible=sub == n_sub - 1)


def _layer_b_prompt_tile(sinks_ref, h_ref, kvg_ref, nbg_ref, fg_ref, wkvt_ref, wqt_ref,
                         wgt_ref, woutt_ref, cost_ref, sint_ref,
                         y_ref, kout_ref, vout_ref,
                         kext_scr, vtext_scr, qt_scr, ogt_scr, bias_scr,
                         *, tile, n_t, t, first_possible, last_possible):
    n_keys = WINDOW + Q_BLOCK

    if first_possible:
        @pl.when((pl.program_id(0) == 0) & (t == 0))
        def _():
            j = lax.broadcasted_iota(jnp.int32, (n_keys, Q_BLOCK), 0)
            i = lax.broadcasted_iota(jnp.int32, (n_keys, Q_BLOCK), 1)
            band = (j >= i) & (j <= WINDOW + i)
            bias_scr[0] = jnp.where(band & (j >= WINDOW), 0.0, -jnp.inf)
            bias_scr[1] = jnp.where(band, 0.0, -jnp.inf)

        @pl.when(t == 0)
        def _():
            kext_scr[0:WINDOW, :] = jnp.zeros((WINDOW, KV_DIM), BF16)
            vtext_scr[:, 0:WINDOW] = jnp.zeros((KV_DIM, WINDOW), BF16)

    h = h_ref[...]
    hn = h * lax.rsqrt(jnp.mean(h * h, axis=-1, keepdims=True) + EPS)
    xkv = (hn * kvg_ref[...]).astype(BF16)
    xb = (hn * nbg_ref[...]).astype(BF16)

    cost, sint = cost_ref[...], sint_ref[...]
    half = ROT_DIM // 2

    def rotate_head(rows):
        lo, hi = rows[0:half, :], rows[half:ROT_DIM, :]
        return jnp.concatenate(
            [lo * cost - hi * sint, hi * cost + lo * sint, rows[ROT_DIM:, :]], axis=0)

    kvt = _dot_nt(wkvt_ref[...], xkv)
    kt = jnp.concatenate(
        [rotate_head(kvt[kh * HEAD_DIM:(kh + 1) * HEAD_DIM, :]) for kh in range(N_KV_HEADS)],
        axis=0)
    vt = kvt[KV_DIM:, :]
    kext_scr[WINDOW:, :] = kt.T.astype(BF16)
    vtext_scr[:, WINDOW:] = vt.astype(BF16)

    if last_possible:
        @pl.when(t == n_t - 1)
        def _():
            kout_ref[0] = kt[:, tile - WINDOW:]
            vout_ref[0] = vt[:, tile - WINDOW:]

    qt = _dot_nt(wqt_ref[...], xb)
    for hd in range(N_HEADS):
        rot = rotate_head(qt[hd * HEAD_DIM:(hd + 1) * HEAD_DIM, :])
        qt_scr[hd * HEAD_DIM:(hd + 1) * HEAD_DIM, :] = (rot * Q_SCALE_LOG2).astype(BF16)

    lane_head = lax.broadcasted_iota(jnp.int32, (1, GQA_GROUP * Q_BLOCK), 1) // Q_BLOCK
    zeros_half = jnp.zeros((HEAD_DIM, GQA_GROUP * Q_BLOCK), BF16)
    ones_rows = jnp.ones((BF16_SUBLANES, n_keys), BF16)

    def scores(qb, kh):
        qcols = slice(qb * Q_BLOCK, (qb + 1) * Q_BLOCK)
        keys = slice(qb * Q_BLOCK, qb * Q_BLOCK + n_keys)
        q4 = jnp.concatenate(
            [qt_scr[(kh * GQA_GROUP + r) * HEAD_DIM:(kh * GQA_GROUP + r + 1) * HEAD_DIM, qcols]
             for r in range(GQA_GROUP)], axis=1)
        q4 = jnp.concatenate([q4, zeros_half] if kh % 2 == 0 else [zeros_half, q4], axis=0)
        kblk = kext_scr[keys, (kh // 2) * LANES:(kh // 2 + 1) * LANES]
        return _dot(kblk, q4)

    def finish(qb, kh, s, s_ahead):
        qcols = slice(qb * Q_BLOCK, (qb + 1) * Q_BLOCK)
        keys = slice(qb * Q_BLOCK, qb * Q_BLOCK + n_keys)
        if first_possible and qb == 0:
            bias = bias_scr[jnp.where(t > 0, 1, 0)]
        else:
            bias = bias_scr[1]
        s = s + jnp.concatenate([bias] * GQA_GROUP, axis=1)
        sink = jnp.zeros((1, GQA_GROUP * Q_BLOCK), F32)
        for r in range(GQA_GROUP):
            sink = jnp.where(lane_head == r, sinks_ref[kh * GQA_GROUP + r] * LOG2_E, sink)
        m = jnp.maximum(jnp.max(s, axis=0, keepdims=True), sink)
        p = jnp.exp2(s - m)
        if s_ahead is not None:
            p = jnp.concatenate(
                [p[:n_keys - 8, :], p[n_keys - 8:, :] + _zero_of(s_ahead[0:8, :])], axis=0)
        vt_ones = jnp.concatenate(
            [vtext_scr[kh * HEAD_DIM:(kh + 1) * HEAD_DIM, keys], ones_rows], axis=0)
        ot = _dot(vt_ones, p.astype(BF16))
        denom = ot[HEAD_DIM:HEAD_DIM + 1, :] + jnp.exp2(sink - m)
        ot = ot[0:HEAD_DIM, :] * (1.0 / denom)
        for r in range(GQA_GROUP):
            hd = kh * GQA_GROUP + r
            ogt_scr[hd * HEAD_DIM:(hd + 1) * HEAD_DIM, qcols] = ot[:, r * Q_BLOCK:(r + 1) * Q_BLOCK]

    blocks = [(qb, kh) for qb in range(tile // Q_BLOCK) for kh in range(N_KV_HEADS)]
    pending = [scores(*blk) for blk in blocks[:ATTN_AHEAD]]
    for n, blk in enumerate(blocks):
        if n + ATTN_AHEAD < len(blocks):
            pending.append(scores(*blocks[n + ATTN_AHEAD]))
        s_cur = pending.pop(0)
        finish(*blk, s_cur, pending[-1] if pending else None)

    kext_scr[0:WINDOW, :] = kext_scr[tile:tile + WINDOW, :]
    vtext_scr[:, 0:WINDOW] = vtext_scr[:, tile:tile + WINDOW]

    ogt = jnp.concatenate(
        [(ogt_scr[rc * OUT_ROWS:(rc + 1) * OUT_ROWS, :]
          * _silu(_dot_nt(wgt_ref[rc * OUT_ROWS:(rc + 1) * OUT_ROWS, :], xb))).astype(BF16)
         for rc in range(D_MODEL // OUT_ROWS)], axis=0)
    h2 = jnp.concatenate(
        [h[:, rc * OUT_ROWS:(rc + 1) * OUT_ROWS]
         + _dot(woutt_ref[rc * OUT_ROWS:(rc + 1) * OUT_ROWS, :], ogt).T
         for rc in range(D_MODEL // OUT_ROWS)], axis=1)
    y_ref[...] = _rms(h2, fg_ref[...])


def _layer_b_prompt(h, sinks, kv_norm, norm_b, final_norm, w_kv_t, w_qg_t, w_out_t, *, batch, seq):
    tile = B_TILE
    n_t = seq // tile
    n_sub = B_SUBTILES
    step = n_sub * tile
    n_steps = n_t // n_sub
    pos = jnp.arange(seq, dtype=F32)
    inv = ROPE_THETA ** (-jnp.arange(0, ROT_DIM, 2, dtype=F32) / ROT_DIM)
    ang_t = inv[:, None] * pos[None, :]
    cos_t, sin_t = jnp.cos(ang_t), jnp.sin(ang_t)
    tok_spec = pl.BlockSpec((step, D_MODEL), lambda b, t, *_: (b * n_steps + t, 0))
    rot_t_spec = pl.BlockSpec((ROT_DIM // 2, step), lambda b, t, *_: (0, t))
    last_spec = pl.BlockSpec((1, KV_DIM, WINDOW), lambda b, t, *_: (b, 0, 0))

    def const(shape):
        return pl.BlockSpec(shape, lambda *_: (0,) * len(shape), pipeline_mode=pl.Buffered(1))

    def half(i):
        return pl.BlockSpec((D_MODEL, D_MODEL), lambda *_: (i, 0), pipeline_mode=pl.Buffered(1))

    return pl.pallas_call(
        functools.partial(_layer_b_prompt_kernel, tile=tile, n_t=n_t, n_sub=n_sub),
        grid_spec=pltpu.PrefetchScalarGridSpec(
            num_scalar_prefetch=1,
            grid=(batch, n_steps),
            in_specs=[tok_spec, const(kv_norm.shape), const(norm_b.shape), const(final_norm.shape),
                      const(w_kv_t.shape), half(0), half(1), const(w_out_t.shape),
                      rot_t_spec, rot_t_spec],
            out_specs=[tok_spec, last_spec, last_spec],
            scratch_shapes=[pltpu.VMEM((WINDOW + tile, KV_DIM), BF16),
                            pltpu.VMEM((KV_DIM, WINDOW + tile), BF16),
                            pltpu.VMEM((D_MODEL, tile), BF16),
                            pltpu.VMEM((D_MODEL, tile), F32),
                            pltpu.VMEM((2, WINDOW + Q_BLOCK, Q_BLOCK), F32)]),
        out_shape=[jax.ShapeDtypeStruct((batch * seq, D_MODEL), F32),
                   jax.ShapeDtypeStruct((batch, KV_DIM, WINDOW), F32),
                   jax.ShapeDtypeStruct((batch, KV_DIM, WINDOW), F32)],
        compiler_params=pltpu.CompilerParams(
            dimension_semantics=("arbitrary", "arbitrary"), vmem_limit_bytes=VMEM_LIMIT_BYTES),
        name="layer_b_prompt",
    )(sinks, h, kv_norm, norm_b, final_norm, w_kv_t, w_qg_t, w_qg_t, w_out_t, cos_t, sin_t)


def _layer_b_sample_kernel(sinks_ref, h_ref, nbg_ref, fg_ref, wqt_ref, wqgt_hbm, woutt_hbm,
                           cos_ref, slo_ref, shi_ref, knew_ref, vnew_ref, ck_hbm, cv_hbm,
                           y_ref, q_scr, gate_scr, o_scr, wgt_scr, woutt_scr, ck_scr, cv_scr,
                           cache_sem, weight_sem, *, n_seq, b_tile):
    step = pl.program_id(0)
    n_steps = n_seq // b_tile

    def cache_copies(s, slot):
        rows = pl.ds(s * b_tile, b_tile)
        return (pltpu.make_async_copy(ck_hbm.at[rows], ck_scr.at[slot], cache_sem.at[0, slot]),
                pltpu.make_async_copy(cv_hbm.at[rows], cv_scr.at[slot], cache_sem.at[1, slot]))

    weight_copies = (
        pltpu.make_async_copy(wqgt_hbm.at[pl.ds(D_MODEL, D_MODEL), :], wgt_scr, weight_sem.at[0]),
        pltpu.make_async_copy(woutt_hbm, woutt_scr, weight_sem.at[1]))

    def normed_h():
        h = h_ref[...]
        hn = h * lax.rsqrt(jnp.mean(h * h, axis=-1, keepdims=True) + EPS)
        return (hn * nbg_ref[...]).astype(BF16)

    @pl.when(step == 0)
    def _():
        for s in range(2):
            for copy in cache_copies(s, s):
                copy.start()
        for copy in weight_copies:
            copy.start()
        q = _dot_nt(normed_h(), wqt_ref[...])
        cos, slo, shi = cos_ref[...], slo_ref[...], shi_ref[...]
        for c in range(D_MODEL // LANES):
            q2 = _rotate(q[:, c * LANES:(c + 1) * LANES], cos, slo, shi) * HEAD_DIM ** -0.5
            for i in range(LANES // HEAD_DIM):
                dst = _member_major(c * (LANES // HEAD_DIM) + i)
                q_scr[:, dst] = q2[:, i * HEAD_DIM:(i + 1) * HEAD_DIM]

    @pl.when(step + 2 < n_steps)
    def _():
        for copy in cache_copies(step + 2, (step + 2) % SAMPLE_SLOTS):
            copy.start()

    slot = step % SAMPLE_SLOTS
    for copy in cache_copies(step, slot):
        copy.wait()

    n_rows = GQA_GROUP * N_KV_HEADS * SAMPLE_GROUP
    row = lax.broadcasted_iota(jnp.int32, (n_rows, 1), 0)
    row_kh = (row // SAMPLE_GROUP) % N_KV_HEADS
    row_seq = row % SAMPLE_GROUP
    lane_kh = lax.broadcasted_iota(jnp.int32, (1, KV_DIM), 1) // HEAD_DIM
    own = row_kh == lane_kh
    sink = jnp.zeros((n_rows, 1), F32)
    for r in range(GQA_GROUP):
        for kh in range(N_KV_HEADS):
            sink = jnp.where(row // SAMPLE_GROUP == r * N_KV_HEADS + kh,
                             sinks_ref[kh * GQA_GROUP + r], sink)
    n_blk = GQA_GROUP * N_KV_HEADS

    def group(i, carry):
        b0 = i * SAMPLE_GROUP
        g0 = pl.multiple_of(step * b_tile + b0, SAMPLE_GROUP)
        seqs = pl.ds(g0, SAMPLE_GROUP)
        q8 = q_scr[seqs, :]
        qexp = jnp.concatenate(
            [q8[:, r * KV_DIM:(r + 1) * KV_DIM] for r in range(GQA_GROUP)
             for _ in range(N_KV_HEADS)], axis=0)
        qexp = jnp.where(own, qexp, 0.0).astype(BF16)
        knew8 = knew_ref[seqs, :].astype(BF16).astype(F32)
        vnew8 = vnew_ref[seqs, :].astype(BF16).astype(F32)
        s_new = jnp.sum(qexp.astype(F32) * jnp.concatenate([knew8] * n_blk, axis=0),
                        axis=1, keepdims=True)
        s_old = jnp.zeros((n_rows, WINDOW), F32)
        for b in range(SAMPLE_GROUP):
            s_b = _dot(qexp, ck_scr[slot, b0 + b])
            s_old = jnp.where(row_seq == b, s_b, s_old)
        m = jnp.maximum(jnp.maximum(jnp.max(s_old, axis=1, keepdims=True), s_new), sink)
        p_old = jnp.exp(s_old - m)
        p_new = jnp.exp(s_new - m)
        denom = jnp.sum(p_old, axis=1, keepdims=True) + p_new + jnp.exp(sink - m)
        p_old = p_old.astype(BF16)
        o = jnp.zeros((n_rows, KV_DIM), F32)
        for b in range(SAMPLE_GROUP):
            o_b = _dot_nt(p_old, cv_scr[slot, b0 + b])
            o = jnp.where(row_seq == b, o_b, o)
        o = (o + p_new.astype(BF16).astype(F32) * jnp.concatenate([vnew8] * n_blk, axis=0)) / denom
        o = jnp.where(own, o, 0.0)
        for r in range(GQA_GROUP):
            blks = [o[(r * N_KV_HEADS + kh) * SAMPLE_GROUP:(r * N_KV_HEADS + kh + 1) * SAMPLE_GROUP]
                    for kh in range(N_KV_HEADS)]
            o_scr[seqs, r * KV_DIM:(r + 1) * KV_DIM] = (blks[0] + blks[1]) + (blks[2] + blks[3])
        return carry

    lax.fori_loop(0, b_tile // SAMPLE_GROUP, group, 0, unroll=True)

    @pl.when(step == pl.num_programs(0) - 1)
    def _():
        for copy in weight_copies:
            copy.wait()
        gate = _dot_nt(normed_h(), wgt_scr[...])
        for hd in range(N_HEADS):
            gate_scr[:, _member_major(hd)] = gate[:, hd * HEAD_DIM:(hd + 1) * HEAD_DIM]
        og_mm = o_scr[...] * _silu(gate_scr[...])
        og = jnp.concatenate(
            [og_mm[:, _member_major(hd)] for hd in range(N_HEADS)], axis=1).astype(BF16)
        h2 = h_ref[...] + _dot_nt(og, woutt_scr[...])
        y_ref[:, 0, :] = _rms(h2, fg_ref[...])


def _layer_b_sample(h, sinks, norm_b, final_norm, w_qg_t, w_out_t, knew, vnew, cache_k, cache_v):
    n_seq = h.shape[0]
    b_tile = SAMPLE_B_TILE
    cos, slo, shi = _rotary_tables(jnp.full((1,), PAST_LEN, F32))

    def const(shape):
        return pl.BlockSpec(shape, lambda *_: (0,) * len(shape))

    assert cache_k.shape == (n_seq, KV_DIM, WINDOW) and n_seq // b_tile >= 2
    in_hbm = pl.BlockSpec(memory_space=pl.ANY)
    w_q_spec = pl.BlockSpec((D_MODEL, D_MODEL), lambda *_: (0, 0))
    tables = (cos, slo, shi, knew, vnew)
    return pl.pallas_call(
        functools.partial(_layer_b_sample_kernel, n_seq=n_seq, b_tile=b_tile),
        grid_spec=pltpu.PrefetchScalarGridSpec(
            num_scalar_prefetch=1,
            grid=(n_seq // b_tile,),
            in_specs=[const(h.shape), const(norm_b.shape), const(final_norm.shape),
                      w_q_spec, in_hbm, in_hbm] + [const(c.shape) for c in tables]
            + [in_hbm, in_hbm],
            out_specs=const((n_seq, 1, D_MODEL)),
            scratch_shapes=[pltpu.VMEM((n_seq, D_MODEL), F32),
                            pltpu.VMEM((n_seq, D_MODEL), F32),
                            pltpu.VMEM((n_seq, D_MODEL), F32),
                            pltpu.VMEM((D_MODEL, D_MODEL), BF16),
                            pltpu.VMEM((D_MODEL, D_MODEL), BF16),
                            pltpu.VMEM((SAMPLE_SLOTS, b_tile, KV_DIM, WINDOW), BF16),
                            pltpu.VMEM((SAMPLE_SLOTS, b_tile, KV_DIM, WINDOW), BF16),
                            pltpu.SemaphoreType.DMA((2, SAMPLE_SLOTS)),
                            pltpu.SemaphoreType.DMA((2,))]),
        out_shape=jax.ShapeDtypeStruct((n_seq, 1, D_MODEL), F32),
        compiler_params=pltpu.CompilerParams(
            dimension_semantics=("arbitrary",), vmem_limit_bytes=VMEM_LIMIT_BYTES),
        name="layer_b_sample",
    )(sinks, h, norm_b, final_norm, w_qg_t, w_qg_t, w_out_t, *tables, cache_k, cache_v)


def kernel(x_prompt, x_sample, cache_k, cache_v, norm_a, w_in_a, v_norm_a, w_s_a, b_s_a, w_out_a,
           kv_norm, w_kv, norm_b, w_in_b, sinks_b, w_out_b, final_norm):
    batch, seq, _ = x_prompt.shape
    n_seq, dec_seq, _ = x_sample.shape
    assert dec_seq == 1 and seq % CHUNK == 0 and cache_k.shape[1] == WINDOW
    assert norm_a.shape[0] == 1 and norm_b.shape[0] == 1

    row = lambda g: g.reshape(1, -1)

    def to_window(x_t):
        n = x_t.shape[0]
        return x_t.reshape(n, N_KV_HEADS, HEAD_DIM, WINDOW).transpose(0, 3, 1, 2)

    def from_window(x):
        return x.transpose(0, 2, 3, 1).reshape(x.shape[0], KV_DIM, WINDOW)

    cache_kt, cache_vt = from_window(cache_k), from_window(cache_v)

    w_in_a16, w_out_a16, w_kv_t, h_s, av_s, knew, vnew, knew_t, vnew_t = _layer_a_sample(
        x_sample, row(norm_a[0]), w_in_a[0], row(v_norm_a[0]), w_s_a[0],
        b_s_a[0], w_out_a[0], row(kv_norm), w_kv)
    h_p, kt_s, vt_s, cache_k16, cache_v16, w_qg_t, w_out_b_t = _layer_a_prompt(
        x_prompt.reshape(batch * seq, D_MODEL), row(norm_a[0]), w_in_a16, row(v_norm_a[0]),
        w_s_a[0], b_s_a[0], w_out_a16, cache_kt, cache_vt, knew_t, vnew_t, w_in_b[0], w_out_b[0])

    y_p, kt_p, vt_p = _layer_b_prompt(
        h_p, sinks_b[0], row(kv_norm), row(norm_b[0]), row(final_norm),
        w_kv_t, w_qg_t, w_out_b_t, batch=batch, seq=seq)
    y_s = _layer_b_sample(h_s, sinks_b[0], row(norm_b[0]), row(final_norm), w_qg_t, w_out_b_t,
                          knew, vnew, cache_k16, cache_v16)

    return (y_p.reshape(batch, seq, D_MODEL),
            y_s,
            to_window(kt_p),
            to_window(vt_p),
            to_window(kt_s),
            to_window(vt_s),
            av_s.reshape(1, n_seq, 1, A_WIDTH))
```

```python
import functools

import jax
import jax.numpy as jnp
from jax import lax
from jax.experimental import pallas as pl
from jax.experimental.pallas import tpu as pltpu

D_MODEL = 1024
PAST_LEN = 8192
CHUNK = 128
A_WIDTH = 2 * D_MODEL
A_GROUPS = 8
A_GROUP_DIM = A_WIDTH // A_GROUPS
HEAD_DIM = 64
N_HEADS = D_MODEL // HEAD_DIM
N_KV_HEADS = 4
GQA_GROUP = N_HEADS // N_KV_HEADS
KV_DIM = N_KV_HEADS * HEAD_DIM
WINDOW = 128
Q_BLOCK = 128
ROT_DIM = HEAD_DIM // 4
ROPE_THETA = 500000.0
EPS = 1e-5

LANES = 128
BF16_SUBLANES = 16
LOG2_E = 1.4426950408889634
Q_SCALE_LOG2 = HEAD_DIM ** -0.5 * LOG2_E
VMEM_LIMIT_BYTES = 56 * 1024 * 1024

A_TILE = 512
B_TILE = 512
B_SUBTILES = 2
A_SAMPLE_COLS = 1024
A_SAMPLE_ROWS = 1024
OUT_ROWS = 256
ATTN_AHEAD = 2
SAMPLE_B_TILE = 16
SAMPLE_GROUP = 8
SAMPLE_SLOTS = 3

F32 = jnp.float32
BF16 = jnp.bfloat16


def _rms(x, g):
    return x * lax.rsqrt(jnp.mean(x * x, axis=-1, keepdims=True) + EPS) * g


def _silu(x):
    return x * jax.nn.sigmoid(x)


def _dot(a, b):
    return jnp.dot(a, b, preferred_element_type=F32)


def _dot_nt(a, b):
    return lax.dot_general(a, b, (((1,), (1,)), ((), ())), preferred_element_type=F32)


def _zero_of(x):
    bits = pltpu.bitcast(x, jnp.uint32)
    return ((bits >> 16) >> 16).astype(F32)


def _member_major(head):
    kh, r = divmod(head, GQA_GROUP)
    start = (r * N_KV_HEADS + kh) * HEAD_DIM
    return slice(start, start + HEAD_DIM)


def _rotate(x, cos, sin_lo, sin_hi):
    return (x * cos + pltpu.roll(x, LANES - ROT_DIM // 2, 1) * sin_lo
            + pltpu.roll(x, ROT_DIM // 2, 1) * sin_hi)


def _rotary_tables(positions):
    lane = jnp.arange(LANES) % HEAD_DIM
    freq = (2 * (lane % (ROT_DIM // 2))).astype(F32)
    ang = positions[:, None] * (ROPE_THETA ** (-freq / ROT_DIM))[None, :]
    first = (lane < ROT_DIM // 2)[None, :]
    second = ((lane >= ROT_DIM // 2) & (lane < ROT_DIM))[None, :]
    cos = jnp.where(first | second, jnp.cos(ang), 1.0)
    sin_lo = jnp.where(first, -jnp.sin(ang), 0.0)
    sin_hi = jnp.where(second, jnp.sin(ang), 0.0)
    return cos, sin_lo, sin_hi


def _layer_a_tile(x, ng_ref, win_ref, vg_ref, ws_ref, bs_ref, wout_ref, y_scr, *, tile):
    xn = _rms(x, ng_ref[...]).astype(BF16)
    vb = _rms(_dot(xn, win_ref[:, A_WIDTH:2 * A_WIDTH]), vg_ref[...]).astype(BF16)
    row = lax.broadcasted_iota(jnp.int32, (CHUNK, CHUNK), 0)
    col = lax.broadcasted_iota(jnp.int32, (CHUNK, CHUNK), 1)
    tri = row >= col
    ws = [jnp.where(tri, ws_ref[g], 0.0).astype(BF16) for g in range(A_GROUPS)]
    z = jnp.concatenate(
        [jnp.concatenate(
            [_dot(ws[g], vb[c * CHUNK:(c + 1) * CHUNK, g * A_GROUP_DIM:(g + 1) * A_GROUP_DIM])
             for g in range(A_GROUPS)], axis=1) + bs_ref[...]
         for c in range(tile // CHUNK)], axis=0)
    uz = _dot(xn, win_ref[:, 0:A_WIDTH]) * z
    gate = _dot(xn, win_ref[:, 2 * A_WIDTH:3 * A_WIDTH])
    y_scr[...] = (uz * _silu(gate)).astype(BF16)
    return x + _dot(y_scr[...], wout_ref[...])


def _layer_a_prompt_kernel(x_ref, ng_ref, win_ref, vg_ref, ws_ref, b_ref, wout_ref,
                           ck_ref, cv_ref, knewt_ref, vnewt_ref, winb_ref, woutb_ref,
                           h_ref, kout_ref, vout_ref, k16_ref, v16_ref, wqgt_ref, woutbt_ref,
                           y_scr, bs_scr, *, tile, n_roll):
    @pl.when(pl.program_id(0) == 0)
    def _():
        b = jnp.concatenate(
            [b_ref[...], jnp.zeros((CHUNK - A_GROUPS, CHUNK), F32)], axis=0).T
        for g in range(A_GROUPS):
            bs_scr[:, g * A_GROUP_DIM:(g + 1) * A_GROUP_DIM] = jnp.broadcast_to(
                b[:, g:g + 1], (CHUNK, A_GROUP_DIM))

    wqgt_ref[...] = winb_ref[...].T.astype(BF16)
    woutbt_ref[...] = woutb_ref[...].T.astype(BF16)
    is_last = lax.broadcasted_iota(jnp.int32, (KV_DIM, WINDOW), 1) == WINDOW - 1
    for b in range(n_roll):
        g = pl.program_id(0) * n_roll + b
        blk = pl.ds(pl.multiple_of((g // LANES) * LANES, LANES), LANES)
        to_last = LANES - 1 - g % LANES
        kout_ref[b] = jnp.where(is_last, pltpu.roll(knewt_ref[:, blk], to_last, 1),
                                pltpu.roll(ck_ref[b], WINDOW - 1, 1))
        vout_ref[b] = jnp.where(is_last, pltpu.roll(vnewt_ref[:, blk], to_last, 1),
                                pltpu.roll(cv_ref[b], WINDOW - 1, 1))
        k16_ref[b] = ck_ref[b].astype(BF16)
        v16_ref[b] = cv_ref[b].astype(BF16)
    h_ref[...] = _layer_a_tile(x_ref[...], ng_ref, win_ref, vg_ref, ws_ref, bs_scr, wout_ref,
                               y_scr, tile=tile)


def _layer_a_sample_kernel(x_ref, ng_ref, win_ref, vg_ref, ws_ref, bs_ref, wout_ref,
                           kvg_ref, wkv_ref, cos_ref, slo_ref, shi_ref,
                           win16_ref, wout16_ref, wkvt_ref, h_ref, av_ref, knew_ref, vnew_ref,
                           knewt_ref, vnewt_ref, xn_scr, proj_scr, y_scr, acc_scr, *, n_in, n_out):
    step = pl.program_id(0)
    blocks_per_branch = n_in // 3

    @pl.when(step == 0)
    def _():
        xn_scr[...] = _rms(x_ref[:, 0, :], ng_ref[...]).astype(BF16)
        wkvt_ref[...] = wkv_ref[...].T.astype(BF16)

    @pl.when(step < n_in)
    def _():
        w = win_ref[...].astype(BF16)
        win16_ref[...] = w
        proj_scr[step] = _dot(xn_scr[...], w)

    @pl.when(step == n_in)
    def _():
        def branch(i):
            return jnp.concatenate(
                [proj_scr[i * blocks_per_branch + j] for j in range(blocks_per_branch)], axis=1)

        v = _rms(branch(1), vg_ref[...])
        av_ref[:, 0, :] = v
        lane_group = lax.broadcasted_iota(jnp.int32, (1, A_WIDTH), 1) // A_GROUP_DIM
        ws_row = jnp.zeros((1, A_WIDTH), F32)
        bs_row = jnp.zeros((1, A_WIDTH), F32)
        for g in range(A_GROUPS):
            ws_row = jnp.where(lane_group == g, ws_ref[g, 0:1, 0:1], ws_row)
            bs_row = jnp.where(lane_group == g, bs_ref[g:g + 1, 0:1], bs_row)
        z = v * ws_row + bs_row
        y = (branch(0) * z * _silu(branch(2))).astype(BF16)
        rows = A_WIDTH // n_out
        for j in range(n_out):
            y_scr[j] = y[:, j * rows:(j + 1) * rows]

    @pl.when(step >= n_in)
    def _():
        w = wout_ref[...].astype(BF16)
        wout16_ref[...] = w
        part = _dot(y_scr[step - n_in], w)

        @pl.when(step == n_in)
        def _():
            acc_scr[...] = part

        @pl.when(step > n_in)
        def _():
            acc_scr[...] += part

    @pl.when(step == n_in + n_out - 1)
    def _():
        h = x_ref[:, 0, :] + acc_scr[...]
        h_ref[...] = h
        hn = h * lax.rsqrt(jnp.mean(h * h, axis=-1, keepdims=True) + EPS)
        kv = _dot_nt((hn * kvg_ref[...]).astype(BF16), wkvt_ref[...])
        cos, slo, shi = cos_ref[...], slo_ref[...], shi_ref[...]
        k = jnp.concatenate(
            [_rotate(kv[:, c * LANES:(c + 1) * LANES], cos, slo, shi)
             for c in range(KV_DIM // LANES)], axis=1)
        knew_ref[...] = k
        vnew_ref[...] = kv[:, KV_DIM:]
        knewt_ref[...] = k.T
        vnewt_ref[...] = kv[:, KV_DIM:].T


def _const_spec(shape):
    return pl.BlockSpec(shape, lambda *_: (0,) * len(shape), pipeline_mode=pl.Buffered(1))


def _layer_a_prompt(x, norm_g, w_in, v_norm_g, ws, bs, w_out, cache_k, cache_v, knew_t, vnew_t,
                    w_in_b, w_out_b):
    n_tok = x.shape[0]
    tile = A_TILE
    n_steps = n_tok // tile
    n_seq = cache_k.shape[0]
    n_roll = n_seq // n_steps
    assert n_roll * n_steps == n_seq and n_seq % LANES == 0
    assert max(w_in_b.shape[1], w_out_b.shape[1]) <= n_steps * LANES
    tok_spec = pl.BlockSpec((tile, D_MODEL), lambda i: (i, 0))
    cache_spec = pl.BlockSpec((n_roll, KV_DIM, WINDOW), lambda i: (i, 0, 0))

    def column_block(w):
        last = w.shape[1] // LANES - 1
        return pl.BlockSpec((w.shape[0], LANES), lambda i: (0, jnp.minimum(i, last)))

    def row_block(w):
        last = w.shape[1] // LANES - 1
        return pl.BlockSpec((LANES, w.shape[0]), lambda i: (jnp.minimum(i, last), 0))

    consts = (norm_g, w_in, v_norm_g, ws, bs, w_out)
    return pl.pallas_call(
        functools.partial(_layer_a_prompt_kernel, tile=tile, n_roll=n_roll),
        grid=(n_steps,),
        in_specs=[tok_spec] + [_const_spec(c.shape) for c in consts]
        + [cache_spec, cache_spec, _const_spec(knew_t.shape), _const_spec(vnew_t.shape),
           column_block(w_in_b), column_block(w_out_b)],
        out_specs=[tok_spec, cache_spec, cache_spec, cache_spec, cache_spec,
                   row_block(w_in_b), row_block(w_out_b)],
        out_shape=[jax.ShapeDtypeStruct((n_tok, D_MODEL), F32),
                   jax.ShapeDtypeStruct(cache_k.shape, F32),
                   jax.ShapeDtypeStruct(cache_v.shape, F32),
                   jax.ShapeDtypeStruct(cache_k.shape, BF16),
                   jax.ShapeDtypeStruct(cache_v.shape, BF16),
                   jax.ShapeDtypeStruct(w_in_b.shape[::-1], BF16),
                   jax.ShapeDtypeStruct(w_out_b.shape[::-1], BF16)],
        scratch_shapes=[pltpu.VMEM((tile, A_WIDTH), BF16), pltpu.VMEM((CHUNK, A_WIDTH), F32)],
        compiler_params=pltpu.CompilerParams(
            dimension_semantics=("arbitrary",), vmem_limit_bytes=VMEM_LIMIT_BYTES),
        name="layer_a_prompt",
    )(x, *consts, cache_k, cache_v, knew_t, vnew_t, w_in_b, w_out_b)


def _layer_a_sample(x, norm_g, w_in, v_norm_g, ws, bs, w_out, kv_norm, w_kv):
    n_seq = x.shape[0]
    n_in = w_in.shape[1] // A_SAMPLE_COLS
    n_out = w_out.shape[0] // A_SAMPLE_ROWS
    assert n_in % 3 == 0
    cos, slo, shi = _rotary_tables(jnp.full((1,), PAST_LEN, F32))
    whole = lambda shape: pl.BlockSpec(shape, lambda i: (0,) * len(shape))
    win_block = lambda i: (0, jnp.minimum(i, n_in - 1))
    wout_block = lambda i: (jnp.maximum(i - n_in, 0), 0)
    small_dims = [(n_seq, D_MODEL), (n_seq, 1, A_WIDTH), (n_seq, KV_DIM), (n_seq, KV_DIM),
                  (KV_DIM, n_seq), (KV_DIM, n_seq)]
    return pl.pallas_call(
        functools.partial(_layer_a_sample_kernel, n_in=n_in, n_out=n_out),
        grid=(n_in + n_out,),
        in_specs=[whole(x.shape), whole(norm_g.shape),
                  pl.BlockSpec((D_MODEL, A_SAMPLE_COLS), win_block),
                  whole(v_norm_g.shape),
                  pl.BlockSpec((A_GROUPS, 8, LANES), lambda i: (0, 0, 0)), whole(bs.shape),
                  pl.BlockSpec((A_SAMPLE_ROWS, D_MODEL), wout_block),
                  whole(kv_norm.shape), whole(w_kv.shape),
                  whole(cos.shape), whole(slo.shape), whole(shi.shape)],
        out_specs=[pl.BlockSpec((D_MODEL, A_SAMPLE_COLS), win_block),
                   pl.BlockSpec((A_SAMPLE_ROWS, D_MODEL), wout_block), whole(w_kv.shape[::-1])]
        + [whole(d) for d in small_dims],
        out_shape=[jax.ShapeDtypeStruct(w_in.shape, BF16), jax.ShapeDtypeStruct(w_out.shape, BF16),
                   jax.ShapeDtypeStruct(w_kv.shape[::-1], BF16)]
        + [jax.ShapeDtypeStruct(d, F32) for d in small_dims],
        scratch_shapes=[pltpu.VMEM((n_seq, D_MODEL), BF16),
                        pltpu.VMEM((n_in, n_seq, A_SAMPLE_COLS), F32),
                        pltpu.VMEM((n_out, n_seq, A_SAMPLE_ROWS), BF16),
                        pltpu.VMEM((n_seq, D_MODEL), F32)],
        compiler_params=pltpu.CompilerParams(
            dimension_semantics=("arbitrary",), vmem_limit_bytes=VMEM_LIMIT_BYTES),
        name="layer_a_sample",
    )(x, norm_g, w_in, v_norm_g, ws, bs, w_out, kv_norm, w_kv, cos, slo, shi)


def _layer_b_prompt_kernel(sinks_ref, h_ref, kvg_ref, nbg_ref, fg_ref, wkvt_ref, wqt_ref,
                           wgt_ref, woutt_ref, cost_ref, sint_ref, y_ref, kout_ref, vout_ref,
                           *scratch, tile, n_t, n_sub):
    for sub in range(n_sub):
        rows = pl.ds(sub * tile, tile)
        _layer_b_prompt_tile(
            sinks_ref, h_ref.at[rows, :], kvg_ref, nbg_ref, fg_ref, wkvt_ref, wqt_ref, wgt_ref,
            woutt_ref, cost_ref.at[:, rows], sint_ref.at[:, rows], y_ref.at[rows, :], kout_ref,
            vout_ref, *scratch, tile=tile, n_t=n_t, t=pl.program_id(1) * n_sub + sub,
            first_possible=sub == 0, last_possible=sub == n_sub - 1)


def _layer_b_prompt_tile(sinks_ref, h_ref, kvg_ref, nbg_ref, fg_ref, wkvt_ref, wqt_ref,
                         wgt_hbm, woutt_hbm, cost_ref, sint_ref,
                         y_ref, kout_ref, vout_ref,
                         kext_scr, vtext_scr, qt_scr, ogt_scr, bias_scr, wgt_ref, woutt_ref, w_sem,
                         *, tile, n_t, t, first_possible, last_possible):
    n_keys = WINDOW + Q_BLOCK
    late_weights = (
        pltpu.make_async_copy(wgt_hbm.at[pl.ds(D_MODEL, D_MODEL), :], wgt_ref, w_sem.at[0]),
        pltpu.make_async_copy(woutt_hbm, woutt_ref, w_sem.at[1]))

    if first_possible:
        @pl.when((pl.program_id(0) == 0) & (t == 0))
        def _():
            for copy in late_weights:
                copy.start()
            j = lax.broadcasted_iota(jnp.int32, (n_keys, Q_BLOCK), 0)
            i = lax.broadcasted_iota(jnp.int32, (n_keys, Q_BLOCK), 1)
            band = (j >= i) & (j <= WINDOW + i)
            bias_scr[0] = jnp.where(band & (j >= WINDOW), 0.0, -jnp.inf)
            bias_scr[1] = jnp.where(band, 0.0, -jnp.inf)

        @pl.when(t == 0)
        def _():
            kext_scr[0:WINDOW, :] = jnp.zeros((WINDOW, KV_DIM), BF16)
            vtext_scr[:, 0:WINDOW] = jnp.zeros((KV_DIM, WINDOW), BF16)

    h = h_ref[...]
    hn = h * lax.rsqrt(jnp.mean(h * h, axis=-1, keepdims=True) + EPS)
    xkv = (hn * kvg_ref[...]).astype(BF16)
    xb = (hn * nbg_ref[...]).astype(BF16)

    cost, sint = cost_ref[...], sint_ref[...]
    half = ROT_DIM // 2

    def rotate_head(rows):
        lo, hi = rows[0:half, :], rows[half:ROT_DIM, :]
        return jnp.concatenate(
            [lo * cost - hi * sint, hi * cost + lo * sint, rows[ROT_DIM:, :]], axis=0)

    kvt = _dot_nt(wkvt_ref[...], xkv)
    kt = jnp.concatenate(
        [rotate_head(kvt[kh * HEAD_DIM:(kh + 1) * HEAD_DIM, :]) for kh in range(N_KV_HEADS)],
        axis=0)
    vt = kvt[KV_DIM:, :]
    kext_scr[WINDOW:, :] = kt.T.astype(BF16)
    vtext_scr[:, WINDOW:] = vt.astype(BF16)

    if last_possible:
        @pl.when(t == n_t - 1)
        def _():
            kout_ref[0] = kt[:, tile - WINDOW:]
            vout_ref[0] = vt[:, tile - WINDOW:]

    qt = _dot_nt(wqt_ref[...], xb)
    for hd in range(N_HEADS):
        rot = rotate_head(qt[hd * HEAD_DIM:(hd + 1) * HEAD_DIM, :])
        qt_scr[hd * HEAD_DIM:(hd + 1) * HEAD_DIM, :] = (rot * Q_SCALE_LOG2).astype(BF16)

    lane_head = lax.broadcasted_iota(jnp.int32, (1, GQA_GROUP * Q_BLOCK), 1) // Q_BLOCK
    zeros_half = jnp.zeros((HEAD_DIM, GQA_GROUP * Q_BLOCK), BF16)
    ones_rows = jnp.ones((BF16_SUBLANES, n_keys), BF16)

    def scores(qb, kh):
        qcols = slice(qb * Q_BLOCK, (qb + 1) * Q_BLOCK)
        keys = slice(qb * Q_BLOCK, qb * Q_BLOCK + n_keys)
        q4 = jnp.concatenate(
            [qt_scr[(kh * GQA_GROUP + r) * HEAD_DIM:(kh * GQA_GROUP + r + 1) * HEAD_DIM, qcols]
             for r in range(GQA_GROUP)], axis=1)
        q4 = jnp.concatenate([q4, zeros_half] if kh % 2 == 0 else [zeros_half, q4], axis=0)
        kblk = kext_scr[keys, (kh // 2) * LANES:(kh // 2 + 1) * LANES]
        return _dot(kblk, q4)

    def finish(qb, kh, s, s_ahead):
        qcols = slice(qb * Q_BLOCK, (qb + 1) * Q_BLOCK)
        keys = slice(qb * Q_BLOCK, qb * Q_BLOCK + n_keys)
        if first_possible and qb == 0:
            bias = bias_scr[jnp.where(t > 0, 1, 0)]
        else:
            bias = bias_scr[1]
        s = s + jnp.concatenate([bias] * GQA_GROUP, axis=1)
        sink = jnp.zeros((1, GQA_GROUP * Q_BLOCK), F32)
        for r in range(GQA_GROUP):
            sink = jnp.where(lane_head == r, sinks_ref[kh * GQA_GROUP + r] * LOG2_E, sink)
        m = jnp.maximum(jnp.max(s, axis=0, keepdims=True), sink)
        p = jnp.exp2(s - m)
        if s_ahead is not None:
            p = jnp.concatenate(
                [p[:n_keys - 8, :], p[n_keys - 8:, :] + _zero_of(s_ahead[0:8, :])], axis=0)
        vt_ones = jnp.concatenate(
            [vtext_scr[kh * HEAD_DIM:(kh + 1) * HEAD_DIM, keys], ones_rows], axis=0)
        ot = _dot(vt_ones, p.astype(BF16))
        denom = ot[HEAD_DIM:HEAD_DIM + 1, :] + jnp.exp2(sink - m)
        ot = ot[0:HEAD_DIM, :] * (1.0 / denom)
        for r in range(GQA_GROUP):
            hd = kh * GQA_GROUP + r
            ogt_scr[hd * HEAD_DIM:(hd + 1) * HEAD_DIM, qcols] = ot[:, r * Q_BLOCK:(r + 1) * Q_BLOCK]

    blocks = [(qb, kh) for qb in range(tile // Q_BLOCK) for kh in range(N_KV_HEADS)]
    pending = [scores(*blk) for blk in blocks[:ATTN_AHEAD]]
    for n, blk in enumerate(blocks):
        if n + ATTN_AHEAD < len(blocks):
            pending.append(scores(*blocks[n + ATTN_AHEAD]))
        s_cur = pending.pop(0)
        finish(*blk, s_cur, pending[-1] if pending else None)

    kext_scr[0:WINDOW, :] = kext_scr[tile:tile + WINDOW, :]
    vtext_scr[:, 0:WINDOW] = vtext_scr[:, tile:tile + WINDOW]

    if first_possible:
        @pl.when((pl.program_id(0) == 0) & (t == 0))
        def _():
            for copy in late_weights:
                copy.wait()

    ogt = jnp.concatenate(
        [(ogt_scr[rc * OUT_ROWS:(rc + 1) * OUT_ROWS, :]
          * _silu(_dot_nt(wgt_ref[rc * OUT_ROWS:(rc + 1) * OUT_ROWS, :], xb))).astype(BF16)
         for rc in range(D_MODEL // OUT_ROWS)], axis=0)
    h2 = jnp.concatenate(
        [h[:, rc * OUT_ROWS:(rc + 1) * OUT_ROWS]
         + _dot(woutt_ref[rc * OUT_ROWS:(rc + 1) * OUT_ROWS, :], ogt).T
         for rc in range(D_MODEL // OUT_ROWS)], axis=1)
    y_ref[...] = _rms(h2, fg_ref[...])


def _layer_b_prompt(h, sinks, kv_norm, norm_b, final_norm, w_kv_t, w_qg_t, w_out_t, *, batch, seq):
    tile = B_TILE
    n_t = seq // tile
    n_sub = B_SUBTILES
    step = n_sub * tile
    n_steps = n_t // n_sub
    pos = jnp.arange(seq, dtype=F32)
    inv = ROPE_THETA ** (-jnp.arange(0, ROT_DIM, 2, dtype=F32) / ROT_DIM)
    ang_t = inv[:, None] * pos[None, :]
    cos_t, sin_t = jnp.cos(ang_t), jnp.sin(ang_t)
    tok_spec = pl.BlockSpec((step, D_MODEL), lambda b, t, *_: (b * n_steps + t, 0))
    rot_t_spec = pl.BlockSpec((ROT_DIM // 2, step), lambda b, t, *_: (0, t))
    last_spec = pl.BlockSpec((1, KV_DIM, WINDOW), lambda b, t, *_: (b, 0, 0))

    def const(shape):
        return pl.BlockSpec(shape, lambda *_: (0,) * len(shape), pipeline_mode=pl.Buffered(1))

    def half(i):
        return pl.BlockSpec((D_MODEL, D_MODEL), lambda *_: (i, 0), pipeline_mode=pl.Buffered(1))

    return pl.pallas_call(
        functools.partial(_layer_b_prompt_kernel, tile=tile, n_t=n_t, n_sub=n_sub),
        grid_spec=pltpu.PrefetchScalarGridSpec(
            num_scalar_prefetch=1,
            grid=(batch, n_steps),
            in_specs=[tok_spec, const(kv_norm.shape), const(norm_b.shape), const(final_norm.shape),
                      const(w_kv_t.shape), half(0), pl.BlockSpec(memory_space=pl.ANY),
                      pl.BlockSpec(memory_space=pl.ANY),
                      rot_t_spec, rot_t_spec],
            out_specs=[tok_spec, last_spec, last_spec],
            scratch_shapes=[pltpu.VMEM((WINDOW + tile, KV_DIM), BF16),
                            pltpu.VMEM((KV_DIM, WINDOW + tile), BF16),
                            pltpu.VMEM((D_MODEL, tile), BF16),
                            pltpu.VMEM((D_MODEL, tile), F32),
                            pltpu.VMEM((2, WINDOW + Q_BLOCK, Q_BLOCK), F32),
                            pltpu.VMEM((D_MODEL, D_MODEL), BF16),
                            pltpu.VMEM((D_MODEL, D_MODEL), BF16),
                            pltpu.SemaphoreType.DMA((2,))]),
        out_shape=[jax.ShapeDtypeStruct((batch * seq, D_MODEL), F32),
                   jax.ShapeDtypeStruct((batch, KV_DIM, WINDOW), F32),
                   jax.ShapeDtypeStruct((batch, KV_DIM, WINDOW), F32)],
        compiler_params=pltpu.CompilerParams(
            dimension_semantics=("arbitrary", "arbitrary"), vmem_limit_bytes=VMEM_LIMIT_BYTES),
        name="layer_b_prompt",
    )(sinks, h, kv_norm, norm_b, final_norm, w_kv_t, w_qg_t, w_qg_t, w_out_t, cos_t, sin_t)


def _layer_b_sample_kernel(sinks_ref, h_ref, nbg_ref, fg_ref, wqt_ref, wqgt_hbm, woutt_hbm,
                           cos_ref, slo_ref, shi_ref, knew_ref, vnew_ref, ck_hbm, cv_hbm,
                           y_ref, q_scr, gate_scr, o_scr, wgt_scr, woutt_scr, ck_scr, cv_scr,
                           cache_sem, weight_sem, *, n_seq, b_tile):
    step = pl.program_id(0)
    n_steps = n_seq // b_tile

    def cache_copies(s, slot):
        rows = pl.ds(s * b_tile, b_tile)
        return (pltpu.make_async_copy(ck_hbm.at[rows], ck_scr.at[slot], cache_sem.at[0, slot]),
                pltpu.make_async_copy(cv_hbm.at[rows], cv_scr.at[slot], cache_sem.at[1, slot]))

    weight_copies = (
        pltpu.make_async_copy(wqgt_hbm.at[pl.ds(D_MODEL, D_MODEL), :], wgt_scr, weight_sem.at[0]),
        pltpu.make_async_copy(woutt_hbm, woutt_scr, weight_sem.at[1]))

    def normed_h():
        h = h_ref[...]
        hn = h * lax.rsqrt(jnp.mean(h * h, axis=-1, keepdims=True) + EPS)
        return (hn * nbg_ref[...]).astype(BF16)

    @pl.when(step == 0)
    def _():
        for s in range(2):
            for copy in cache_copies(s, s):
                copy.start()
        for copy in weight_copies:
            copy.start()
        q = _dot_nt(normed_h(), wqt_ref[...])
        cos, slo, shi = cos_ref[...], slo_ref[...], shi_ref[...]
        for c in range(D_MODEL // LANES):
            q2 = _rotate(q[:, c * LANES:(c + 1) * LANES], cos, slo, shi) * HEAD_DIM ** -0.5
            for i in range(LANES // HEAD_DIM):
                dst = _member_major(c * (LANES // HEAD_DIM) + i)
                q_scr[:, dst] = q2[:, i * HEAD_DIM:(i + 1) * HEAD_DIM]

    @pl.when(step + 2 < n_steps)
    def _():
        for copy in cache_copies(step + 2, (step + 2) % SAMPLE_SLOTS):
            copy.start()

    slot = step % SAMPLE_SLOTS
    for copy in cache_copies(step, slot):
        copy.wait()

    n_rows = GQA_GROUP * N_KV_HEADS * SAMPLE_GROUP
    row = lax.broadcasted_iota(jnp.int32, (n_rows, 1), 0)
    row_kh = (row // SAMPLE_GROUP) % N_KV_HEADS
    row_seq = row % SAMPLE_GROUP
    lane_kh = lax.broadcasted_iota(jnp.int32, (1, KV_DIM), 1) // HEAD_DIM
    own = row_kh == lane_kh
    sink = jnp.zeros((n_rows, 1), F32)
    for r in range(GQA_GROUP):
        for kh in range(N_KV_HEADS):
            sink = jnp.where(row // SAMPLE_GROUP == r * N_KV_HEADS + kh,
                             sinks_ref[kh * GQA_GROUP + r], sink)
    n_blk = GQA_GROUP * N_KV_HEADS

    def group(i, carry):
        b0 = i * SAMPLE_GROUP
        g0 = pl.multiple_of(step * b_tile + b0, SAMPLE_GROUP)
        seqs = pl.ds(g0, SAMPLE_GROUP)
        q8 = q_scr[seqs, :]
        qexp = jnp.concatenate(
            [q8[:, r * KV_DIM:(r + 1) * KV_DIM] for r in range(GQA_GROUP)
             for _ in range(N_KV_HEADS)], axis=0)
        qexp = jnp.where(own, qexp, 0.0).astype(BF16)
        knew8 = knew_ref[seqs, :].astype(BF16).astype(F32)
        vnew8 = vnew_ref[seqs, :].astype(BF16).astype(F32)
        s_new = jnp.sum(qexp.astype(F32) * jnp.concatenate([knew8] * n_blk, axis=0),
                        axis=1, keepdims=True)
        s_old = jnp.zeros((n_rows, WINDOW), F32)
        for b in range(SAMPLE_GROUP):
            s_b = _dot(qexp, ck_scr[slot, b0 + b])
            s_old = jnp.where(row_seq == b, s_b, s_old)
        m = jnp.maximum(jnp.maximum(jnp.max(s_old, axis=1, keepdims=True), s_new), sink)
        p_old = jnp.exp(s_old - m)
        p_new = jnp.exp(s_new - m)
        denom = jnp.sum(p_old, axis=1, keepdims=True) + p_new + jnp.exp(sink - m)
        p_old = p_old.astype(BF16)
        o = jnp.zeros((n_rows, KV_DIM), F32)
        for b in range(SAMPLE_GROUP):
            o_b = _dot_nt(p_old, cv_scr[slot, b0 + b])
            o = jnp.where(row_seq == b, o_b, o)
        o = (o + p_new.astype(BF16).astype(F32) * jnp.concatenate([vnew8] * n_blk, axis=0)) / denom
        o = jnp.where(own, o, 0.0)
        for r in range(GQA_GROUP):
            blks = [o[(r * N_KV_HEADS + kh) * SAMPLE_GROUP:(r * N_KV_HEADS + kh + 1) * SAMPLE_GROUP]
                    for kh in range(N_KV_HEADS)]
            o_scr[seqs, r * KV_DIM:(r + 1) * KV_DIM] = (blks[0] + blks[1]) + (blks[2] + blks[3])
        return carry

    lax.fori_loop(0, b_tile // SAMPLE_GROUP, group, 0, unroll=True)

    @pl.when(step == pl.num_programs(0) - 1)
    def _():
        for copy in weight_copies:
            copy.wait()
        gate = _dot_nt(normed_h(), wgt_scr[...])
        for hd in range(N_HEADS):
            gate_scr[:, _member_major(hd)] = gate[:, hd * HEAD_DIM:(hd + 1) * HEAD_DIM]
        og_mm = o_scr[...] * _silu(gate_scr[...])
        og = jnp.concatenate(
            [og_mm[:, _member_major(hd)] for hd in range(N_HEADS)], axis=1).astype(BF16)
        h2 = h_ref[...] + _dot_nt(og, woutt_scr[...])
        y_ref[:, 0, :] = _rms(h2, fg_ref[...])


def _layer_b_sample(h, sinks, norm_b, final_norm, w_qg_t, w_out_t, knew, vnew, cache_k, cache_v):
    n_seq = h.shape[0]
    b_tile = SAMPLE_B_TILE
    cos, slo, shi = _rotary_tables(jnp.full((1,), PAST_LEN, F32))

    def const(shape):
        return pl.BlockSpec(shape, lambda *_: (0,) * len(shape))

    assert cache_k.shape == (n_seq, KV_DIM, WINDOW) and n_seq // b_tile >= 2
    in_hbm = pl.BlockSpec(memory_space=pl.ANY)
    w_q_spec = pl.BlockSpec((D_MODEL, D_MODEL), lambda *_: (0, 0))
    tables = (cos, slo, shi, knew, vnew)
    return pl.pallas_call(
        functools.partial(_layer_b_sample_kernel, n_seq=n_seq, b_tile=b_tile),
        grid_spec=pltpu.PrefetchScalarGridSpec(
            num_scalar_prefetch=1,
            grid=(n_seq // b_tile,),
            in_specs=[const(h.shape), const(norm_b.shape), const(final_norm.shape),
                      w_q_spec, in_hbm, in_hbm] + [const(c.shape) for c in tables]
            + [in_hbm, in_hbm],
            out_specs=const((n_seq, 1, D_MODEL)),
            scratch_shapes=[pltpu.VMEM((n_seq, D_MODEL), F32),
                            pltpu.VMEM((n_seq, D_MODEL), F32),
                            pltpu.VMEM((n_seq, D_MODEL), F32),
                            pltpu.VMEM((D_MODEL, D_MODEL), BF16),
                            pltpu.VMEM((D_MODEL, D_MODEL), BF16),
                            pltpu.VMEM((SAMPLE_SLOTS, b_tile, KV_DIM, WINDOW), BF16),
                            pltpu.VMEM((SAMPLE_SLOTS, b_tile, KV_DIM, WINDOW), BF16),
                            pltpu.SemaphoreType.DMA((2, SAMPLE_SLOTS)),
                            pltpu.SemaphoreType.DMA((2,))]),
        out_shape=jax.ShapeDtypeStruct((n_seq, 1, D_MODEL), F32),
        compiler_params=pltpu.CompilerParams(
            dimension_semantics=("arbitrary",), vmem_limit_bytes=VMEM_LIMIT_BYTES),
        name="layer_b_sample",
    )(sinks, h, norm_b, final_norm, w_qg_t, w_qg_t, w_out_t, *tables, cache_k, cache_v)


def kernel(x_prompt, x_sample, cache_k, cache_v, norm_a, w_in_a, v_norm_a, w_s_a, b_s_a, w_out_a,
           kv_norm, w_kv, norm_b, w_in_b, sinks_b, w_out_b, final_norm):
    batch, seq, _ = x_prompt.shape
    n_seq, dec_seq, _ = x_sample.shape
    assert dec_seq == 1 and seq % CHUNK == 0 and cache_k.shape[1] == WINDOW
    assert norm_a.shape[0] == 1 and norm_b.shape[0] == 1

    row = lambda g: g.reshape(1, -1)

    def to_window(x_t):
        n = x_t.shape[0]
        return x_t.reshape(n, N_KV_HEADS, HEAD_DIM, WINDOW).transpose(0, 3, 1, 2)

    def from_window(x):
        return x.transpose(0, 2, 3, 1).reshape(x.shape[0], KV_DIM, WINDOW)

    cache_kt, cache_vt = from_window(cache_k), from_window(cache_v)

    w_in_a16, w_out_a16, w_kv_t, h_s, av_s, knew, vnew, knew_t, vnew_t = _layer_a_sample(
        x_sample, row(norm_a[0]), w_in_a[0], row(v_norm_a[0]), w_s_a[0],
        b_s_a[0], w_out_a[0], row(kv_norm), w_kv)
    h_p, kt_s, vt_s, cache_k16, cache_v16, w_qg_t, w_out_b_t = _layer_a_prompt(
        x_prompt.reshape(batch * seq, D_MODEL), row(norm_a[0]), w_in_a16, row(v_norm_a[0]),
        w_s_a[0], b_s_a[0], w_out_a16, cache_kt, cache_vt, knew_t, vnew_t, w_in_b[0], w_out_b[0])

    y_p, kt_p, vt_p = _layer_b_prompt(
        h_p, sinks_b[0], row(kv_norm), row(norm_b[0]), row(final_norm),
        w_kv_t, w_qg_t, w_out_b_t, batch=batch, seq=seq)
    y_s = _layer_b_sample(h_s, sinks_b[0], row(norm_b[0]), row(final_norm), w_qg_t, w_out_b_t,
                          knew, vnew, cache_k16, cache_v16)

    return (y_p.reshape(batch, seq, D_MODEL),
            y_s,
            to_window(kt_p),
            to_window(vt_p),
            to_window(kt_s),
            to_window(vt_s),
            av_s.reshape(1, n_seq, 1, A_WIDTH))
```
